```python
import math
import jax, jax.numpy as jnp
from jax import lax
import numpy as np

D_MODEL = 1024
BATCH = 16
SEQ = 2048
DEPTH = 2
DEC_BATCH = 32
DEC_SEQ = 4
PAST_LEN = 16384
PAGE_SIZE = 128

SSM_HEADS = 16
SSM_HEAD_DIM = 64
SSM_D = SSM_HEADS * SSM_HEAD_DIM
SSM_STATE = 128
SSM_GROUPS = 4
CONV_W = 4
CONV_DIM = SSM_D + 2 * SSM_GROUPS * SSM_STATE
SSD_CHUNK = 128
RET_HEADS = 4
RET_DK = 128
RET_DV = 256
RET_CHUNK = 128
ROPE_BASE = 10000.0
DIFF_HEADS = 8
DIFF_KV_HEADS = 4
DIFF_HD = 64
DIFF_VD = 2 * DIFF_HD
Q_BLOCK = 128
N_BUCKETS = 32
MAX_DISTANCE = 128
N_BRANCHES = 3
D_FF = -(-8 * D_MODEL // (3 * 256)) * 256
NORM_EPS = 1e-6

IN_SPLITS = (SSM_D, CONV_DIM, SSM_HEADS,
             RET_HEADS * RET_DK, RET_HEADS * RET_DK, RET_HEADS * RET_DV, RET_HEADS * RET_DV,
             DIFF_HEADS * 2 * DIFF_HD, DIFF_KV_HEADS * 2 * DIFF_HD, DIFF_KV_HEADS * DIFF_VD,
             N_BRANCHES * D_MODEL)
IN_OFFSETS = tuple(int(v) for v in np.cumsum(IN_SPLITS)[:-1])
D_IN = int(sum(IN_SPLITS))

kernel_name = 'hybrid_ssd_retention_diffattn_step'


def rmsnorm(x, g):
    xf = x.astype(jnp.float32)
    y = xf * lax.rsqrt(jnp.mean(xf * xf, axis=-1, keepdims=True) + NORM_EPS)
    return (y * g.astype(jnp.float32)).astype(x.dtype)


def group_rmsnorm(y, n_groups, g):
    B, T, C = y.shape
    yg = y.reshape(B, T, n_groups, C // n_groups)
    yg = yg * lax.rsqrt(jnp.mean(yg * yg, axis=-1, keepdims=True) + NORM_EPS)
    return yg.reshape(B, T, C) * g.astype(jnp.float32)


def head_groupnorm(y, g):
    B, T = y.shape[:2]
    yc = y - jnp.mean(y, axis=-1, keepdims=True)
    yn = yc * lax.rsqrt(jnp.mean(yc * yc, axis=-1, keepdims=True) + NORM_EPS)
    return yn.reshape(B, T, -1) * g.astype(jnp.float32)


def rotary(x, pos):
    half = x.shape[-1] // 2
    inv = 1.0 / (ROPE_BASE ** (jnp.arange(half, dtype=jnp.float32) / half))
    ang = pos.astype(jnp.float32)[:, None] * inv[None, :]
    cos = jnp.cos(ang)[None, :, None, :]
    sin = jnp.sin(ang)[None, :, None, :]
    xf = x.astype(jnp.float32)
    x1, x2 = xf[..., :half], xf[..., half:]
    return jnp.concatenate([x1 * cos - x2 * sin, x1 * sin + x2 * cos], axis=-1)


def t5_bucket(dist):
    n = jnp.maximum(dist, 0)
    max_exact = N_BUCKETS // 2
    large = max_exact + (jnp.log(jnp.maximum(n, 1).astype(jnp.float32) / max_exact)
                         / math.log(MAX_DISTANCE / max_exact) * (N_BUCKETS - max_exact)).astype(jnp.int32)
    large = jnp.minimum(large, N_BUCKETS - 1)
    return jnp.where(n < max_exact, n, large)


def ssd_scan(x, dt, a, bm, cm, h0):
    B, T = x.shape[:2]
    E = SSM_HEADS // SSM_GROUPS
    L = SSD_CHUNK if T % SSD_CHUNK == 0 else T
    nc = T // L

    def chunks(t):
        return t.reshape((B, nc, L) + t.shape[2:]).swapaxes(0, 1)

    xs = (chunks(x.reshape(B, T, SSM_GROUPS, E, SSM_HEAD_DIM)),
          chunks(dt.reshape(B, T, SSM_GROUPS, E)), chunks(bm), chunks(cm))
    a_ge = a.reshape(SSM_GROUPS, E)
    causal = jnp.tril(jnp.ones((L, L), dtype=bool))[None, :, :, None, None]

    def step(h, inp):
        xc, dtc, bc, cc = inp
        cs = jnp.cumsum(dtc * a_ge, axis=1)
        seg = cs[:, :, None] - cs[:, None, :]
        decay = jnp.exp(jnp.where(causal, seg, -jnp.inf))
        xdt = xc * dtc[..., None]
        cb = jnp.einsum('blgn,bsgn->blsg', cc, bc)
        y_in = jnp.einsum('blsg,blsge,bsgep->blgep', cb, decay, xdt)
        y_st = jnp.einsum('blgn,bgepn->blgep', cc, h) * jnp.exp(cs)[..., None]
        last = cs[:, -1]
        w_end = jnp.exp(last[:, None] - cs)
        h_new = (h * jnp.exp(last)[..., None, None]
                 + jnp.einsum('blgep,blgn->bgepn', xdt * w_end[..., None], bc))
        return h_new, y_in + y_st

    h_T, ys = lax.scan(step, h0.reshape(B, SSM_GROUPS, E, SSM_HEAD_DIM, SSM_STATE), xs)
    y = ys.swapaxes(0, 1).reshape(B, T, SSM_HEADS, SSM_HEAD_DIM)
    return y, h_T.reshape(B, SSM_HEADS, SSM_HEAD_DIM, SSM_STATE)


def retention_scan(q, k, v, s0):
    B, T = q.shape[:2]
    L = RET_CHUNK if T % RET_CHUNK == 0 else T
    nc = T // L
    log_g = jnp.log(1.0 - 2.0 ** (-5.0 - jnp.arange(RET_HEADS, dtype=jnp.float32)))
    idx = jnp.arange(L, dtype=jnp.float32)
    rel = idx[:, None] - idx[None, :]
    dmat = jnp.where(rel >= 0, jnp.exp(jnp.maximum(rel, 0.0)[None] * log_g[:, None, None]), 0.0)
    q_dec = jnp.exp((idx[:, None] + 1.0) * log_g[None, :])
    k_dec = jnp.exp((L - 1.0 - idx[:, None]) * log_g[None, :])
    c_dec = jnp.exp(L * log_g)

    def chunks(t):
        return t.reshape((B, nc, L) + t.shape[2:]).swapaxes(0, 1)

    def step(s, inp):
        qc, kc, vc = inp
        att = jnp.einsum('bihd,bjhd->bhij', qc, kc) * dmat
        o = (jnp.einsum('bhij,bjhe->bihe', att, vc)
             + jnp.einsum('bihd,bhde->bihe', qc, s) * q_dec[None, :, :, None])
        s_new = s * c_dec[:, None, None] + jnp.einsum('bjhd,bjhe->bhde', kc * k_dec[None, :, :, None], vc)
        return s_new, o

    s_T, os_ = lax.scan(step, s0, (chunks(q), chunks(k), chunks(v)))
    return os_.swapaxes(0, 1).reshape(B, T, RET_HEADS, RET_DV), s_T


def diff_attention(q, k, v, q_start, rel_bias, lam, sub_g, lam_init):
    B, Tq = q.shape[:2]
    Tk = k.shape[1]
    R = DIFF_HEADS // DIFF_KV_HEADS
    qb = Q_BLOCK if Tq % Q_BLOCK == 0 else Tq
    nb = Tq // qb
    k_pos = jnp.arange(Tk)
    bias_tab = rel_bias.astype(jnp.float32)

    def block(i):
        qi = lax.dynamic_slice_in_dim(q, i * qb, qb, axis=1).reshape(B, qb, DIFF_KV_HEADS, R, 2, DIFF_HD)
        dist = (q_start + i * qb + jnp.arange(qb))[:, None] - k_pos[None, :]
        bias = bias_tab[t5_bucket(dist)].transpose(2, 0, 1).reshape(DIFF_KV_HEADS, R, qb, Tk)
        s = jnp.einsum('bqgrmd,bkgmd->mbgrqk', qi, k,
                       preferred_element_type=jnp.float32) * DIFF_HD ** -0.5 + bias
        s = jnp.where(dist >= 0, s, -jnp.inf)
        pm = jax.nn.softmax(s, axis=-1)
        a = pm[0] - lam * pm[1]
        o = jnp.einsum('bgrqk,bkgv->bqgrv', a, v).reshape(B, qb, DIFF_HEADS, DIFF_VD)
        o = o * lax.rsqrt(jnp.mean(o * o, axis=-1, keepdims=True) + NORM_EPS)
        o = o * sub_g.astype(jnp.float32) * (1.0 - lam_init)
        return o.reshape(B, qb, DIFF_HEADS * DIFF_VD)

    o = lax.map(block, jnp.arange(nb))
    return o.swapaxes(0, 1).reshape(B, Tq, DIFF_HEADS * DIFF_VD)


def hybrid_layer(x, pos0, conv_buf, h_ssm, s_ret, past_k, past_v, rel_bias, lam_init, p):
    f32 = jnp.float32
    B, T, _ = x.shape
    xn = rmsnorm(x, p['norm_mix'])
    proj = jnp.einsum('btd,de->bte', xn, p['w_in'])
    (z, xbc, dt_raw, rq, rk, rv, rg, dq, dk, dv, gates) = jnp.split(proj, IN_OFFSETS, axis=-1)
    pos = pos0 + jnp.arange(T)

    xpad = jnp.concatenate([conv_buf.astype(xbc.dtype), xbc], axis=1)
    conv_new = xpad[:, T:]
    xbc_c = lax.conv_general_dilated(xpad, p['conv_w'][:, None, :].astype(xbc.dtype), (1,), 'VALID',
                                     dimension_numbers=('NWC', 'WIO', 'NWC'),
                                     feature_group_count=CONV_DIM)
    xbc_c = jax.nn.silu(xbc_c.astype(f32) + p['conv_b'].astype(f32))
    xs, bm, cm = jnp.split(xbc_c, [SSM_D, SSM_D + SSM_GROUPS * SSM_STATE], axis=-1)
    xs = xs.reshape(B, T, SSM_HEADS, SSM_HEAD_DIM)
    dt = jax.nn.softplus(dt_raw.astype(f32) + p['dt_bias'].astype(f32))
    a = -jnp.exp(p['a_log'].astype(f32))
    y_ssm, h_new = ssd_scan(xs, dt, a, bm.reshape(B, T, SSM_GROUPS, SSM_STATE),
                            cm.reshape(B, T, SSM_GROUPS, SSM_STATE), h_ssm.astype(f32))
    y_ssm = (y_ssm + p['d_skip'].astype(f32)[:, None] * xs).reshape(B, T, SSM_D) * jax.nn.silu(z.astype(f32))
    y_ssm = group_rmsnorm(y_ssm, SSM_GROUPS, p['ssm_norm'])
    br_ssm = jnp.einsum('bte,ed->btd', y_ssm.astype(x.dtype), p['w_ssm_out'])

    rq = rotary(rq.reshape(B, T, RET_HEADS, RET_DK), pos)
    rk = rotary(rk.reshape(B, T, RET_HEADS, RET_DK), pos) * RET_DK ** -0.5
    y_ret, s_new = retention_scan(rq, rk, rv.reshape(B, T, RET_HEADS, RET_DV).astype(f32), s_ret.astype(f32))
    y_ret = head_groupnorm(y_ret, p['ret_norm']) * jax.nn.silu(rg.astype(f32))
    br_ret = jnp.einsum('bte,ed->btd', y_ret.astype(x.dtype), p['w_ret_out'])

    dq = rmsnorm(dq.reshape(B, T, DIFF_HEADS, 2, DIFF_HD), p['qk_norm_q'])
    dk = rmsnorm(dk.reshape(B, T, DIFF_KV_HEADS, 2, DIFF_HD), p['qk_norm_k'])
    dv = dv.reshape(B, T, DIFF_KV_HEADS, DIFF_VD)
    if past_k is None:
        k_all, v_all = dk, dv
    else:
        k_all = jnp.concatenate([past_k.astype(dk.dtype), dk], axis=1)
        v_all = jnp.concatenate([past_v.astype(dv.dtype), dv], axis=1)
    lam = (jnp.exp(jnp.sum(p['lambda_q1'].astype(f32) * p['lambda_k1'].astype(f32)))
           - jnp.exp(jnp.sum(p['lambda_q2'].astype(f32) * p['lambda_k2'].astype(f32))) + lam_init)
    y_diff = diff_attention(dq, k_all, v_all, pos0, rel_bias, lam, p['diff_norm'], lam_init)
    br_diff = jnp.einsum('bte,ed->btd', y_diff.astype(x.dtype), p['w_diff_out'])

    g = jax.nn.sigmoid(gates.astype(f32).reshape(B, T, N_BRANCHES, D_MODEL) + p['b_gate'].astype(f32))
    merged = g[:, :, 0] * br_ssm.astype(f32) + g[:, :, 1] * br_ret.astype(f32) + g[:, :, 2] * br_diff.astype(f32)
    h = x + jnp.einsum('btd,de->bte', merged.astype(x.dtype), p['w_o'])

    hn = rmsnorm(h, p['norm_ffn'])
    gt, up = jnp.split(jnp.einsum('btd,df->btf', hn, p['w_gate_up']), 2, axis=-1)
    y = h + jnp.einsum('btf,fd->btd', jax.nn.silu(gt) * up, p['w_down'])
    return y, dk, dv, h_new, conv_new, s_new


def setup_inputs(seed: int = 0) -> dict:
    key = jax.random.key(seed)
    ks = jax.random.split(key, 40)
    f32 = jnp.float32

    def nrm(k, shape, scale):
        return jax.random.normal(k, shape, f32) * scale

    def gain(k, shape):
        return 1.0 + 0.02 * jax.random.normal(k, shape, f32)

    n_pages = PAST_LEN // PAGE_SIZE
    n_used = DEC_BATCH * n_pages
    n_pool = n_used + max(1, n_used // 4)
    page_table = jax.random.permutation(ks[4], n_pool)[:n_used].reshape(DEC_BATCH, n_pages).astype(jnp.int32)
    dt0 = jnp.exp(jax.random.uniform(ks[14], (DEPTH, SSM_HEADS), f32, math.log(1e-3), math.log(1e-1)))
    return {
        'x_prompt': nrm(ks[0], (BATCH, SEQ, D_MODEL), 1.0),
        'x_sample': nrm(ks[1], (DEC_BATCH, DEC_SEQ, D_MODEL), 1.0),
        'cache_k': nrm(ks[2], (DEPTH, n_pool, PAGE_SIZE, DIFF_KV_HEADS, 2, DIFF_HD), 1.0),
        'cache_v': nrm(ks[3], (DEPTH, n_pool, PAGE_SIZE, DIFF_KV_HEADS, DIFF_VD), 1.0),
        'page_table': page_table,
        'state_ssm': nrm(ks[5], (DEPTH, DEC_BATCH, SSM_HEADS, SSM_HEAD_DIM, SSM_STATE), 0.1),
        'state_conv': nrm(ks[6], (DEPTH, DEC_BATCH, CONV_W - 1, CONV_DIM), 1.0),
        'state_ret': nrm(ks[7], (DEPTH, DEC_BATCH, RET_HEADS, RET_DK, RET_DV), 0.3),
        'rel_bias': nrm(ks[8], (N_BUCKETS, DIFF_HEADS), 0.3),
        'norm_mix': gain(ks[9], (DEPTH, D_MODEL)),
        'w_in': nrm(ks[10], (DEPTH, D_MODEL, D_IN), D_MODEL ** -0.5),
        'b_gate': nrm(ks[11], (DEPTH, N_BRANCHES, D_MODEL), 0.01),
        'conv_w': nrm(ks[12], (DEPTH, CONV_W, CONV_DIM), CONV_W ** -0.5),
        'conv_b': nrm(ks[13], (DEPTH, CONV_DIM), 0.01),
        'dt_bias': dt0 + jnp.log(-jnp.expm1(-dt0)),
        'a_log': jnp.log(jax.random.uniform(ks[15], (DEPTH, SSM_HEADS), f32, 1.0, 16.0)),
        'd_skip': gain(ks[16], (DEPTH, SSM_HEADS)),
        'ssm_norm': gain(ks[17], (DEPTH, SSM_D)),
        'w_ssm_out': nrm(ks[18], (DEPTH, SSM_D, D_MODEL), SSM_D ** -0.5),
        'ret_norm': gain(ks[19], (DEPTH, RET_HEADS * RET_DV)),
        'w_ret_out': nrm(ks[20], (DEPTH, RET_HEADS * RET_DV, D_MODEL), (RET_HEADS * RET_DV) ** -0.5),
        'qk_norm_q': gain(ks[21], (DEPTH, DIFF_HD)),
        'qk_norm_k': gain(ks[22], (DEPTH, DIFF_HD)),
        'lambda_q1': nrm(ks[23], (DEPTH, DIFF_HD), 0.1),
        'lambda_k1': nrm(ks[24], (DEPTH, DIFF_HD), 0.1),
        'lambda_q2': nrm(ks[25], (DEPTH, DIFF_HD), 0.1),
        'lambda_k2': nrm(ks[26], (DEPTH, DIFF_HD), 0.1),
        'diff_norm': gain(ks[27], (DEPTH, DIFF_VD)),
        'w_diff_out': nrm(ks[28], (DEPTH, DIFF_HEADS * DIFF_VD, D_MODEL), (DIFF_HEADS * DIFF_VD) ** -0.5),
        'w_o': nrm(ks[29], (DEPTH, D_MODEL, D_MODEL), D_MODEL ** -0.5),
        'norm_ffn': gain(ks[30], (DEPTH, D_MODEL)),
        'w_gate_up': nrm(ks[31], (DEPTH, D_MODEL, 2 * D_FF), D_MODEL ** -0.5),
        'w_down': nrm(ks[32], (DEPTH, D_FF, D_MODEL), D_FF ** -0.5),
    }


def reference(x_prompt, x_sample, cache_k, cache_v, page_table, state_ssm, state_conv, state_ret,
              rel_bias, norm_mix, w_in, b_gate, conv_w, conv_b, dt_bias, a_log, d_skip, ssm_norm,
              w_ssm_out, ret_norm, w_ret_out, qk_norm_q, qk_norm_k, lambda_q1, lambda_k1, lambda_q2,
              lambda_k2, diff_norm, w_diff_out, w_o, norm_ffn, w_gate_up, w_down):
    bp = x_prompt.shape[0]
    bs = page_table.shape[0]
    past_len = page_table.shape[1] * PAGE_SIZE
    yp, ys = x_prompt, x_sample
    kp, vp, hp, cp, rp = [], [], [], [], []
    ksm, vsm, hsm, csm, rsm = [], [], [], [], []
    for l in range(DEPTH):
        lam_init = 0.8 - 0.6 * math.exp(-0.3 * l)
        p = {'norm_mix': norm_mix[l], 'w_in': w_in[l], 'b_gate': b_gate[l], 'conv_w': conv_w[l],
             'conv_b': conv_b[l], 'dt_bias': dt_bias[l], 'a_log': a_log[l], 'd_skip': d_skip[l],
             'ssm_norm': ssm_norm[l], 'w_ssm_out': w_ssm_out[l], 'ret_norm': ret_norm[l],
             'w_ret_out': w_ret_out[l], 'qk_norm_q': qk_norm_q[l], 'qk_norm_k': qk_norm_k[l],
             'lambda_q1': lambda_q1[l], 'lambda_k1': lambda_k1[l], 'lambda_q2': lambda_q2[l],
             'lambda_k2': lambda_k2[l], 'diff_norm': diff_norm[l], 'w_diff_out': w_diff_out[l],
             'w_o': w_o[l], 'norm_ffn': norm_ffn[l], 'w_gate_up': w_gate_up[l], 'w_down': w_down[l]}
        yp, k1, v1, h1, c1, r1 = hybrid_layer(
            yp, 0,
            jnp.zeros((bp, CONV_W - 1, CONV_DIM), yp.dtype),
            jnp.zeros((bp, SSM_HEADS, SSM_HEAD_DIM, SSM_STATE), jnp.float32),
            jnp.zeros((bp, RET_HEADS, RET_DK, RET_DV), jnp.float32),
            None, None, rel_bias, lam_init, p)
        past_k = cache_k[l, page_table].reshape(bs, past_len, DIFF_KV_HEADS, 2, DIFF_HD)
        past_v = cache_v[l, page_table].reshape(bs, past_len, DIFF_KV_HEADS, DIFF_VD)
        ys, k2, v2, h2, c2, r2 = hybrid_layer(
            ys, past_len, state_conv[l], state_ssm[l], state_ret[l],
            past_k, past_v, rel_bias, lam_init, p)
        kp.append(k1); vp.append(v1); hp.append(h1); cp.append(c1); rp.append(r1)
        ksm.append(k2); vsm.append(v2); hsm.append(h2); csm.append(c2); rsm.append(r2)
    return (yp, ys,
            jnp.stack(kp), jnp.stack(vp), jnp.stack(hp), jnp.stack(cp), jnp.stack(rp),
            jnp.stack(ksm), jnp.stack(vsm), jnp.stack(hsm), jnp.stack(csm), jnp.stack(rsm))
```

```python
import functools
import math

import numpy as np
import jax
import jax.numpy as jnp
from jax import lax
from jax.experimental import pallas as pl
from jax.experimental.pallas import tpu as pltpu

F32 = jnp.float32
BF16 = jnp.bfloat16

D_MODEL = 1024
SSM_HEADS = 16
SSM_HEAD_DIM = 64
SSM_D = SSM_HEADS * SSM_HEAD_DIM
SSM_STATE = 128
SSM_GROUPS = 4
CONV_W = 4
CONV_DIM = SSM_D + 2 * SSM_GROUPS * SSM_STATE
RET_HEADS = 4
RET_DK = 128
RET_DV = 256
ROPE_BASE = 10000.0
DIFF_HEADS = 8
DIFF_KV_HEADS = 4
DIFF_HD = 64
DIFF_VD = 2 * DIFF_HD
N_BUCKETS = 32
MAX_DISTANCE = 128
N_BRANCHES = 3
D_FF = 2816
NORM_EPS = 1e-6
PAGE_SIZE = 128

CHUNK = 128
LANES = 128
SUBLANES = 8
NEG_BIG = -1e30
VMEM_LIMIT = 52 * 1024 * 1024

IN_SPLITS = (SSM_D, CONV_DIM, SSM_HEADS,
             RET_HEADS * RET_DK, RET_HEADS * RET_DK, RET_HEADS * RET_DV, RET_HEADS * RET_DV,
             DIFF_HEADS * 2 * DIFF_HD, DIFF_KV_HEADS * 2 * DIFF_HD, DIFF_KV_HEADS * DIFF_VD,
             N_BRANCHES * D_MODEL)
IN_OFFSETS = tuple(int(v) for v in np.cumsum(IN_SPLITS)[:-1])

C_XBC, C_Z, C_RQ, C_RK, C_RV, C_RG, C_DQ, C_GATE, C_MAIN = 0, 2048, 3072, 3584, 4096, 5120, 6144, 7168, 10240
TN_PROJ = 1024
KV_COLS = DIFF_KV_HEADS * 2 * DIFF_HD


def _cparams(*sem):
    return pltpu.CompilerParams(dimension_semantics=sem, vmem_limit_bytes=VMEM_LIMIT)


def _nt_dot(a, b):
    return lax.dot_general(a, b, (((1,), (1,)), ((), ())), preferred_element_type=F32)


def _tn_dot(a, b):
    return lax.dot_general(a, b, (((0,), (0,)), ((), ())), preferred_element_type=F32)


def _dot(a, b):
    return jnp.dot(a, b, preferred_element_type=F32)


def _split3(x):
    hi = x.astype(BF16)
    r1 = x - hi.astype(F32)
    mid = r1.astype(BF16)
    lo = (r1 - mid.astype(F32)).astype(BF16)
    return hi, mid, lo


def _silu(x):
    return x * jax.nn.sigmoid(x)


def _inproj_kernel(x_ref, g_ref, w_ref, wdt_ref, main_ref, kf_ref, vf_ref, dt_ref, xn_ref, *, n_main):
    j = pl.program_id(1)

    @pl.when(j == 0)
    def _():
        x = x_ref[...]
        ms = jnp.mean(x * x, axis=-1, keepdims=True)
        xn = (x * lax.rsqrt(ms + NORM_EPS) * g_ref[...]).astype(BF16)
        xn_ref[...] = xn
        dt_ref[...] = _dot(xn, wdt_ref[...])

    acc = _dot(xn_ref[...], w_ref[...])

    @pl.when(j < n_main)
    def _():
        main_ref[...] = acc.astype(BF16)

    @pl.when(j == n_main)
    def _():
        kf_ref[...] = acc[:, :KV_COLS]
        vf_ref[...] = acc[:, KV_COLS:]


def _inproj(x2d, gain, w_all, w_dt, tm):
    n = x2d.shape[0]
    n_main = C_MAIN // TN_PROJ
    grid = (n // tm, n_main + 1)
    return pl.pallas_call(
        functools.partial(_inproj_kernel, n_main=n_main),
        grid=grid,
        in_specs=[
            pl.BlockSpec((tm, D_MODEL), lambda i, j: (i, 0)),
            pl.BlockSpec((1, D_MODEL), lambda i, j: (0, 0)),
            pl.BlockSpec((D_MODEL, TN_PROJ), lambda i, j: (0, j)),
            pl.BlockSpec((D_MODEL, LANES), lambda i, j: (0, 0)),
        ],
        out_specs=[
            pl.BlockSpec((tm, TN_PROJ), lambda i, j: (i, jnp.minimum(j, n_main - 1))),
            pl.BlockSpec((tm, KV_COLS), lambda i, j: (i, 0)),
            pl.BlockSpec((tm, KV_COLS), lambda i, j: (i, 0)),
            pl.BlockSpec((tm, LANES), lambda i, j: (i, 0)),
        ],
        out_shape=[
            jax.ShapeDtypeStruct((n, C_MAIN), BF16),
            jax.ShapeDtypeStruct((n, KV_COLS), F32),
            jax.ShapeDtypeStruct((n, KV_COLS), F32),
            jax.ShapeDtypeStruct((n, LANES), F32),
        ],
        scratch_shapes=[pltpu.VMEM((tm, D_MODEL), BF16)],
        compiler_params=_cparams("arbitrary", "arbitrary"),
        name="inproj",
    )(x2d, gain, w_all, w_dt)


def _ssd_kernel(xbc_ref, z_ref, dt_ref, tail0_ref, h0_ref, cw_ref, cb_ref, dtb_ref, alog_ref, dskip_ref,
                gn_ref, ex_ref, y_ref, hout_ref, convout_ref, xext_ref, xc_ref, ht_ref, *, nvalid, nchunks):
    c = pl.program_id(1)
    L = CHUNK
    GW = SSM_D // SSM_GROUPS

    @pl.when(c == 0)
    def _():
        xext_ref[0:SUBLANES, :] = tail0_ref[0]
        for k in range(SSM_D // LANES):
            ht_ref[:, k * LANES:(k + 1) * LANES] = h0_ref[0, k * LANES:(k + 1) * LANES, :].T

    xext_ref[SUBLANES:SUBLANES + L, :] = xbc_ref[0].astype(F32)
    SL = 512
    for s in range(CONV_DIM // SL):
        cols = slice(s * SL, (s + 1) * SL)
        conv = cw_ref[0:1, cols] * xext_ref[SUBLANES - 3:SUBLANES - 3 + L, cols]
        for w in range(1, CONV_W):
            conv = conv + cw_ref[w:w + 1, cols] * xext_ref[SUBLANES - 3 + w:SUBLANES - 3 + w + L, cols]
        xc_ref[:, cols] = _silu(conv + cb_ref[:, cols])

    @pl.when(c == nchunks - 1)
    def _():
        convout_ref[0] = xext_ref[nvalid:nvalid + SUBLANES, :]

    xext_ref[0:SUBLANES, :] = xext_ref[L:L + SUBLANES, :]

    row = lax.broadcasted_iota(jnp.int32, (L, L), 0)
    col = lax.broadcasted_iota(jnp.int32, (L, L), 1)
    causal = row >= col
    left = col < SSM_HEAD_DIM

    x = dt_ref[0] + dtb_ref[...]
    dt = jnp.maximum(x, 0.0) + jnp.log1p(jnp.exp(-jnp.abs(x)))
    if nvalid < L:
        dt = jnp.where(row < nvalid, dt, 0.0)
    a = -jnp.exp(alog_ref[...])
    da = dt * a
    tri = jnp.where(causal, 1.0, 0.0).astype(BF16)
    cs = sum(_dot(tri, p) for p in _split3(da))
    cs_t = cs.T
    cs_parts = _split3(cs)
    dt_parts = _split3(dt)

    for g in range(SSM_GROUPS):
        gc = slice(g * GW, (g + 1) * GW)
        ex = ex_ref[:, gc]
        csx = sum(_dot(p, ex) for p in cs_parts)
        dtx = sum(_dot(p, ex) for p in dt_parts)
        lastx = csx[L - 1:L, :]
        xs = xc_ref[:, gc]
        xdt = xs * dtx
        xdt_b = xdt.astype(BF16)
        xdtw_b = (xdt * jnp.exp(lastx - csx)).astype(BF16)
        bg = xc_ref[:, SSM_D + g * SSM_STATE:SSM_D + (g + 1) * SSM_STATE].astype(BF16)
        cg = xc_ref[:, SSM_D + SSM_GROUPS * SSM_STATE + g * SSM_STATE:
                    SSM_D + SSM_GROUPS * SSM_STATE + (g + 1) * SSM_STATE].astype(BF16)
        cb = _nt_dot(cg, bg)
        htg = ht_ref[:, gc]
        y_st = _dot(cg, htg.astype(BF16)) * jnp.exp(csx)
        pairs = []
        for pr in range(2):
            xp = xdt_b[:, pr * LANES:(pr + 1) * LANES]
            acc = None
            for e2 in range(2):
                h = g * (SSM_HEADS // SSM_GROUPS) + pr * 2 + e2
                seg = cs[:, h:h + 1] - cs_t[h:h + 1, :]
                decay = jnp.exp(jnp.where(causal, seg, NEG_BIG))
                m = (cb * decay).astype(BF16)
                xm = jnp.where(left if e2 == 0 else jnp.logical_not(left), xp, jnp.zeros_like(xp))
                t = _dot(m, xm)
                acc = t if acc is None else acc + t
            pairs.append(acc)
        y_in = jnp.concatenate(pairs, axis=1)
        ht_ref[:, gc] = htg * jnp.exp(lastx) + _tn_dot(bg, xdtw_b)
        y = (y_in + y_st + dskip_ref[:, gc] * xs)
        zz = z_ref[0, :, gc].astype(F32)
        y = y * _silu(zz)
        ms = jnp.mean(y * y, axis=-1, keepdims=True)
        y_ref[0, :, gc] = (y * lax.rsqrt(ms + NORM_EPS) * gn_ref[:, gc]).astype(BF16)

    @pl.when(c == nchunks - 1)
    def _():
        for k in range(SSM_D // LANES):
            hout_ref[0, k * LANES:(k + 1) * LANES, :] = ht_ref[:, k * LANES:(k + 1) * LANES].T


def _ssd(proj, dt_raw, tail0, h0, p, nvalid):
    b, t, _ = proj.shape
    nchunks = t // CHUNK
    L = CHUNK
    const = lambda shape: pl.BlockSpec(shape, lambda i, c: (0,) * len(shape))
    return pl.pallas_call(
        functools.partial(_ssd_kernel, nvalid=nvalid, nchunks=nchunks),
        grid=(b, nchunks),
        in_specs=[
            pl.BlockSpec((1, L, CONV_DIM), lambda i, c: (i, c, C_XBC // CONV_DIM)),
            pl.BlockSpec((1, L, SSM_D), lambda i, c: (i, c, C_Z // SSM_D)),
            pl.BlockSpec((1, L, LANES), lambda i, c: (i, c, 0)),
            pl.BlockSpec((1, SUBLANES, CONV_DIM), lambda i, c: (i, 0, 0)),
            pl.BlockSpec((1, SSM_D, SSM_STATE), lambda i, c: (i, 0, 0)),
            const((CONV_W, CONV_DIM)), const((1, CONV_DIM)), const((1, LANES)), const((1, LANES)),
            const((1, SSM_D)), const((1, SSM_D)), const((LANES, SSM_D)),
        ],
        out_specs=[
            pl.BlockSpec((1, L, SSM_D), lambda i, c: (i, c, 0)),
            pl.BlockSpec((1, SSM_D, SSM_STATE), lambda i, c: (i, 0, 0)),
            pl.BlockSpec((1, SUBLANES, CONV_DIM), lambda i, c: (i, 0, 0)),
        ],
        out_shape=[
            jax.ShapeDtypeStruct((b, t, SSM_D), BF16),
            jax.ShapeDtypeStruct((b, SSM_D, SSM_STATE), F32),
            jax.ShapeDtypeStruct((b, SUBLANES, CONV_DIM), F32),
        ],
        scratch_shapes=[
            pltpu.VMEM((SUBLANES + L + SUBLANES, CONV_DIM), F32),
            pltpu.VMEM((L, CONV_DIM), F32),
            pltpu.VMEM((SSM_STATE, SSM_D), F32),
        ],
        compiler_params=_cparams("arbitrary", "arbitrary"),
        name="ssd",
    )(proj, proj, dt_raw, tail0, h0, p["conv_w"], p["conv_b"], p["dt_bias"], p["a_log"], p["d_skip"],
      p["ssm_norm"], p["head_expand"])


def _ret_kernel(q_ref, k_ref, v_ref, rg_ref, cos_ref, sin_ref, s0_ref, gn_ref, y_ref, sout_ref, *, ltrue):
    c = pl.program_id(1)
    L = CHUNK

    @pl.when(c == 0)
    def _():
        sout_ref[...] = s0_ref[...]

    row = lax.broadcasted_iota(jnp.int32, (L, L), 0)
    col = lax.broadcasted_iota(jnp.int32, (L, L), 1)
    rel = (row - col).astype(F32)
    idx = row[:, 0:1].astype(F32)
    cosf = cos_ref[...]
    sins = sin_ref[...]
    for h in range(RET_HEADS):
        lg = math.log(1.0 - 2.0 ** (-5.0 - h))
        dmat = jnp.where(rel >= 0, jnp.exp(jnp.maximum(rel, 0.0) * lg), 0.0)
        q_dec = jnp.exp((idx + 1.0) * lg)
        k_dec = jnp.exp((ltrue - 1.0 - idx) * lg)
        c_dec = math.exp(ltrue * lg)
        kc = slice(h * RET_DK, (h + 1) * RET_DK)
        vc = slice(h * RET_DV, (h + 1) * RET_DV)
        qh = q_ref[0, :, kc].astype(F32)
        kh = k_ref[0, :, kc].astype(F32)
        qr = qh * cosf + pltpu.roll(qh, RET_DK // 2, 1) * sins
        kr = (kh * cosf + pltpu.roll(kh, RET_DK // 2, 1) * sins) * RET_DK ** -0.5
        qr_b = qr.astype(BF16)
        vh = v_ref[0, :, vc]
        att = _nt_dot(qr_b, kr.astype(BF16)) * dmat
        s_old = sout_ref[0, h]
        o = _dot(att.astype(BF16), vh) + _dot(qr_b, s_old.astype(BF16)) * q_dec
        sout_ref[0, h] = s_old * c_dec + _tn_dot((kr * k_dec).astype(BF16), vh)
        oc = o - jnp.mean(o, axis=-1, keepdims=True)
        on = oc * lax.rsqrt(jnp.mean(oc * oc, axis=-1, keepdims=True) + NORM_EPS)
        y_ref[0, :, vc] = (on * gn_ref[:, vc] * _silu(rg_ref[0, :, vc].astype(F32))).astype(BF16)


def _retention(proj, cos_t, sin_t, s0, gn, ltrue):
    b, t, _ = proj.shape
    nchunks = t // CHUNK
    L = CHUNK
    qk_w = RET_HEADS * RET_DK
    v_w = RET_HEADS * RET_DV
    return pl.pallas_call(
        functools.partial(_ret_kernel, ltrue=float(ltrue)),
        grid=(b, nchunks),
        in_specs=[
            pl.BlockSpec((1, L, qk_w), lambda i, c: (i, c, C_RQ // qk_w)),
            pl.BlockSpec((1, L, qk_w), lambda i, c: (i, c, C_RK // qk_w)),
            pl.BlockSpec((1, L, v_w), lambda i, c: (i, c, C_RV // v_w)),
            pl.BlockSpec((1, L, v_w), lambda i, c: (i, c, C_RG // v_w)),
            pl.BlockSpec((L, RET_DK), lambda i, c: (c, 0)),
            pl.BlockSpec((L, RET_DK), lambda i, c: (c, 0)),
            pl.BlockSpec((1, RET_HEADS, RET_DK, RET_DV), lambda i, c: (i, 0, 0, 0)),
            pl.BlockSpec((1, v_w), lambda i, c: (0, 0)),
        ],
        out_specs=[
            pl.BlockSpec((1, L, v_w), lambda i, c: (i, c, 0)),
            pl.BlockSpec((1, RET_HEADS, RET_DK, RET_DV), lambda i, c: (i, 0, 0, 0)),
        ],
        out_shape=[
            jax.ShapeDtypeStruct((b, t, v_w), BF16),
            jax.ShapeDtypeStruct((b, RET_HEADS, RET_DK, RET_DV), F32),
        ],
        compiler_params=_cparams("arbitrary", "arbitrary"),
        name="retention",
    )(proj, proj, proj, proj, cos_t, sin_t, s0, gn)


def _bias_kernel(tab_ref, idx_ref, out_ref):
    h = pl.program_id(0)
    idx = idx_ref[...]
    acc = jnp.zeros(idx.shape, F32)
    for b in range(N_BUCKETS):
        acc = acc + jnp.where(idx == b, tab_ref[b * DIFF_HEADS + h], 0.0)
    out_ref[0] = acc


def _bias_tiles(rel_bias, idx):
    r, c = idx.shape
    return pl.pallas_call(
        _bias_kernel,
        grid=(DIFF_HEADS,),
        in_specs=[pl.BlockSpec(memory_space=pltpu.SMEM), pl.BlockSpec((r, c), lambda h: (0, 0))],
        out_specs=pl.BlockSpec((1, r, c), lambda h: (h, 0, 0)),
        out_shape=jax.ShapeDtypeStruct((DIFF_HEADS, r, c), F32),
        compiler_params=_cparams("arbitrary"),
        name="t5_bias",
    )(rel_bias.reshape(-1), idx)


def _half_rmsnorm(x, gain, bd):
    x2 = x * x
    hi = x2.astype(BF16)
    lo = (x2 - hi.astype(F32)).astype(BF16)
    ss = _dot(hi, bd) + _dot(lo, bd)
    return x * lax.rsqrt(ss * (1.0 / DIFF_HD) + NORM_EPS) * gain


def _lambda(lamv_ref, lam_init):
    s1 = jnp.sum(lamv_ref[0:1, :] * lamv_ref[1:2, :], axis=-1, keepdims=True)
    s2 = jnp.sum(lamv_ref[2:3, :] * lamv_ref[3:4, :], axis=-1, keepdims=True)
    return jnp.exp(s1) - jnp.exp(s2) + lam_init


def _softmax_block(sb, m_ref, l_ref, rows):
    m_old = m_ref[rows, :]
    m_new = jnp.maximum(m_old, jnp.max(sb, axis=-1, keepdims=True))
    alpha = jnp.exp(m_old - m_new)
    p = jnp.exp(sb - m_new)
    l_ref[rows, :] = alpha * l_ref[rows, :] + jnp.sum(p, axis=-1, keepdims=True)
    m_ref[rows, :] = m_new
    return p, alpha


def _dattn_kernel(q_ref, kf_ref, vf_ref, bias_ref, gq_ref, gk_ref, lamv_ref, sg_ref, bd_ref,
                  y_ref, kout_ref, kn_ref, vb_ref, qs_ref, m_ref, l_ref, acc_ref, *, t, lam_init):
    qi = pl.program_id(2)
    TQ = CHUNK
    R = DIFF_HEADS // DIFF_KV_HEADS
    bd = bd_ref[...]

    @pl.when(qi == 0)
    def _():
        RB = 256

        def body(i, carry):
            r = pl.ds(pl.multiple_of(i * RB, RB), RB)
            kn = _half_rmsnorm(kf_ref[0, r, :], gk_ref[...], bd)
            kout_ref[0, r, :] = kn
            kn_ref[r, :] = kn.astype(BF16)
            vb_ref[r, :] = vf_ref[0, r, :].astype(BF16)
            return carry

        lax.fori_loop(0, t // RB, body, 0)

    row = lax.broadcasted_iota(jnp.int32, (TQ, TQ), 0)
    col = lax.broadcasted_iota(jnp.int32, (TQ, TQ), 1)
    left = col < DIFF_HD
    for r in range(R):
        qn = _half_rmsnorm(q_ref[0, :, r * LANES:(r + 1) * LANES].astype(F32), gq_ref[...], bd) * DIFF_HD ** -0.5
        qs_ref[(0 * R + r) * TQ:(0 * R + r + 1) * TQ, :] = jnp.where(left, qn, 0.0).astype(BF16)
        qs_ref[(1 * R + r) * TQ:(1 * R + r + 1) * TQ, :] = jnp.where(left, 0.0, qn).astype(BF16)
    m_ref[...] = jnp.full(m_ref.shape, NEG_BIG, F32)
    l_ref[...] = jnp.zeros(l_ref.shape, F32)
    acc_ref[...] = jnp.zeros(acc_ref.shape, F32)

    def step(kb, tile, diag):
        kr = pl.ds(pl.multiple_of(kb * TQ, TQ), TQ)
        s = _nt_dot(qs_ref[...], kn_ref[kr, :])
        ps = []
        for mr in range(2 * R):
            rows = slice(mr * TQ, (mr + 1) * TQ)
            sb = s[rows, :] + bias_ref[mr % R, tile]
            if diag:
                sb = jnp.where(row >= col, sb, NEG_BIG)
            p, alpha = _softmax_block(sb, m_ref, l_ref, rows)
            acc_ref[rows, :] = acc_ref[rows, :] * alpha
            ps.append(p.astype(BF16))
        acc_ref[...] += _dot(jnp.concatenate(ps, axis=0), vb_ref[kr, :])

    def far(kb, carry):
        step(kb, 2, False)
        return carry

    lax.fori_loop(0, jnp.maximum(qi - 1, 0), far, 0)

    @pl.when(qi >= 1)
    def _():
        step(qi - 1, 1, False)

    step(qi, 0, True)

    lam = _lambda(lamv_ref, lam_init)
    for r in range(R):
        r0 = slice((0 * R + r) * TQ, (0 * R + r + 1) * TQ)
        r1 = slice((1 * R + r) * TQ, (1 * R + r + 1) * TQ)
        o = acc_ref[r0, :] / l_ref[r0, :] - lam * (acc_ref[r1, :] / l_ref[r1, :])
        o = o * lax.rsqrt(jnp.mean(o * o, axis=-1, keepdims=True) + NORM_EPS)
        y_ref[0, :, r * LANES:(r + 1) * LANES] = (o * sg_ref[...] * (1.0 - lam_init)).astype(BF16)


def _diff_attention_prompt(proj, kf, vf, bias, p, lam_init):
    b, t, _ = proj.shape
    TQ = CHUNK
    R = DIFF_HEADS // DIFF_KV_HEADS
    qw = R * 2 * DIFF_HD
    const = lambda shape: pl.BlockSpec(shape, lambda i, g, q: (0,) * len(shape))
    return pl.pallas_call(
        functools.partial(_dattn_kernel, t=t, lam_init=lam_init),
        grid=(b, DIFF_KV_HEADS, t // TQ),
        in_specs=[
            pl.BlockSpec((1, TQ, qw), lambda i, g, q: (i, q, C_DQ // qw + g)),
            pl.BlockSpec((1, t, LANES), lambda i, g, q: (i, 0, g)),
            pl.BlockSpec((1, t, LANES), lambda i, g, q: (i, 0, g)),
            pl.BlockSpec((R, 3, TQ, TQ), lambda i, g, q: (g, 0, 0, 0)),
            const((1, LANES)), const((1, LANES)), const((4, LANES)), const((1, LANES)), const((LANES, LANES)),
        ],
        out_specs=[
            pl.BlockSpec((1, TQ, qw), lambda i, g, q: (i, q, g)),
            pl.BlockSpec((1, t, LANES), lambda i, g, q: (i, 0, g)),
        ],
        out_shape=[
            jax.ShapeDtypeStruct((b, t, DIFF_HEADS * DIFF_VD), BF16),
            jax.ShapeDtypeStruct((b, t, KV_COLS), F32),
        ],
        scratch_shapes=[
            pltpu.VMEM((t, LANES), BF16), pltpu.VMEM((t, LANES), BF16),
            pltpu.VMEM((2 * R * TQ, LANES), BF16),
            pltpu.VMEM((2 * R * TQ, 1), F32), pltpu.VMEM((2 * R * TQ, 1), F32),
            pltpu.VMEM((2 * R * TQ, LANES), F32),
        ],
        compiler_params=_cparams("arbitrary", "arbitrary", "arbitrary"),
        name="diff_attn_prompt",
    )(proj, kf, vf, bias, p["qk_norm_q"], p["qk_norm_k"], p["lamv"], p["diff_norm"], p["blockdiag"])


PAGES_PER_STEP = 8


def _sattn_kernel(pt_ref, *refs, layer, nvalid, lam_init, npp):
    del pt_ref, layer
    k_refs = refs[0:npp]
    v_refs = refs[npp:2 * npp]
    (q_ref, kf_ref, vf_ref, bias_ref, gq_ref, gk_ref, lamv_ref, sg_ref, bd_ref,
     y_ref, kout_ref, qs_ref, knew_ref, vnew_ref, m_ref, l_ref, acc_ref) = refs[2 * npp:]
    s_id = pl.program_id(1)
    nsteps = pl.num_programs(1)
    G = DIFF_KV_HEADS
    R = DIFF_HEADS // DIFF_KV_HEADS
    TP = SUBLANES
    MR = 2 * R * TP
    bd = bd_ref[...]
    lane = lax.broadcasted_iota(jnp.int32, (TP, LANES), 1)
    left = lane < DIFF_HD

    @pl.when(s_id == 0)
    def _():
        knew_ref[...] = jnp.zeros(knew_ref.shape, BF16)
        vnew_ref[...] = jnp.zeros(vnew_ref.shape, BF16)
        for g in range(G):
            gc = slice(g * LANES, (g + 1) * LANES)
            kn = _half_rmsnorm(kf_ref[0, :, gc], gk_ref[...], bd)
            kout_ref[0, :, gc] = kn
            knew_ref[g, 0:TP, :] = kn.astype(BF16)
            vnew_ref[g, 0:TP, :] = vf_ref[0, :, gc].astype(BF16)
            for r in range(R):
                hc = slice((g * R + r) * LANES, (g * R + r + 1) * LANES)
                qn = _half_rmsnorm(q_ref[0, :, hc].astype(F32), gq_ref[...], bd) * DIFF_HD ** -0.5
                qs_ref[g, (0 * R + r) * TP:(0 * R + r + 1) * TP, :] = jnp.where(left, qn, 0.0).astype(BF16)
                qs_ref[g, (1 * R + r) * TP:(1 * R + r + 1) * TP, :] = jnp.where(left, 0.0, qn).astype(BF16)
        m_ref[...] = jnp.full(m_ref.shape, NEG_BIG, F32)
        l_ref[...] = jnp.zeros(l_ref.shape, F32)
        acc_ref[...] = jnp.zeros(acc_ref.shape, F32)

    def bias_rows(g, seg):
        per_head = [bias_ref[g * R + r, :, seg * LANES:(seg + 1) * LANES] for r in range(R)]
        return jnp.concatenate(per_head + per_head, axis=0)

    def update(g, s, vs):
        m_old = m_ref[g]
        m_new = jnp.maximum(m_old, jnp.max(s, axis=-1, keepdims=True))
        alpha = jnp.exp(m_old - m_new)
        p = jnp.exp(s - m_new)
        l_ref[g] = alpha * l_ref[g] + jnp.sum(p, axis=-1, keepdims=True)
        m_ref[g] = m_new
        pb = p.astype(BF16)
        pv = _dot(pb[:, 0:LANES], vs[0])
        for i in range(1, len(vs)):
            pv = pv + _dot(pb[:, i * LANES:(i + 1) * LANES], vs[i])
        acc_ref[g] = acc_ref[g] * alpha + pv

    def pages(last):
        for g in range(G):
            gc = slice(g * LANES, (g + 1) * LANES)
            qs = qs_ref[g]
            far = bias_rows(g, 0)
            parts = []
            for i in range(npp):
                sc = _nt_dot(qs, k_refs[i][:, gc].astype(BF16))
                if last and i == npp - 1:
                    sc = sc + bias_rows(g, 1)
                else:
                    sc = sc + far
                parts.append(sc)
            update(g, jnp.concatenate(parts, axis=1), [v_refs[i][:, gc].astype(BF16) for i in range(npp)])

    @pl.when(s_id < nsteps - 1)
    def _():
        pages(False)

    @pl.when(s_id == nsteps - 1)
    def _():
        pages(True)
        rowt = lax.broadcasted_iota(jnp.int32, (MR, LANES), 0) % TP
        colj = lax.broadcasted_iota(jnp.int32, (MR, LANES), 1)
        ok = jnp.logical_and(colj <= rowt, colj < nvalid)
        lam = _lambda(lamv_ref, lam_init)
        for g in range(G):
            sc = _nt_dot(qs_ref[g], knew_ref[g]) + bias_rows(g, 2)
            update(g, jnp.where(ok, sc, NEG_BIG), [vnew_ref[g]])
            acc = acc_ref[g] / l_ref[g]
            for r in range(R):
                o = acc[(0 * R + r) * TP:(0 * R + r + 1) * TP, :] - lam * acc[(1 * R + r) * TP:(1 * R + r + 1) * TP, :]
                o = o * lax.rsqrt(jnp.mean(o * o, axis=-1, keepdims=True) + NORM_EPS)
                hc = slice((g * R + r) * LANES, (g * R + r + 1) * LANES)
                y_ref[0, :, hc] = (o * sg_ref[...] * (1.0 - lam_init)).astype(BF16)


def _diff_attention_sample(proj, kf, vf, cache_k, cache_v, page_table, bias, p, layer, nvalid, lam_init):
    b, tp, _ = proj.shape
    n_pages = page_table.shape[1]
    npp = PAGES_PER_STEP
    while n_pages % npp:
        npp //= 2
    nsteps = n_pages // npp
    G = DIFF_KV_HEADS
    R = DIFF_HEADS // DIFF_KV_HEADS
    MR = 2 * R * tp
    ck = cache_k.reshape(cache_k.shape[0], cache_k.shape[1], PAGE_SIZE, KV_COLS)
    cv = cache_v.reshape(cache_v.shape[0], cache_v.shape[1], PAGE_SIZE, KV_COLS)

    def page_spec(i):
        return pl.BlockSpec((None, None, PAGE_SIZE, KV_COLS),
                            lambda bi, s, pt: (layer, pt[bi * n_pages + s * npp + i], 0, 0))

    const = lambda shape: pl.BlockSpec(shape, lambda bi, s, pt: (0,) * len(shape))
    grid_spec = pltpu.PrefetchScalarGridSpec(
        num_scalar_prefetch=1,
        grid=(b, nsteps),
        in_specs=[page_spec(i) for i in range(npp)] + [page_spec(i) for i in range(npp)] + [
            pl.BlockSpec((1, tp, DIFF_HEADS * 2 * DIFF_HD), lambda bi, s, pt: (bi, 0, C_DQ // (DIFF_HEADS * 2 * DIFF_HD))),
            pl.BlockSpec((1, tp, KV_COLS), lambda bi, s, pt: (bi, 0, 0)),
            pl.BlockSpec((1, tp, KV_COLS), lambda bi, s, pt: (bi, 0, 0)),
            const((DIFF_HEADS, tp, 3 * LANES)),
            const((1, LANES)), const((1, LANES)), const((4, LANES)), const((1, LANES)), const((LANES, LANES)),
        ],
        out_specs=[
            pl.BlockSpec((1, tp, DIFF_HEADS * DIFF_VD), lambda bi, s, pt: (bi, 0, 0)),
            pl.BlockSpec((1, tp, KV_COLS), lambda bi, s, pt: (bi, 0, 0)),
        ],
        scratch_shapes=[
            pltpu.VMEM((G, MR, LANES), BF16),
            pltpu.VMEM((G, PAGE_SIZE, LANES), BF16), pltpu.VMEM((G, PAGE_SIZE, LANES), BF16),
            pltpu.VMEM((G, MR, 1), F32), pltpu.VMEM((G, MR, 1), F32), pltpu.VMEM((G, MR, LANES), F32),
        ],
    )
    return pl.pallas_call(
        functools.partial(_sattn_kernel, layer=layer, nvalid=nvalid, lam_init=lam_init, npp=npp),
        grid_spec=grid_spec,
        out_shape=[
            jax.ShapeDtypeStruct((b, tp, DIFF_HEADS * DIFF_VD), BF16),
            jax.ShapeDtypeStruct((b, tp, KV_COLS), F32),
        ],
        compiler_params=_cparams("arbitrary", "arbitrary"),
        name="diff_attn_sample",
    )(page_table.reshape(-1), *([ck] * npp), *([cv] * npp), proj, kf, vf, bias,
      p["qk_norm_q"], p["qk_norm_k"], p["lamv"], p["diff_norm"], p["blockdiag"])


def _merge_kernel(ys_ref, yr_ref, yd_ref, g0_ref, g1_ref, g2_ref, x_ref, ws_ref, wr_ref, wd_ref, wo_ref,
                  bg_ref, h_ref):
    merged = None
    for i, (y_ref, w_ref, g_ref) in enumerate(((ys_ref, ws_ref, g0_ref), (yr_ref, wr_ref, g1_ref),
                                               (yd_ref, wd_ref, g2_ref))):
        br = _dot(y_ref[...], w_ref[...])
        t = jax.nn.sigmoid(g_ref[...].astype(F32) + bg_ref[i:i + 1, :]) * br
        merged = t if merged is None else merged + t
    h_ref[...] = x_ref[...] + _dot(merged.astype(BF16), wo_ref[...])


def _merge(y_ssm, y_ret, y_diff, proj2d, x2d, p, tm):
    n = x2d.shape[0]
    tok = lambda cb: pl.BlockSpec((tm, D_MODEL), lambda i: (i, cb))
    wspec = pl.BlockSpec((D_MODEL, D_MODEL), lambda i: (0, 0))
    g0 = C_GATE // D_MODEL
    return pl.pallas_call(
        _merge_kernel,
        grid=(n // tm,),
        in_specs=[tok(0), tok(0), tok(0), tok(g0), tok(g0 + 1), tok(g0 + 2), tok(0),
                  wspec, wspec, wspec, wspec, pl.BlockSpec((N_BRANCHES, D_MODEL), lambda i: (0, 0))],
        out_specs=tok(0),
        out_shape=jax.ShapeDtypeStruct((n, D_MODEL), F32),
        compiler_params=_cparams("arbitrary"),
        name="merge",
    )(y_ssm, y_ret, y_diff, proj2d, proj2d, proj2d, x2d, p["w_ssm_out"], p["w_ret_out"], p["w_diff_out"],
      p["w_o"], p["b_gate"])


TF_FFN = D_FF // 2


def _ffn_kernel(h_ref, g_ref, wg_ref, wu_ref, wd_ref, y_ref, hn_ref, acc_ref):
    j = pl.program_id(1)

    @pl.when(j == 0)
    def _():
        h = h_ref[...]
        ms = jnp.mean(h * h, axis=-1, keepdims=True)
        hn_ref[...] = (h * lax.rsqrt(ms + NORM_EPS) * g_ref[...]).astype(BF16)
        acc_ref[...] = h

    hn = hn_ref[...]
    act = _silu(_dot(hn, wg_ref[...])) * _dot(hn, wu_ref[...])
    acc_ref[...] += _dot(act.astype(BF16), wd_ref[...])

    @pl.when(j == pl.num_programs(1) - 1)
    def _():
        y_ref[...] = acc_ref[...]


def _ffn(h2d, p, tm):
    n = h2d.shape[0]
    nj = D_FF // TF_FFN
    return pl.pallas_call(
        _ffn_kernel,
        grid=(n // tm, nj),
        in_specs=[
            pl.BlockSpec((tm, D_MODEL), lambda i, j: (i, 0)),
            pl.BlockSpec((1, D_MODEL), lambda i, j: (0, 0)),
            pl.BlockSpec((D_MODEL, TF_FFN), lambda i, j: (0, j)),
            pl.BlockSpec((D_MODEL, TF_FFN), lambda i, j: (0, nj + j)),
            pl.BlockSpec((TF_FFN, D_MODEL), lambda i, j: (j, 0)),
        ],
        out_specs=pl.BlockSpec((tm, D_MODEL), lambda i, j: (i, 0)),
        out_shape=jax.ShapeDtypeStruct((n, D_MODEL), F32),
        scratch_shapes=[pltpu.VMEM((tm, D_MODEL), BF16), pltpu.VMEM((tm, D_MODEL), F32)],
        compiler_params=_cparams("arbitrary", "arbitrary"),
        name="ffn",
    )(h2d, p["norm_ffn"], p["w_gate_up"], p["w_gate_up"], p["w_down"])


def _t5_bucket(dist):
    n = jnp.maximum(dist, 0)
    max_exact = N_BUCKETS // 2
    large = max_exact + (jnp.log(jnp.maximum(n, 1).astype(F32) / max_exact)
                         / math.log(MAX_DISTANCE / max_exact) * (N_BUCKETS - max_exact)).astype(jnp.int32)
    large = jnp.minimum(large, N_BUCKETS - 1)
    return jnp.where(n < max_exact, n, large)


def _far_bucket_is_constant(min_dist):
    max_exact = N_BUCKETS // 2
    d = np.float32(min_dist)
    large = max_exact + int(np.float32(np.log(d / np.float32(max_exact))) / np.float32(math.log(MAX_DISTANCE / max_exact))
                            * (N_BUCKETS - max_exact))
    return min_dist >= max_exact and large >= N_BUCKETS - 1


def _rope_tables(pos):
    half = RET_DK // 2
    inv = 1.0 / (ROPE_BASE ** (jnp.arange(half, dtype=F32) / half))
    ang = pos.astype(F32)[:, None] * inv[None, :]
    cos, sin = jnp.cos(ang), jnp.sin(ang)
    return jnp.concatenate([cos, cos], axis=1), jnp.concatenate([-sin, sin], axis=1)


def _layer_params(l, w_in, named):
    p = {k: v[l] for k, v in named.items()}
    w = w_in[l]
    o = (0,) + IN_OFFSETS + (w.shape[1],)
    z, xbc, dt, rq, rk, rv, rg, dq, dk, dv, gates = [w[:, o[i]:o[i + 1]] for i in range(len(IN_SPLITS))]
    out = {}
    out["w_all"] = jnp.concatenate([xbc, z, rq, rk, rv, rg, dq, gates, dk, dv], axis=1).astype(BF16)
    out["w_dt"] = jnp.pad(dt, ((0, 0), (0, LANES - SSM_HEADS))).astype(BF16)
    out["norm_mix"] = p["norm_mix"].reshape(1, D_MODEL)
    out["conv_w"] = p["conv_w"]
    out["conv_b"] = p["conv_b"].reshape(1, CONV_DIM)
    out["dt_bias"] = jnp.pad(p["dt_bias"], (0, LANES - SSM_HEADS)).reshape(1, LANES)
    out["a_log"] = jnp.pad(p["a_log"], (0, LANES - SSM_HEADS)).reshape(1, LANES)
    out["d_skip"] = jnp.repeat(p["d_skip"], SSM_HEAD_DIM).reshape(1, SSM_D)
    out["ssm_norm"] = p["ssm_norm"].reshape(1, SSM_D)
    head_of_channel = np.arange(SSM_D) // SSM_HEAD_DIM
    out["head_expand"] = jnp.asarray(np.arange(LANES)[:, None] == head_of_channel[None, :], dtype=BF16)
    out["ret_norm"] = p["ret_norm"].reshape(1, RET_HEADS * RET_DV)
    out["qk_norm_q"] = jnp.tile(p["qk_norm_q"], 2).reshape(1, LANES)
    out["qk_norm_k"] = jnp.tile(p["qk_norm_k"], 2).reshape(1, LANES)
    lamv = jnp.stack([p["lambda_q1"], p["lambda_k1"], p["lambda_q2"], p["lambda_k2"]])
    out["lamv"] = jnp.pad(lamv, ((0, 0), (0, LANES - DIFF_HD)))
    out["diff_norm"] = p["diff_norm"].reshape(1, DIFF_VD)
    half = np.arange(LANES) // DIFF_HD
    out["blockdiag"] = jnp.asarray(half[:, None] == half[None, :], dtype=BF16)
    for k in ("w_ssm_out", "w_ret_out", "w_diff_out", "w_o", "w_gate_up", "w_down"):
        out[k] = p[k].astype(BF16)
    out["b_gate"] = p["b_gate"]
    out["norm_ffn"] = p["norm_ffn"].reshape(1, D_MODEL)
    return out


def _token_tile(n, cap):
    tm = min(n, cap)
    while n % tm:
        tm //= 2
    return tm


def _layer_common(x, p, tail0, h0, s0, cos_t, sin_t, nvalid, attn_fn):
    b, tp, _ = x.shape
    n = b * tp
    x2d = x.reshape(n, D_MODEL)
    proj, kf, vf, dt_raw = _inproj(x2d, p["norm_mix"], p["w_all"], p["w_dt"], _token_tile(n, 1024))
    proj3 = proj.reshape(b, tp, C_MAIN)
    y_ssm, h_new, conv8 = _ssd(proj3, dt_raw.reshape(b, tp, LANES), tail0, h0, p, nvalid)
    y_ret, s_new = _retention(proj3, cos_t, sin_t, s0, p["ret_norm"], nvalid)
    y_diff, k_new = attn_fn(proj3, kf.reshape(b, tp, KV_COLS), vf.reshape(b, tp, KV_COLS))
    tm = _token_tile(n, 512)
    h = _merge(y_ssm.reshape(n, -1), y_ret.reshape(n, -1), y_diff.reshape(n, -1), proj, x2d, p, tm)
    y = _ffn(h, p, tm)
    return (y.reshape(b, tp, D_MODEL), k_new, vf.reshape(b, tp, KV_COLS),
            h_new.reshape(b, SSM_HEADS, SSM_HEAD_DIM, SSM_STATE), conv8[:, SUBLANES - (CONV_W - 1):, :], s_new)


def kernel(x_prompt, x_sample, cache_k, cache_v, page_table, state_ssm, state_conv, state_ret, rel_bias, norm_mix, w_in, b_gate, conv_w, conv_b, dt_bias, a_log, d_skip, ssm_norm, w_ssm_out, ret_norm, w_ret_out, qk_norm_q, qk_norm_k, lambda_q1, lambda_k1, lambda_q2, lambda_k2, diff_norm, w_diff_out, w_o, norm_ffn, w_gate_up, w_down):
    named = dict(norm_mix=norm_mix, b_gate=b_gate, conv_w=conv_w, conv_b=conv_b, dt_bias=dt_bias, a_log=a_log,
                 d_skip=d_skip, ssm_norm=ssm_norm, w_ssm_out=w_ssm_out, ret_norm=ret_norm, w_ret_out=w_ret_out,
                 qk_norm_q=qk_norm_q, qk_norm_k=qk_norm_k, lambda_q1=lambda_q1, lambda_k1=lambda_k1,
                 lambda_q2=lambda_q2, lambda_k2=lambda_k2, diff_norm=diff_norm, w_diff_out=w_diff_out, w_o=w_o,
                 norm_ffn=norm_ffn, w_gate_up=w_gate_up, w_down=w_down)
    depth = w_in.shape[0]
    bp, seq, _ = x_prompt.shape
    bs, dec, _ = x_sample.shape
    n_pages = page_table.shape[1]
    past = n_pages * PAGE_SIZE
    assert seq % CHUNK == 0 and dec <= SUBLANES and dec % CHUNK != 0
    assert _far_bucket_is_constant(CHUNK + 1)

    ii = jnp.arange(CHUNK)[:, None]
    jj = jnp.arange(CHUNK)[None, :]
    idx_p = jnp.concatenate([_t5_bucket(k * CHUNK + ii - jj) for k in range(3)], axis=0)
    bias_p = _bias_tiles(rel_bias, idx_p).reshape(DIFF_HEADS, 3, CHUNK, CHUNK)
    tt = jnp.arange(SUBLANES)[:, None]
    idx_s = jnp.concatenate([_t5_bucket(jnp.broadcast_to(past + tt, (SUBLANES, LANES))),
                             _t5_bucket(tt + PAGE_SIZE - jj), _t5_bucket(tt - jj)], axis=1)
    bias_s = _bias_tiles(rel_bias, idx_s)

    cos_p, sin_p = _rope_tables(jnp.arange(seq))
    cos_s, sin_s = _rope_tables(past + jnp.arange(CHUNK))

    xs = jnp.pad(x_sample, ((0, 0), (0, CHUNK - dec), (0, 0)))
    zeros_tail = jnp.zeros((bp, SUBLANES, CONV_DIM), F32)
    zeros_h = jnp.zeros((bp, SSM_D, SSM_STATE), F32)
    zeros_s = jnp.zeros((bp, RET_HEADS, RET_DK, RET_DV), F32)

    yp, ys = x_prompt, xs
    outs_p, outs_s = [], []
    for l in range(depth):
        lam_init = 0.8 - 0.6 * math.exp(-0.3 * l)
        p = _layer_params(l, w_in, named)
        attn_p = lambda proj3, kf, vf: _diff_attention_prompt(proj3, kf, vf, bias_p, p, lam_init)
        yp, k1, v1, h1, c1, r1 = _layer_common(yp, p, zeros_tail, zeros_h, zeros_s, cos_p, sin_p, CHUNK, attn_p)
        outs_p.append((k1.reshape(bp, seq, DIFF_KV_HEADS, 2, DIFF_HD), v1.reshape(bp, seq, DIFF_KV_HEADS, DIFF_VD),
                       h1, c1, r1))

        tail_s = jnp.pad(state_conv[l], ((0, 0), (SUBLANES - (CONV_W - 1), 0), (0, 0)))
        h0_s = state_ssm[l].reshape(bs, SSM_D, SSM_STATE)

        def attn_s(proj3, kf, vf, l=l, p=p, lam_init=lam_init):
            y8, k8 = _diff_attention_sample(proj3[:, :SUBLANES], kf[:, :SUBLANES], vf[:, :SUBLANES], cache_k, cache_v,
                                            page_table, bias_s, p, l, dec, lam_init)
            y_full = jnp.pad(y8, ((0, 0), (0, CHUNK - SUBLANES), (0, 0)))
            k_full = jnp.pad(k8, ((0, 0), (0, CHUNK - SUBLANES), (0, 0)))
            return y_full, k_full

        ys_new, k2, v2, h2, c2, r2 = _layer_common(ys, p, tail_s, h0_s, state_ret[l], cos_s, sin_s, dec, attn_s)
        ys = jnp.where(jnp.arange(CHUNK)[None, :, None] < dec, ys_new, 0.0)
        outs_s.append((k2[:, :dec].reshape(bs, dec, DIFF_KV_HEADS, 2, DIFF_HD),
                       v2[:, :dec].reshape(bs, dec, DIFF_KV_HEADS, DIFF_VD), h2, c2, r2))

    stack = lambda outs, i: jnp.stack([o[i] for o in outs])
    return (yp, ys[:, :dec],
            stack(outs_p, 0), stack(outs_p, 1), stack(outs_p, 2), stack(outs_p, 3), stack(outs_p, 4),
            stack(outs_s, 0), stack(outs_s, 1), stack(outs_s, 2), stack(outs_s, 3), stack(outs_s, 4))
```

```python
import functools
import math

import numpy as np
import jax
import jax.numpy as jnp
from jax import lax
from jax.experimental import pallas as pl
from jax.experimental.pallas import tpu as pltpu

F32 = jnp.float32
BF16 = jnp.bfloat16

D_MODEL = 1024
SSM_HEADS = 16
SSM_HEAD_DIM = 64
SSM_D = SSM_HEADS * SSM_HEAD_DIM
SSM_STATE = 128
SSM_GROUPS = 4
CONV_W = 4
CONV_DIM = SSM_D + 2 * SSM_GROUPS * SSM_STATE
RET_HEADS = 4
RET_DK = 128
RET_DV = 256
ROPE_BASE = 10000.0
DIFF_HEADS = 8
DIFF_KV_HEADS = 4
DIFF_HD = 64
DIFF_VD = 2 * DIFF_HD
N_BUCKETS = 32
MAX_DISTANCE = 128
N_BRANCHES = 3
D_FF = 2816
NORM_EPS = 1e-6
PAGE_SIZE = 128

CHUNK = 128
LANES = 128
SUBLANES = 8
NEG_BIG = -1e30
LOG2E = 1.4426950408889634
VMEM_LIMIT = 52 * 1024 * 1024

IN_SPLITS = (SSM_D, CONV_DIM, SSM_HEADS,
             RET_HEADS * RET_DK, RET_HEADS * RET_DK, RET_HEADS * RET_DV, RET_HEADS * RET_DV,
             DIFF_HEADS * 2 * DIFF_HD, DIFF_KV_HEADS * 2 * DIFF_HD, DIFF_KV_HEADS * DIFF_VD,
             N_BRANCHES * D_MODEL)
IN_OFFSETS = tuple(int(v) for v in np.cumsum(IN_SPLITS)[:-1])

C_XBC, C_Z, C_RQ, C_RK, C_RV, C_RG, C_DQ, C_GATE, C_MAIN = 0, 2048, 3072, 3584, 4096, 5120, 6144, 7168, 10240
TN_PROJ = 1024
KV_COLS = DIFF_KV_HEADS * 2 * DIFF_HD


def _cparams(*sem):
    return pltpu.CompilerParams(dimension_semantics=sem, vmem_limit_bytes=VMEM_LIMIT)


def _nt_dot(a, b):
    return lax.dot_general(a, b, (((1,), (1,)), ((), ())), preferred_element_type=F32)


def _tn_dot(a, b):
    return lax.dot_general(a, b, (((0,), (0,)), ((), ())), preferred_element_type=F32)


def _dot(a, b):
    return jnp.dot(a, b, preferred_element_type=F32)


def _split3(x):
    hi = x.astype(BF16)
    r1 = x - hi.astype(F32)
    mid = r1.astype(BF16)
    lo = (r1 - mid.astype(F32)).astype(BF16)
    return hi, mid, lo


def _silu(x):
    return x * jax.nn.sigmoid(x)


def _inproj_kernel(x_ref, g_ref, w_ref, wdt_ref, main_ref, kf_ref, vf_ref, dt_ref, xn_ref, *, n_main):
    j = pl.program_id(1)

    @pl.when(j == 0)
    def _():
        x = x_ref[...]
        ms = jnp.mean(x * x, axis=-1, keepdims=True)
        xn = (x * lax.rsqrt(ms + NORM_EPS) * g_ref[...]).astype(BF16)
        xn_ref[...] = xn
        dt_ref[...] = _dot(xn, wdt_ref[...])

    acc = _dot(xn_ref[...], w_ref[...])

    @pl.when(j < n_main)
    def _():
        main_ref[...] = acc.astype(BF16)

    @pl.when(j == n_main)
    def _():
        kf_ref[...] = acc[:, :KV_COLS]
        vf_ref[...] = acc[:, KV_COLS:]


def _inproj(x2d, gain, w_all, w_dt, tm):
    n = x2d.shape[0]
    n_main = C_MAIN // TN_PROJ
    grid = (n // tm, n_main + 1)
    return pl.pallas_call(
        functools.partial(_inproj_kernel, n_main=n_main),
        grid=grid,
        in_specs=[
            pl.BlockSpec((tm, D_MODEL), lambda i, j: (i, 0)),
            pl.BlockSpec((1, D_MODEL), lambda i, j: (0, 0)),
            pl.BlockSpec((D_MODEL, TN_PROJ), lambda i, j: (0, j)),
            pl.BlockSpec((D_MODEL, LANES), lambda i, j: (0, 0)),
        ],
        out_specs=[
            pl.BlockSpec((tm, TN_PROJ), lambda i, j: (i, jnp.minimum(j, n_main - 1))),
            pl.BlockSpec((tm, KV_COLS), lambda i, j: (i, 0)),
            pl.BlockSpec((tm, KV_COLS), lambda i, j: (i, 0)),
            pl.BlockSpec((tm, LANES), lambda i, j: (i, 0)),
        ],
        out_shape=[
            jax.ShapeDtypeStruct((n, C_MAIN), BF16),
            jax.ShapeDtypeStruct((n, KV_COLS), F32),
            jax.ShapeDtypeStruct((n, KV_COLS), F32),
            jax.ShapeDtypeStruct((n, LANES), F32),
        ],
        scratch_shapes=[pltpu.VMEM((tm, D_MODEL), BF16)],
        compiler_params=_cparams("arbitrary", "arbitrary"),
        name="inproj",
    )(x2d, gain, w_all, w_dt)


def _ssd_kernel(xbc_ref, z_ref, dt_ref, tail0_ref, h0_ref, cw_ref, cb_ref, dtb_ref, alog_ref, dskip_ref,
                gn_ref, ex_ref, y_ref, hout_ref, convout_ref, xext_ref, xc_ref, ht_ref, *, nvalid, nchunks):
    c = pl.program_id(1)
    L = CHUNK
    GW = SSM_D // SSM_GROUPS

    @pl.when(c == 0)
    def _():
        xext_ref[0:SUBLANES, :] = tail0_ref[0]
        for k in range(SSM_D // LANES):
            ht_ref[:, k * LANES:(k + 1) * LANES] = h0_ref[0, k * LANES:(k + 1) * LANES, :].T

    xext_ref[SUBLANES:SUBLANES + L, :] = xbc_ref[0].astype(F32)
    SL = 512
    for s in range(CONV_DIM // SL):
        cols = slice(s * SL, (s + 1) * SL)
        conv = cw_ref[0:1, cols] * xext_ref[SUBLANES - 3:SUBLANES - 3 + L, cols]
        for w in range(1, CONV_W):
            conv = conv + cw_ref[w:w + 1, cols] * xext_ref[SUBLANES - 3 + w:SUBLANES - 3 + w + L, cols]
        xc_ref[:, cols] = _silu(conv + cb_ref[:, cols])

    @pl.when(c == nchunks - 1)
    def _():
        convout_ref[0] = xext_ref[nvalid:nvalid + SUBLANES, :]

    xext_ref[0:SUBLANES, :] = xext_ref[L:L + SUBLANES, :]

    row = lax.broadcasted_iota(jnp.int32, (L, L), 0)
    col = lax.broadcasted_iota(jnp.int32, (L, L), 1)
    causal = row >= col
    left = col < SSM_HEAD_DIM

    x = dt_ref[0] + dtb_ref[...]
    dt = jnp.maximum(x, 0.0) + jnp.log1p(jnp.exp(-jnp.abs(x)))
    if nvalid < L:
        dt = jnp.where(row < nvalid, dt, 0.0)
    a = -jnp.exp(alog_ref[...])
    da = dt * a
    tri = jnp.where(causal, 1.0, 0.0).astype(BF16)
    cs = sum(_dot(tri, p) for p in _split3(da))
    cs_t = cs.T
    cs_parts = _split3(cs)
    dt_parts = _split3(dt)

    for g in range(SSM_GROUPS):
        gc = slice(g * GW, (g + 1) * GW)
        ex = ex_ref[:, gc]
        csx = sum(_dot(p, ex) for p in cs_parts)
        dtx = sum(_dot(p, ex) for p in dt_parts)
        lastx = csx[L - 1:L, :]
        xs = xc_ref[:, gc]
        xdt = xs * dtx
        xdt_b = xdt.astype(BF16)
        xdtw_b = (xdt * jnp.exp(lastx - csx)).astype(BF16)
        bg = xc_ref[:, SSM_D + g * SSM_STATE:SSM_D + (g + 1) * SSM_STATE].astype(BF16)
        cg = xc_ref[:, SSM_D + SSM_GROUPS * SSM_STATE + g * SSM_STATE:
                    SSM_D + SSM_GROUPS * SSM_STATE + (g + 1) * SSM_STATE].astype(BF16)
        cb = _nt_dot(cg, bg)
        htg = ht_ref[:, gc]
        y_st = _dot(cg, htg.astype(BF16)) * jnp.exp(csx)
        pairs = []
        for pr in range(2):
            xp = xdt_b[:, pr * LANES:(pr + 1) * LANES]
            acc = None
            for e2 in range(2):
                h = g * (SSM_HEADS // SSM_GROUPS) + pr * 2 + e2
                seg = cs[:, h:h + 1] - cs_t[h:h + 1, :]
                decay = jnp.exp(jnp.where(causal, seg, NEG_BIG))
                m = (cb * decay).astype(BF16)
                xm = jnp.where(left if e2 == 0 else jnp.logical_not(left), xp, jnp.zeros_like(xp))
                t = _dot(m, xm)
                acc = t if acc is None else acc + t
            pairs.append(acc)
        y_in = jnp.concatenate(pairs, axis=1)
        ht_ref[:, gc] = htg * jnp.exp(lastx) + _tn_dot(bg, xdtw_b)
        y = (y_in + y_st + dskip_ref[:, gc] * xs)
        zz = z_ref[0, :, gc].astype(F32)
        y = y * _silu(zz)
        ms = jnp.mean(y * y, axis=-1, keepdims=True)
        y_ref[0, :, gc] = (y * lax.rsqrt(ms + NORM_EPS) * gn_ref[:, gc]).astype(BF16)

    @pl.when(c == nchunks - 1)
    def _():
        for k in range(SSM_D // LANES):
            hout_ref[0, k * LANES:(k + 1) * LANES, :] = ht_ref[:, k * LANES:(k + 1) * LANES].T


def _ssd(proj, dt_raw, tail0, h0, p, nvalid):
    b, t, _ = proj.shape
    nchunks = t // CHUNK
    L = CHUNK
    const = lambda shape: pl.BlockSpec(shape, lambda i, c: (0,) * len(shape))
    return pl.pallas_call(
        functools.partial(_ssd_kernel, nvalid=nvalid, nchunks=nchunks),
        grid=(b, nchunks),
        in_specs=[
            pl.BlockSpec((1, L, CONV_DIM), lambda i, c: (i, c, C_XBC // CONV_DIM)),
            pl.BlockSpec((1, L, SSM_D), lambda i, c: (i, c, C_Z // SSM_D)),
            pl.BlockSpec((1, L, LANES), lambda i, c: (i, c, 0)),
            pl.BlockSpec((1, SUBLANES, CONV_DIM), lambda i, c: (i, 0, 0)),
            pl.BlockSpec((1, SSM_D, SSM_STATE), lambda i, c: (i, 0, 0)),
            const((CONV_W, CONV_DIM)), const((1, CONV_DIM)), const((1, LANES)), const((1, LANES)),
            const((1, SSM_D)), const((1, SSM_D)), const((LANES, SSM_D)),
        ],
        out_specs=[
            pl.BlockSpec((1, L, SSM_D), lambda i, c: (i, c, 0)),
            pl.BlockSpec((1, SSM_D, SSM_STATE), lambda i, c: (i, 0, 0)),
            pl.BlockSpec((1, SUBLANES, CONV_DIM), lambda i, c: (i, 0, 0)),
        ],
        out_shape=[
            jax.ShapeDtypeStruct((b, t, SSM_D), BF16),
            jax.ShapeDtypeStruct((b, SSM_D, SSM_STATE), F32),
            jax.ShapeDtypeStruct((b, SUBLANES, CONV_DIM), F32),
        ],
        scratch_shapes=[
            pltpu.VMEM((SUBLANES + L + SUBLANES, CONV_DIM), F32),
            pltpu.VMEM((L, CONV_DIM), F32),
            pltpu.VMEM((SSM_STATE, SSM_D), F32),
        ],
        compiler_params=_cparams("arbitrary", "arbitrary"),
        name="ssd",
    )(proj, proj, dt_raw, tail0, h0, p["conv_w"], p["conv_b"], p["dt_bias"], p["a_log"], p["d_skip"],
      p["ssm_norm"], p["head_expand"])


def _ret_kernel(q_ref, k_ref, v_ref, rg_ref, cos_ref, sin_ref, s0_ref, gn_ref, y_ref, sout_ref, *, ltrue):
    c = pl.program_id(1)
    L = CHUNK

    @pl.when(c == 0)
    def _():
        sout_ref[...] = s0_ref[...]

    row = lax.broadcasted_iota(jnp.int32, (L, L), 0)
    col = lax.broadcasted_iota(jnp.int32, (L, L), 1)
    rel = (row - col).astype(F32)
    idx = row[:, 0:1].astype(F32)
    cosf = cos_ref[...]
    sins = sin_ref[...]
    for h in range(RET_HEADS):
        lg = math.log(1.0 - 2.0 ** (-5.0 - h))
        dmat = jnp.where(rel >= 0, jnp.exp(jnp.maximum(rel, 0.0) * lg), 0.0)
        q_dec = jnp.exp((idx + 1.0) * lg)
        k_dec = jnp.exp((ltrue - 1.0 - idx) * lg)
        c_dec = math.exp(ltrue * lg)
        kc = slice(h * RET_DK, (h + 1) * RET_DK)
        vc = slice(h * RET_DV, (h + 1) * RET_DV)
        qh = q_ref[0, :, kc].astype(F32)
        kh = k_ref[0, :, kc].astype(F32)
        qr = qh * cosf + pltpu.roll(qh, RET_DK // 2, 1) * sins
        kr = (kh * cosf + pltpu.roll(kh, RET_DK // 2, 1) * sins) * RET_DK ** -0.5
        qr_b = qr.astype(BF16)
        vh = v_ref[0, :, vc]
        att = _nt_dot(qr_b, kr.astype(BF16)) * dmat
        s_old = sout_ref[0, h]
        o = _dot(att.astype(BF16), vh) + _dot(qr_b, s_old.astype(BF16)) * q_dec
        sout_ref[0, h] = s_old * c_dec + _tn_dot((kr * k_dec).astype(BF16), vh)
        oc = o - jnp.mean(o, axis=-1, keepdims=True)
        on = oc * lax.rsqrt(jnp.mean(oc * oc, axis=-1, keepdims=True) + NORM_EPS)
        y_ref[0, :, vc] = (on * gn_ref[:, vc] * _silu(rg_ref[0, :, vc].astype(F32))).astype(BF16)


def _retention(proj, cos_t, sin_t, s0, gn, ltrue):
    b, t, _ = proj.shape
    nchunks = t // CHUNK
    L = CHUNK
    qk_w = RET_HEADS * RET_DK
    v_w = RET_HEADS * RET_DV
    return pl.pallas_call(
        functools.partial(_ret_kernel, ltrue=float(ltrue)),
        grid=(b, nchunks),
        in_specs=[
            pl.BlockSpec((1, L, qk_w), lambda i, c: (i, c, C_RQ // qk_w)),
            pl.BlockSpec((1, L, qk_w), lambda i, c: (i, c, C_RK // qk_w)),
            pl.BlockSpec((1, L, v_w), lambda i, c: (i, c, C_RV // v_w)),
            pl.BlockSpec((1, L, v_w), lambda i, c: (i, c, C_RG // v_w)),
            pl.BlockSpec((L, RET_DK), lambda i, c: (c, 0)),
            pl.BlockSpec((L, RET_DK), lambda i, c: (c, 0)),
            pl.BlockSpec((1, RET_HEADS, RET_DK, RET_DV), lambda i, c: (i, 0, 0, 0)),
            pl.BlockSpec((1, v_w), lambda i, c: (0, 0)),
        ],
        out_specs=[
            pl.BlockSpec((1, L, v_w), lambda i, c: (i, c, 0)),
            pl.BlockSpec((1, RET_HEADS, RET_DK, RET_DV), lambda i, c: (i, 0, 0, 0)),
        ],
        out_shape=[
            jax.ShapeDtypeStruct((b, t, v_w), BF16),
            jax.ShapeDtypeStruct((b, RET_HEADS, RET_DK, RET_DV), F32),
        ],
        compiler_params=_cparams("arbitrary", "arbitrary"),
        name="retention",
    )(proj, proj, proj, proj, cos_t, sin_t, s0, gn)


def _bias_kernel(tab_ref, idx_ref, out_ref, *, scale):
    h = pl.program_id(0)
    idx = idx_ref[...]
    acc = jnp.zeros(idx.shape, F32)
    for b in range(N_BUCKETS):
        acc = acc + jnp.where(idx == b, tab_ref[b * DIFF_HEADS + h], 0.0)
    out_ref[0] = acc * scale


def _bias_tiles(rel_bias, idx, scale):
    r, c = idx.shape
    return pl.pallas_call(
        functools.partial(_bias_kernel, scale=scale),
        grid=(DIFF_HEADS,),
        in_specs=[pl.BlockSpec(memory_space=pltpu.SMEM), pl.BlockSpec((r, c), lambda h: (0, 0))],
        out_specs=pl.BlockSpec((1, r, c), lambda h: (h, 0, 0)),
        out_shape=jax.ShapeDtypeStruct((DIFF_HEADS, r, c), F32),
        compiler_params=_cparams("arbitrary"),
        name="t5_bias",
    )(rel_bias.reshape(-1), idx)


def _half_rmsnorm(x, gain, bd):
    x2 = x * x
    hi = x2.astype(BF16)
    lo = (x2 - hi.astype(F32)).astype(BF16)
    ss = _dot(hi, bd) + _dot(lo, bd)
    return x * lax.rsqrt(ss * (1.0 / DIFF_HD) + NORM_EPS) * gain


def _lambda(lamv_ref, lam_init):
    s1 = jnp.sum(lamv_ref[0:1, :] * lamv_ref[1:2, :], axis=-1, keepdims=True)
    s2 = jnp.sum(lamv_ref[2:3, :] * lamv_ref[3:4, :], axis=-1, keepdims=True)
    return jnp.exp(s1) - jnp.exp(s2) + lam_init


FAR_UNITS = 4


def _dattn_kernel(q_ref, kf_ref, vf_ref, bias_ref, gq_ref, gk_ref, lamv_ref, sgc_ref, bd_ref,
                  y_ref, kout_ref, kn_ref, vt_ref, qs_ref, m_ref, l_ref, acc_ref, *, t, lam_init):
    qi = pl.program_id(1)
    TQ = CHUNK
    G = DIFF_KV_HEADS
    R = DIFF_HEADS // DIFF_KV_HEADS
    NC = 2 * R * TQ
    bd = bd_ref[...]

    @pl.when(qi == 0)
    def _():
        def body(i, carry):
            r = pl.ds(pl.multiple_of(i * TQ, TQ), TQ)
            for g in range(G):
                gc = slice(g * LANES, (g + 1) * LANES)
                kn = _half_rmsnorm(kf_ref[0, r, gc], gk_ref[...], bd)
                kout_ref[0, r, gc] = kn
                kn_ref[g, r, :] = kn.astype(BF16)
                vt_ref[g, i] = vf_ref[0, r, gc].T.astype(BF16)
            return carry

        lax.fori_loop(0, t // TQ, body, 0)

    lane = lax.broadcasted_iota(jnp.int32, (TQ, LANES), 1)
    left = lane < DIFF_HD
    for g in range(G):
        for r in range(R):
            hc = slice((g * R + r) * LANES, (g * R + r + 1) * LANES)
            qn = _half_rmsnorm(q_ref[0, :, hc].astype(F32), gq_ref[...], bd) * (DIFF_HD ** -0.5 * LOG2E)
            qs_ref[g, (0 * R + r) * TQ:(0 * R + r + 1) * TQ, :] = jnp.where(left, qn, 0.0).astype(BF16)
            qs_ref[g, (1 * R + r) * TQ:(1 * R + r + 1) * TQ, :] = jnp.where(left, 0.0, qn).astype(BF16)
    m_ref[...] = jnp.full(m_ref.shape, NEG_BIG, F32)
    l_ref[...] = jnp.zeros(l_ref.shape, F32)
    acc_ref[...] = jnp.zeros(acc_ref.shape, F32)

    def step(k0, nunits, tile, diag):
        tk = nunits * TQ
        rows = pl.ds(pl.multiple_of(k0 * TQ, TQ), tk)
        ss = [_nt_dot(kn_ref[g, rows, :], qs_ref[g]) for g in range(G)]
        ps, alphas = [], []
        for g in range(G):
            bias = bias_ref[g, tile]
            s = ss[g] + (bias if nunits == 1 else jnp.concatenate([bias] * nunits, axis=0))
            if diag:
                key = lax.broadcasted_iota(jnp.int32, (TQ, NC), 0)
                qry = lax.broadcasted_iota(jnp.int32, (TQ, NC), 1) & (TQ - 1)
                s = jnp.where(key <= qry, s, NEG_BIG)
            m_old = m_ref[g]
            m_new = jnp.maximum(m_old, jnp.max(s, axis=0, keepdims=True))
            alpha = jnp.exp2(m_old - m_new)
            p = jnp.exp2(s - m_new)
            l_ref[g] = alpha * l_ref[g] + jnp.sum(p, axis=0, keepdims=True)
            m_ref[g] = m_new
            ps.append(p.astype(BF16))
            alphas.append(alpha)
        for g in range(G):
            if nunits == 1:
                vt = vt_ref[g, k0]
            else:
                vt = jnp.concatenate([vt_ref[g, k0 + u] for u in range(nunits)], axis=1)
            acc_ref[g] = acc_ref[g] * alphas[g] + _dot(vt, ps[g])

    nfar = jnp.maximum(qi - 1, 0)
    nbig = nfar // FAR_UNITS

    def far_big(i, carry):
        step(i * FAR_UNITS, FAR_UNITS, 2, False)
        return carry

    def far_one(i, carry):
        step(nbig * FAR_UNITS + i, 1, 2, False)
        return carry

    lax.fori_loop(0, nbig, far_big, 0)
    lax.fori_loop(0, nfar - nbig * FAR_UNITS, far_one, 0)

    @pl.when(qi >= 1)
    def _():
        step(qi - 1, 1, 1, False)

    step(qi, 1, 0, True)

    lam = _lambda(lamv_ref, lam_init)
    for g in range(G):
        inv_l = 1.0 / l_ref[g]
        for r in range(R):
            c0 = slice((0 * R + r) * TQ, (0 * R + r + 1) * TQ)
            c1 = slice((1 * R + r) * TQ, (1 * R + r + 1) * TQ)
            o = acc_ref[g, :, c0] * inv_l[:, c0] - lam * (acc_ref[g, :, c1] * inv_l[:, c1])
            o = o * lax.rsqrt(jnp.mean(o * o, axis=0, keepdims=True) + NORM_EPS)
            o = o * sgc_ref[...] * (1.0 - lam_init)
            y_ref[0, :, (g * R + r) * LANES:(g * R + r + 1) * LANES] = o.T.astype(BF16)


def _diff_attention_prompt(proj, kf, vf, bias, p, lam_init):
    b, t, _ = proj.shape
    TQ = CHUNK
    G = DIFF_KV_HEADS
    R = DIFF_HEADS // DIFF_KV_HEADS
    NC = 2 * R * TQ
    qw = DIFF_HEADS * 2 * DIFF_HD
    const = lambda shape: pl.BlockSpec(shape, lambda i, q: (0,) * len(shape))
    return pl.pallas_call(
        functools.partial(_dattn_kernel, t=t, lam_init=lam_init),
        grid=(b, t // TQ),
        in_specs=[
            pl.BlockSpec((1, TQ, qw), lambda i, q: (i, q, C_DQ // qw)),
            pl.BlockSpec((1, t, KV_COLS), lambda i, q: (i, 0, 0)),
            pl.BlockSpec((1, t, KV_COLS), lambda i, q: (i, 0, 0)),
            const((G, 3, TQ, NC)),
            const((1, LANES)), const((1, LANES)), const((4, LANES)), const((DIFF_VD, 1)), const((LANES, LANES)),
        ],
        out_specs=[
            pl.BlockSpec((1, TQ, qw), lambda i, q: (i, q, 0)),
            pl.BlockSpec((1, t, KV_COLS), lambda i, q: (i, 0, 0)),
        ],
        out_shape=[
            jax.ShapeDtypeStruct((b, t, DIFF_HEADS * DIFF_VD), BF16),
            jax.ShapeDtypeStruct((b, t, KV_COLS), F32),
        ],
        scratch_shapes=[
            pltpu.VMEM((G, t, LANES), BF16), pltpu.VMEM((G, t // TQ, DIFF_VD, TQ), BF16),
            pltpu.VMEM((G, NC, LANES), BF16),
            pltpu.VMEM((G, 1, NC), F32), pltpu.VMEM((G, 1, NC), F32),
            pltpu.VMEM((G, DIFF_VD, NC), F32),
        ],
        compiler_params=_cparams("arbitrary", "arbitrary"),
        name="diff_attn_prompt",
    )(proj, kf, vf, bias, p["qk_norm_q"], p["qk_norm_k"], p["lamv"], p["diff_norm_col"], p["blockdiag"])


PAGES_PER_STEP = 8


def _sattn_kernel(pt_ref, *refs, layer, nvalid, lam_init, npp):
    del pt_ref, layer
    k_refs = refs[0:npp]
    v_refs = refs[npp:2 * npp]
    (q_ref, kf_ref, vf_ref, bias_ref, gq_ref, gk_ref, lamv_ref, sg_ref, bd_ref,
     y_ref, kout_ref, qs_ref, knew_ref, vnew_ref, m_ref, l_ref, acc_ref) = refs[2 * npp:]
    s_id = pl.program_id(1)
    nsteps = pl.num_programs(1)
    G = DIFF_KV_HEADS
    R = DIFF_HEADS // DIFF_KV_HEADS
    TP = SUBLANES
    MR = 2 * R * TP
    bd = bd_ref[...]
    lane = lax.broadcasted_iota(jnp.int32, (TP, LANES), 1)
    left = lane < DIFF_HD

    @pl.when(s_id == 0)
    def _():
        knew_ref[...] = jnp.zeros(knew_ref.shape, BF16)
        vnew_ref[...] = jnp.zeros(vnew_ref.shape, BF16)
        for g in range(G):
            gc = slice(g * LANES, (g + 1) * LANES)
            kn = _half_rmsnorm(kf_ref[0, :, gc], gk_ref[...], bd)
            kout_ref[0, :, gc] = kn
            knew_ref[g, 0:TP, :] = kn.astype(BF16)
            vnew_ref[g, 0:TP, :] = vf_ref[0, :, gc].astype(BF16)
            for r in range(R):
                hc = slice((g * R + r) * LANES, (g * R + r + 1) * LANES)
                qn = _half_rmsnorm(q_ref[0, :, hc].astype(F32), gq_ref[...], bd) * DIFF_HD ** -0.5
                qs_ref[g, (0 * R + r) * TP:(0 * R + r + 1) * TP, :] = jnp.where(left, qn, 0.0).astype(BF16)
                qs_ref[g, (1 * R + r) * TP:(1 * R + r + 1) * TP, :] = jnp.where(left, 0.0, qn).astype(BF16)
        m_ref[...] = jnp.full(m_ref.shape, NEG_BIG, F32)
        l_ref[...] = jnp.zeros(l_ref.shape, F32)
        acc_ref[...] = jnp.zeros(acc_ref.shape, F32)

    def bias_rows(g, seg):
        per_head = [bias_ref[g * R + r, :, seg * LANES:(seg + 1) * LANES] for r in range(R)]
        return jnp.concatenate(per_head + per_head, axis=0)

    def update(g, s, vs):
        m_old = m_ref[g]
        m_new = jnp.maximum(m_old, jnp.max(s, axis=-1, keepdims=True))
        alpha = jnp.exp(m_old - m_new)
        p = jnp.exp(s - m_new)
        l_ref[g] = alpha * l_ref[g] + jnp.sum(p, axis=-1, keepdims=True)
        m_ref[g] = m_new
        pb = p.astype(BF16)
        pv = _dot(pb[:, 0:LANES], vs[0])
        for i in range(1, len(vs)):
            pv = pv + _dot(pb[:, i * LANES:(i + 1) * LANES], vs[i])
        acc_ref[g] = acc_ref[g] * alpha + pv

    def pages(last):
        for g in range(G):
            gc = slice(g * LANES, (g + 1) * LANES)
            qs = qs_ref[g]
            far = bias_rows(g, 0)
            parts = []
            for i in range(npp):
                sc = _dot(qs, k_refs[i][gc, :].astype(BF16))
                if last and i == npp - 1:
                    sc = sc + bias_rows(g, 1)
                else:
                    sc = sc + far
                parts.append(sc)
            vs = [v_refs[i][pl.ds(g, PAGE_SIZE, stride=G), :].astype(BF16) for i in range(npp)]
            update(g, jnp.concatenate(parts, axis=1), vs)

    @pl.when(s_id < nsteps - 1)
    def _():
        pages(False)

    @pl.when(s_id == nsteps - 1)
    def _():
        pages(True)
        rowt = lax.broadcasted_iota(jnp.int32, (MR, LANES), 0) % TP
        colj = lax.broadcasted_iota(jnp.int32, (MR, LANES), 1)
        ok = jnp.logical_and(colj <= rowt, colj < nvalid)
        lam = _lambda(lamv_ref, lam_init)
        for g in range(G):
            sc = _nt_dot(qs_ref[g], knew_ref[g]) + bias_rows(g, 2)
            update(g, jnp.where(ok, sc, NEG_BIG), [vnew_ref[g]])
            acc = acc_ref[g] / l_ref[g]
            for r in range(R):
                o = acc[(0 * R + r) * TP:(0 * R + r + 1) * TP, :] - lam * acc[(1 * R + r) * TP:(1 * R + r + 1) * TP, :]
                o = o * lax.rsqrt(jnp.mean(o * o, axis=-1, keepdims=True) + NORM_EPS)
                hc = slice((g * R + r) * LANES, (g * R + r + 1) * LANES)
                y_ref[0, :, hc] = (o * sg_ref[...] * (1.0 - lam_init)).astype(BF16)


def _diff_attention_sample(proj, kf, vf, cache_k, cache_v, page_table, bias, p, layer, nvalid, lam_init):
    b, tp, _ = proj.shape
    n_pages = page_table.shape[1]
    npp = PAGES_PER_STEP
    while n_pages % npp:
        npp //= 2
    nsteps = n_pages // npp
    G = DIFF_KV_HEADS
    R = DIFF_HEADS // DIFF_KV_HEADS
    MR = 2 * R * tp
    ck = jnp.transpose(cache_k, (0, 1, 3, 4, 5, 2)).reshape(cache_k.shape[0], cache_k.shape[1], KV_COLS, PAGE_SIZE)
    cv = cache_v.reshape(cache_v.shape[0], cache_v.shape[1], PAGE_SIZE * DIFF_KV_HEADS, DIFF_VD)

    def page_spec(i):
        return pl.BlockSpec((None, None, KV_COLS, PAGE_SIZE),
                            lambda bi, s, pt: (layer, pt[bi * n_pages + s * npp + i], 0, 0))

    const = lambda shape: pl.BlockSpec(shape, lambda bi, s, pt: (0,) * len(shape))
    grid_spec = pltpu.PrefetchScalarGridSpec(
        num_scalar_prefetch=1,
        grid=(b, nsteps),
        in_specs=[page_spec(i) for i in range(npp)] + [page_spec(i) for i in range(npp)] + [
            pl.BlockSpec((1, tp, DIFF_HEADS * 2 * DIFF_HD), lambda bi, s, pt: (bi, 0, C_DQ // (DIFF_HEADS * 2 * DIFF_HD))),
            pl.BlockSpec((1, tp, KV_COLS), lambda bi, s, pt: (bi, 0, 0)),
            pl.BlockSpec((1, tp, KV_COLS), lambda bi, s, pt: (bi, 0, 0)),
            const((DIFF_HEADS, tp, 3 * LANES)),
            const((1, LANES)), const((1, LANES)), const((4, LANES)), const((1, LANES)), const((LANES, LANES)),
        ],
        out_specs=[
            pl.BlockSpec((1, tp, DIFF_HEADS * DIFF_VD), lambda bi, s, pt: (bi, 0, 0)),
            pl.BlockSpec((1, tp, KV_COLS), lambda bi, s, pt: (bi, 0, 0)),
        ],
        scratch_shapes=[
            pltpu.VMEM((G, MR, LANES), BF16),
            pltpu.VMEM((G, PAGE_SIZE, LANES), BF16), pltpu.VMEM((G, PAGE_SIZE, LANES), BF16),
            pltpu.VMEM((G, MR, 1), F32), pltpu.VMEM((G, MR, 1), F32), pltpu.VMEM((G, MR, LANES), F32),
        ],
    )
    return pl.pallas_call(
        functools.partial(_sattn_kernel, layer=layer, nvalid=nvalid, lam_init=lam_init, npp=npp),
        grid_spec=grid_spec,
        out_shape=[
            jax.ShapeDtypeStruct((b, tp, DIFF_HEADS * DIFF_VD), BF16),
            jax.ShapeDtypeStruct((b, tp, KV_COLS), F32),
        ],
        compiler_params=_cparams("arbitrary", "arbitrary"),
        name="diff_attn_sample",
    )(page_table.reshape(-1), *([ck] * npp), *([cv] * npp), proj, kf, vf, bias,
      p["qk_norm_q"], p["qk_norm_k"], p["lamv"], p["diff_norm"], p["blockdiag"])


def _merge_kernel(ys_ref, yr_ref, yd_ref, g0_ref, g1_ref, g2_ref, x_ref, ws_ref, wr_ref, wd_ref, wo_ref,
                  bg_ref, h_ref):
    merged = None
    for i, (y_ref, w_ref, g_ref) in enumerate(((ys_ref, ws_ref, g0_ref), (yr_ref, wr_ref, g1_ref),
                                               (yd_ref, wd_ref, g2_ref))):
        br = _dot(y_ref[...], w_ref[...])
        t = jax.nn.sigmoid(g_ref[...].astype(F32) + bg_ref[i:i + 1, :]) * br
        merged = t if merged is None else merged + t
    h_ref[...] = x_ref[...] + _dot(merged.astype(BF16), wo_ref[...])


def _merge(y_ssm, y_ret, y_diff, proj2d, x2d, p, tm):
    n = x2d.shape[0]
    tok = lambda cb: pl.BlockSpec((tm, D_MODEL), lambda i: (i, cb))
    wspec = pl.BlockSpec((D_MODEL, D_MODEL), lambda i: (0, 0))
    g0 = C_GATE // D_MODEL
    return pl.pallas_call(
        _merge_kernel,
        grid=(n // tm,),
        in_specs=[tok(0), tok(0), tok(0), tok(g0), tok(g0 + 1), tok(g0 + 2), tok(0),
                  wspec, wspec, wspec, wspec, pl.BlockSpec((N_BRANCHES, D_MODEL), lambda i: (0, 0))],
        out_specs=tok(0),
        out_shape=jax.ShapeDtypeStruct((n, D_MODEL), F32),
        compiler_params=_cparams("arbitrary"),
        name="merge",
    )(y_ssm, y_ret, y_diff, proj2d, proj2d, proj2d, x2d, p["w_ssm_out"], p["w_ret_out"], p["w_diff_out"],
      p["w_o"], p["b_gate"])


TF_FFN = D_FF // 2


def _ffn_kernel(h_ref, g_ref, wg_ref, wu_ref, wd_ref, y_ref, hn_ref, acc_ref):
    j = pl.program_id(1)

    @pl.when(j == 0)
    def _():
        h = h_ref[...]
        ms = jnp.mean(h * h, axis=-1, keepdims=True)
        hn_ref[...] = (h * lax.rsqrt(ms + NORM_EPS) * g_ref[...]).astype(BF16)
        acc_ref[...] = h

    hn = hn_ref[...]
    act = _silu(_dot(hn, wg_ref[...])) * _dot(hn, wu_ref[...])
    acc_ref[...] += _dot(act.astype(BF16), wd_ref[...])

    @pl.when(j == pl.num_programs(1) - 1)
    def _():
        y_ref[...] = acc_ref[...]


def _ffn(h2d, p, tm):
    n = h2d.shape[0]
    nj = D_FF // TF_FFN
    return pl.pallas_call(
        _ffn_kernel,
        grid=(n // tm, nj),
        in_specs=[
            pl.BlockSpec((tm, D_MODEL), lambda i, j: (i, 0)),
            pl.BlockSpec((1, D_MODEL), lambda i, j: (0, 0)),
            pl.BlockSpec((D_MODEL, TF_FFN), lambda i, j: (0, j)),
            pl.BlockSpec((D_MODEL, TF_FFN), lambda i, j: (0, nj + j)),
            pl.BlockSpec((TF_FFN, D_MODEL), lambda i, j: (j, 0)),
        ],
        out_specs=pl.BlockSpec((tm, D_MODEL), lambda i, j: (i, 0)),
        out_shape=jax.ShapeDtypeStruct((n, D_MODEL), F32),
        scratch_shapes=[pltpu.VMEM((tm, D_MODEL), BF16), pltpu.VMEM((tm, D_MODEL), F32)],
        compiler_params=_cparams("arbitrary", "arbitrary"),
        name="ffn",
    )(h2d, p["norm_ffn"], p["w_gate_up"], p["w_gate_up"], p["w_down"])


def _t5_bucket(dist):
    n = jnp.maximum(dist, 0)
    max_exact = N_BUCKETS // 2
    large = max_exact + (jnp.log(jnp.maximum(n, 1).astype(F32) / max_exact)
                         / math.log(MAX_DISTANCE / max_exact) * (N_BUCKETS - max_exact)).astype(jnp.int32)
    large = jnp.minimum(large, N_BUCKETS - 1)
    return jnp.where(n < max_exact, n, large)


def _far_bucket_is_constant(min_dist):
    max_exact = N_BUCKETS // 2
    d = np.float32(min_dist)
    large = max_exact + int(np.float32(np.log(d / np.float32(max_exact))) / np.float32(math.log(MAX_DISTANCE / max_exact))
                            * (N_BUCKETS - max_exact))
    return min_dist >= max_exact and large >= N_BUCKETS - 1


def _rope_tables(pos):
    half = RET_DK // 2
    inv = 1.0 / (ROPE_BASE ** (jnp.arange(half, dtype=F32) / half))
    ang = pos.astype(F32)[:, None] * inv[None, :]
    cos, sin = jnp.cos(ang), jnp.sin(ang)
    return jnp.concatenate([cos, cos], axis=1), jnp.concatenate([-sin, sin], axis=1)


def _layer_params(l, w_in, named):
    p = {k: v[l] for k, v in named.items()}
    w = w_in[l]
    o = (0,) + IN_OFFSETS + (w.shape[1],)
    z, xbc, dt, rq, rk, rv, rg, dq, dk, dv, gates = [w[:, o[i]:o[i + 1]] for i in range(len(IN_SPLITS))]
    out = {}
    out["w_all"] = jnp.concatenate([xbc, z, rq, rk, rv, rg, dq, gates, dk, dv], axis=1).astype(BF16)
    out["w_dt"] = jnp.pad(dt, ((0, 0), (0, LANES - SSM_HEADS))).astype(BF16)
    out["norm_mix"] = p["norm_mix"].reshape(1, D_MODEL)
    out["conv_w"] = p["conv_w"]
    out["conv_b"] = p["conv_b"].reshape(1, CONV_DIM)
    out["dt_bias"] = jnp.pad(p["dt_bias"], (0, LANES - SSM_HEADS)).reshape(1, LANES)
    out["a_log"] = jnp.pad(p["a_log"], (0, LANES - SSM_HEADS)).reshape(1, LANES)
    out["d_skip"] = jnp.repeat(p["d_skip"], SSM_HEAD_DIM).reshape(1, SSM_D)
    out["ssm_norm"] = p["ssm_norm"].reshape(1, SSM_D)
    head_of_channel = np.arange(SSM_D) // SSM_HEAD_DIM
    out["head_expand"] = jnp.asarray(np.arange(LANES)[:, None] == head_of_channel[None, :], dtype=BF16)
    out["ret_norm"] = p["ret_norm"].reshape(1, RET_HEADS * RET_DV)
    out["qk_norm_q"] = jnp.tile(p["qk_norm_q"], 2).reshape(1, LANES)
    out["qk_norm_k"] = jnp.tile(p["qk_norm_k"], 2).reshape(1, LANES)
    lamv = jnp.stack([p["lambda_q1"], p["lambda_k1"], p["lambda_q2"], p["lambda_k2"]])
    out["lamv"] = jnp.pad(lamv, ((0, 0), (0, LANES - DIFF_HD)))
    out["diff_norm"] = p["diff_norm"].reshape(1, DIFF_VD)
    out["diff_norm_col"] = p["diff_norm"].reshape(DIFF_VD, 1)
    half = np.arange(LANES) // DIFF_HD
    out["blockdiag"] = jnp.asarray(half[:, None] == half[None, :], dtype=BF16)
    for k in ("w_ssm_out", "w_ret_out", "w_diff_out", "w_o", "w_gate_up", "w_down"):
        out[k] = p[k].astype(BF16)
    out["b_gate"] = p["b_gate"]
    out["norm_ffn"] = p["norm_ffn"].reshape(1, D_MODEL)
    return out


def _token_tile(n, cap):
    tm = min(n, cap)
    while n % tm:
        tm //= 2
    return tm


def _layer_common(x, p, tail0, h0, s0, cos_t, sin_t, nvalid, attn_fn):
    b, tp, _ = x.shape
    n = b * tp
    x2d = x.reshape(n, D_MODEL)
    proj, kf, vf, dt_raw = _inproj(x2d, p["norm_mix"], p["w_all"], p["w_dt"], _token_tile(n, 1024))
    proj3 = proj.reshape(b, tp, C_MAIN)
    y_ssm, h_new, conv8 = _ssd(proj3, dt_raw.reshape(b, tp, LANES), tail0, h0, p, nvalid)
    y_ret, s_new = _retention(proj3, cos_t, sin_t, s0, p["ret_norm"], nvalid)
    y_diff, k_new = attn_fn(proj3, kf.reshape(b, tp, KV_COLS), vf.reshape(b, tp, KV_COLS))
    tm = _token_tile(n, 512)
    h = _merge(y_ssm.reshape(n, -1), y_ret.reshape(n, -1), y_diff.reshape(n, -1), proj, x2d, p, tm)
    y = _ffn(h, p, tm)
    return (y.reshape(b, tp, D_MODEL), k_new, vf.reshape(b, tp, KV_COLS),
            h_new.reshape(b, SSM_HEADS, SSM_HEAD_DIM, SSM_STATE), conv8[:, SUBLANES - (CONV_W - 1):, :], s_new)


def kernel(x_prompt, x_sample, cache_k, cache_v, page_table, state_ssm, state_conv, state_ret, rel_bias, norm_mix, w_in, b_gate, conv_w, conv_b, dt_bias, a_log, d_skip, ssm_norm, w_ssm_out, ret_norm, w_ret_out, qk_norm_q, qk_norm_k, lambda_q1, lambda_k1, lambda_q2, lambda_k2, diff_norm, w_diff_out, w_o, norm_ffn, w_gate_up, w_down):
    named = dict(norm_mix=norm_mix, b_gate=b_gate, conv_w=conv_w, conv_b=conv_b, dt_bias=dt_bias, a_log=a_log,
                 d_skip=d_skip, ssm_norm=ssm_norm, w_ssm_out=w_ssm_out, ret_norm=ret_norm, w_ret_out=w_ret_out,
                 qk_norm_q=qk_norm_q, qk_norm_k=qk_norm_k, lambda_q1=lambda_q1, lambda_k1=lambda_k1,
                 lambda_q2=lambda_q2, lambda_k2=lambda_k2, diff_norm=diff_norm, w_diff_out=w_diff_out, w_o=w_o,
                 norm_ffn=norm_ffn, w_gate_up=w_gate_up, w_down=w_down)
    depth = w_in.shape[0]
    bp, seq, _ = x_prompt.shape
    bs, dec, _ = x_sample.shape
    n_pages = page_table.shape[1]
    past = n_pages * PAGE_SIZE
    assert seq % CHUNK == 0 and dec <= SUBLANES and dec % CHUNK != 0
    assert _far_bucket_is_constant(CHUNK + 1)

    ii = jnp.arange(CHUNK)[:, None]
    jj = jnp.arange(CHUNK)[None, :]
    idx_p = jnp.concatenate([_t5_bucket(k * CHUNK + jj - ii) for k in range(3)], axis=0)
    bias_h = _bias_tiles(rel_bias, idx_p, LOG2E).reshape(DIFF_KV_HEADS, DIFF_HEADS // DIFF_KV_HEADS, 3, CHUNK, CHUNK)
    bias_p = jnp.concatenate([bias_h[:, r] for r in range(DIFF_HEADS // DIFF_KV_HEADS)] * 2, axis=-1)
    tt = jnp.arange(SUBLANES)[:, None]
    idx_s = jnp.concatenate([_t5_bucket(jnp.broadcast_to(past + tt, (SUBLANES, LANES))),
                             _t5_bucket(tt + PAGE_SIZE - jj), _t5_bucket(tt - jj)], axis=1)
    bias_s = _bias_tiles(rel_bias, idx_s, 1.0)

    cos_p, sin_p = _rope_tables(jnp.arange(seq))
    cos_s, sin_s = _rope_tables(past + jnp.arange(CHUNK))

    xs = jnp.pad(x_sample, ((0, 0), (0, CHUNK - dec), (0, 0)))
    zeros_tail = jnp.zeros((bp, SUBLANES, CONV_DIM), F32)
    zeros_h = jnp.zeros((bp, SSM_D, SSM_STATE), F32)
    zeros_s = jnp.zeros((bp, RET_HEADS, RET_DK, RET_DV), F32)

    yp, ys = x_prompt, xs
    outs_p, outs_s = [], []
    for l in range(depth):
        lam_init = 0.8 - 0.6 * math.exp(-0.3 * l)
        p = _layer_params(l, w_in, named)
        attn_p = lambda proj3, kf, vf: _diff_attention_prompt(proj3, kf, vf, bias_p, p, lam_init)
        yp, k1, v1, h1, c1, r1 = _layer_common(yp, p, zeros_tail, zeros_h, zeros_s, cos_p, sin_p, CHUNK, attn_p)
        outs_p.append((k1.reshape(bp, seq, DIFF_KV_HEADS, 2, DIFF_HD), v1.reshape(bp, seq, DIFF_KV_HEADS, DIFF_VD),
                       h1, c1, r1))

        tail_s = jnp.pad(state_conv[l], ((0, 0), (SUBLANES - (CONV_W - 1), 0), (0, 0)))
        h0_s = state_ssm[l].reshape(bs, SSM_D, SSM_STATE)

        def attn_s(proj3, kf, vf, l=l, p=p, lam_init=lam_init):
            y8, k8 = _diff_attention_sample(proj3[:, :SUBLANES], kf[:, :SUBLANES], vf[:, :SUBLANES], cache_k, cache_v,
                                            page_table, bias_s, p, l, dec, lam_init)
            y_full = jnp.pad(y8, ((0, 0), (0, CHUNK - SUBLANES), (0, 0)))
            k_full = jnp.pad(k8, ((0, 0), (0, CHUNK - SUBLANES), (0, 0)))
            return y_full, k_full

        ys_new, k2, v2, h2, c2, r2 = _layer_common(ys, p, tail_s, h0_s, state_ret[l], cos_s, sin_s, dec, attn_s)
        ys = jnp.where(jnp.arange(CHUNK)[None, :, None] < dec, ys_new, 0.0)
        outs_s.append((k2[:, :dec].reshape(bs, dec, DIFF_KV_HEADS, 2, DIFF_HD),
                       v2[:, :dec].reshape(bs, dec, DIFF_KV_HEADS, DIFF_VD), h2, c2, r2))

    stack = lambda outs, i: jnp.stack([o[i] for o in outs])
    return (yp, ys[:, :dec],
            stack(outs_p, 0), stack(outs_p, 1), stack(outs_p, 2), stack(outs_p, 3), stack(outs_p, 4),
            stack(outs_s, 0), stack(outs_s, 1), stack(outs_s, 2), stack(outs_s, 3), stack(outs_s, 4))
```

```python
import functools
import math

import numpy as np
import jax
import jax.numpy as jnp
from jax import lax
from jax.experimental import pallas as pl
from jax.experimental.pallas import tpu as pltpu

F32 = jnp.float32
BF16 = jnp.bfloat16

D_MODEL = 1024
SSM_HEADS = 16
SSM_HEAD_DIM = 64
SSM_D = SSM_HEADS * SSM_HEAD_DIM
SSM_STATE = 128
SSM_GROUPS = 4
CONV_W = 4
CONV_DIM = SSM_D + 2 * SSM_GROUPS * SSM_STATE
RET_HEADS = 4
RET_DK = 128
RET_DV = 256
ROPE_BASE = 10000.0
DIFF_HEADS = 8
DIFF_KV_HEADS = 4
DIFF_HD = 64
DIFF_VD = 2 * DIFF_HD
N_BUCKETS = 32
MAX_DISTANCE = 128
N_BRANCHES = 3
D_FF = 2816
NORM_EPS = 1e-6
PAGE_SIZE = 128

CHUNK = 128
LANES = 128
SUBLANES = 8
CONV_TAIL = 16
NEG_BIG = -1e30
LOG2E = 1.4426950408889634
VMEM_LIMIT = 52 * 1024 * 1024

IN_SPLITS = (SSM_D, CONV_DIM, SSM_HEADS,
             RET_HEADS * RET_DK, RET_HEADS * RET_DK, RET_HEADS * RET_DV, RET_HEADS * RET_DV,
             DIFF_HEADS * 2 * DIFF_HD, DIFF_KV_HEADS * 2 * DIFF_HD, DIFF_KV_HEADS * DIFF_VD,
             N_BRANCHES * D_MODEL)
IN_OFFSETS = tuple(int(v) for v in np.cumsum(IN_SPLITS)[:-1])

C_XBC, C_Z, C_RQ, C_RK, C_RV, C_RG, C_DQ, C_GATE, C_MAIN = 0, 2048, 3072, 3584, 4096, 5120, 6144, 7168, 10240
TN_PROJ = 1024
KV_COLS = DIFF_KV_HEADS * 2 * DIFF_HD


def _cparams(*sem):
    return pltpu.CompilerParams(dimension_semantics=sem, vmem_limit_bytes=VMEM_LIMIT)


def _nt_dot(a, b):
    return lax.dot_general(a, b, (((1,), (1,)), ((), ())), preferred_element_type=F32)


def _tn_dot(a, b):
    return lax.dot_general(a, b, (((0,), (0,)), ((), ())), preferred_element_type=F32)


def _dot(a, b):
    return jnp.dot(a, b, preferred_element_type=F32)


def _split3(x):
    hi = x.astype(BF16)
    r1 = x - hi.astype(F32)
    mid = r1.astype(BF16)
    lo = (r1 - mid.astype(F32)).astype(BF16)
    return hi, mid, lo


def _sigmoid(x):
    return 0.5 * jnp.tanh(0.5 * x) + 0.5


def _silu(x):
    return x * _sigmoid(x)


def _inproj_kernel(x_ref, g_ref, w_ref, wdt_ref, main_ref, kf_ref, vf_ref, dt_ref, xn_ref, *, n_main):
    j = pl.program_id(1)

    @pl.when(j == 0)
    def _():
        x = x_ref[...]
        ms = jnp.mean(x * x, axis=-1, keepdims=True)
        xn = (x * lax.rsqrt(ms + NORM_EPS) * g_ref[...]).astype(BF16)
        xn_ref[...] = xn
        dt_ref[...] = _dot(xn, wdt_ref[...])

    acc = _dot(xn_ref[...], w_ref[...])

    @pl.when(j < n_main)
    def _():
        main_ref[...] = acc.astype(BF16)

    @pl.when(j == n_main)
    def _():
        kf_ref[...] = acc[:, :KV_COLS]
        vf_ref[...] = acc[:, KV_COLS:]


def _inproj(x2d, gain, w_all, w_dt, tm):
    n = x2d.shape[0]
    n_main = C_MAIN // TN_PROJ
    grid = (n // tm, n_main + 1)
    return pl.pallas_call(
        functools.partial(_inproj_kernel, n_main=n_main),
        grid=grid,
        in_specs=[
            pl.BlockSpec((tm, D_MODEL), lambda i, j: (i, 0)),
            pl.BlockSpec((1, D_MODEL), lambda i, j: (0, 0)),
            pl.BlockSpec((D_MODEL, TN_PROJ), lambda i, j: (0, j)),
            pl.BlockSpec((D_MODEL, LANES), lambda i, j: (0, 0)),
        ],
        out_specs=[
            pl.BlockSpec((tm, TN_PROJ), lambda i, j: (i, jnp.minimum(j, n_main - 1))),
            pl.BlockSpec((tm, KV_COLS), lambda i, j: (i, 0)),
            pl.BlockSpec((tm, KV_COLS), lambda i, j: (i, 0)),
            pl.BlockSpec((tm, LANES), lambda i, j: (i, 0)),
        ],
        out_shape=[
            jax.ShapeDtypeStruct((n, C_MAIN), BF16),
            jax.ShapeDtypeStruct((n, KV_COLS), F32),
            jax.ShapeDtypeStruct((n, KV_COLS), F32),
            jax.ShapeDtypeStruct((n, LANES), F32),
        ],
        scratch_shapes=[pltpu.VMEM((tm, D_MODEL), BF16)],
        compiler_params=_cparams("arbitrary", "arbitrary"),
        name="inproj",
    )(x2d, gain, w_all, w_dt)


def _ssd_kernel(xbc_ref, z_ref, dt_ref, tail0_ref, h0_ref, cw_ref, cb_ref, dtb_ref, alog_ref, dskip_ref,
                gn_ref, ex_ref, y_ref, hout_ref, convout_ref, xext_ref, tlo_ref, xc_ref, ht_ref, *, nvalid, nchunks):
    c = pl.program_id(1)
    L = CHUNK
    GW = SSM_D // SSM_GROUPS
    SL = 512

    @pl.when(c == 0)
    def _():
        t0 = tail0_ref[0]
        t0b = t0.astype(BF16)
        xext_ref[0:CONV_TAIL, :] = t0b
        tlo_ref[...] = (t0 - t0b.astype(F32)).astype(BF16)
        for k in range(SSM_D // LANES):
            ht_ref[:, k * LANES:(k + 1) * LANES] = h0_ref[0, k * LANES:(k + 1) * LANES, :].T

    xext_ref[CONV_TAIL:CONV_TAIL + L, :] = xbc_ref[0]
    srow = lax.broadcasted_iota(jnp.int32, (L, CONV_TAIL + L), 0)
    scol = lax.broadcasted_iota(jnp.int32, (L, CONV_TAIL + L), 1)
    shifts = [jnp.where(scol == srow + CONV_TAIL - s, 1.0, 0.0).astype(BF16) for s in range(1, CONV_W)]
    for sl in range(CONV_DIM // SL):
        cols = slice(sl * SL, (sl + 1) * SL)
        conv = cw_ref[CONV_W - 1:CONV_W, cols] * xbc_ref[0, :, cols].astype(F32) + cb_ref[:, cols]
        for s in range(1, CONV_W):
            conv = conv + cw_ref[CONV_W - 1 - s:CONV_W - s, cols] * _dot(shifts[s - 1], xext_ref[:, cols])
        xc_ref[:, cols] = conv

    @pl.when(c == 0)
    def _():
        for sl in range(CONV_DIM // SL):
            cols = slice(sl * SL, (sl + 1) * SL)
            corr = None
            for s in range(1, CONV_W):
                t = cw_ref[CONV_W - 1 - s:CONV_W - s, cols] * _dot(shifts[s - 1][0:CONV_TAIL, 0:CONV_TAIL],
                                                                    tlo_ref[:, cols])
                corr = t if corr is None else corr + t
            xc_ref[0:CONV_TAIL, cols] += corr

    @pl.when(c == nchunks - 1)
    def _():
        a = CONV_TAIL * ((nvalid - 1) // CONV_TAIL)
        convout_ref[0] = xext_ref[CONV_TAIL + a:CONV_TAIL + a + CONV_TAIL, :].astype(F32)

    xext_ref[0:CONV_TAIL, :] = xext_ref[L:L + CONV_TAIL, :]

    row = lax.broadcasted_iota(jnp.int32, (L, L), 0)
    col = lax.broadcasted_iota(jnp.int32, (L, L), 1)
    causal = row >= col
    left = col < SSM_HEAD_DIM

    x = dt_ref[0] + dtb_ref[...]
    dt = jnp.maximum(x, 0.0) + jnp.log1p(jnp.exp(-jnp.abs(x)))
    if nvalid < L:
        dt = jnp.where(row < nvalid, dt, 0.0)
    a = -jnp.exp(alog_ref[...])
    da = dt * a
    tri = jnp.where(causal, 1.0, 0.0).astype(BF16)
    cs = sum(_dot(tri, p) for p in _split3(da))
    cs_t = cs.T
    cs_parts = _split3(cs)
    dt_parts = _split3(dt)

    for g in range(SSM_GROUPS):
        gc = slice(g * GW, (g + 1) * GW)
        ex = ex_ref[:, gc]
        csx = sum(_dot(p, ex) for p in cs_parts)
        dtx = sum(_dot(p, ex) for p in dt_parts)
        lastx = csx[L - 1:L, :]
        xs = _silu(xc_ref[:, gc])
        xdt = xs * dtx
        xdt_b = xdt.astype(BF16)
        xdtw_b = (xdt * jnp.exp(lastx - csx)).astype(BF16)
        bg = _silu(xc_ref[:, SSM_D + g * SSM_STATE:SSM_D + (g + 1) * SSM_STATE]).astype(BF16)
        cg = _silu(xc_ref[:, SSM_D + SSM_GROUPS * SSM_STATE + g * SSM_STATE:
                          SSM_D + SSM_GROUPS * SSM_STATE + (g + 1) * SSM_STATE]).astype(BF16)
        cb = _nt_dot(cg, bg)
        htg = ht_ref[:, gc]
        y_st = _dot(cg, htg.astype(BF16)) * jnp.exp(csx)
        pairs = []
        for pr in range(2):
            xp = xdt_b[:, pr * LANES:(pr + 1) * LANES]
            acc = None
            for e2 in range(2):
                h = g * (SSM_HEADS // SSM_GROUPS) + pr * 2 + e2
                seg = cs[:, h:h + 1] - cs_t[h:h + 1, :]
                decay = jnp.exp(jnp.where(causal, seg, NEG_BIG))
                m = (cb * decay).astype(BF16)
                xm = jnp.where(left if e2 == 0 else jnp.logical_not(left), xp, jnp.zeros_like(xp))
                t = _dot(m, xm)
                acc = t if acc is None else acc + t
            pairs.append(acc)
        y_in = jnp.concatenate(pairs, axis=1)
        ht_ref[:, gc] = htg * jnp.exp(lastx) + _tn_dot(bg, xdtw_b)
        y = (y_in + y_st + dskip_ref[:, gc] * xs)
        zz = z_ref[0, :, gc].astype(F32)
        y = y * _silu(zz)
        ms = jnp.mean(y * y, axis=-1, keepdims=True)
        y_ref[0, :, gc] = (y * lax.rsqrt(ms + NORM_EPS) * gn_ref[:, gc]).astype(BF16)

    @pl.when(c == nchunks - 1)
    def _():
        for k in range(SSM_D // LANES):
            hout_ref[0, k * LANES:(k + 1) * LANES, :] = ht_ref[:, k * LANES:(k + 1) * LANES].T


def _ssd(proj, dt_raw, tail0, h0, p, nvalid):
    b, t, _ = proj.shape
    nchunks = t // CHUNK
    L = CHUNK
    const = lambda shape: pl.BlockSpec(shape, lambda i, c: (0,) * len(shape))
    return pl.pallas_call(
        functools.partial(_ssd_kernel, nvalid=nvalid, nchunks=nchunks),
        grid=(b, nchunks),
        in_specs=[
            pl.BlockSpec((1, L, CONV_DIM), lambda i, c: (i, c, C_XBC // CONV_DIM)),
            pl.BlockSpec((1, L, SSM_D), lambda i, c: (i, c, C_Z // SSM_D)),
            pl.BlockSpec((1, L, LANES), lambda i, c: (i, c, 0)),
            pl.BlockSpec((1, CONV_TAIL, CONV_DIM), lambda i, c: (i, 0, 0)),
            pl.BlockSpec((1, SSM_D, SSM_STATE), lambda i, c: (i, 0, 0)),
            const((CONV_W, CONV_DIM)), const((1, CONV_DIM)), const((1, LANES)), const((1, LANES)),
            const((1, SSM_D)), const((1, SSM_D)), const((LANES, SSM_D)),
        ],
        out_specs=[
            pl.BlockSpec((1, L, SSM_D), lambda i, c: (i, c, 0)),
            pl.BlockSpec((1, SSM_D, SSM_STATE), lambda i, c: (i, 0, 0)),
            pl.BlockSpec((1, CONV_TAIL, CONV_DIM), lambda i, c: (i, 0, 0)),
        ],
        out_shape=[
            jax.ShapeDtypeStruct((b, t, SSM_D), BF16),
            jax.ShapeDtypeStruct((b, SSM_D, SSM_STATE), F32),
            jax.ShapeDtypeStruct((b, CONV_TAIL, CONV_DIM), F32),
        ],
        scratch_shapes=[
            pltpu.VMEM((CONV_TAIL + L, CONV_DIM), BF16),
            pltpu.VMEM((CONV_TAIL, CONV_DIM), BF16),
            pltpu.VMEM((L, CONV_DIM), F32),
            pltpu.VMEM((SSM_STATE, SSM_D), F32),
        ],
        compiler_params=_cparams("arbitrary", "arbitrary"),
        name="ssd",
    )(proj, proj, dt_raw, tail0, h0, p["conv_w"], p["conv_b"], p["dt_bias"], p["a_log"], p["d_skip"],
      p["ssm_norm"], p["head_expand"])


def _ret_kernel(q_ref, k_ref, v_ref, rg_ref, cos_ref, sin_ref, s0_ref, gn_ref, y_ref, sout_ref, *, ltrue):
    c = pl.program_id(1)
    L = CHUNK

    @pl.when(c == 0)
    def _():
        sout_ref[...] = s0_ref[...]

    row = lax.broadcasted_iota(jnp.int32, (L, L), 0)
    col = lax.broadcasted_iota(jnp.int32, (L, L), 1)
    rel = (row - col).astype(F32)
    idx = row[:, 0:1].astype(F32)
    cosf = cos_ref[...]
    sins = sin_ref[...]
    for h in range(RET_HEADS):
        lg = math.log(1.0 - 2.0 ** (-5.0 - h))
        dmat = jnp.where(rel >= 0, jnp.exp(jnp.maximum(rel, 0.0) * lg), 0.0)
        q_dec = jnp.exp((idx + 1.0) * lg)
        k_dec = jnp.exp((ltrue - 1.0 - idx) * lg)
        c_dec = math.exp(ltrue * lg)
        kc = slice(h * RET_DK, (h + 1) * RET_DK)
        vc = slice(h * RET_DV, (h + 1) * RET_DV)
        qh = q_ref[0, :, kc].astype(F32)
        kh = k_ref[0, :, kc].astype(F32)
        qr = qh * cosf + pltpu.roll(qh, RET_DK // 2, 1) * sins
        kr = (kh * cosf + pltpu.roll(kh, RET_DK // 2, 1) * sins) * RET_DK ** -0.5
        qr_b = qr.astype(BF16)
        vh = v_ref[0, :, vc]
        att = _nt_dot(qr_b, kr.astype(BF16)) * dmat
        s_old = sout_ref[0, h]
        o = _dot(att.astype(BF16), vh) + _dot(qr_b, s_old.astype(BF16)) * q_dec
        sout_ref[0, h] = s_old * c_dec + _tn_dot((kr * k_dec).astype(BF16), vh)
        oc = o - jnp.mean(o, axis=-1, keepdims=True)
        on = oc * lax.rsqrt(jnp.mean(oc * oc, axis=-1, keepdims=True) + NORM_EPS)
        y_ref[0, :, vc] = (on * gn_ref[:, vc] * _silu(rg_ref[0, :, vc].astype(F32))).astype(BF16)


def _retention(proj, cos_t, sin_t, s0, gn, ltrue):
    b, t, _ = proj.shape
    nchunks = t // CHUNK
    L = CHUNK
    qk_w = RET_HEADS * RET_DK
    v_w = RET_HEADS * RET_DV
    return pl.pallas_call(
        functools.partial(_ret_kernel, ltrue=float(ltrue)),
        grid=(b, nchunks),
        in_specs=[
            pl.BlockSpec((1, L, qk_w), lambda i, c: (i, c, C_RQ // qk_w)),
            pl.BlockSpec((1, L, qk_w), lambda i, c: (i, c, C_RK // qk_w)),
            pl.BlockSpec((1, L, v_w), lambda i, c: (i, c, C_RV // v_w)),
            pl.BlockSpec((1, L, v_w), lambda i, c: (i, c, C_RG // v_w)),
            pl.BlockSpec((L, RET_DK), lambda i, c: (c, 0)),
            pl.BlockSpec((L, RET_DK), lambda i, c: (c, 0)),
            pl.BlockSpec((1, RET_HEADS, RET_DK, RET_DV), lambda i, c: (i, 0, 0, 0)),
            pl.BlockSpec((1, v_w), lambda i, c: (0, 0)),
        ],
        out_specs=[
            pl.BlockSpec((1, L, v_w), lambda i, c: (i, c, 0)),
            pl.BlockSpec((1, RET_HEADS, RET_DK, RET_DV), lambda i, c: (i, 0, 0, 0)),
        ],
        out_shape=[
            jax.ShapeDtypeStruct((b, t, v_w), BF16),
            jax.ShapeDtypeStruct((b, RET_HEADS, RET_DK, RET_DV), F32),
        ],
        compiler_params=_cparams("arbitrary", "arbitrary"),
        name="retention",
    )(proj, proj, proj, proj, cos_t, sin_t, s0, gn)


def _bias_kernel(tab_ref, idx_ref, out_ref, *, scale):
    h = pl.program_id(0)
    idx = idx_ref[...]
    acc = jnp.zeros(idx.shape, F32)
    for b in range(N_BUCKETS):
        acc = acc + jnp.where(idx == b, tab_ref[b * DIFF_HEADS + h], 0.0)
    out_ref[0] = acc * scale


def _bias_tiles(rel_bias, idx, scale):
    r, c = idx.shape
    return pl.pallas_call(
        functools.partial(_bias_kernel, scale=scale),
        grid=(DIFF_HEADS,),
        in_specs=[pl.BlockSpec(memory_space=pltpu.SMEM), pl.BlockSpec((r, c), lambda h: (0, 0))],
        out_specs=pl.BlockSpec((1, r, c), lambda h: (h, 0, 0)),
        out_shape=jax.ShapeDtypeStruct((DIFF_HEADS, r, c), F32),
        compiler_params=_cparams("arbitrary"),
        name="t5_bias",
    )(rel_bias.reshape(-1), idx)


def _half_rmsnorm(x, gain, bd):
    x2 = x * x
    hi = x2.astype(BF16)
    lo = (x2 - hi.astype(F32)).astype(BF16)
    ss = _dot(hi, bd) + _dot(lo, bd)
    return x * lax.rsqrt(ss * (1.0 / DIFF_HD) + NORM_EPS) * gain


def _lambda(lamv_ref, lam_init):
    s1 = jnp.sum(lamv_ref[0:1, :] * lamv_ref[1:2, :], axis=-1, keepdims=True)
    s2 = jnp.sum(lamv_ref[2:3, :] * lamv_ref[3:4, :], axis=-1, keepdims=True)
    return jnp.exp(s1) - jnp.exp(s2) + lam_init


FAR_UNITS = 4


def _dattn_kernel(q_ref, kf_ref, vf_ref, bias_ref, gq_ref, gk_ref, lamv_ref, sgc_ref, bd_ref,
                  y_ref, kout_ref, kn_ref, vt_ref, qs_ref, m_ref, l_ref, acc_ref, *, t, lam_init):
    qi = pl.program_id(1)
    TQ = CHUNK
    G = DIFF_KV_HEADS
    R = DIFF_HEADS // DIFF_KV_HEADS
    NC = 2 * R * TQ
    bd = bd_ref[...]

    @pl.when(qi == 0)
    def _():
        def body(i, carry):
            r = pl.ds(pl.multiple_of(i * TQ, TQ), TQ)
            for g in range(G):
                gc = slice(g * LANES, (g + 1) * LANES)
                kn = _half_rmsnorm(kf_ref[0, r, gc], gk_ref[...], bd)
                kout_ref[0, r, gc] = kn
                kn_ref[g, r, :] = kn.astype(BF16)
                vt_ref[g, i] = vf_ref[0, r, gc].T.astype(BF16)
            return carry

        lax.fori_loop(0, t // TQ, body, 0)

    lane = lax.broadcasted_iota(jnp.int32, (TQ, LANES), 1)
    left = lane < DIFF_HD
    for g in range(G):
        for r in range(R):
            hc = slice((g * R + r) * LANES, (g * R + r + 1) * LANES)
            qn = _half_rmsnorm(q_ref[0, :, hc].astype(F32), gq_ref[...], bd) * (DIFF_HD ** -0.5 * LOG2E)
            qs_ref[g, (0 * R + r) * TQ:(0 * R + r + 1) * TQ, :] = jnp.where(left, qn, 0.0).astype(BF16)
            qs_ref[g, (1 * R + r) * TQ:(1 * R + r + 1) * TQ, :] = jnp.where(left, 0.0, qn).astype(BF16)
    m_ref[...] = jnp.full(m_ref.shape, NEG_BIG, F32)
    l_ref[...] = jnp.zeros(l_ref.shape, F32)
    acc_ref[...] = jnp.zeros(acc_ref.shape, F32)

    def step(k0, nunits, tile, diag):
        tk = nunits * TQ
        rows = pl.ds(pl.multiple_of(k0 * TQ, TQ), tk)
        ss = [_nt_dot(kn_ref[g, rows, :], qs_ref[g]) for g in range(G)]
        ps, alphas = [], []
        for g in range(G):
            bias = bias_ref[g, tile]
            s = ss[g] + (bias if nunits == 1 else jnp.concatenate([bias] * nunits, axis=0))
            if diag:
                key = lax.broadcasted_iota(jnp.int32, (TQ, NC), 0)
                qry = lax.broadcasted_iota(jnp.int32, (TQ, NC), 1) & (TQ - 1)
                s = jnp.where(key <= qry, s, NEG_BIG)
            m_old = m_ref[g]
            m_new = jnp.maximum(m_old, jnp.max(s, axis=0, keepdims=True))
            alpha = jnp.exp2(m_old - m_new)
            p = jnp.exp2(s - m_new)
            l_ref[g] = alpha * l_ref[g] + jnp.sum(p, axis=0, keepdims=True)
            m_ref[g] = m_new
            ps.append(p.astype(BF16))
            alphas.append(alpha)
        for g in range(G):
            if nunits == 1:
                vt = vt_ref[g, k0]
            else:
                vt = jnp.concatenate([vt_ref[g, k0 + u] for u in range(nunits)], axis=1)
            acc_ref[g] = acc_ref[g] * alphas[g] + _dot(vt, ps[g])

    nfar = jnp.maximum(qi - 1, 0)
    nbig = nfar // FAR_UNITS

    def far_big(i, carry):
        step(i * FAR_UNITS, FAR_UNITS, 2, False)
        return carry

    def far_one(i, carry):
        step(nbig * FAR_UNITS + i, 1, 2, False)
        return carry

    lax.fori_loop(0, nbig, far_big, 0)
    lax.fori_loop(0, nfar - nbig * FAR_UNITS, far_one, 0)

    @pl.when(qi >= 1)
    def _():
        step(qi - 1, 1, 1, False)

    step(qi, 1, 0, True)

    lam = _lambda(lamv_ref, lam_init)
    for g in range(G):
        inv_l = 1.0 / l_ref[g]
        for r in range(R):
            c0 = slice((0 * R + r) * TQ, (0 * R + r + 1) * TQ)
            c1 = slice((1 * R + r) * TQ, (1 * R + r + 1) * TQ)
            o = acc_ref[g, :, c0] * inv_l[:, c0] - lam * (acc_ref[g, :, c1] * inv_l[:, c1])
            o = o * lax.rsqrt(jnp.mean(o * o, axis=0, keepdims=True) + NORM_EPS)
            o = o * sgc_ref[...] * (1.0 - lam_init)
            y_ref[0, :, (g * R + r) * LANES:(g * R + r + 1) * LANES] = o.T.astype(BF16)


def _diff_attention_prompt(proj, kf, vf, bias, p, lam_init):
    b, t, _ = proj.shape
    TQ = CHUNK
    G = DIFF_KV_HEADS
    R = DIFF_HEADS // DIFF_KV_HEADS
    NC = 2 * R * TQ
    qw = DIFF_HEADS * 2 * DIFF_HD
    const = lambda shape: pl.BlockSpec(shape, lambda i, q: (0,) * len(shape))
    return pl.pallas_call(
        functools.partial(_dattn_kernel, t=t, lam_init=lam_init),
        grid=(b, t // TQ),
        in_specs=[
            pl.BlockSpec((1, TQ, qw), lambda i, q: (i, q, C_DQ // qw)),
            pl.BlockSpec((1, t, KV_COLS), lambda i, q: (i, 0, 0)),
            pl.BlockSpec((1, t, KV_COLS), lambda i, q: (i, 0, 0)),
            const((G, 3, TQ, NC)),
            const((1, LANES)), const((1, LANES)), const((4, LANES)), const((DIFF_VD, 1)), const((LANES, LANES)),
        ],
        out_specs=[
            pl.BlockSpec((1, TQ, qw), lambda i, q: (i, q, 0)),
            pl.BlockSpec((1, t, KV_COLS), lambda i, q: (i, 0, 0)),
        ],
        out_shape=[
            jax.ShapeDtypeStruct((b, t, DIFF_HEADS * DIFF_VD), BF16),
            jax.ShapeDtypeStruct((b, t, KV_COLS), F32),
        ],
        scratch_shapes=[
            pltpu.VMEM((G, t, LANES), BF16), pltpu.VMEM((G, t // TQ, DIFF_VD, TQ), BF16),
            pltpu.VMEM((G, NC, LANES), BF16),
            pltpu.VMEM((G, 1, NC), F32), pltpu.VMEM((G, 1, NC), F32),
            pltpu.VMEM((G, DIFF_VD, NC), F32),
        ],
        compiler_params=_cparams("arbitrary", "arbitrary"),
        name="diff_attn_prompt",
    )(proj, kf, vf, bias, p["qk_norm_q"], p["qk_norm_k"], p["lamv"], p["diff_norm_col"], p["blockdiag"])


PAGES_PER_STEP = 16


def _sattn_kernel(pt_ref, *refs, layer, nvalid, lam_init, npp):
    del pt_ref, layer
    k_refs = refs[0:npp]
    v_refs = refs[npp:2 * npp]
    (q_ref, kf_ref, vf_ref, bias_ref, gq_ref, gk_ref, lamv_ref, sg_ref, bd_ref,
     y_ref, kout_ref, qs_ref, knew_ref, vnew_ref, m_ref, l_ref, acc_ref) = refs[2 * npp:]
    s_id = pl.program_id(1)
    nsteps = pl.num_programs(1)
    G = DIFF_KV_HEADS
    R = DIFF_HEADS // DIFF_KV_HEADS
    TP = SUBLANES
    MR = 2 * R * TP
    bd = bd_ref[...]
    lane = lax.broadcasted_iota(jnp.int32, (TP, LANES), 1)
    left = lane < DIFF_HD

    @pl.when(s_id == 0)
    def _():
        knew_ref[...] = jnp.zeros(knew_ref.shape, BF16)
        vnew_ref[...] = jnp.zeros(vnew_ref.shape, BF16)
        for g in range(G):
            gc = slice(g * LANES, (g + 1) * LANES)
            kn = _half_rmsnorm(kf_ref[0, :, gc], gk_ref[...], bd)
            kout_ref[0, :, gc] = kn
            knew_ref[g, 0:TP, :] = kn.astype(BF16)
            vnew_ref[g, 0:TP, :] = vf_ref[0, :, gc].astype(BF16)
            for r in range(R):
                hc = slice((g * R + r) * LANES, (g * R + r + 1) * LANES)
                qn = _half_rmsnorm(q_ref[0, :, hc].astype(F32), gq_ref[...], bd) * DIFF_HD ** -0.5
                qs_ref[g, (0 * R + r) * TP:(0 * R + r + 1) * TP, :] = jnp.where(left, qn, 0.0).astype(BF16)
                qs_ref[g, (1 * R + r) * TP:(1 * R + r + 1) * TP, :] = jnp.where(left, 0.0, qn).astype(BF16)
        m_ref[...] = jnp.full(m_ref.shape, NEG_BIG, F32)
        l_ref[...] = jnp.zeros(l_ref.shape, F32)
        acc_ref[...] = jnp.zeros(acc_ref.shape, F32)

    def bias_rows(g, seg):
        per_head = [bias_ref[g * R + r, :, seg * LANES:(seg + 1) * LANES] for r in range(R)]
        return jnp.concatenate(per_head + per_head, axis=0)

    def update(g, s, vs):
        m_old = m_ref[g]
        m_new = jnp.maximum(m_old, jnp.max(s, axis=-1, keepdims=True))
        alpha = jnp.exp(m_old - m_new)
        p = jnp.exp(s - m_new)
        l_ref[g] = alpha * l_ref[g] + jnp.sum(p, axis=-1, keepdims=True)
        m_ref[g] = m_new
        pb = p.astype(BF16)
        pv = _dot(pb[:, 0:LANES], vs[0])
        for i in range(1, len(vs)):
            pv = pv + _dot(pb[:, i * LANES:(i + 1) * LANES], vs[i])
        acc_ref[g] = acc_ref[g] * alpha + pv

    def pages(last):
        scores = []
        for g in range(G):
            gc = slice(g * LANES, (g + 1) * LANES)
            kcat = jnp.concatenate([k_refs[i][gc, :].astype(BF16) for i in range(npp)], axis=1)
            far = bias_rows(g, 0)
            near = bias_rows(g, 1) if last else far
            scores.append(_dot(qs_ref[g], kcat) + jnp.concatenate([far] * (npp - 1) + [near], axis=1))
        ps, alphas = [], []
        for g in range(G):
            s = scores[g]
            m_old = m_ref[g]
            m_new = jnp.maximum(m_old, jnp.max(s, axis=-1, keepdims=True))
            alpha = jnp.exp(m_old - m_new)
            p = jnp.exp(s - m_new)
            l_ref[g] = alpha * l_ref[g] + jnp.sum(p, axis=-1, keepdims=True)
            m_ref[g] = m_new
            ps.append(p.astype(BF16))
            alphas.append(alpha)
        for g in range(G):
            vcat = jnp.concatenate([v_refs[i][pl.ds(g, PAGE_SIZE, stride=G), :].astype(BF16) for i in range(npp)],
                                   axis=0)
            acc_ref[g] = acc_ref[g] * alphas[g] + _dot(ps[g], vcat)

    @pl.when(s_id < nsteps - 1)
    def _():
        pages(False)

    @pl.when(s_id == nsteps - 1)
    def _():
        pages(True)
        rowt = lax.broadcasted_iota(jnp.int32, (MR, LANES), 0) % TP
        colj = lax.broadcasted_iota(jnp.int32, (MR, LANES), 1)
        ok = jnp.logical_and(colj <= rowt, colj < nvalid)
        lam = _lambda(lamv_ref, lam_init)
        for g in range(G):
            sc = _nt_dot(qs_ref[g], knew_ref[g]) + bias_rows(g, 2)
            update(g, jnp.where(ok, sc, NEG_BIG), [vnew_ref[g]])
            acc = acc_ref[g] / l_ref[g]
            for r in range(R):
                o = acc[(0 * R + r) * TP:(0 * R + r + 1) * TP, :] - lam * acc[(1 * R + r) * TP:(1 * R + r + 1) * TP, :]
                o = o * lax.rsqrt(jnp.mean(o * o, axis=-1, keepdims=True) + NORM_EPS)
                hc = slice((g * R + r) * LANES, (g * R + r + 1) * LANES)
                y_ref[0, :, hc] = (o * sg_ref[...] * (1.0 - lam_init)).astype(BF16)


def _diff_attention_sample(proj, kf, vf, cache_k, cache_v, page_table, bias, p, layer, nvalid, lam_init):
    b, tp, _ = proj.shape
    n_pages = page_table.shape[1]
    npp = PAGES_PER_STEP
    while n_pages % npp:
        npp //= 2
    nsteps = n_pages // npp
    G = DIFF_KV_HEADS
    R = DIFF_HEADS // DIFF_KV_HEADS
    MR = 2 * R * tp
    ck = jnp.transpose(cache_k, (0, 1, 3, 4, 5, 2)).reshape(cache_k.shape[0], cache_k.shape[1], KV_COLS, PAGE_SIZE)
    cv = cache_v.reshape(cache_v.shape[0], cache_v.shape[1], PAGE_SIZE * DIFF_KV_HEADS, DIFF_VD)

    def page_spec(i):
        return pl.BlockSpec((None, None, KV_COLS, PAGE_SIZE),
                            lambda bi, s, pt: (layer, pt[bi * n_pages + s * npp + i], 0, 0))

    const = lambda shape: pl.BlockSpec(shape, lambda bi, s, pt: (0,) * len(shape))
    grid_spec = pltpu.PrefetchScalarGridSpec(
        num_scalar_prefetch=1,
        grid=(b, nsteps),
        in_specs=[page_spec(i) for i in range(npp)] + [page_spec(i) for i in range(npp)] + [
            pl.BlockSpec((1, tp, DIFF_HEADS * 2 * DIFF_HD), lambda bi, s, pt: (bi, 0, C_DQ // (DIFF_HEADS * 2 * DIFF_HD))),
            pl.BlockSpec((1, tp, KV_COLS), lambda bi, s, pt: (bi, 0, 0)),
            pl.BlockSpec((1, tp, KV_COLS), lambda bi, s, pt: (bi, 0, 0)),
            const((DIFF_HEADS, tp, 3 * LANES)),
            const((1, LANES)), const((1, LANES)), const((4, LANES)), const((1, LANES)), const((LANES, LANES)),
        ],
        out_specs=[
            pl.BlockSpec((1, tp, DIFF_HEADS * DIFF_VD), lambda bi, s, pt: (bi, 0, 0)),
            pl.BlockSpec((1, tp, KV_COLS), lambda bi, s, pt: (bi, 0, 0)),
        ],
        scratch_shapes=[
            pltpu.VMEM((G, MR, LANES), BF16),
            pltpu.VMEM((G, PAGE_SIZE, LANES), BF16), pltpu.VMEM((G, PAGE_SIZE, LANES), BF16),
            pltpu.VMEM((G, MR, 1), F32), pltpu.VMEM((G, MR, 1), F32), pltpu.VMEM((G, MR, LANES), F32),
        ],
    )
    return pl.pallas_call(
        functools.partial(_sattn_kernel, layer=layer, nvalid=nvalid, lam_init=lam_init, npp=npp),
        grid_spec=grid_spec,
        out_shape=[
            jax.ShapeDtypeStruct((b, tp, DIFF_HEADS * DIFF_VD), BF16),
            jax.ShapeDtypeStruct((b, tp, KV_COLS), F32),
        ],
        compiler_params=_cparams("arbitrary", "arbitrary"),
        name="diff_attn_sample",
    )(page_table.reshape(-1), *([ck] * npp), *([cv] * npp), proj, kf, vf, bias,
      p["qk_norm_q"], p["qk_norm_k"], p["lamv"], p["diff_norm"], p["blockdiag"])


def _merge_kernel(ys_ref, yr_ref, yd_ref, g0_ref, g1_ref, g2_ref, x_ref, ws_ref, wr_ref, wd_ref, wo_ref,
                  bg_ref, h_ref):
    merged = None
    for i, (y_ref, w_ref, g_ref) in enumerate(((ys_ref, ws_ref, g0_ref), (yr_ref, wr_ref, g1_ref),
                                               (yd_ref, wd_ref, g2_ref))):
        br = _dot(y_ref[...], w_ref[...])
        t = _sigmoid(g_ref[...].astype(F32) + bg_ref[i:i + 1, :]) * br
        merged = t if merged is None else merged + t
    h_ref[...] = x_ref[...] + _dot(merged.astype(BF16), wo_ref[...])


def _merge(y_ssm, y_ret, y_diff, proj2d, x2d, p, tm):
    n = x2d.shape[0]
    tok = lambda cb: pl.BlockSpec((tm, D_MODEL), lambda i: (i, cb))
    wspec = pl.BlockSpec((D_MODEL, D_MODEL), lambda i: (0, 0))
    g0 = C_GATE // D_MODEL
    return pl.pallas_call(
        _merge_kernel,
        grid=(n // tm,),
        in_specs=[tok(0), tok(0), tok(0), tok(g0), tok(g0 + 1), tok(g0 + 2), tok(0),
                  wspec, wspec, wspec, wspec, pl.BlockSpec((N_BRANCHES, D_MODEL), lambda i: (0, 0))],
        out_specs=tok(0),
        out_shape=jax.ShapeDtypeStruct((n, D_MODEL), F32),
        compiler_params=_cparams("arbitrary"),
        name="merge",
    )(y_ssm, y_ret, y_diff, proj2d, proj2d, proj2d, x2d, p["w_ssm_out"], p["w_ret_out"], p["w_diff_out"],
      p["w_o"], p["b_gate"])


TF_FFN = D_FF // 2


def _ffn_kernel(h_ref, g_ref, wg_ref, wu_ref, wd_ref, y_ref, hn_ref, acc_ref):
    j = pl.program_id(1)

    @pl.when(j == 0)
    def _():
        h = h_ref[...]
        ms = jnp.mean(h * h, axis=-1, keepdims=True)
        hn_ref[...] = (h * lax.rsqrt(ms + NORM_EPS) * g_ref[...]).astype(BF16)
        acc_ref[...] = h

    hn = hn_ref[...]
    act = _silu(_dot(hn, wg_ref[...])) * _dot(hn, wu_ref[...])
    acc_ref[...] += _dot(act.astype(BF16), wd_ref[...])

    @pl.when(j == pl.num_programs(1) - 1)
    def _():
        y_ref[...] = acc_ref[...]


def _ffn(h2d, p, tm):
    n = h2d.shape[0]
    nj = D_FF // TF_FFN
    return pl.pallas_call(
        _ffn_kernel,
        grid=(n // tm, nj),
        in_specs=[
            pl.BlockSpec((tm, D_MODEL), lambda i, j: (i, 0)),
            pl.BlockSpec((1, D_MODEL), lambda i, j: (0, 0)),
            pl.BlockSpec((D_MODEL, TF_FFN), lambda i, j: (0, j)),
            pl.BlockSpec((D_MODEL, TF_FFN), lambda i, j: (0, nj + j)),
            pl.BlockSpec((TF_FFN, D_MODEL), lambda i, j: (j, 0)),
        ],
        out_specs=pl.BlockSpec((tm, D_MODEL), lambda i, j: (i, 0)),
        out_shape=jax.ShapeDtypeStruct((n, D_MODEL), F32),
        scratch_shapes=[pltpu.VMEM((tm, D_MODEL), BF16), pltpu.VMEM((tm, D_MODEL), F32)],
        compiler_params=_cparams("arbitrary", "arbitrary"),
        name="ffn",
    )(h2d, p["norm_ffn"], p["w_gate_up"], p["w_gate_up"], p["w_down"])


def _t5_bucket(dist):
    n = jnp.maximum(dist, 0)
    max_exact = N_BUCKETS // 2
    large = max_exact + (jnp.log(jnp.maximum(n, 1).astype(F32) / max_exact)
                         / math.log(MAX_DISTANCE / max_exact) * (N_BUCKETS - max_exact)).astype(jnp.int32)
    large = jnp.minimum(large, N_BUCKETS - 1)
    return jnp.where(n < max_exact, n, large)


def _far_bucket_is_constant(min_dist):
    max_exact = N_BUCKETS // 2
    d = np.float32(min_dist)
    large = max_exact + int(np.float32(np.log(d / np.float32(max_exact))) / np.float32(math.log(MAX_DISTANCE / max_exact))
                            * (N_BUCKETS - max_exact))
    return min_dist >= max_exact and large >= N_BUCKETS - 1


def _rope_tables(pos):
    half = RET_DK // 2
    inv = 1.0 / (ROPE_BASE ** (jnp.arange(half, dtype=F32) / half))
    ang = pos.astype(F32)[:, None] * inv[None, :]
    cos, sin = jnp.cos(ang), jnp.sin(ang)
    return jnp.concatenate([cos, cos], axis=1), jnp.concatenate([-sin, sin], axis=1)


def _layer_params(l, w_in, named):
    p = {k: v[l] for k, v in named.items()}
    w = w_in[l]
    o = (0,) + IN_OFFSETS + (w.shape[1],)
    z, xbc, dt, rq, rk, rv, rg, dq, dk, dv, gates = [w[:, o[i]:o[i + 1]] for i in range(len(IN_SPLITS))]
    out = {}
    out["w_all"] = jnp.concatenate([xbc, z, rq, rk, rv, rg, dq, gates, dk, dv], axis=1).astype(BF16)
    out["w_dt"] = jnp.pad(dt, ((0, 0), (0, LANES - SSM_HEADS))).astype(BF16)
    out["norm_mix"] = p["norm_mix"].reshape(1, D_MODEL)
    out["conv_w"] = p["conv_w"]
    out["conv_b"] = p["conv_b"].reshape(1, CONV_DIM)
    out["dt_bias"] = jnp.pad(p["dt_bias"], (0, LANES - SSM_HEADS)).reshape(1, LANES)
    out["a_log"] = jnp.pad(p["a_log"], (0, LANES - SSM_HEADS)).reshape(1, LANES)
    out["d_skip"] = jnp.repeat(p["d_skip"], SSM_HEAD_DIM).reshape(1, SSM_D)
    out["ssm_norm"] = p["ssm_norm"].reshape(1, SSM_D)
    head_of_channel = np.arange(SSM_D) // SSM_HEAD_DIM
    out["head_expand"] = jnp.asarray(np.arange(LANES)[:, None] == head_of_channel[None, :], dtype=BF16)
    out["ret_norm"] = p["ret_norm"].reshape(1, RET_HEADS * RET_DV)
    out["qk_norm_q"] = jnp.tile(p["qk_norm_q"], 2).reshape(1, LANES)
    out["qk_norm_k"] = jnp.tile(p["qk_norm_k"], 2).reshape(1, LANES)
    lamv = jnp.stack([p["lambda_q1"], p["lambda_k1"], p["lambda_q2"], p["lambda_k2"]])
    out["lamv"] = jnp.pad(lamv, ((0, 0), (0, LANES - DIFF_HD)))
    out["diff_norm"] = p["diff_norm"].reshape(1, DIFF_VD)
    out["diff_norm_col"] = p["diff_norm"].reshape(DIFF_VD, 1)
    half = np.arange(LANES) // DIFF_HD
    out["blockdiag"] = jnp.asarray(half[:, None] == half[None, :], dtype=BF16)
    for k in ("w_ssm_out", "w_ret_out", "w_diff_out", "w_o", "w_gate_up", "w_down"):
        out[k] = p[k].astype(BF16)
    out["b_gate"] = p["b_gate"]
    out["norm_ffn"] = p["norm_ffn"].reshape(1, D_MODEL)
    return out


def _token_tile(n, cap):
    tm = min(n, cap)
    while n % tm:
        tm //= 2
    return tm


def _layer_common(x, p, tail0, h0, s0, cos_t, sin_t, nvalid, attn_fn):
    b, tp, _ = x.shape
    n = b * tp
    x2d = x.reshape(n, D_MODEL)
    proj, kf, vf, dt_raw = _inproj(x2d, p["norm_mix"], p["w_all"], p["w_dt"], _token_tile(n, 1024))
    proj3 = proj.reshape(b, tp, C_MAIN)
    dt3 = dt_raw.reshape(b, tp, LANES)
    tpad = -tp % CHUNK
    scan_in = lambda a: jnp.pad(a, ((0, 0), (0, tpad), (0, 0))) if tpad else a
    y_ssm, h_new, conv_rows = _ssd(scan_in(proj3), scan_in(dt3), tail0, h0, p, nvalid)
    y_ret, s_new = _retention(scan_in(proj3), cos_t, sin_t, s0, p["ret_norm"], nvalid)
    y_diff, k_new = attn_fn(proj3, kf.reshape(b, tp, KV_COLS), vf.reshape(b, tp, KV_COLS))
    tm = _token_tile(n, 512)
    h = _merge(y_ssm[:, :tp].reshape(n, -1), y_ret[:, :tp].reshape(n, -1), y_diff.reshape(n, -1), proj, x2d, p, tm)
    y = _ffn(h, p, tm)
    last = (nvalid - 1) % CONV_TAIL
    assert last >= CONV_W - 2
    return (y.reshape(b, tp, D_MODEL), k_new, vf.reshape(b, tp, KV_COLS),
            h_new.reshape(b, SSM_HEADS, SSM_HEAD_DIM, SSM_STATE), conv_rows[:, last - (CONV_W - 2):last + 1, :], s_new)


def kernel(x_prompt, x_sample, cache_k, cache_v, page_table, state_ssm, state_conv, state_ret, rel_bias, norm_mix, w_in, b_gate, conv_w, conv_b, dt_bias, a_log, d_skip, ssm_norm, w_ssm_out, ret_norm, w_ret_out, qk_norm_q, qk_norm_k, lambda_q1, lambda_k1, lambda_q2, lambda_k2, diff_norm, w_diff_out, w_o, norm_ffn, w_gate_up, w_down):
    named = dict(norm_mix=norm_mix, b_gate=b_gate, conv_w=conv_w, conv_b=conv_b, dt_bias=dt_bias, a_log=a_log,
                 d_skip=d_skip, ssm_norm=ssm_norm, w_ssm_out=w_ssm_out, ret_norm=ret_norm, w_ret_out=w_ret_out,
                 qk_norm_q=qk_norm_q, qk_norm_k=qk_norm_k, lambda_q1=lambda_q1, lambda_k1=lambda_k1,
                 lambda_q2=lambda_q2, lambda_k2=lambda_k2, diff_norm=diff_norm, w_diff_out=w_diff_out, w_o=w_o,
                 norm_ffn=norm_ffn, w_gate_up=w_gate_up, w_down=w_down)
    depth = w_in.shape[0]
    bp, seq, _ = x_prompt.shape
    bs, dec, _ = x_sample.shape
    n_pages = page_table.shape[1]
    past = n_pages * PAGE_SIZE
    assert seq % CHUNK == 0 and CONV_W - 1 <= dec <= SUBLANES
    assert _far_bucket_is_constant(CHUNK + 1)

    ii = jnp.arange(CHUNK)[:, None]
    jj = jnp.arange(CHUNK)[None, :]
    idx_p = jnp.concatenate([_t5_bucket(k * CHUNK + jj - ii) for k in range(3)], axis=0)
    bias_h = _bias_tiles(rel_bias, idx_p, LOG2E).reshape(DIFF_KV_HEADS, DIFF_HEADS // DIFF_KV_HEADS, 3, CHUNK, CHUNK)
    bias_p = jnp.concatenate([bias_h[:, r] for r in range(DIFF_HEADS // DIFF_KV_HEADS)] * 2, axis=-1)
    tt = jnp.arange(SUBLANES)[:, None]
    idx_s = jnp.concatenate([_t5_bucket(jnp.broadcast_to(past + tt, (SUBLANES, LANES))),
                             _t5_bucket(tt + PAGE_SIZE - jj), _t5_bucket(tt - jj)], axis=1)
    bias_s = _bias_tiles(rel_bias, idx_s, 1.0)

    cos_p, sin_p = _rope_tables(jnp.arange(seq))
    cos_s, sin_s = _rope_tables(past + jnp.arange(CHUNK))

    xs = jnp.pad(x_sample, ((0, 0), (0, SUBLANES - dec), (0, 0)))
    zeros_tail = jnp.zeros((bp, CONV_TAIL, CONV_DIM), F32)
    zeros_h = jnp.zeros((bp, SSM_D, SSM_STATE), F32)
    zeros_s = jnp.zeros((bp, RET_HEADS, RET_DK, RET_DV), F32)

    yp, ys = x_prompt, xs
    outs_p, outs_s = [], []
    for l in range(depth):
        lam_init = 0.8 - 0.6 * math.exp(-0.3 * l)
        p = _layer_params(l, w_in, named)
        attn_p = lambda proj3, kf, vf: _diff_attention_prompt(proj3, kf, vf, bias_p, p, lam_init)
        yp, k1, v1, h1, c1, r1 = _layer_common(yp, p, zeros_tail, zeros_h, zeros_s, cos_p, sin_p, CHUNK, attn_p)
        outs_p.append((k1.reshape(bp, seq, DIFF_KV_HEADS, 2, DIFF_HD), v1.reshape(bp, seq, DIFF_KV_HEADS, DIFF_VD),
                       h1, c1, r1))

        tail_s = jnp.pad(state_conv[l], ((0, 0), (CONV_TAIL - (CONV_W - 1), 0), (0, 0)))
        h0_s = state_ssm[l].reshape(bs, SSM_D, SSM_STATE)

        def attn_s(proj3, kf, vf, l=l, p=p, lam_init=lam_init):
            return _diff_attention_sample(proj3, kf, vf, cache_k, cache_v, page_table, bias_s, p, l, dec, lam_init)

        ys_new, k2, v2, h2, c2, r2 = _layer_common(ys, p, tail_s, h0_s, state_ret[l], cos_s, sin_s, dec, attn_s)
        ys = jnp.where(jnp.arange(SUBLANES)[None, :, None] < dec, ys_new, 0.0)
        outs_s.append((k2[:, :dec].reshape(bs, dec, DIFF_KV_HEADS, 2, DIFF_HD),
                       v2[:, :dec].reshape(bs, dec, DIFF_KV_HEADS, DIFF_VD), h2, c2, r2))

    stack = lambda outs, i: jnp.stack([o[i] for o in outs])
    return (yp, ys[:, :dec],
            stack(outs_p, 0), stack(outs_p, 1), stack(outs_p, 2), stack(outs_p, 3), stack(outs_p, 4),
            stack(outs_s, 0), stack(outs_s, 1), stack(outs_s, 2), stack(outs_s, 3), stack(outs_s, 4))
```

```python
import functools
import math

import numpy as np
import jax
import jax.numpy as jnp
from jax import lax
from jax.experimental import pallas as pl
from jax.experimental.pallas import tpu as pltpu

F32 = jnp.float32
BF16 = jnp.bfloat16

D_MODEL = 1024
SSM_HEADS = 16
SSM_HEAD_DIM = 64
SSM_D = SSM_HEADS * SSM_HEAD_DIM
SSM_STATE = 128
SSM_GROUPS = 4
CONV_W = 4
CONV_DIM = SSM_D + 2 * SSM_GROUPS * SSM_STATE
RET_HEADS = 4
RET_DK = 128
RET_DV = 256
ROPE_BASE = 10000.0
DIFF_HEADS = 8
DIFF_KV_HEADS = 4
DIFF_HD = 64
DIFF_VD = 2 * DIFF_HD
N_BUCKETS = 32
MAX_DISTANCE = 128
N_BRANCHES = 3
D_FF = 2816
NORM_EPS = 1e-6
PAGE_SIZE = 128

CHUNK = 128
LANES = 128
SUBLANES = 8
CONV_TAIL = 16
NEG_BIG = -1e30
LOG2E = 1.4426950408889634
VMEM_LIMIT = 52 * 1024 * 1024

IN_SPLITS = (SSM_D, CONV_DIM, SSM_HEADS,
             RET_HEADS * RET_DK, RET_HEADS * RET_DK, RET_HEADS * RET_DV, RET_HEADS * RET_DV,
             DIFF_HEADS * 2 * DIFF_HD, DIFF_KV_HEADS * 2 * DIFF_HD, DIFF_KV_HEADS * DIFF_VD,
             N_BRANCHES * D_MODEL)
IN_OFFSETS = tuple(int(v) for v in np.cumsum(IN_SPLITS)[:-1])

C_XBC, C_Z, C_RQ, C_RK, C_RV, C_RG, C_DQ, C_GATE, C_MAIN = 0, 2048, 3072, 3584, 4096, 5120, 6144, 7168, 10240
TN_PROJ = 1024
KV_COLS = DIFF_KV_HEADS * 2 * DIFF_HD


def _cparams(*sem):
    return pltpu.CompilerParams(dimension_semantics=sem, vmem_limit_bytes=VMEM_LIMIT)


def _nt_dot(a, b):
    return lax.dot_general(a, b, (((1,), (1,)), ((), ())), preferred_element_type=F32)


def _tn_dot(a, b):
    return lax.dot_general(a, b, (((0,), (0,)), ((), ())), preferred_element_type=F32)


def _dot(a, b):
    return jnp.dot(a, b, preferred_element_type=F32)


def _split3(x):
    hi = x.astype(BF16)
    r1 = x - hi.astype(F32)
    mid = r1.astype(BF16)
    lo = (r1 - mid.astype(F32)).astype(BF16)
    return hi, mid, lo


def _sigmoid(x):
    return 0.5 * jnp.tanh(0.5 * x) + 0.5


def _silu(x):
    h = 0.5 * x
    return h + h * jnp.tanh(h)


def _inproj_kernel(x_ref, g_ref, w_ref, wdt_ref, vprev_ref, main_ref, kf_ref, vf_ref, dt_ref, xn_ref, *, n_main):
    del vprev_ref
    j = pl.program_id(1)
    tm = x_ref.shape[0]

    @pl.when(j == 0)
    def _():
        x = x_ref[...]
        ms = jnp.mean(x * x, axis=-1, keepdims=True)
        xn = (x * lax.rsqrt(ms + NORM_EPS) * g_ref[...]).astype(BF16)
        xn_ref[...] = xn
        dt_ref[...] = _dot(xn, wdt_ref[...])

    acc = _dot(xn_ref[...], w_ref[...])

    @pl.when(j < n_main)
    def _():
        main_ref[...] = acc.astype(BF16)

    @pl.when(j == n_main)
    def _():
        kf_ref[...] = acc[:, :KV_COLS]
        for g in range(DIFF_KV_HEADS):
            vf_ref[pl.ds(g, tm, stride=DIFF_KV_HEADS), :] = acc[:, KV_COLS + g * DIFF_VD:KV_COLS + (g + 1) * DIFF_VD]


def _inproj(x2d, gain, w_all, w_dt, tm, vbuf, layer):
    n = x2d.shape[0]
    n_main = C_MAIN // TN_PROJ
    grid = (n // tm, n_main + 1)
    return pl.pallas_call(
        functools.partial(_inproj_kernel, n_main=n_main),
        grid=grid,
        in_specs=[
            pl.BlockSpec((tm, D_MODEL), lambda i, j: (i, 0)),
            pl.BlockSpec((1, D_MODEL), lambda i, j: (0, 0)),
            pl.BlockSpec((D_MODEL, TN_PROJ), lambda i, j: (0, j)),
            pl.BlockSpec((D_MODEL, LANES), lambda i, j: (0, 0)),
            pl.BlockSpec(memory_space=pl.ANY),
        ],
        out_specs=[
            pl.BlockSpec((tm, TN_PROJ), lambda i, j: (i, jnp.minimum(j, n_main - 1))),
            pl.BlockSpec((tm, KV_COLS), lambda i, j: (i, 0)),
            pl.BlockSpec((None, tm * DIFF_KV_HEADS, DIFF_VD), lambda i, j: (layer, i, 0)),
            pl.BlockSpec((tm, LANES), lambda i, j: (i, 0)),
        ],
        out_shape=[
            jax.ShapeDtypeStruct((n, C_MAIN), BF16),
            jax.ShapeDtypeStruct((n, KV_COLS), F32),
            jax.ShapeDtypeStruct(vbuf.shape, F32),
            jax.ShapeDtypeStruct((n, LANES), F32),
        ],
        scratch_shapes=[pltpu.VMEM((tm, D_MODEL), BF16)],
        input_output_aliases={4: 2},
        compiler_params=_cparams("arbitrary", "arbitrary"),
        name="inproj",
    )(x2d, gain, w_all, w_dt, vbuf)


def _ssd_kernel(xbc_ref, z_ref, dt_ref, tail0_ref, h0_ref, cw_ref, cb_ref, dtb_ref, alog_ref, dskip_ref,
                gn_ref, ex_ref, y_ref, hout_ref, convout_ref, xext_ref, tlo_ref, xc_ref, ht_ref, *, nvalid, nchunks):
    c = pl.program_id(1)
    L = CHUNK
    GW = SSM_D // SSM_GROUPS
    SL = 512

    @pl.when(c == 0)
    def _():
        t0 = tail0_ref[0]
        t0b = t0.astype(BF16)
        xext_ref[0:CONV_TAIL, :] = t0b
        tlo_ref[...] = (t0 - t0b.astype(F32)).astype(BF16)
        for k in range(SSM_D // LANES):
            ht_ref[:, k * LANES:(k + 1) * LANES] = h0_ref[0, k * LANES:(k + 1) * LANES, :].T

    xext_ref[CONV_TAIL:CONV_TAIL + L, :] = xbc_ref[0]
    srow = lax.broadcasted_iota(jnp.int32, (L, CONV_TAIL + L), 0)
    scol = lax.broadcasted_iota(jnp.int32, (L, CONV_TAIL + L), 1)
    shifts = [jnp.where(scol == srow + CONV_TAIL - s, 1.0, 0.0).astype(BF16) for s in range(1, CONV_W)]
    for sl in range(CONV_DIM // SL):
        cols = slice(sl * SL, (sl + 1) * SL)
        conv = cw_ref[CONV_W - 1:CONV_W, cols] * xbc_ref[0, :, cols].astype(F32) + cb_ref[:, cols]
        for s in range(1, CONV_W):
            conv = conv + cw_ref[CONV_W - 1 - s:CONV_W - s, cols] * _dot(shifts[s - 1], xext_ref[:, cols])
        xc_ref[:, cols] = conv

    @pl.when(c == 0)
    def _():
        for sl in range(CONV_DIM // SL):
            cols = slice(sl * SL, (sl + 1) * SL)
            corr = None
            for s in range(1, CONV_W):
                t = cw_ref[CONV_W - 1 - s:CONV_W - s, cols] * _dot(shifts[s - 1][0:CONV_TAIL, 0:CONV_TAIL],
                                                                    tlo_ref[:, cols])
                corr = t if corr is None else corr + t
            xc_ref[0:CONV_TAIL, cols] += corr

    @pl.when(c == nchunks - 1)
    def _():
        a = CONV_TAIL * ((nvalid - 1) // CONV_TAIL)
        convout_ref[0] = xext_ref[CONV_TAIL + a:CONV_TAIL + a + CONV_TAIL, :].astype(F32)

    xext_ref[0:CONV_TAIL, :] = xext_ref[L:L + CONV_TAIL, :]

    row = lax.broadcasted_iota(jnp.int32, (L, L), 0)
    col = lax.broadcasted_iota(jnp.int32, (L, L), 1)
    causal = row >= col
    left = col < SSM_HEAD_DIM

    x = dt_ref[0] + dtb_ref[...]
    dt = jnp.maximum(x, 0.0) + jnp.log1p(jnp.exp(-jnp.abs(x)))
    if nvalid < L:
        dt = jnp.where(row < nvalid, dt, 0.0)
    a = -jnp.exp(alog_ref[...])
    da = dt * a
    tri = jnp.where(causal, 1.0, 0.0).astype(BF16)
    cs = sum(_dot(tri, p) for p in _split3(da))
    cs_t = cs.T
    cs_parts = _split3(cs)
    dt_parts = _split3(dt)

    NG = SSM_GROUPS
    HPG = SSM_HEADS // SSM_GROUPS
    gcs = [slice(g * GW, (g + 1) * GW) for g in range(NG)]
    bcol = lambda g: slice(SSM_D + g * SSM_STATE, SSM_D + (g + 1) * SSM_STATE)
    ccol = lambda g: slice(SSM_D + NG * SSM_STATE + g * SSM_STATE, SSM_D + NG * SSM_STATE + (g + 1) * SSM_STATE)

    csx = [sum(_dot(p, ex_ref[:, gc]) for p in cs_parts) for gc in gcs]
    dtx = [sum(_dot(p, ex_ref[:, gc]) for p in dt_parts) for gc in gcs]
    bgs = [_silu(xc_ref[:, bcol(g)]).astype(BF16) for g in range(NG)]
    cgs = [_silu(xc_ref[:, ccol(g)]).astype(BF16) for g in range(NG)]
    cbs = [_nt_dot(cgs[g], bgs[g]) for g in range(NG)]
    htgs = [ht_ref[:, gc] for gc in gcs]
    ysts = [_dot(cgs[g], htgs[g].astype(BF16)) for g in range(NG)]
    xss = [_silu(xc_ref[:, gc]) for gc in gcs]
    xdts = [xss[g] * dtx[g] for g in range(NG)]
    xdt_bs = [x_.astype(BF16) for x_ in xdts]
    lastxs = [csx[g][L - 1:L, :] for g in range(NG)]
    xdtw_bs = [(xdts[g] * jnp.exp(lastxs[g] - csx[g])).astype(BF16) for g in range(NG)]
    mats = []
    for h in range(SSM_HEADS):
        seg = cs[:, h:h + 1] - cs_t[h:h + 1, :]
        decay = jnp.exp(jnp.where(causal, seg, NEG_BIG))
        mats.append((cbs[h // HPG] * decay).astype(BF16))
    y_ins = []
    for g in range(NG):
        pairs = []
        for pr in range(HPG // 2):
            xp = xdt_bs[g][:, pr * LANES:(pr + 1) * LANES]
            h0_ = g * HPG + pr * 2
            t0 = _dot(mats[h0_], jnp.where(left, xp, jnp.zeros_like(xp)))
            t1 = _dot(mats[h0_ + 1], jnp.where(left, jnp.zeros_like(xp), xp))
            pairs.append(t0 + t1)
        y_ins.append(jnp.concatenate(pairs, axis=1))
    for g in range(NG):
        ht_ref[:, gcs[g]] = htgs[g] * jnp.exp(lastxs[g]) + _tn_dot(bgs[g], xdtw_bs[g])
    for g in range(NG):
        gc = gcs[g]
        y = y_ins[g] + ysts[g] * jnp.exp(csx[g]) + dskip_ref[:, gc] * xss[g]
        y = y * _silu(z_ref[0, :, gc].astype(F32))
        ms = jnp.mean(y * y, axis=-1, keepdims=True)
        y_ref[0, :, gc] = (y * lax.rsqrt(ms + NORM_EPS) * gn_ref[:, gc]).astype(BF16)

    @pl.when(c == nchunks - 1)
    def _():
        for k in range(SSM_D // LANES):
            hout_ref[0, k * LANES:(k + 1) * LANES, :] = ht_ref[:, k * LANES:(k + 1) * LANES].T


def _ret_kernel(q_ref, k_ref, v_ref, rg_ref, cos_ref, sin_ref, s0_ref, gn_ref, y_ref, sout_ref, *, ltrue):
    c = pl.program_id(1)
    L = CHUNK

    @pl.when(c == 0)
    def _():
        sout_ref[...] = s0_ref[...]

    row = lax.broadcasted_iota(jnp.int32, (L, L), 0)
    col = lax.broadcasted_iota(jnp.int32, (L, L), 1)
    rel = (row - col).astype(F32)
    idx = row[:, 0:1].astype(F32)
    cosf = cos_ref[...]
    sins = sin_ref[...]
    H = RET_HEADS
    lgs = [math.log(1.0 - 2.0 ** (-5.0 - h)) for h in range(H)]
    kcs = [slice(h * RET_DK, (h + 1) * RET_DK) for h in range(H)]
    vcs = [slice(h * RET_DV, (h + 1) * RET_DV) for h in range(H)]
    qrs, krs = [], []
    for h in range(H):
        qh = q_ref[0, :, kcs[h]].astype(F32)
        kh = k_ref[0, :, kcs[h]].astype(F32)
        qrs.append(qh * cosf + pltpu.roll(qh, RET_DK // 2, 1) * sins)
        krs.append((kh * cosf + pltpu.roll(kh, RET_DK // 2, 1) * sins) * RET_DK ** -0.5)
    qr_bs = [q_.astype(BF16) for q_ in qrs]
    vhs = [v_ref[0, :, vcs[h]] for h in range(H)]
    s_olds = [sout_ref[0, h] for h in range(H)]
    scores = [_nt_dot(qr_bs[h], krs[h].astype(BF16)) for h in range(H)]
    cross = [_dot(qr_bs[h], s_olds[h].astype(BF16)) for h in range(H)]
    atts = []
    for h in range(H):
        dmat = jnp.where(rel >= 0, jnp.exp(jnp.maximum(rel, 0.0) * lgs[h]), 0.0)
        atts.append((scores[h] * dmat).astype(BF16))
    inner = [_dot(atts[h], vhs[h]) for h in range(H)]
    for h in range(H):
        k_dec = jnp.exp((ltrue - 1.0 - idx) * lgs[h])
        sout_ref[0, h] = s_olds[h] * math.exp(ltrue * lgs[h]) + _tn_dot((krs[h] * k_dec).astype(BF16), vhs[h])
    for h in range(H):
        o = inner[h] + cross[h] * jnp.exp((idx + 1.0) * lgs[h])
        oc = o - jnp.mean(o, axis=-1, keepdims=True)
        on = oc * lax.rsqrt(jnp.mean(oc * oc, axis=-1, keepdims=True) + NORM_EPS)
        y_ref[0, :, vcs[h]] = (on * gn_ref[:, vcs[h]] * _silu(rg_ref[0, :, vcs[h]].astype(F32))).astype(BF16)


N_SSD_IN, N_RET_IN, N_SSD_OUT, N_RET_OUT = 12, 8, 3, 2


def _scan_kernel(*refs, nvalid, nchunks):
    i0 = N_SSD_IN
    i1 = i0 + N_RET_IN
    i2 = i1 + N_SSD_OUT
    i3 = i2 + N_RET_OUT
    _ret_kernel(*refs[i0:i1], *refs[i2:i3], ltrue=float(nvalid))
    _ssd_kernel(*refs[0:i0], *refs[i1:i2], *refs[i3:], nvalid=nvalid, nchunks=nchunks)


def _scans(proj, dt_raw, tail0, h0, cos_t, sin_t, s0, p, nvalid):
    b, t, _ = proj.shape
    nchunks = t // CHUNK
    L = CHUNK
    qk_w = RET_HEADS * RET_DK
    v_w = RET_HEADS * RET_DV
    const = lambda shape: pl.BlockSpec(shape, lambda i, c: (0,) * len(shape))
    ssd_in = [
        pl.BlockSpec((1, L, CONV_DIM), lambda i, c: (i, c, C_XBC // CONV_DIM)),
        pl.BlockSpec((1, L, SSM_D), lambda i, c: (i, c, C_Z // SSM_D)),
        pl.BlockSpec((1, L, LANES), lambda i, c: (i, c, 0)),
        pl.BlockSpec((1, CONV_TAIL, CONV_DIM), lambda i, c: (i, 0, 0)),
        pl.BlockSpec((1, SSM_D, SSM_STATE), lambda i, c: (i, 0, 0)),
        const((CONV_W, CONV_DIM)), const((1, CONV_DIM)), const((1, LANES)), const((1, LANES)),
        const((1, SSM_D)), const((1, SSM_D)), const((LANES, SSM_D)),
    ]
    ret_in = [
        pl.BlockSpec((1, L, qk_w), lambda i, c: (i, c, C_RQ // qk_w)),
        pl.BlockSpec((1, L, qk_w), lambda i, c: (i, c, C_RK // qk_w)),
        pl.BlockSpec((1, L, v_w), lambda i, c: (i, c, C_RV // v_w)),
        pl.BlockSpec((1, L, v_w), lambda i, c: (i, c, C_RG // v_w)),
        pl.BlockSpec((L, RET_DK), lambda i, c: (c, 0)),
        pl.BlockSpec((L, RET_DK), lambda i, c: (c, 0)),
        pl.BlockSpec((1, RET_HEADS, RET_DK, RET_DV), lambda i, c: (i, 0, 0, 0)),
        const((1, v_w)),
    ]
    ssd_out = [
        pl.BlockSpec((1, L, SSM_D), lambda i, c: (i, c, 0)),
        pl.BlockSpec((1, SSM_D, SSM_STATE), lambda i, c: (i, 0, 0)),
        pl.BlockSpec((1, CONV_TAIL, CONV_DIM), lambda i, c: (i, 0, 0)),
    ]
    ret_out = [
        pl.BlockSpec((1, L, v_w), lambda i, c: (i, c, 0)),
        pl.BlockSpec((1, RET_HEADS, RET_DK, RET_DV), lambda i, c: (i, 0, 0, 0)),
    ]
    assert (len(ssd_in), len(ret_in), len(ssd_out), len(ret_out)) == (N_SSD_IN, N_RET_IN, N_SSD_OUT, N_RET_OUT)
    return pl.pallas_call(
        functools.partial(_scan_kernel, nvalid=nvalid, nchunks=nchunks),
        grid=(b, nchunks),
        in_specs=ssd_in + ret_in,
        out_specs=ssd_out + ret_out,
        out_shape=[
            jax.ShapeDtypeStruct((b, t, SSM_D), BF16),
            jax.ShapeDtypeStruct((b, SSM_D, SSM_STATE), F32),
            jax.ShapeDtypeStruct((b, CONV_TAIL, CONV_DIM), F32),
            jax.ShapeDtypeStruct((b, t, v_w), BF16),
            jax.ShapeDtypeStruct((b, RET_HEADS, RET_DK, RET_DV), F32),
        ],
        scratch_shapes=[
            pltpu.VMEM((CONV_TAIL + L, CONV_DIM), BF16),
            pltpu.VMEM((CONV_TAIL, CONV_DIM), BF16),
            pltpu.VMEM((L, CONV_DIM), F32),
            pltpu.VMEM((SSM_STATE, SSM_D), F32),
        ],
        compiler_params=_cparams("arbitrary", "arbitrary"),
        name="scans",
    )(proj, proj, dt_raw, tail0, h0, p["conv_w"], p["conv_b"], p["dt_bias"], p["a_log"], p["d_skip"],
      p["ssm_norm"], p["head_expand"], proj, proj, proj, proj, cos_t, sin_t, s0, p["ret_norm"])


def _bias_kernel(tab_ref, idx_ref, out_ref, *, scale):
    h = pl.program_id(0)
    idx = idx_ref[...]
    acc = jnp.zeros(idx.shape, F32)
    for b in range(N_BUCKETS):
        acc = acc + jnp.where(idx == b, tab_ref[b * DIFF_HEADS + h], 0.0)
    out_ref[0] = acc * scale


def _bias_tiles(rel_bias, idx, scale):
    r, c = idx.shape
    return pl.pallas_call(
        functools.partial(_bias_kernel, scale=scale),
        grid=(DIFF_HEADS,),
        in_specs=[pl.BlockSpec(memory_space=pltpu.SMEM), pl.BlockSpec((r, c), lambda h: (0, 0))],
        out_specs=pl.BlockSpec((1, r, c), lambda h: (h, 0, 0)),
        out_shape=jax.ShapeDtypeStruct((DIFF_HEADS, r, c), F32),
        compiler_params=_cparams("arbitrary"),
        name="t5_bias",
    )(rel_bias.reshape(-1), idx)


def _half_rmsnorm(x, gain, bd):
    x2 = x * x
    hi = x2.astype(BF16)
    lo = (x2 - hi.astype(F32)).astype(BF16)
    ss = _dot(hi, bd) + _dot(lo, bd)
    return x * lax.rsqrt(ss * (1.0 / DIFF_HD) + NORM_EPS) * gain


def _half_rmsnorm_blocks(blocks, gain, bd):
    rows = blocks[0].shape[0]
    y = _half_rmsnorm(jnp.concatenate(blocks, axis=0), gain, bd)
    return [y[i * rows:(i + 1) * rows] for i in range(len(blocks))]


def _lambda(lamv_ref, lam_init):
    s1 = jnp.sum(lamv_ref[0:1, :] * lamv_ref[1:2, :], axis=-1, keepdims=True)
    s2 = jnp.sum(lamv_ref[2:3, :] * lamv_ref[3:4, :], axis=-1, keepdims=True)
    return jnp.exp(s1) - jnp.exp(s2) + lam_init


FAR_UNITS = 4


def _dattn_kernel(q_ref, kf_ref, vf_ref, bias_ref, gq_ref, gk_ref, lamv_ref, sgc_ref, bd_ref, kprev_ref,
                  y_ref, kout_ref, kn_ref, vt_ref, qs_ref, m_ref, l_ref, acc_ref, *, t, lam_init):
    del kprev_ref
    qi = pl.program_id(1)
    TQ = CHUNK
    G = DIFF_KV_HEADS
    R = DIFF_HEADS // DIFF_KV_HEADS
    NC = 2 * R * TQ
    bd = bd_ref[...]

    @pl.when(qi == 0)
    def _():
        for i in range(t // TQ):
            r = slice(i * TQ, (i + 1) * TQ)
            kns = _half_rmsnorm_blocks([kf_ref[0, r, g * LANES:(g + 1) * LANES] for g in range(G)], gk_ref[...], bd)
            for g in range(G):
                kout_ref[0, g * LANES:(g + 1) * LANES, r] = kns[g].T
                kn_ref[g, r, :] = kns[g].astype(BF16)
                vt_ref[g, i] = vf_ref[0, pl.ds(i * TQ * G + g, TQ, stride=G), :].T.astype(BF16)

    lane = lax.broadcasted_iota(jnp.int32, (TQ, LANES), 1)
    left = lane < DIFF_HD
    qns = _half_rmsnorm_blocks([q_ref[0, :, h * LANES:(h + 1) * LANES].astype(F32) for h in range(DIFF_HEADS)],
                               gq_ref[...], bd)
    for g in range(G):
        for r in range(R):
            qn = qns[g * R + r] * (DIFF_HD ** -0.5 * LOG2E)
            qs_ref[g, (0 * R + r) * TQ:(0 * R + r + 1) * TQ, :] = jnp.where(left, qn, 0.0).astype(BF16)
            qs_ref[g, (1 * R + r) * TQ:(1 * R + r + 1) * TQ, :] = jnp.where(left, 0.0, qn).astype(BF16)
    m_ref[...] = jnp.full(m_ref.shape, NEG_BIG, F32)
    l_ref[...] = jnp.zeros(l_ref.shape, F32)
    acc_ref[...] = jnp.zeros(acc_ref.shape, F32)

    def step(k0, nunits, tile, diag):
        tk = nunits * TQ
        rows = pl.ds(pl.multiple_of(k0 * TQ, TQ), tk)
        ss = [_nt_dot(kn_ref[g, rows, :], qs_ref[g]) for g in range(G)]
        ps, alphas = [], []
        for g in range(G):
            bias = bias_ref[g, tile]
            s = ss[g] + (bias if nunits == 1 else jnp.concatenate([bias] * nunits, axis=0))
            if diag:
                key = lax.broadcasted_iota(jnp.int32, (TQ, NC), 0)
                qry = lax.broadcasted_iota(jnp.int32, (TQ, NC), 1) & (TQ - 1)
                s = jnp.where(key <= qry, s, NEG_BIG)
            m_old = m_ref[g]
            m_new = jnp.maximum(m_old, jnp.max(s, axis=0, keepdims=True))
            alpha = jnp.exp2(m_old - m_new)
            p = jnp.exp2(s - m_new)
            l_ref[g] = alpha * l_ref[g] + jnp.sum(p, axis=0, keepdims=True)
            m_ref[g] = m_new
            ps.append(p.astype(BF16))
            alphas.append(alpha)
        for g in range(G):
            if nunits == 1:
                vt = vt_ref[g, k0]
            else:
                vt = jnp.concatenate([vt_ref[g, k0 + u] for u in range(nunits)], axis=1)
            acc_ref[g] = acc_ref[g] * alphas[g] + _dot(vt, ps[g])

    nfar = jnp.maximum(qi - 1, 0)
    nbig = nfar // FAR_UNITS

    def far_big(i, carry):
        step(i * FAR_UNITS, FAR_UNITS, 2, False)
        return carry

    def far_one(i, carry):
        step(nbig * FAR_UNITS + i, 1, 2, False)
        return carry

    lax.fori_loop(0, nbig, far_big, 0)
    lax.fori_loop(0, nfar - nbig * FAR_UNITS, far_one, 0)

    @pl.when(qi >= 1)
    def _():
        step(qi - 1, 1, 1, False)

    step(qi, 1, 0, True)

    lam = _lambda(lamv_ref, lam_init)
    for g in range(G):
        inv_l = 1.0 / l_ref[g]
        for r in range(R):
            c0 = slice((0 * R + r) * TQ, (0 * R + r + 1) * TQ)
            c1 = slice((1 * R + r) * TQ, (1 * R + r + 1) * TQ)
            o = acc_ref[g, :, c0] * inv_l[:, c0] - lam * (acc_ref[g, :, c1] * inv_l[:, c1])
            o = o * lax.rsqrt(jnp.mean(o * o, axis=0, keepdims=True) + NORM_EPS)
            o = o * sgc_ref[...] * (1.0 - lam_init)
            y_ref[0, :, (g * R + r) * LANES:(g * R + r + 1) * LANES] = o.T.astype(BF16)


def _diff_attention_prompt(proj, kf, vbuf, kbuf, bias, p, layer, lam_init):
    b, t, _ = proj.shape
    vf = vbuf.reshape(vbuf.shape[0], b, t * DIFF_KV_HEADS, DIFF_VD)
    TQ = CHUNK
    G = DIFF_KV_HEADS
    R = DIFF_HEADS // DIFF_KV_HEADS
    NC = 2 * R * TQ
    qw = DIFF_HEADS * 2 * DIFF_HD
    const = lambda shape: pl.BlockSpec(shape, lambda i, q: (0,) * len(shape))
    return pl.pallas_call(
        functools.partial(_dattn_kernel, t=t, lam_init=lam_init),
        grid=(b, t // TQ),
        in_specs=[
            pl.BlockSpec((1, TQ, qw), lambda i, q: (i, q, C_DQ // qw)),
            pl.BlockSpec((1, t, KV_COLS), lambda i, q: (i, 0, 0)),
            pl.BlockSpec((None, 1, t * G, DIFF_VD), lambda i, q: (layer, i, 0, 0)),
            const((G, 3, TQ, NC)),
            const((1, LANES)), const((1, LANES)), const((4, LANES)), const((DIFF_VD, 1)), const((LANES, LANES)),
            pl.BlockSpec(memory_space=pl.ANY),
        ],
        out_specs=[
            pl.BlockSpec((1, TQ, qw), lambda i, q: (i, q, 0)),
            pl.BlockSpec((None, 1, KV_COLS, t), lambda i, q: (layer, i, 0, 0)),
        ],
        out_shape=[
            jax.ShapeDtypeStruct((b, t, DIFF_HEADS * DIFF_VD), BF16),
            jax.ShapeDtypeStruct(kbuf.shape, F32),
        ],
        input_output_aliases={9: 1},
        scratch_shapes=[
            pltpu.VMEM((G, t, LANES), BF16), pltpu.VMEM((G, t // TQ, DIFF_VD, TQ), BF16),
            pltpu.VMEM((G, NC, LANES), BF16),
            pltpu.VMEM((G, 1, NC), F32), pltpu.VMEM((G, 1, NC), F32),
            pltpu.VMEM((G, DIFF_VD, NC), F32),
        ],
        compiler_params=_cparams("arbitrary", "arbitrary"),
        name="diff_attn_prompt",
    )(proj, kf, vf, bias, p["qk_norm_q"], p["qk_norm_k"], p["lamv"], p["diff_norm_col"], p["blockdiag"], kbuf)


PAGES_PER_STEP = 16


def _sattn_kernel(pt_ref, *refs, layer, nvalid, lam_init, npp):
    del pt_ref, layer
    k_refs = refs[0:npp]
    v_refs = refs[npp:2 * npp]
    (q_ref, kf_ref, vf_ref, bias_ref, gq_ref, gk_ref, lamv_ref, sg_ref, bd_ref,
     y_ref, kout_ref, qs_ref, knew_ref, vnew_ref, m_ref, l_ref, acc_ref) = refs[2 * npp:]
    s_id = pl.program_id(1)
    nsteps = pl.num_programs(1)
    G = DIFF_KV_HEADS
    R = DIFF_HEADS // DIFF_KV_HEADS
    TP = SUBLANES
    MR = 2 * R * TP
    bd = bd_ref[...]
    lane = lax.broadcasted_iota(jnp.int32, (TP, LANES), 1)
    left = lane < DIFF_HD

    @pl.when(s_id == 0)
    def _():
        knew_ref[...] = jnp.zeros(knew_ref.shape, BF16)
        vnew_ref[...] = jnp.zeros(vnew_ref.shape, BF16)
        for g in range(G):
            gc = slice(g * LANES, (g + 1) * LANES)
            kn = _half_rmsnorm(kf_ref[0, :, gc], gk_ref[...], bd)
            kout_ref[0, :, gc] = kn
            knew_ref[g, 0:TP, :] = kn.astype(BF16)
            vnew_ref[g, 0:TP, :] = vf_ref[0, pl.ds(g, TP, stride=G), :].astype(BF16)
            for r in range(R):
                hc = slice((g * R + r) * LANES, (g * R + r + 1) * LANES)
                qn = _half_rmsnorm(q_ref[0, :, hc].astype(F32), gq_ref[...], bd) * DIFF_HD ** -0.5
                qs_ref[g, (0 * R + r) * TP:(0 * R + r + 1) * TP, :] = jnp.where(left, qn, 0.0).astype(BF16)
                qs_ref[g, (1 * R + r) * TP:(1 * R + r + 1) * TP, :] = jnp.where(left, 0.0, qn).astype(BF16)
        m_ref[...] = jnp.full(m_ref.shape, NEG_BIG, F32)
        l_ref[...] = jnp.zeros(l_ref.shape, F32)
        acc_ref[...] = jnp.zeros(acc_ref.shape, F32)

    def bias_rows(g, seg):
        per_head = [bias_ref[g * R + r, :, seg * LANES:(seg + 1) * LANES] for r in range(R)]
        return jnp.concatenate(per_head + per_head, axis=0)

    def update(g, s, vs):
        m_old = m_ref[g]
        m_new = jnp.maximum(m_old, jnp.max(s, axis=-1, keepdims=True))
        alpha = jnp.exp(m_old - m_new)
        p = jnp.exp(s - m_new)
        l_ref[g] = alpha * l_ref[g] + jnp.sum(p, axis=-1, keepdims=True)
        m_ref[g] = m_new
        pb = p.astype(BF16)
        pv = _dot(pb[:, 0:LANES], vs[0])
        for i in range(1, len(vs)):
            pv = pv + _dot(pb[:, i * LANES:(i + 1) * LANES], vs[i])
        acc_ref[g] = acc_ref[g] * alpha + pv

    def pages(last):
        scores = []
        for g in range(G):
            gc = slice(g * LANES, (g + 1) * LANES)
            kcat = jnp.concatenate([k_refs[i][gc, :].astype(BF16) for i in range(npp)], axis=1)
            far = bias_rows(g, 0)
            near = bias_rows(g, 1) if last else far
            scores.append(_dot(qs_ref[g], kcat) + jnp.concatenate([far] * (npp - 1) + [near], axis=1))
        ps, alphas = [], []
        for g in range(G):
            s = scores[g]
            m_old = m_ref[g]
            m_new = jnp.maximum(m_old, jnp.max(s, axis=-1, keepdims=True))
            alpha = jnp.exp(m_old - m_new)
            p = jnp.exp(s - m_new)
            l_ref[g] = alpha * l_ref[g] + jnp.sum(p, axis=-1, keepdims=True)
            m_ref[g] = m_new
            ps.append(p.astype(BF16))
            alphas.append(alpha)
        for g in range(G):
            vcat = jnp.concatenate([v_refs[i][pl.ds(g, PAGE_SIZE, stride=G), :].astype(BF16) for i in range(npp)],
                                   axis=0)
            acc_ref[g] = acc_ref[g] * alphas[g] + _dot(ps[g], vcat)

    @pl.when(s_id < nsteps - 1)
    def _():
        pages(False)

    @pl.when(s_id == nsteps - 1)
    def _():
        pages(True)
        rowt = lax.broadcasted_iota(jnp.int32, (MR, LANES), 0) % TP
        colj = lax.broadcasted_iota(jnp.int32, (MR, LANES), 1)
        ok = jnp.logical_and(colj <= rowt, colj < nvalid)
        lam = _lambda(lamv_ref, lam_init)
        for g in range(G):
            sc = _nt_dot(qs_ref[g], knew_ref[g]) + bias_rows(g, 2)
            update(g, jnp.where(ok, sc, NEG_BIG), [vnew_ref[g]])
            acc = acc_ref[g] / l_ref[g]
            for r in range(R):
                o = acc[(0 * R + r) * TP:(0 * R + r + 1) * TP, :] - lam * acc[(1 * R + r) * TP:(1 * R + r + 1) * TP, :]
                o = o * lax.rsqrt(jnp.mean(o * o, axis=-1, keepdims=True) + NORM_EPS)
                hc = slice((g * R + r) * LANES, (g * R + r + 1) * LANES)
                y_ref[0, :, hc] = (o * sg_ref[...] * (1.0 - lam_init)).astype(BF16)


def _diff_attention_sample(proj, kf, vbuf, cache_k, cache_v, page_table, bias, p, layer, nvalid, lam_init):
    b, tp, _ = proj.shape
    vf = vbuf.reshape(vbuf.shape[0], b, tp * DIFF_KV_HEADS, DIFF_VD)
    n_pages = page_table.shape[1]
    npp = PAGES_PER_STEP
    while n_pages % npp:
        npp //= 2
    nsteps = n_pages // npp
    G = DIFF_KV_HEADS
    R = DIFF_HEADS // DIFF_KV_HEADS
    MR = 2 * R * tp
    ck = jnp.transpose(cache_k, (0, 1, 3, 4, 5, 2)).reshape(cache_k.shape[0], cache_k.shape[1], KV_COLS, PAGE_SIZE)
    cv = cache_v.reshape(cache_v.shape[0], cache_v.shape[1], PAGE_SIZE * DIFF_KV_HEADS, DIFF_VD)

    def page_spec(i):
        return pl.BlockSpec((None, None, KV_COLS, PAGE_SIZE),
                            lambda bi, s, pt: (layer, pt[bi * n_pages + s * npp + i], 0, 0))

    const = lambda shape: pl.BlockSpec(shape, lambda bi, s, pt: (0,) * len(shape))
    grid_spec = pltpu.PrefetchScalarGridSpec(
        num_scalar_prefetch=1,
        grid=(b, nsteps),
        in_specs=[page_spec(i) for i in range(npp)] + [page_spec(i) for i in range(npp)] + [
            pl.BlockSpec((1, tp, DIFF_HEADS * 2 * DIFF_HD), lambda bi, s, pt: (bi, 0, C_DQ // (DIFF_HEADS * 2 * DIFF_HD))),
            pl.BlockSpec((1, tp, KV_COLS), lambda bi, s, pt: (bi, 0, 0)),
            pl.BlockSpec((None, 1, tp * G, DIFF_VD), lambda bi, s, pt: (layer, bi, 0, 0)),
            const((DIFF_HEADS, tp, 3 * LANES)),
            const((1, LANES)), const((1, LANES)), const((4, LANES)), const((1, LANES)), const((LANES, LANES)),
        ],
        out_specs=[
            pl.BlockSpec((1, tp, DIFF_HEADS * DIFF_VD), lambda bi, s, pt: (bi, 0, 0)),
            pl.BlockSpec((1, tp, KV_COLS), lambda bi, s, pt: (bi, 0, 0)),
        ],
        scratch_shapes=[
            pltpu.VMEM((G, MR, LANES), BF16),
            pltpu.VMEM((G, PAGE_SIZE, LANES), BF16), pltpu.VMEM((G, PAGE_SIZE, LANES), BF16),
            pltpu.VMEM((G, MR, 1), F32), pltpu.VMEM((G, MR, 1), F32), pltpu.VMEM((G, MR, LANES), F32),
        ],
    )
    return pl.pallas_call(
        functools.partial(_sattn_kernel, layer=layer, nvalid=nvalid, lam_init=lam_init, npp=npp),
        grid_spec=grid_spec,
        out_shape=[
            jax.ShapeDtypeStruct((b, tp, DIFF_HEADS * DIFF_VD), BF16),
            jax.ShapeDtypeStruct((b, tp, KV_COLS), F32),
        ],
        compiler_params=_cparams("arbitrary", "arbitrary"),
        name="diff_attn_sample",
    )(page_table.reshape(-1), *([ck] * npp), *([cv] * npp), proj, kf, vf, bias,
      p["qk_norm_q"], p["qk_norm_k"], p["lamv"], p["diff_norm"], p["blockdiag"])


def _merge_kernel(ys_ref, yr_ref, yd_ref, g0_ref, g1_ref, g2_ref, x_ref, ws_ref, wr_ref, wd_ref, wo_ref,
                  bg_ref, h_ref):
    merged = None
    for i, (y_ref, w_ref, g_ref) in enumerate(((ys_ref, ws_ref, g0_ref), (yr_ref, wr_ref, g1_ref),
                                               (yd_ref, wd_ref, g2_ref))):
        br = _dot(y_ref[...], w_ref[...])
        t = _sigmoid(g_ref[...].astype(F32) + bg_ref[i:i + 1, :]) * br
        merged = t if merged is None else merged + t
    h_ref[...] = x_ref[...] + _dot(merged.astype(BF16), wo_ref[...])


def _merge(y_ssm, y_ret, y_diff, proj2d, x2d, p, tm):
    n = x2d.shape[0]
    tok = lambda cb: pl.BlockSpec((tm, D_MODEL), lambda i: (i, cb))
    wspec = pl.BlockSpec((D_MODEL, D_MODEL), lambda i: (0, 0))
    g0 = C_GATE // D_MODEL
    return pl.pallas_call(
        _merge_kernel,
        grid=(n // tm,),
        in_specs=[tok(0), tok(0), tok(0), tok(g0), tok(g0 + 1), tok(g0 + 2), tok(0),
                  wspec, wspec, wspec, wspec, pl.BlockSpec((N_BRANCHES, D_MODEL), lambda i: (0, 0))],
        out_specs=tok(0),
        out_shape=jax.ShapeDtypeStruct((n, D_MODEL), F32),
        compiler_params=_cparams("arbitrary"),
        name="merge",
    )(y_ssm, y_ret, y_diff, proj2d, proj2d, proj2d, x2d, p["w_ssm_out"], p["w_ret_out"], p["w_diff_out"],
      p["w_o"], p["b_gate"])


TF_FFN = D_FF // 2


def _ffn_kernel(h_ref, g_ref, wg_ref, wu_ref, wd_ref, y_ref, hn_ref, acc_ref):
    j = pl.program_id(1)

    @pl.when(j == 0)
    def _():
        h = h_ref[...]
        ms = jnp.mean(h * h, axis=-1, keepdims=True)
        hn_ref[...] = (h * lax.rsqrt(ms + NORM_EPS) * g_ref[...]).astype(BF16)
        acc_ref[...] = h

    hn = hn_ref[...]
    act = _silu(_dot(hn, wg_ref[...])) * _dot(hn, wu_ref[...])
    acc_ref[...] += _dot(act.astype(BF16), wd_ref[...])

    @pl.when(j == pl.num_programs(1) - 1)
    def _():
        y_ref[...] = acc_ref[...]


def _ffn(h2d, p, tm):
    n = h2d.shape[0]
    nj = D_FF // TF_FFN
    return pl.pallas_call(
        _ffn_kernel,
        grid=(n // tm, nj),
        in_specs=[
            pl.BlockSpec((tm, D_MODEL), lambda i, j: (i, 0)),
            pl.BlockSpec((1, D_MODEL), lambda i, j: (0, 0)),
            pl.BlockSpec((D_MODEL, TF_FFN), lambda i, j: (0, j)),
            pl.BlockSpec((D_MODEL, TF_FFN), lambda i, j: (0, nj + j)),
            pl.BlockSpec((TF_FFN, D_MODEL), lambda i, j: (j, 0)),
        ],
        out_specs=pl.BlockSpec((tm, D_MODEL), lambda i, j: (i, 0)),
        out_shape=jax.ShapeDtypeStruct((n, D_MODEL), F32),
        scratch_shapes=[pltpu.VMEM((tm, D_MODEL), BF16), pltpu.VMEM((tm, D_MODEL), F32)],
        compiler_params=_cparams("arbitrary", "arbitrary"),
        name="ffn",
    )(h2d, p["norm_ffn"], p["w_gate_up"], p["w_gate_up"], p["w_down"])


def _t5_bucket(dist):
    n = jnp.maximum(dist, 0)
    max_exact = N_BUCKETS // 2
    large = max_exact + (jnp.log(jnp.maximum(n, 1).astype(F32) / max_exact)
                         / math.log(MAX_DISTANCE / max_exact) * (N_BUCKETS - max_exact)).astype(jnp.int32)
    large = jnp.minimum(large, N_BUCKETS - 1)
    return jnp.where(n < max_exact, n, large)


def _far_bucket_is_constant(min_dist):
    max_exact = N_BUCKETS // 2
    d = np.float32(min_dist)
    large = max_exact + int(np.float32(np.log(d / np.float32(max_exact))) / np.float32(math.log(MAX_DISTANCE / max_exact))
                            * (N_BUCKETS - max_exact))
    return min_dist >= max_exact and large >= N_BUCKETS - 1


def _rope_tables(pos):
    half = RET_DK // 2
    inv = 1.0 / (ROPE_BASE ** (jnp.arange(half, dtype=F32) / half))
    ang = pos.astype(F32)[:, None] * inv[None, :]
    cos, sin = jnp.cos(ang), jnp.sin(ang)
    return jnp.concatenate([cos, cos], axis=1), jnp.concatenate([-sin, sin], axis=1)


def _layer_params(l, w_in, named):
    p = {k: v[l] for k, v in named.items()}
    w = w_in[l]
    o = (0,) + IN_OFFSETS + (w.shape[1],)
    z, xbc, dt, rq, rk, rv, rg, dq, dk, dv, gates = [w[:, o[i]:o[i + 1]] for i in range(len(IN_SPLITS))]
    out = {}
    out["w_all"] = jnp.concatenate([xbc, z, rq, rk, rv, rg, dq, gates, dk, dv], axis=1).astype(BF16)
    out["w_dt"] = jnp.pad(dt, ((0, 0), (0, LANES - SSM_HEADS))).astype(BF16)
    out["norm_mix"] = p["norm_mix"].reshape(1, D_MODEL)
    out["conv_w"] = p["conv_w"]
    out["conv_b"] = p["conv_b"].reshape(1, CONV_DIM)
    out["dt_bias"] = jnp.pad(p["dt_bias"], (0, LANES - SSM_HEADS)).reshape(1, LANES)
    out["a_log"] = jnp.pad(p["a_log"], (0, LANES - SSM_HEADS)).reshape(1, LANES)
    out["d_skip"] = jnp.repeat(p["d_skip"], SSM_HEAD_DIM).reshape(1, SSM_D)
    out["ssm_norm"] = p["ssm_norm"].reshape(1, SSM_D)
    head_of_channel = np.arange(SSM_D) // SSM_HEAD_DIM
    out["head_expand"] = jnp.asarray(np.arange(LANES)[:, None] == head_of_channel[None, :], dtype=BF16)
    out["ret_norm"] = p["ret_norm"].reshape(1, RET_HEADS * RET_DV)
    out["qk_norm_q"] = jnp.tile(p["qk_norm_q"], 2).reshape(1, LANES)
    out["qk_norm_k"] = jnp.tile(p["qk_norm_k"], 2).reshape(1, LANES)
    lamv = jnp.stack([p["lambda_q1"], p["lambda_k1"], p["lambda_q2"], p["lambda_k2"]])
    out["lamv"] = jnp.pad(lamv, ((0, 0), (0, LANES - DIFF_HD)))
    out["diff_norm"] = p["diff_norm"].reshape(1, DIFF_VD)
    out["diff_norm_col"] = p["diff_norm"].reshape(DIFF_VD, 1)
    half = np.arange(LANES) // DIFF_HD
    out["blockdiag"] = jnp.asarray(half[:, None] == half[None, :], dtype=BF16)
    for k in ("w_ssm_out", "w_ret_out", "w_diff_out", "w_o", "w_gate_up", "w_down"):
        out[k] = p[k].astype(BF16)
    out["b_gate"] = p["b_gate"]
    out["norm_ffn"] = p["norm_ffn"].reshape(1, D_MODEL)
    return out


def _token_tile(n, cap):
    tm = min(n, cap)
    while n % tm:
        tm //= 2
    return tm


def _layer_common(x, p, layer, vbuf, tail0, h0, s0, cos_t, sin_t, nvalid, attn_fn):
    b, tp, _ = x.shape
    n = b * tp
    x2d = x.reshape(n, D_MODEL)
    proj, kf, vbuf, dt_raw = _inproj(x2d, p["norm_mix"], p["w_all"], p["w_dt"], _token_tile(n, 1024), vbuf, layer)
    proj3 = proj.reshape(b, tp, C_MAIN)
    dt3 = dt_raw.reshape(b, tp, LANES)
    tpad = -tp % CHUNK
    scan_in = lambda a: jnp.pad(a, ((0, 0), (0, tpad), (0, 0))) if tpad else a
    y_ssm, h_new, conv_rows, y_ret, s_new = _scans(scan_in(proj3), scan_in(dt3), tail0, h0, cos_t, sin_t, s0, p,
                                                   nvalid)
    y_diff, k_new = attn_fn(proj3, kf.reshape(b, tp, KV_COLS), vbuf)
    tm = _token_tile(n, 512)
    h = _merge(y_ssm[:, :tp].reshape(n, -1), y_ret[:, :tp].reshape(n, -1), y_diff.reshape(n, -1), proj, x2d, p, tm)
    y = _ffn(h, p, tm)
    last = (nvalid - 1) % CONV_TAIL
    assert last >= CONV_W - 2
    return (y.reshape(b, tp, D_MODEL), k_new, vbuf,
            h_new.reshape(b, SSM_HEADS, SSM_HEAD_DIM, SSM_STATE), conv_rows[:, last - (CONV_W - 2):last + 1, :], s_new)


def kernel(x_prompt, x_sample, cache_k, cache_v, page_table, state_ssm, state_conv, state_ret, rel_bias, norm_mix, w_in, b_gate, conv_w, conv_b, dt_bias, a_log, d_skip, ssm_norm, w_ssm_out, ret_norm, w_ret_out, qk_norm_q, qk_norm_k, lambda_q1, lambda_k1, lambda_q2, lambda_k2, diff_norm, w_diff_out, w_o, norm_ffn, w_gate_up, w_down):
    named = dict(norm_mix=norm_mix, b_gate=b_gate, conv_w=conv_w, conv_b=conv_b, dt_bias=dt_bias, a_log=a_log,
                 d_skip=d_skip, ssm_norm=ssm_norm, w_ssm_out=w_ssm_out, ret_norm=ret_norm, w_ret_out=w_ret_out,
                 qk_norm_q=qk_norm_q, qk_norm_k=qk_norm_k, lambda_q1=lambda_q1, lambda_k1=lambda_k1,
                 lambda_q2=lambda_q2, lambda_k2=lambda_k2, diff_norm=diff_norm, w_diff_out=w_diff_out, w_o=w_o,
                 norm_ffn=norm_ffn, w_gate_up=w_gate_up, w_down=w_down)
    depth = w_in.shape[0]
    bp, seq, _ = x_prompt.shape
    bs, dec, _ = x_sample.shape
    n_pages = page_table.shape[1]
    past = n_pages * PAGE_SIZE
    assert seq % CHUNK == 0 and CONV_W - 1 <= dec <= SUBLANES
    assert _far_bucket_is_constant(CHUNK + 1)

    ii = jnp.arange(CHUNK)[:, None]
    jj = jnp.arange(CHUNK)[None, :]
    idx_p = jnp.concatenate([_t5_bucket(k * CHUNK + jj - ii) for k in range(3)], axis=0)
    bias_h = _bias_tiles(rel_bias, idx_p, LOG2E).reshape(DIFF_KV_HEADS, DIFF_HEADS // DIFF_KV_HEADS, 3, CHUNK, CHUNK)
    bias_p = jnp.concatenate([bias_h[:, r] for r in range(DIFF_HEADS // DIFF_KV_HEADS)] * 2, axis=-1)
    tt = jnp.arange(SUBLANES)[:, None]
    idx_s = jnp.concatenate([_t5_bucket(jnp.broadcast_to(past + tt, (SUBLANES, LANES))),
                             _t5_bucket(tt + PAGE_SIZE - jj), _t5_bucket(tt - jj)], axis=1)
    bias_s = _bias_tiles(rel_bias, idx_s, 1.0)

    cos_p, sin_p = _rope_tables(jnp.arange(seq))
    cos_s, sin_s = _rope_tables(past + jnp.arange(CHUNK))

    xs = jnp.pad(x_sample, ((0, 0), (0, SUBLANES - dec), (0, 0)))
    zeros_tail = jnp.zeros((bp, CONV_TAIL, CONV_DIM), F32)
    zeros_h = jnp.zeros((bp, SSM_D, SSM_STATE), F32)
    zeros_s = jnp.zeros((bp, RET_HEADS, RET_DK, RET_DV), F32)

    vbuf_p = jnp.zeros((depth, bp * seq * DIFF_KV_HEADS, DIFF_VD), F32)
    vbuf_s = jnp.zeros((depth, bs * SUBLANES * DIFF_KV_HEADS, DIFF_VD), F32)
    kbuf_p = jnp.zeros((depth, bp, KV_COLS, seq), F32)

    yp, ys = x_prompt, xs
    outs_p, outs_s = [], []
    for l in range(depth):
        lam_init = 0.8 - 0.6 * math.exp(-0.3 * l)
        p = _layer_params(l, w_in, named)

        def attn_p(proj3, kf, vbuf, l=l, p=p, lam_init=lam_init, kbuf=kbuf_p):
            return _diff_attention_prompt(proj3, kf, vbuf, kbuf, bias_p, p, l, lam_init)

        yp, kbuf_p, vbuf_p, h1, c1, r1 = _layer_common(yp, p, l, vbuf_p, zeros_tail, zeros_h, zeros_s, cos_p, sin_p,
                                                       CHUNK, attn_p)
        outs_p.append((h1, c1, r1))

        tail_s = jnp.pad(state_conv[l], ((0, 0), (CONV_TAIL - (CONV_W - 1), 0), (0, 0)))
        h0_s = state_ssm[l].reshape(bs, SSM_D, SSM_STATE)

        def attn_s(proj3, kf, vbuf, l=l, p=p, lam_init=lam_init):
            return _diff_attention_sample(proj3, kf, vbuf, cache_k, cache_v, page_table, bias_s, p, l, dec, lam_init)

        ys_new, k2, vbuf_s, h2, c2, r2 = _layer_common(ys, p, l, vbuf_s, tail_s, h0_s, state_ret[l], cos_s, sin_s, dec,
                                                       attn_s)
        ys = jnp.where(jnp.arange(SUBLANES)[None, :, None] < dec, ys_new, 0.0)
        outs_s.append((k2[:, :dec].reshape(bs, dec, DIFF_KV_HEADS, 2, DIFF_HD), h2, c2, r2))

    stack = lambda outs, i: jnp.stack([o[i] for o in outs])
    k_prompt = kbuf_p.reshape(depth, bp, DIFF_KV_HEADS, 2, DIFF_HD, seq).transpose(0, 1, 5, 2, 3, 4)
    v_prompt = vbuf_p.reshape(depth, bp, seq, DIFF_KV_HEADS, DIFF_VD)
    v_sample = vbuf_s.reshape(depth, bs, SUBLANES, DIFF_KV_HEADS, DIFF_VD)[:, :, :dec]
    return (yp, ys[:, :dec],
            k_prompt, v_prompt, stack(outs_p, 0), stack(outs_p, 1), stack(outs_p, 2),
            stack(outs_s, 0), v_sample, stack(outs_s, 1), stack(outs_s, 2), stack(outs_s, 3))
```

```python
import functools
import math

import numpy as np
import jax
import jax.numpy as jnp
from jax import lax
from jax.experimental import pallas as pl
from jax.experimental.pallas import tpu as pltpu

F32 = jnp.float32
BF16 = jnp.bfloat16

D_MODEL = 1024
SSM_HEADS = 16
SSM_HEAD_DIM = 64
SSM_D = SSM_HEADS * SSM_HEAD_DIM
SSM_STATE = 128
SSM_GROUPS = 4
CONV_W = 4
CONV_DIM = SSM_D + 2 * SSM_GROUPS * SSM_STATE
RET_HEADS = 4
RET_DK = 128
RET_DV = 256
ROPE_BASE = 10000.0
DIFF_HEADS = 8
DIFF_KV_HEADS = 4
DIFF_HD = 64
DIFF_VD = 2 * DIFF_HD
N_BUCKETS = 32
MAX_DISTANCE = 128
N_BRANCHES = 3
D_FF = 2816
NORM_EPS = 1e-6
PAGE_SIZE = 128

CHUNK = 128
LANES = 128
SUBLANES = 8
CONV_TAIL = 16
NEG_BIG = -1e30
LOG2E = 1.4426950408889634
VMEM_LIMIT = 52 * 1024 * 1024

IN_SPLITS = (SSM_D, CONV_DIM, SSM_HEADS,
             RET_HEADS * RET_DK, RET_HEADS * RET_DK, RET_HEADS * RET_DV, RET_HEADS * RET_DV,
             DIFF_HEADS * 2 * DIFF_HD, DIFF_KV_HEADS * 2 * DIFF_HD, DIFF_KV_HEADS * DIFF_VD,
             N_BRANCHES * D_MODEL)
IN_OFFSETS = tuple(int(v) for v in np.cumsum(IN_SPLITS)[:-1])

C_XBC, C_Z, C_RQ, C_RK, C_RV, C_RG, C_DQ, C_GATE, C_MAIN = 0, 2048, 3072, 3584, 4096, 5120, 6144, 7168, 10240
TN_PROJ = 1024
KV_COLS = DIFF_KV_HEADS * 2 * DIFF_HD


def _cparams(*sem):
    return pltpu.CompilerParams(dimension_semantics=sem, vmem_limit_bytes=VMEM_LIMIT)


def _nt_dot(a, b):
    return lax.dot_general(a, b, (((1,), (1,)), ((), ())), preferred_element_type=F32)


def _tn_dot(a, b):
    return lax.dot_general(a, b, (((0,), (0,)), ((), ())), preferred_element_type=F32)


def _dot(a, b):
    return jnp.dot(a, b, preferred_element_type=F32)


def _split3(x):
    hi = x.astype(BF16)
    r1 = x - hi.astype(F32)
    mid = r1.astype(BF16)
    lo = (r1 - mid.astype(F32)).astype(BF16)
    return hi, mid, lo


def _sigmoid(x):
    return 0.5 * jnp.tanh(0.5 * x) + 0.5


def _silu(x):
    h = 0.5 * x
    return h + h * jnp.tanh(h)


def _inproj_kernel(x_ref, g_ref, w_ref, wkvdt_ref, vprev_ref, main_ref, kf_ref, vf_ref, dt_ref, xn_ref):
    del vprev_ref
    tm = x_ref.shape[0]

    @pl.when(pl.program_id(1) == 0)
    def _():
        x = x_ref[...]
        ms = jnp.mean(x * x, axis=-1, keepdims=True)
        xn = (x * lax.rsqrt(ms + NORM_EPS) * g_ref[...]).astype(BF16)
        xn_ref[...] = xn
        kvdt = _dot(xn, wkvdt_ref[...])
        kf_ref[...] = kvdt[:, :KV_COLS]
        for g in range(DIFF_KV_HEADS):
            vf_ref[pl.ds(g, tm, stride=DIFF_KV_HEADS), :] = kvdt[:, KV_COLS + g * DIFF_VD:KV_COLS + (g + 1) * DIFF_VD]
        dt_ref[...] = kvdt[:, 2 * KV_COLS:]

    main_ref[...] = _dot(xn_ref[...], w_ref[...]).astype(BF16)


def _inproj(x2d, gain, w_main, w_kvdt, tm, vbuf, layer):
    n = x2d.shape[0]
    grid = (n // tm, C_MAIN // TN_PROJ)
    return pl.pallas_call(
        _inproj_kernel,
        grid=grid,
        in_specs=[
            pl.BlockSpec((tm, D_MODEL), lambda i, j: (i, 0)),
            pl.BlockSpec((1, D_MODEL), lambda i, j: (0, 0)),
            pl.BlockSpec((D_MODEL, TN_PROJ), lambda i, j: (0, j)),
            pl.BlockSpec((D_MODEL, 2 * KV_COLS + LANES), lambda i, j: (0, 0)),
            pl.BlockSpec(memory_space=pl.ANY),
        ],
        out_specs=[
            pl.BlockSpec((tm, TN_PROJ), lambda i, j: (i, j)),
            pl.BlockSpec((tm, KV_COLS), lambda i, j: (i, 0)),
            pl.BlockSpec((None, tm * DIFF_KV_HEADS, DIFF_VD), lambda i, j: (layer, i, 0)),
            pl.BlockSpec((tm, LANES), lambda i, j: (i, 0)),
        ],
        out_shape=[
            jax.ShapeDtypeStruct((n, C_MAIN), BF16),
            jax.ShapeDtypeStruct((n, KV_COLS), F32),
            jax.ShapeDtypeStruct(vbuf.shape, F32),
            jax.ShapeDtypeStruct((n, LANES), F32),
        ],
        scratch_shapes=[pltpu.VMEM((tm, D_MODEL), BF16)],
        input_output_aliases={4: 2},
        compiler_params=_cparams("arbitrary", "arbitrary"),
        name="inproj",
    )(x2d, gain, w_main, w_kvdt, vbuf)


def _ssd_kernel(xbc_ref, z_ref, dt_ref, tail0_ref, h0_ref, cw_ref, cb_ref, dtb_ref, alog_ref, dskip_ref,
                gn_ref, ex_ref, y_ref, hout_ref, convout_ref, xext_ref, tlo_ref, xc_ref, ht_ref, *, nvalid, nchunks):
    c = pl.program_id(1)
    L = CHUNK
    GW = SSM_D // SSM_GROUPS
    SL = 512

    @pl.when(c == 0)
    def _():
        t0 = tail0_ref[0]
        t0b = t0.astype(BF16)
        xext_ref[0:CONV_TAIL, :] = t0b
        tlo_ref[...] = (t0 - t0b.astype(F32)).astype(BF16)
        for k in range(SSM_D // LANES):
            ht_ref[:, k * LANES:(k + 1) * LANES] = h0_ref[0, k * LANES:(k + 1) * LANES, :].T

    xext_ref[CONV_TAIL:CONV_TAIL + L, :] = xbc_ref[0]
    srow = lax.broadcasted_iota(jnp.int32, (L, CONV_TAIL + L), 0)
    scol = lax.broadcasted_iota(jnp.int32, (L, CONV_TAIL + L), 1)
    shifts = [jnp.where(scol == srow + CONV_TAIL - s, 1.0, 0.0).astype(BF16) for s in range(1, CONV_W)]
    for sl in range(CONV_DIM // SL):
        cols = slice(sl * SL, (sl + 1) * SL)
        conv = cw_ref[CONV_W - 1:CONV_W, cols] * xbc_ref[0, :, cols].astype(F32) + cb_ref[:, cols]
        for s in range(1, CONV_W):
            conv = conv + cw_ref[CONV_W - 1 - s:CONV_W - s, cols] * _dot(shifts[s - 1], xext_ref[:, cols])
        xc_ref[:, cols] = conv

    @pl.when(c == 0)
    def _():
        for sl in range(CONV_DIM // SL):
            cols = slice(sl * SL, (sl + 1) * SL)
            corr = None
            for s in range(1, CONV_W):
                t = cw_ref[CONV_W - 1 - s:CONV_W - s, cols] * _dot(shifts[s - 1][0:CONV_TAIL, 0:CONV_TAIL],
                                                                    tlo_ref[:, cols])
                corr = t if corr is None else corr + t
            xc_ref[0:CONV_TAIL, cols] += corr

    @pl.when(c == nchunks - 1)
    def _():
        a = CONV_TAIL * ((nvalid - 1) // CONV_TAIL)
        convout_ref[0] = xext_ref[CONV_TAIL + a:CONV_TAIL + a + CONV_TAIL, :].astype(F32)

    xext_ref[0:CONV_TAIL, :] = xext_ref[L:L + CONV_TAIL, :]

    row = lax.broadcasted_iota(jnp.int32, (L, L), 0)
    col = lax.broadcasted_iota(jnp.int32, (L, L), 1)
    causal = row >= col
    left = col < SSM_HEAD_DIM

    x = dt_ref[0] + dtb_ref[...]
    dt = jnp.maximum(x, 0.0) + jnp.log1p(jnp.exp(-jnp.abs(x)))
    if nvalid < L:
        dt = jnp.where(row < nvalid, dt, 0.0)
    a = -jnp.exp(alog_ref[...])
    da = dt * a
    tri = jnp.where(causal, 1.0, 0.0).astype(BF16)
    cs = sum(_dot(tri, p) for p in _split3(da))
    cs_t = cs.T
    cs_parts = _split3(cs)
    dt_parts = _split3(dt)

    NG = SSM_GROUPS
    HPG = SSM_HEADS // SSM_GROUPS
    gcs = [slice(g * GW, (g + 1) * GW) for g in range(NG)]
    bcol = lambda g: slice(SSM_D + g * SSM_STATE, SSM_D + (g + 1) * SSM_STATE)
    ccol = lambda g: slice(SSM_D + NG * SSM_STATE + g * SSM_STATE, SSM_D + NG * SSM_STATE + (g + 1) * SSM_STATE)

    csx = [sum(_dot(p, ex_ref[:, gc]) for p in cs_parts) for gc in gcs]
    dtx = [sum(_dot(p, ex_ref[:, gc]) for p in dt_parts) for gc in gcs]
    bgs = [_silu(xc_ref[:, bcol(g)]).astype(BF16) for g in range(NG)]
    cgs = [_silu(xc_ref[:, ccol(g)]).astype(BF16) for g in range(NG)]
    cbs = [_nt_dot(cgs[g], bgs[g]) for g in range(NG)]
    htgs = [ht_ref[:, gc] for gc in gcs]
    ysts = [_dot(cgs[g], htgs[g].astype(BF16)) for g in range(NG)]
    xss = [_silu(xc_ref[:, gc]) for gc in gcs]
    xdts = [xss[g] * dtx[g] for g in range(NG)]
    xdt_bs = [x_.astype(BF16) for x_ in xdts]
    lastxs = [csx[g][L - 1:L, :] for g in range(NG)]
    xdtw_bs = [(xdts[g] * jnp.exp(lastxs[g] - csx[g])).astype(BF16) for g in range(NG)]
    prods = []
    for h in range(SSM_HEADS):
        g, pr = h // HPG, (h % HPG) // 2
        seg = cs[:, h:h + 1] - cs_t[h:h + 1, :]
        decay = jnp.exp(jnp.where(causal, seg, NEG_BIG))
        mat = (cbs[g] * decay).astype(BF16)
        xp = xdt_bs[g][:, pr * LANES:(pr + 1) * LANES]
        keep = left if h % 2 == 0 else jnp.logical_not(left)
        prods.append(_dot(mat, jnp.where(keep, xp, jnp.zeros_like(xp))))
    upd = [_tn_dot(bgs[g], xdtw_bs[g]) for g in range(NG)]
    for g in range(NG):
        ht_ref[:, gcs[g]] = htgs[g] * jnp.exp(lastxs[g]) + upd[g]
    y_ins = [jnp.concatenate([prods[g * HPG + 2 * pr] + prods[g * HPG + 2 * pr + 1] for pr in range(HPG // 2)], axis=1)
             for g in range(NG)]
    for g in range(NG):
        gc = gcs[g]
        y = y_ins[g] + ysts[g] * jnp.exp(csx[g]) + dskip_ref[:, gc] * xss[g]
        y = y * _silu(z_ref[0, :, gc].astype(F32))
        ms = jnp.mean(y * y, axis=-1, keepdims=True)
        y_ref[0, :, gc] = (y * lax.rsqrt(ms + NORM_EPS) * gn_ref[:, gc]).astype(BF16)

    @pl.when(c == nchunks - 1)
    def _():
        for k in range(SSM_D // LANES):
            hout_ref[0, k * LANES:(k + 1) * LANES, :] = ht_ref[:, k * LANES:(k + 1) * LANES].T


def _ret_kernel(q_ref, k_ref, v_ref, rg_ref, cos_ref, sin_ref, s0_ref, gn_ref, y_ref, sout_ref, *, ltrue):
    c = pl.program_id(1)
    L = CHUNK

    @pl.when(c == 0)
    def _():
        sout_ref[...] = s0_ref[...]

    row = lax.broadcasted_iota(jnp.int32, (L, L), 0)
    col = lax.broadcasted_iota(jnp.int32, (L, L), 1)
    rel = (row - col).astype(F32)
    idx = row[:, 0:1].astype(F32)
    cosf = cos_ref[...]
    sins = sin_ref[...]
    H = RET_HEADS
    lgs = [math.log(1.0 - 2.0 ** (-5.0 - h)) for h in range(H)]
    kcs = [slice(h * RET_DK, (h + 1) * RET_DK) for h in range(H)]
    vcs = [slice(h * RET_DV, (h + 1) * RET_DV) for h in range(H)]
    qrs, krs = [], []
    for h in range(H):
        qh = q_ref[0, :, kcs[h]].astype(F32)
        kh = k_ref[0, :, kcs[h]].astype(F32)
        qrs.append(qh * cosf + pltpu.roll(qh, RET_DK // 2, 1) * sins)
        krs.append((kh * cosf + pltpu.roll(kh, RET_DK // 2, 1) * sins) * RET_DK ** -0.5)
    qr_bs = [q_.astype(BF16) for q_ in qrs]
    vhs = [v_ref[0, :, vcs[h]] for h in range(H)]
    s_olds = [sout_ref[0, h] for h in range(H)]
    scores = [_nt_dot(qr_bs[h], krs[h].astype(BF16)) for h in range(H)]
    cross = [_dot(qr_bs[h], s_olds[h].astype(BF16)) for h in range(H)]
    atts = []
    for h in range(H):
        dmat = jnp.where(rel >= 0, jnp.exp(jnp.maximum(rel, 0.0) * lgs[h]), 0.0)
        atts.append((scores[h] * dmat).astype(BF16))
    inner = [_dot(atts[h], vhs[h]) for h in range(H)]
    for h in range(H):
        k_dec = jnp.exp((ltrue - 1.0 - idx) * lgs[h])
        sout_ref[0, h] = s_olds[h] * math.exp(ltrue * lgs[h]) + _tn_dot((krs[h] * k_dec).astype(BF16), vhs[h])
    for h in range(H):
        o = inner[h] + cross[h] * jnp.exp((idx + 1.0) * lgs[h])
        oc = o - jnp.mean(o, axis=-1, keepdims=True)
        on = oc * lax.rsqrt(jnp.mean(oc * oc, axis=-1, keepdims=True) + NORM_EPS)
        y_ref[0, :, vcs[h]] = (on * gn_ref[:, vcs[h]] * _silu(rg_ref[0, :, vcs[h]].astype(F32))).astype(BF16)


N_SSD_IN, N_RET_IN, N_SSD_OUT, N_RET_OUT = 12, 8, 3, 2


def _scan_kernel(*refs, nvalid, nchunks):
    i0 = N_SSD_IN
    i1 = i0 + N_RET_IN
    i2 = i1 + N_SSD_OUT
    i3 = i2 + N_RET_OUT
    _ret_kernel(*refs[i0:i1], *refs[i2:i3], ltrue=float(nvalid))
    _ssd_kernel(*refs[0:i0], *refs[i1:i2], *refs[i3:], nvalid=nvalid, nchunks=nchunks)


def _scans(proj, dt_raw, tail0, h0, cos_t, sin_t, s0, p, nvalid):
    b, t, _ = proj.shape
    nchunks = t // CHUNK
    L = CHUNK
    qk_w = RET_HEADS * RET_DK
    v_w = RET_HEADS * RET_DV
    const = lambda shape: pl.BlockSpec(shape, lambda i, c: (0,) * len(shape))
    ssd_in = [
        pl.BlockSpec((1, L, CONV_DIM), lambda i, c: (i, c, C_XBC // CONV_DIM)),
        pl.BlockSpec((1, L, SSM_D), lambda i, c: (i, c, C_Z // SSM_D)),
        pl.BlockSpec((1, L, LANES), lambda i, c: (i, c, 0)),
        pl.BlockSpec((1, CONV_TAIL, CONV_DIM), lambda i, c: (i, 0, 0)),
        pl.BlockSpec((1, SSM_D, SSM_STATE), lambda i, c: (i, 0, 0)),
        const((CONV_W, CONV_DIM)), const((1, CONV_DIM)), const((1, LANES)), const((1, LANES)),
        const((1, SSM_D)), const((1, SSM_D)), const((LANES, SSM_D)),
    ]
    ret_in = [
        pl.BlockSpec((1, L, qk_w), lambda i, c: (i, c, C_RQ // qk_w)),
        pl.BlockSpec((1, L, qk_w), lambda i, c: (i, c, C_RK // qk_w)),
        pl.BlockSpec((1, L, v_w), lambda i, c: (i, c, C_RV // v_w)),
        pl.BlockSpec((1, L, v_w), lambda i, c: (i, c, C_RG // v_w)),
        pl.BlockSpec((L, RET_DK), lambda i, c: (c, 0)),
        pl.BlockSpec((L, RET_DK), lambda i, c: (c, 0)),
        pl.BlockSpec((1, RET_HEADS, RET_DK, RET_DV), lambda i, c: (i, 0, 0, 0)),
        const((1, v_w)),
    ]
    ssd_out = [
        pl.BlockSpec((1, L, SSM_D), lambda i, c: (i, c, 0)),
        pl.BlockSpec((1, SSM_D, SSM_STATE), lambda i, c: (i, 0, 0)),
        pl.BlockSpec((1, CONV_TAIL, CONV_DIM), lambda i, c: (i, 0, 0)),
    ]
    ret_out = [
        pl.BlockSpec((1, L, v_w), lambda i, c: (i, c, 0)),
        pl.BlockSpec((1, RET_HEADS, RET_DK, RET_DV), lambda i, c: (i, 0, 0, 0)),
    ]
    assert (len(ssd_in), len(ret_in), len(ssd_out), len(ret_out)) == (N_SSD_IN, N_RET_IN, N_SSD_OUT, N_RET_OUT)
    return pl.pallas_call(
        functools.partial(_scan_kernel, nvalid=nvalid, nchunks=nchunks),
        grid=(b, nchunks),
        in_specs=ssd_in + ret_in,
        out_specs=ssd_out + ret_out,
        out_shape=[
            jax.ShapeDtypeStruct((b, t, SSM_D), BF16),
            jax.ShapeDtypeStruct((b, SSM_D, SSM_STATE), F32),
            jax.ShapeDtypeStruct((b, CONV_TAIL, CONV_DIM), F32),
            jax.ShapeDtypeStruct((b, t, v_w), BF16),
            jax.ShapeDtypeStruct((b, RET_HEADS, RET_DK, RET_DV), F32),
        ],
        scratch_shapes=[
            pltpu.VMEM((CONV_TAIL + L, CONV_DIM), BF16),
            pltpu.VMEM((CONV_TAIL, CONV_DIM), BF16),
            pltpu.VMEM((L, CONV_DIM), F32),
            pltpu.VMEM((SSM_STATE, SSM_D), F32),
        ],
        compiler_params=_cparams("arbitrary", "arbitrary"),
        name="scans",
    )(proj, proj, dt_raw, tail0, h0, p["conv_w"], p["conv_b"], p["dt_bias"], p["a_log"], p["d_skip"],
      p["ssm_norm"], p["head_expand"], proj, proj, proj, proj, cos_t, sin_t, s0, p["ret_norm"])


def _bias_kernel(tab_ref, idx_ref, out_ref, *, scale):
    h = pl.program_id(0)
    idx = idx_ref[...]
    acc = jnp.zeros(idx.shape, F32)
    for b in range(N_BUCKETS):
        acc = acc + jnp.where(idx == b, tab_ref[b * DIFF_HEADS + h], 0.0)
    out_ref[0] = acc * scale


def _bias_tiles(rel_bias, idx, scale):
    r, c = idx.shape
    return pl.pallas_call(
        functools.partial(_bias_kernel, scale=scale),
        grid=(DIFF_HEADS,),
        in_specs=[pl.BlockSpec(memory_space=pltpu.SMEM), pl.BlockSpec((r, c), lambda h: (0, 0))],
        out_specs=pl.BlockSpec((1, r, c), lambda h: (h, 0, 0)),
        out_shape=jax.ShapeDtypeStruct((DIFF_HEADS, r, c), F32),
        compiler_params=_cparams("arbitrary"),
        name="t5_bias",
    )(rel_bias.reshape(-1), idx)


def _half_rmsnorm(x, gain, bd):
    x2 = x * x
    hi = x2.astype(BF16)
    lo = (x2 - hi.astype(F32)).astype(BF16)
    ss = _dot(hi, bd) + _dot(lo, bd)
    return x * lax.rsqrt(ss * (1.0 / DIFF_HD) + NORM_EPS) * gain


def _half_rmsnorm_blocks(blocks, gain, bd):
    rows = blocks[0].shape[0]
    y = _half_rmsnorm(jnp.concatenate(blocks, axis=0), gain, bd)
    return [y[i * rows:(i + 1) * rows] for i in range(len(blocks))]


def _lambda(lamv_ref, lam_init):
    s1 = jnp.sum(lamv_ref[0:1, :] * lamv_ref[1:2, :], axis=-1, keepdims=True)
    s2 = jnp.sum(lamv_ref[2:3, :] * lamv_ref[3:4, :], axis=-1, keepdims=True)
    return jnp.exp(s1) - jnp.exp(s2) + lam_init


FAR_UNITS = 4


def _dattn_kernel(q_ref, kf_ref, vf_ref, bias_ref, gq_ref, gk_ref, lamv_ref, sgc_ref, bd_ref, kprev_ref,
                  y_ref, kout_ref, kn_ref, vt_ref, qs_ref, m_ref, l_ref, acc_ref, *, t, lam_init):
    del kprev_ref
    qi = pl.program_id(1)
    TQ = CHUNK
    G = DIFF_KV_HEADS
    R = DIFF_HEADS // DIFF_KV_HEADS
    NC = 2 * R * TQ
    bd = bd_ref[...]

    @pl.when(qi == 0)
    def _():
        for i in range(t // TQ):
            r = slice(i * TQ, (i + 1) * TQ)
            kns = _half_rmsnorm_blocks([kf_ref[0, r, g * LANES:(g + 1) * LANES] for g in range(G)], gk_ref[...], bd)
            for g in range(G):
                kout_ref[0, g * LANES:(g + 1) * LANES, r] = kns[g].T
                kn_ref[g, r, :] = kns[g].astype(BF16)
                vt_ref[g, i] = vf_ref[0, pl.ds(i * TQ * G + g, TQ, stride=G), :].T.astype(BF16)

    lane = lax.broadcasted_iota(jnp.int32, (TQ, LANES), 1)
    left = lane < DIFF_HD
    qns = _half_rmsnorm_blocks([q_ref[0, :, h * LANES:(h + 1) * LANES].astype(F32) for h in range(DIFF_HEADS)],
                               gq_ref[...], bd)
    for g in range(G):
        for r in range(R):
            qn = qns[g * R + r] * (DIFF_HD ** -0.5 * LOG2E)
            qs_ref[g, (0 * R + r) * TQ:(0 * R + r + 1) * TQ, :] = jnp.where(left, qn, 0.0).astype(BF16)
            qs_ref[g, (1 * R + r) * TQ:(1 * R + r + 1) * TQ, :] = jnp.where(left, 0.0, qn).astype(BF16)
    m_ref[...] = jnp.full(m_ref.shape, NEG_BIG, F32)
    l_ref[...] = jnp.zeros(l_ref.shape, F32)
    acc_ref[...] = jnp.zeros(acc_ref.shape, F32)

    def step(k0, nunits, tile, diag):
        tk = nunits * TQ
        rows = pl.ds(pl.multiple_of(k0 * TQ, TQ), tk)
        ss = [_nt_dot(kn_ref[g, rows, :], qs_ref[g]) for g in range(G)]
        ps, alphas = [], []
        for g in range(G):
            bias = bias_ref[g, tile]
            s = ss[g] + (bias if nunits == 1 else jnp.concatenate([bias] * nunits, axis=0))
            if diag:
                key = lax.broadcasted_iota(jnp.int32, (TQ, NC), 0)
                qry = lax.broadcasted_iota(jnp.int32, (TQ, NC), 1) & (TQ - 1)
                s = jnp.where(key <= qry, s, NEG_BIG)
            m_old = m_ref[g]
            m_new = jnp.maximum(m_old, jnp.max(s, axis=0, keepdims=True))
            alpha = jnp.exp2(m_old - m_new)
            p = jnp.exp2(s - m_new)
            l_ref[g] = alpha * l_ref[g] + jnp.sum(p, axis=0, keepdims=True)
            m_ref[g] = m_new
            ps.append(p.astype(BF16))
            alphas.append(alpha)
        for g in range(G):
            if nunits == 1:
                vt = vt_ref[g, k0]
            else:
                vt = jnp.concatenate([vt_ref[g, k0 + u] for u in range(nunits)], axis=1)
            acc_ref[g] = acc_ref[g] * alphas[g] + _dot(vt, ps[g])

    nfar = jnp.maximum(qi - 1, 0)
    nbig = nfar // FAR_UNITS

    def far_big(i, carry):
        step(i * FAR_UNITS, FAR_UNITS, 2, False)
        return carry

    def far_one(i, carry):
        step(nbig * FAR_UNITS + i, 1, 2, False)
        return carry

    lax.fori_loop(0, nbig, far_big, 0)
    lax.fori_loop(0, nfar - nbig * FAR_UNITS, far_one, 0)

    @pl.when(qi >= 1)
    def _():
        step(qi - 1, 1, 1, False)

    step(qi, 1, 0, True)

    lam = _lambda(lamv_ref, lam_init)
    for g in range(G):
        inv_l = 1.0 / l_ref[g]
        for r in range(R):
            c0 = slice((0 * R + r) * TQ, (0 * R + r + 1) * TQ)
            c1 = slice((1 * R + r) * TQ, (1 * R + r + 1) * TQ)
            o = acc_ref[g, :, c0] * inv_l[:, c0] - lam * (acc_ref[g, :, c1] * inv_l[:, c1])
            o = o * lax.rsqrt(jnp.mean(o * o, axis=0, keepdims=True) + NORM_EPS)
            o = o * sgc_ref[...] * (1.0 - lam_init)
            y_ref[0, :, (g * R + r) * LANES:(g * R + r + 1) * LANES] = o.T.astype(BF16)


def _diff_attention_prompt(proj, kf, vbuf, kbuf, bias, p, layer, lam_init):
    b, t, _ = proj.shape
    vf = vbuf.reshape(vbuf.shape[0], b, t * DIFF_KV_HEADS, DIFF_VD)
    TQ = CHUNK
    G = DIFF_KV_HEADS
    R = DIFF_HEADS // DIFF_KV_HEADS
    NC = 2 * R * TQ
    qw = DIFF_HEADS * 2 * DIFF_HD
    const = lambda shape: pl.BlockSpec(shape, lambda i, q: (0,) * len(shape))
    return pl.pallas_call(
        functools.partial(_dattn_kernel, t=t, lam_init=lam_init),
        grid=(b, t // TQ),
        in_specs=[
            pl.BlockSpec((1, TQ, qw), lambda i, q: (i, q, C_DQ // qw)),
            pl.BlockSpec((1, t, KV_COLS), lambda i, q: (i, 0, 0)),
            pl.BlockSpec((None, 1, t * G, DIFF_VD), lambda i, q: (layer, i, 0, 0)),
            const((G, 3, TQ, NC)),
            const((1, LANES)), const((1, LANES)), const((4, LANES)), const((DIFF_VD, 1)), const((LANES, LANES)),
            pl.BlockSpec(memory_space=pl.ANY),
        ],
        out_specs=[
            pl.BlockSpec((1, TQ, qw), lambda i, q: (i, q, 0)),
            pl.BlockSpec((None, 1, KV_COLS, t), lambda i, q: (layer, i, 0, 0)),
        ],
        out_shape=[
            jax.ShapeDtypeStruct((b, t, DIFF_HEADS * DIFF_VD), BF16),
            jax.ShapeDtypeStruct(kbuf.shape, F32),
        ],
        input_output_aliases={9: 1},
        scratch_shapes=[
            pltpu.VMEM((G, t, LANES), BF16), pltpu.VMEM((G, t // TQ, DIFF_VD, TQ), BF16),
            pltpu.VMEM((G, NC, LANES), BF16),
            pltpu.VMEM((G, 1, NC), F32), pltpu.VMEM((G, 1, NC), F32),
            pltpu.VMEM((G, DIFF_VD, NC), F32),
        ],
        compiler_params=_cparams("arbitrary", "arbitrary"),
        name="diff_attn_prompt",
    )(proj, kf, vf, bias, p["qk_norm_q"], p["qk_norm_k"], p["lamv"], p["diff_norm_col"], p["blockdiag"], kbuf)


PAGES_PER_STEP = 16


def _sattn_kernel(pt_ref, *refs, layer, nvalid, lam_init, npp):
    del pt_ref, layer
    k_refs = refs[0:npp]
    v_refs = refs[npp:2 * npp]
    (q_ref, kf_ref, vf_ref, bias_ref, gq_ref, gk_ref, lamv_ref, sg_ref, bd_ref,
     y_ref, kout_ref, qs_ref, knew_ref, vnew_ref, m_ref, l_ref, acc_ref) = refs[2 * npp:]
    s_id = pl.program_id(1)
    nsteps = pl.num_programs(1)
    G = DIFF_KV_HEADS
    R = DIFF_HEADS // DIFF_KV_HEADS
    TP = SUBLANES
    MR = 2 * R * TP
    bd = bd_ref[...]
    lane = lax.broadcasted_iota(jnp.int32, (TP, LANES), 1)
    left = lane < DIFF_HD

    @pl.when(s_id == 0)
    def _():
        knew_ref[...] = jnp.zeros(knew_ref.shape, BF16)
        vnew_ref[...] = jnp.zeros(vnew_ref.shape, BF16)
        for g in range(G):
            gc = slice(g * LANES, (g + 1) * LANES)
            kn = _half_rmsnorm(kf_ref[0, :, gc], gk_ref[...], bd)
            kout_ref[0, :, gc] = kn
            knew_ref[g, 0:TP, :] = kn.astype(BF16)
            vnew_ref[g, 0:TP, :] = vf_ref[0, pl.ds(g, TP, stride=G), :].astype(BF16)
            for r in range(R):
                hc = slice((g * R + r) * LANES, (g * R + r + 1) * LANES)
                qn = _half_rmsnorm(q_ref[0, :, hc].astype(F32), gq_ref[...], bd) * DIFF_HD ** -0.5
                qs_ref[g, (0 * R + r) * TP:(0 * R + r + 1) * TP, :] = jnp.where(left, qn, 0.0).astype(BF16)
                qs_ref[g, (1 * R + r) * TP:(1 * R + r + 1) * TP, :] = jnp.where(left, 0.0, qn).astype(BF16)
        m_ref[...] = jnp.full(m_ref.shape, NEG_BIG, F32)
        l_ref[...] = jnp.zeros(l_ref.shape, F32)
        acc_ref[...] = jnp.zeros(acc_ref.shape, F32)

    def bias_rows(g, seg):
        per_head = [bias_ref[g * R + r, :, seg * LANES:(seg + 1) * LANES] for r in range(R)]
        return jnp.concatenate(per_head + per_head, axis=0)

    def update(g, s, vs):
        m_old = m_ref[g]
        m_new = jnp.maximum(m_old, jnp.max(s, axis=-1, keepdims=True))
        alpha = jnp.exp(m_old - m_new)
        p = jnp.exp(s - m_new)
        l_ref[g] = alpha * l_ref[g] + jnp.sum(p, axis=-1, keepdims=True)
        m_ref[g] = m_new
        pb = p.astype(BF16)
        pv = _dot(pb[:, 0:LANES], vs[0])
        for i in range(1, len(vs)):
            pv = pv + _dot(pb[:, i * LANES:(i + 1) * LANES], vs[i])
        acc_ref[g] = acc_ref[g] * alpha + pv

    def pages(last):
        scores = []
        for g in range(G):
            gc = slice(g * LANES, (g + 1) * LANES)
            kcat = jnp.concatenate([k_refs[i][gc, :].astype(BF16) for i in range(npp)], axis=1)
            far = bias_rows(g, 0)
            near = bias_rows(g, 1) if last else far
            scores.append(_dot(qs_ref[g], kcat) + jnp.concatenate([far] * (npp - 1) + [near], axis=1))
        pvs, alphas = [], []
        for g in range(G):
            s = scores[g]
            m_old = m_ref[g]
            m_new = jnp.maximum(m_old, jnp.max(s, axis=-1, keepdims=True))
            alpha = jnp.exp(m_old - m_new)
            p = jnp.exp(s - m_new)
            l_ref[g] = alpha * l_ref[g] + jnp.sum(p, axis=-1, keepdims=True)
            m_ref[g] = m_new
            vcat = jnp.concatenate([v_refs[i][pl.ds(g, PAGE_SIZE, stride=G), :].astype(BF16) for i in range(npp)],
                                   axis=0)
            pvs.append(_dot(p.astype(BF16), vcat))
            alphas.append(alpha)
        for g in range(G):
            acc_ref[g] = acc_ref[g] * alphas[g] + pvs[g]

    @pl.when(s_id < nsteps - 1)
    def _():
        pages(False)

    @pl.when(s_id == nsteps - 1)
    def _():
        pages(True)
        rowt = lax.broadcasted_iota(jnp.int32, (MR, LANES), 0) % TP
        colj = lax.broadcasted_iota(jnp.int32, (MR, LANES), 1)
        ok = jnp.logical_and(colj <= rowt, colj < nvalid)
        lam = _lambda(lamv_ref, lam_init)
        for g in range(G):
            sc = _nt_dot(qs_ref[g], knew_ref[g]) + bias_rows(g, 2)
            update(g, jnp.where(ok, sc, NEG_BIG), [vnew_ref[g]])
            acc = acc_ref[g] / l_ref[g]
            for r in range(R):
                o = acc[(0 * R + r) * TP:(0 * R + r + 1) * TP, :] - lam * acc[(1 * R + r) * TP:(1 * R + r + 1) * TP, :]
                o = o * lax.rsqrt(jnp.mean(o * o, axis=-1, keepdims=True) + NORM_EPS)
                hc = slice((g * R + r) * LANES, (g * R + r + 1) * LANES)
                y_ref[0, :, hc] = (o * sg_ref[...] * (1.0 - lam_init)).astype(BF16)


def _diff_attention_sample(proj, kf, vbuf, cache_k, cache_v, page_table, bias, p, layer, nvalid, lam_init):
    b, tp, _ = proj.shape
    vf = vbuf.reshape(vbuf.shape[0], b, tp * DIFF_KV_HEADS, DIFF_VD)
    n_pages = page_table.shape[1]
    npp = PAGES_PER_STEP
    while n_pages % npp:
        npp //= 2
    nsteps = n_pages // npp
    G = DIFF_KV_HEADS
    R = DIFF_HEADS // DIFF_KV_HEADS
    MR = 2 * R * tp
    ck = jnp.transpose(cache_k, (0, 1, 3, 4, 5, 2)).reshape(cache_k.shape[0], cache_k.shape[1], KV_COLS, PAGE_SIZE)
    cv = cache_v.reshape(cache_v.shape[0], cache_v.shape[1], PAGE_SIZE * DIFF_KV_HEADS, DIFF_VD)

    def page_spec(i):
        return pl.BlockSpec((None, None, KV_COLS, PAGE_SIZE),
                            lambda bi, s, pt: (layer, pt[bi * n_pages + s * npp + i], 0, 0))

    const = lambda shape: pl.BlockSpec(shape, lambda bi, s, pt: (0,) * len(shape))
    grid_spec = pltpu.PrefetchScalarGridSpec(
        num_scalar_prefetch=1,
        grid=(b, nsteps),
        in_specs=[page_spec(i) for i in range(npp)] + [page_spec(i) for i in range(npp)] + [
            pl.BlockSpec((1, tp, DIFF_HEADS * 2 * DIFF_HD), lambda bi, s, pt: (bi, 0, C_DQ // (DIFF_HEADS * 2 * DIFF_HD))),
            pl.BlockSpec((1, tp, KV_COLS), lambda bi, s, pt: (bi, 0, 0)),
            pl.BlockSpec((None, 1, tp * G, DIFF_VD), lambda bi, s, pt: (layer, bi, 0, 0)),
            const((DIFF_HEADS, tp, 3 * LANES)),
            const((1, LANES)), const((1, LANES)), const((4, LANES)), const((1, LANES)), const((LANES, LANES)),
        ],
        out_specs=[
            pl.BlockSpec((1, tp, DIFF_HEADS * DIFF_VD), lambda bi, s, pt: (bi, 0, 0)),
            pl.BlockSpec((1, tp, KV_COLS), lambda bi, s, pt: (bi, 0, 0)),
        ],
        scratch_shapes=[
            pltpu.VMEM((G, MR, LANES), BF16),
            pltpu.VMEM((G, PAGE_SIZE, LANES), BF16), pltpu.VMEM((G, PAGE_SIZE, LANES), BF16),
            pltpu.VMEM((G, MR, 1), F32), pltpu.VMEM((G, MR, 1), F32), pltpu.VMEM((G, MR, LANES), F32),
        ],
    )
    return pl.pallas_call(
        functools.partial(_sattn_kernel, layer=layer, nvalid=nvalid, lam_init=lam_init, npp=npp),
        grid_spec=grid_spec,
        out_shape=[
            jax.ShapeDtypeStruct((b, tp, DIFF_HEADS * DIFF_VD), BF16),
            jax.ShapeDtypeStruct((b, tp, KV_COLS), F32),
        ],
        compiler_params=_cparams("arbitrary", "arbitrary"),
        name="diff_attn_sample",
    )(page_table.reshape(-1), *([ck] * npp), *([cv] * npp), proj, kf, vf, bias,
      p["qk_norm_q"], p["qk_norm_k"], p["lamv"], p["diff_norm"], p["blockdiag"])


def _merge_kernel(ys_ref, yr_ref, yd_ref, g0_ref, g1_ref, g2_ref, x_ref, ws_ref, wr_ref, wd_ref, wo_ref,
                  bg_ref, h_ref):
    merged = None
    for i, (y_ref, w_ref, g_ref) in enumerate(((ys_ref, ws_ref, g0_ref), (yr_ref, wr_ref, g1_ref),
                                               (yd_ref, wd_ref, g2_ref))):
        br = _dot(y_ref[...], w_ref[...])
        t = _sigmoid(g_ref[...].astype(F32) + bg_ref[i:i + 1, :]) * br
        merged = t if merged is None else merged + t
    h_ref[...] = x_ref[...] + _dot(merged.astype(BF16), wo_ref[...])


def _merge(y_ssm, y_ret, y_diff, proj2d, x2d, p, tm):
    n = x2d.shape[0]
    tok = lambda cb: pl.BlockSpec((tm, D_MODEL), lambda i: (i, cb))
    wspec = pl.BlockSpec((D_MODEL, D_MODEL), lambda i: (0, 0))
    g0 = C_GATE // D_MODEL
    return pl.pallas_call(
        _merge_kernel,
        grid=(n // tm,),
        in_specs=[tok(0), tok(0), tok(0), tok(g0), tok(g0 + 1), tok(g0 + 2), tok(0),
                  wspec, wspec, wspec, wspec, pl.BlockSpec((N_BRANCHES, D_MODEL), lambda i: (0, 0))],
        out_specs=tok(0),
        out_shape=jax.ShapeDtypeStruct((n, D_MODEL), F32),
        compiler_params=_cparams("arbitrary"),
        name="merge",
    )(y_ssm, y_ret, y_diff, proj2d, proj2d, proj2d, x2d, p["w_ssm_out"], p["w_ret_out"], p["w_diff_out"],
      p["w_o"], p["b_gate"])


def _ffn_kernel(h_ref, g_ref, wg_ref, wu_ref, wd_ref, y_ref):
    h = h_ref[...]
    ms = jnp.mean(h * h, axis=-1, keepdims=True)
    hn = (h * lax.rsqrt(ms + NORM_EPS) * g_ref[...]).astype(BF16)
    act = _silu(_dot(hn, wg_ref[...])) * _dot(hn, wu_ref[...])
    y_ref[...] = h + _dot(act.astype(BF16), wd_ref[...])


def _ffn(h2d, p, tm):
    n = h2d.shape[0]
    resident = lambda shape, cb: pl.BlockSpec(shape, lambda i: (0, cb), pipeline_mode=pl.Buffered(1))
    return pl.pallas_call(
        _ffn_kernel,
        grid=(n // tm,),
        in_specs=[
            pl.BlockSpec((tm, D_MODEL), lambda i: (i, 0)),
            pl.BlockSpec((1, D_MODEL), lambda i: (0, 0)),
            resident((D_MODEL, D_FF), 0),
            resident((D_MODEL, D_FF), 1),
            resident((D_FF, D_MODEL), 0),
        ],
        out_specs=pl.BlockSpec((tm, D_MODEL), lambda i: (i, 0)),
        out_shape=jax.ShapeDtypeStruct((n, D_MODEL), F32),
        compiler_params=_cparams("arbitrary"),
        name="ffn",
    )(h2d, p["norm_ffn"], p["w_gate_up"], p["w_gate_up"], p["w_down"])


def _t5_bucket(dist):
    n = jnp.maximum(dist, 0)
    max_exact = N_BUCKETS // 2
    large = max_exact + (jnp.log(jnp.maximum(n, 1).astype(F32) / max_exact)
                         / math.log(MAX_DISTANCE / max_exact) * (N_BUCKETS - max_exact)).astype(jnp.int32)
    large = jnp.minimum(large, N_BUCKETS - 1)
    return jnp.where(n < max_exact, n, large)


def _far_bucket_is_constant(min_dist):
    max_exact = N_BUCKETS // 2
    d = np.float32(min_dist)
    large = max_exact + int(np.float32(np.log(d / np.float32(max_exact))) / np.float32(math.log(MAX_DISTANCE / max_exact))
                            * (N_BUCKETS - max_exact))
    return min_dist >= max_exact and large >= N_BUCKETS - 1


def _rope_tables(pos):
    half = RET_DK // 2
    inv = 1.0 / (ROPE_BASE ** (jnp.arange(half, dtype=F32) / half))
    ang = pos.astype(F32)[:, None] * inv[None, :]
    cos, sin = jnp.cos(ang), jnp.sin(ang)
    return jnp.concatenate([cos, cos], axis=1), jnp.concatenate([-sin, sin], axis=1)


def _layer_params(l, w_in, named):
    p = {k: v[l] for k, v in named.items()}
    w = w_in[l]
    o = (0,) + IN_OFFSETS + (w.shape[1],)
    z, xbc, dt, rq, rk, rv, rg, dq, dk, dv, gates = [w[:, o[i]:o[i + 1]] for i in range(len(IN_SPLITS))]
    out = {}
    out["w_main"] = jnp.concatenate([xbc, z, rq, rk, rv, rg, dq, gates], axis=1).astype(BF16)
    out["w_kvdt"] = jnp.concatenate([dk, dv, jnp.pad(dt, ((0, 0), (0, LANES - SSM_HEADS)))], axis=1).astype(BF16)
    out["norm_mix"] = p["norm_mix"].reshape(1, D_MODEL)
    out["conv_w"] = p["conv_w"]
    out["conv_b"] = p["conv_b"].reshape(1, CONV_DIM)
    out["dt_bias"] = jnp.pad(p["dt_bias"], (0, LANES - SSM_HEADS)).reshape(1, LANES)
    out["a_log"] = jnp.pad(p["a_log"], (0, LANES - SSM_HEADS)).reshape(1, LANES)
    out["d_skip"] = jnp.repeat(p["d_skip"], SSM_HEAD_DIM).reshape(1, SSM_D)
    out["ssm_norm"] = p["ssm_norm"].reshape(1, SSM_D)
    head_of_channel = np.arange(SSM_D) // SSM_HEAD_DIM
    out["head_expand"] = jnp.asarray(np.arange(LANES)[:, None] == head_of_channel[None, :], dtype=BF16)
    out["ret_norm"] = p["ret_norm"].reshape(1, RET_HEADS * RET_DV)
    out["qk_norm_q"] = jnp.tile(p["qk_norm_q"], 2).reshape(1, LANES)
    out["qk_norm_k"] = jnp.tile(p["qk_norm_k"], 2).reshape(1, LANES)
    lamv = jnp.stack([p["lambda_q1"], p["lambda_k1"], p["lambda_q2"], p["lambda_k2"]])
    out["lamv"] = jnp.pad(lamv, ((0, 0), (0, LANES - DIFF_HD)))
    out["diff_norm"] = p["diff_norm"].reshape(1, DIFF_VD)
    out["diff_norm_col"] = p["diff_norm"].reshape(DIFF_VD, 1)
    half = np.arange(LANES) // DIFF_HD
    out["blockdiag"] = jnp.asarray(half[:, None] == half[None, :], dtype=BF16)
    for k in ("w_ssm_out", "w_ret_out", "w_diff_out", "w_o", "w_gate_up", "w_down"):
        out[k] = p[k].astype(BF16)
    out["b_gate"] = p["b_gate"]
    out["norm_ffn"] = p["norm_ffn"].reshape(1, D_MODEL)
    return out


def _token_tile(n, cap):
    tm = min(n, cap)
    while n % tm:
        tm //= 2
    return tm


def _layer_common(x, p, layer, vbuf, tail0, h0, s0, cos_t, sin_t, nvalid, attn_fn):
    b, tp, _ = x.shape
    n = b * tp
    x2d = x.reshape(n, D_MODEL)
    proj, kf, vbuf, dt_raw = _inproj(x2d, p["norm_mix"], p["w_main"], p["w_kvdt"], _token_tile(n, 1024), vbuf, layer)
    proj3 = proj.reshape(b, tp, C_MAIN)
    dt3 = dt_raw.reshape(b, tp, LANES)
    tpad = -tp % CHUNK
    scan_in = lambda a: jnp.pad(a, ((0, 0), (0, tpad), (0, 0))) if tpad else a
    y_ssm, h_new, conv_rows, y_ret, s_new = _scans(scan_in(proj3), scan_in(dt3), tail0, h0, cos_t, sin_t, s0, p,
                                                   nvalid)
    y_diff, k_new = attn_fn(proj3, kf.reshape(b, tp, KV_COLS), vbuf)
    tm = _token_tile(n, 512)
    h = _merge(y_ssm[:, :tp].reshape(n, -1), y_ret[:, :tp].reshape(n, -1), y_diff.reshape(n, -1), proj, x2d, p, tm)
    y = _ffn(h, p, tm)
    last = (nvalid - 1) % CONV_TAIL
    assert last >= CONV_W - 2
    return (y.reshape(b, tp, D_MODEL), k_new, vbuf,
            h_new.reshape(b, SSM_HEADS, SSM_HEAD_DIM, SSM_STATE), conv_rows[:, last - (CONV_W - 2):last + 1, :], s_new)


def kernel(x_prompt, x_sample, cache_k, cache_v, page_table, state_ssm, state_conv, state_ret, rel_bias, norm_mix, w_in, b_gate, conv_w, conv_b, dt_bias, a_log, d_skip, ssm_norm, w_ssm_out, ret_norm, w_ret_out, qk_norm_q, qk_norm_k, lambda_q1, lambda_k1, lambda_q2, lambda_k2, diff_norm, w_diff_out, w_o, norm_ffn, w_gate_up, w_down):
    named = dict(norm_mix=norm_mix, b_gate=b_gate, conv_w=conv_w, conv_b=conv_b, dt_bias=dt_bias, a_log=a_log,
                 d_skip=d_skip, ssm_norm=ssm_norm, w_ssm_out=w_ssm_out, ret_norm=ret_norm, w_ret_out=w_ret_out,
                 qk_norm_q=qk_norm_q, qk_norm_k=qk_norm_k, lambda_q1=lambda_q1, lambda_k1=lambda_k1,
                 lambda_q2=lambda_q2, lambda_k2=lambda_k2, diff_norm=diff_norm, w_diff_out=w_diff_out, w_o=w_o,
                 norm_ffn=norm_ffn, w_gate_up=w_gate_up, w_down=w_down)
    depth = w_in.shape[0]
    bp, seq, _ = x_prompt.shape
    bs, dec, _ = x_sample.shape
    n_pages = page_table.shape[1]
    past = n_pages * PAGE_SIZE
    assert seq % CHUNK == 0 and CONV_W - 1 <= dec <= SUBLANES
    assert _far_bucket_is_constant(CHUNK + 1)

    ii = jnp.arange(CHUNK)[:, None]
    jj = jnp.arange(CHUNK)[None, :]
    idx_p = jnp.concatenate([_t5_bucket(k * CHUNK + jj - ii) for k in range(3)], axis=0)
    bias_h = _bias_tiles(rel_bias, idx_p, LOG2E).reshape(DIFF_KV_HEADS, DIFF_HEADS // DIFF_KV_HEADS, 3, CHUNK, CHUNK)
    bias_p = jnp.concatenate([bias_h[:, r] for r in range(DIFF_HEADS // DIFF_KV_HEADS)] * 2, axis=-1)
    tt = jnp.arange(SUBLANES)[:, None]
    idx_s = jnp.concatenate([_t5_bucket(jnp.broadcast_to(past + tt, (SUBLANES, LANES))),
                             _t5_bucket(tt + PAGE_SIZE - jj), _t5_bucket(tt - jj)], axis=1)
    bias_s = _bias_tiles(rel_bias, idx_s, 1.0)

    cos_p, sin_p = _rope_tables(jnp.arange(seq))
    cos_s, sin_s = _rope_tables(past + jnp.arange(CHUNK))

    xs = jnp.pad(x_sample, ((0, 0), (0, SUBLANES - dec), (0, 0)))
    zeros_tail = jnp.zeros((bp, CONV_TAIL, CONV_DIM), F32)
    zeros_h = jnp.zeros((bp, SSM_D, SSM_STATE), F32)
    zeros_s = jnp.zeros((bp, RET_HEADS, RET_DK, RET_DV), F32)

    vbuf_p = jnp.zeros((depth, bp * seq * DIFF_KV_HEADS, DIFF_VD), F32)
    vbuf_s = jnp.zeros((depth, bs * SUBLANES * DIFF_KV_HEADS, DIFF_VD), F32)
    kbuf_p = jnp.zeros((depth, bp, KV_COLS, seq), F32)

    yp, ys = x_prompt, xs
    outs_p, outs_s = [], []
    for l in range(depth):
        lam_init = 0.8 - 0.6 * math.exp(-0.3 * l)
        p = _layer_params(l, w_in, named)

        def attn_p(proj3, kf, vbuf, l=l, p=p, lam_init=lam_init, kbuf=kbuf_p):
            return _diff_attention_prompt(proj3, kf, vbuf, kbuf, bias_p, p, l, lam_init)

        yp, kbuf_p, vbuf_p, h1, c1, r1 = _layer_common(yp, p, l, vbuf_p, zeros_tail, zeros_h, zeros_s, cos_p, sin_p,
                                                       CHUNK, attn_p)
        outs_p.append((h1, c1, r1))

        tail_s = jnp.pad(state_conv[l], ((0, 0), (CONV_TAIL - (CONV_W - 1), 0), (0, 0)))
        h0_s = state_ssm[l].reshape(bs, SSM_D, SSM_STATE)

        def attn_s(proj3, kf, vbuf, l=l, p=p, lam_init=lam_init):
            return _diff_attention_sample(proj3, kf, vbuf, cache_k, cache_v, page_table, bias_s, p, l, dec, lam_init)

        ys_new, k2, vbuf_s, h2, c2, r2 = _layer_common(ys, p, l, vbuf_s, tail_s, h0_s, state_ret[l], cos_s, sin_s, dec,
                                                       attn_s)
        ys = jnp.where(jnp.arange(SUBLANES)[None, :, None] < dec, ys_new, 0.0)
        outs_s.append((k2[:, :dec].reshape(bs, dec, DIFF_KV_HEADS, 2, DIFF_HD), h2, c2, r2))

    stack = lambda outs, i: jnp.stack([o[i] for o in outs])
    k_prompt = kbuf_p.reshape(depth, bp, DIFF_KV_HEADS, 2, DIFF_HD, seq).transpose(0, 1, 5, 2, 3, 4)
    v_prompt = vbuf_p.reshape(depth, bp, seq, DIFF_KV_HEADS, DIFF_VD)
    v_sample = vbuf_s.reshape(depth, bs, SUBLANES, DIFF_KV_HEADS, DIFF_VD)[:, :, :dec]
    return (yp, ys[:, :dec],
            k_prompt, v_prompt, stack(outs_p, 0), stack(outs_p, 1), stack(outs_p, 2),
            stack(outs_s, 0), v_sample, stack(outs_s, 1), stack(outs_s, 2), stack(outs_s, 3))
```

```python
import functools
import math

import numpy as np
import jax
import jax.numpy as jnp
from jax import lax
from jax.experimental import pallas as pl
from jax.experimental.pallas import tpu as pltpu

F32 = jnp.float32
BF16 = jnp.bfloat16

D_MODEL = 1024
SSM_HEADS = 16
SSM_HEAD_DIM = 64
SSM_D = SSM_HEADS * SSM_HEAD_DIM
SSM_STATE = 128
SSM_GROUPS = 4
CONV_W = 4
CONV_DIM = SSM_D + 2 * SSM_GROUPS * SSM_STATE
RET_HEADS = 4
RET_DK = 128
RET_DV = 256
ROPE_BASE = 10000.0
DIFF_HEADS = 8
DIFF_KV_HEADS = 4
DIFF_HD = 64
DIFF_VD = 2 * DIFF_HD
N_BUCKETS = 32
MAX_DISTANCE = 128
N_BRANCHES = 3
D_FF = 2816
NORM_EPS = 1e-6
PAGE_SIZE = 128

CHUNK = 128
LANES = 128
SUBLANES = 8
CONV_TAIL = 16
NEG_BIG = -1e30
LOG2E = 1.4426950408889634
VMEM_LIMIT = 52 * 1024 * 1024

IN_SPLITS = (SSM_D, CONV_DIM, SSM_HEADS,
             RET_HEADS * RET_DK, RET_HEADS * RET_DK, RET_HEADS * RET_DV, RET_HEADS * RET_DV,
             DIFF_HEADS * 2 * DIFF_HD, DIFF_KV_HEADS * 2 * DIFF_HD, DIFF_KV_HEADS * DIFF_VD,
             N_BRANCHES * D_MODEL)
IN_OFFSETS = tuple(int(v) for v in np.cumsum(IN_SPLITS)[:-1])

C_XBC, C_Z, C_RQ, C_RK, C_RV, C_RG, C_DQ, C_GATE, C_MAIN = 0, 2048, 3072, 3584, 4096, 5120, 6144, 7168, 10240
TN_PROJ = 2048
KV_COLS = DIFF_KV_HEADS * 2 * DIFF_HD


def _cparams(*sem):
    return pltpu.CompilerParams(dimension_semantics=sem, vmem_limit_bytes=VMEM_LIMIT)


def _nt_dot(a, b):
    return lax.dot_general(a, b, (((1,), (1,)), ((), ())), preferred_element_type=F32)


def _tn_dot(a, b):
    return lax.dot_general(a, b, (((0,), (0,)), ((), ())), preferred_element_type=F32)


def _dot(a, b):
    return jnp.dot(a, b, preferred_element_type=F32)


def _split3(x):
    hi = x.astype(BF16)
    r1 = x - hi.astype(F32)
    mid = r1.astype(BF16)
    lo = (r1 - mid.astype(F32)).astype(BF16)
    return hi, mid, lo


def _sigmoid(x):
    return 0.5 * jnp.tanh(0.5 * x) + 0.5


def _silu(x):
    h = 0.5 * x
    return h + h * jnp.tanh(h)


def _inproj_kernel(x_ref, g_ref, w_ref, wkvdt_ref, vprev_ref, main_ref, kf_ref, vf_ref, dt_ref, xn_ref):
    del vprev_ref
    tm = x_ref.shape[0]

    @pl.when(pl.program_id(1) == 0)
    def _():
        x = x_ref[...]
        ms = jnp.mean(x * x, axis=-1, keepdims=True)
        xn = (x * lax.rsqrt(ms + NORM_EPS) * g_ref[...]).astype(BF16)
        xn_ref[...] = xn
        kvdt = _dot(xn, wkvdt_ref[...])
        kf_ref[...] = kvdt[:, :KV_COLS]
        for g in range(DIFF_KV_HEADS):
            vf_ref[pl.ds(g, tm, stride=DIFF_KV_HEADS), :] = kvdt[:, KV_COLS + g * DIFF_VD:KV_COLS + (g + 1) * DIFF_VD]
        dt_ref[...] = kvdt[:, 2 * KV_COLS:]

    main_ref[...] = _dot(xn_ref[...], w_ref[...]).astype(BF16)


def _inproj(x2d, gain, w_main, w_kvdt, tm, vbuf, layer):
    n = x2d.shape[0]
    grid = (n // tm, C_MAIN // TN_PROJ)
    return pl.pallas_call(
        _inproj_kernel,
        grid=grid,
        in_specs=[
            pl.BlockSpec((tm, D_MODEL), lambda i, j: (i, 0)),
            pl.BlockSpec((1, D_MODEL), lambda i, j: (0, 0)),
            pl.BlockSpec((D_MODEL, TN_PROJ), lambda i, j: (0, j)),
            pl.BlockSpec((D_MODEL, 2 * KV_COLS + LANES), lambda i, j: (0, 0)),
            pl.BlockSpec(memory_space=pl.ANY),
        ],
        out_specs=[
            pl.BlockSpec((tm, TN_PROJ), lambda i, j: (i, j)),
            pl.BlockSpec((tm, KV_COLS), lambda i, j: (i, 0)),
            pl.BlockSpec((None, tm * DIFF_KV_HEADS, DIFF_VD), lambda i, j: (layer, i, 0)),
            pl.BlockSpec((tm, LANES), lambda i, j: (i, 0)),
        ],
        out_shape=[
            jax.ShapeDtypeStruct((n, C_MAIN), BF16),
            jax.ShapeDtypeStruct((n, KV_COLS), F32),
            jax.ShapeDtypeStruct(vbuf.shape, F32),
            jax.ShapeDtypeStruct((n, LANES), F32),
        ],
        scratch_shapes=[pltpu.VMEM((tm, D_MODEL), BF16)],
        input_output_aliases={4: 2},
        compiler_params=_cparams("arbitrary", "arbitrary"),
        name="inproj",
    )(x2d, gain, w_main, w_kvdt, vbuf)


def _ssd_kernel(xbc_ref, z_ref, dt_ref, tail0_ref, h0_ref, cw_ref, cb_ref, dtb_ref, alog_ref, dskip_ref,
                gn_ref, ex_ref, y_ref, hout_ref, convout_ref, xext_ref, tlo_ref, xc_ref, ht_ref, *, nvalid, nchunks):
    c = pl.program_id(1)
    L = CHUNK
    GW = SSM_D // SSM_GROUPS
    SL = 512

    @pl.when(c == 0)
    def _():
        t0 = tail0_ref[0]
        t0b = t0.astype(BF16)
        xext_ref[0:CONV_TAIL, :] = t0b
        tlo_ref[...] = (t0 - t0b.astype(F32)).astype(BF16)
        for k in range(SSM_D // LANES):
            ht_ref[:, k * LANES:(k + 1) * LANES] = h0_ref[0, k * LANES:(k + 1) * LANES, :].T

    xext_ref[CONV_TAIL:CONV_TAIL + L, :] = xbc_ref[0]
    srow = lax.broadcasted_iota(jnp.int32, (L, CONV_TAIL + L), 0)
    scol = lax.broadcasted_iota(jnp.int32, (L, CONV_TAIL + L), 1)
    shifts = [jnp.where(scol == srow + CONV_TAIL - s, 1.0, 0.0).astype(BF16) for s in range(1, CONV_W)]
    shifted = [[_dot(shifts[s - 1], xext_ref[:, sl * SL:(sl + 1) * SL]) for s in range(1, CONV_W)]
               for sl in range(CONV_DIM // SL)]
    for sl in range(CONV_DIM // SL):
        cols = slice(sl * SL, (sl + 1) * SL)
        conv = cw_ref[CONV_W - 1:CONV_W, cols] * xbc_ref[0, :, cols].astype(F32) + cb_ref[:, cols]
        for s in range(1, CONV_W):
            conv = conv + cw_ref[CONV_W - 1 - s:CONV_W - s, cols] * shifted[sl][s - 1]
        xc_ref[:, cols] = conv

    @pl.when(c == 0)
    def _():
        for sl in range(CONV_DIM // SL):
            cols = slice(sl * SL, (sl + 1) * SL)
            corr = None
            for s in range(1, CONV_W):
                t = cw_ref[CONV_W - 1 - s:CONV_W - s, cols] * _dot(shifts[s - 1][0:CONV_TAIL, 0:CONV_TAIL],
                                                                    tlo_ref[:, cols])
                corr = t if corr is None else corr + t
            xc_ref[0:CONV_TAIL, cols] += corr

    @pl.when(c == nchunks - 1)
    def _():
        a = CONV_TAIL * ((nvalid - 1) // CONV_TAIL)
        convout_ref[0] = xext_ref[CONV_TAIL + a:CONV_TAIL + a + CONV_TAIL, :].astype(F32)

    xext_ref[0:CONV_TAIL, :] = xext_ref[L:L + CONV_TAIL, :]

    row = lax.broadcasted_iota(jnp.int32, (L, L), 0)
    col = lax.broadcasted_iota(jnp.int32, (L, L), 1)
    causal = row >= col
    left = col < SSM_HEAD_DIM

    x = dt_ref[0] + dtb_ref[...]
    dt = jnp.maximum(x, 0.0) + jnp.log1p(jnp.exp(-jnp.abs(x)))
    if nvalid < L:
        dt = jnp.where(row < nvalid, dt, 0.0)
    a = -jnp.exp(alog_ref[...])
    da = dt * a
    tri = jnp.where(causal, 1.0, 0.0).astype(BF16)
    cs = sum(_dot(tri, p) for p in _split3(da))
    cs_t = cs.T
    cs_parts = _split3(cs)
    dt_parts = _split3(dt)

    NG = SSM_GROUPS
    HPG = SSM_HEADS // SSM_GROUPS
    gcs = [slice(g * GW, (g + 1) * GW) for g in range(NG)]
    bcol = lambda g: slice(SSM_D + g * SSM_STATE, SSM_D + (g + 1) * SSM_STATE)
    ccol = lambda g: slice(SSM_D + NG * SSM_STATE + g * SSM_STATE, SSM_D + NG * SSM_STATE + (g + 1) * SSM_STATE)

    csx = [sum(_dot(p, ex_ref[:, gc]) for p in cs_parts) for gc in gcs]
    dtx = [sum(_dot(p, ex_ref[:, gc]) for p in dt_parts) for gc in gcs]
    bgs = [_silu(xc_ref[:, bcol(g)]).astype(BF16) for g in range(NG)]
    cgs = [_silu(xc_ref[:, ccol(g)]).astype(BF16) for g in range(NG)]
    cbs = [_nt_dot(cgs[g], bgs[g]) for g in range(NG)]
    htgs = [ht_ref[:, gc] for gc in gcs]
    ysts = [_dot(cgs[g], htgs[g].astype(BF16)) for g in range(NG)]
    xss = [_silu(xc_ref[:, gc]) for gc in gcs]
    xdts = [xss[g] * dtx[g] for g in range(NG)]
    xdt_bs = [x_.astype(BF16) for x_ in xdts]
    lastxs = [csx[g][L - 1:L, :] for g in range(NG)]
    xdtw_bs = [(xdts[g] * jnp.exp(lastxs[g] - csx[g])).astype(BF16) for g in range(NG)]
    prods = []
    for h in range(SSM_HEADS):
        g, pr = h // HPG, (h % HPG) // 2
        seg = cs[:, h:h + 1] - cs_t[h:h + 1, :]
        decay = jnp.exp(jnp.where(causal, seg, NEG_BIG))
        mat = (cbs[g] * decay).astype(BF16)
        xp = xdt_bs[g][:, pr * LANES:(pr + 1) * LANES]
        keep = left if h % 2 == 0 else jnp.logical_not(left)
        prods.append(_dot(mat, jnp.where(keep, xp, jnp.zeros_like(xp))))
    upd = [_tn_dot(bgs[g], xdtw_bs[g]) for g in range(NG)]
    for g in range(NG):
        ht_ref[:, gcs[g]] = htgs[g] * jnp.exp(lastxs[g]) + upd[g]
    y_ins = [jnp.concatenate([prods[g * HPG + 2 * pr] + prods[g * HPG + 2 * pr + 1] for pr in range(HPG // 2)], axis=1)
             for g in range(NG)]
    for g in range(NG):
        gc = gcs[g]
        y = y_ins[g] + ysts[g] * jnp.exp(csx[g]) + dskip_ref[:, gc] * xss[g]
        y = y * _silu(z_ref[0, :, gc].astype(F32))
        ms = jnp.mean(y * y, axis=-1, keepdims=True)
        y_ref[0, :, gc] = (y * lax.rsqrt(ms + NORM_EPS) * gn_ref[:, gc]).astype(BF16)

    @pl.when(c == nchunks - 1)
    def _():
        for k in range(SSM_D // LANES):
            hout_ref[0, k * LANES:(k + 1) * LANES, :] = ht_ref[:, k * LANES:(k + 1) * LANES].T


def _ret_kernel(q_ref, k_ref, v_ref, rg_ref, cos_ref, sin_ref, s0_ref, gn_ref, y_ref, sout_ref, *, ltrue):
    c = pl.program_id(1)
    L = CHUNK

    @pl.when(c == 0)
    def _():
        sout_ref[...] = s0_ref[...]

    row = lax.broadcasted_iota(jnp.int32, (L, L), 0)
    col = lax.broadcasted_iota(jnp.int32, (L, L), 1)
    rel = (row - col).astype(F32)
    idx = row[:, 0:1].astype(F32)
    cosf = cos_ref[...]
    sins = sin_ref[...]
    H = RET_HEADS
    lgs = [math.log(1.0 - 2.0 ** (-5.0 - h)) for h in range(H)]
    kcs = [slice(h * RET_DK, (h + 1) * RET_DK) for h in range(H)]
    vcs = [slice(h * RET_DV, (h + 1) * RET_DV) for h in range(H)]
    qrs, krs = [], []
    for h in range(H):
        qh = q_ref[0, :, kcs[h]].astype(F32)
        kh = k_ref[0, :, kcs[h]].astype(F32)
        qrs.append(qh * cosf + pltpu.roll(qh, RET_DK // 2, 1) * sins)
        krs.append((kh * cosf + pltpu.roll(kh, RET_DK // 2, 1) * sins) * RET_DK ** -0.5)
    qr_bs = [q_.astype(BF16) for q_ in qrs]
    vhs = [v_ref[0, :, vcs[h]] for h in range(H)]
    s_olds = [sout_ref[0, h] for h in range(H)]
    scores = [_nt_dot(qr_bs[h], krs[h].astype(BF16)) for h in range(H)]
    cross = [_dot(qr_bs[h], s_olds[h].astype(BF16)) for h in range(H)]
    atts = []
    for h in range(H):
        dmat = jnp.where(rel >= 0, jnp.exp(jnp.maximum(rel, 0.0) * lgs[h]), 0.0)
        atts.append((scores[h] * dmat).astype(BF16))
    inner = [_dot(atts[h], vhs[h]) for h in range(H)]
    for h in range(H):
        k_dec = jnp.exp((ltrue - 1.0 - idx) * lgs[h])
        sout_ref[0, h] = s_olds[h] * math.exp(ltrue * lgs[h]) + _tn_dot((krs[h] * k_dec).astype(BF16), vhs[h])
    for h in range(H):
        o = inner[h] + cross[h] * jnp.exp((idx + 1.0) * lgs[h])
        oc = o - jnp.mean(o, axis=-1, keepdims=True)
        on = oc * lax.rsqrt(jnp.mean(oc * oc, axis=-1, keepdims=True) + NORM_EPS)
        y_ref[0, :, vcs[h]] = (on * gn_ref[:, vcs[h]] * _silu(rg_ref[0, :, vcs[h]].astype(F32))).astype(BF16)


N_SSD_IN, N_RET_IN, N_SSD_OUT, N_RET_OUT = 12, 8, 3, 2


def _scan_kernel(*refs, nvalid, nchunks):
    i0 = N_SSD_IN
    i1 = i0 + N_RET_IN
    i2 = i1 + N_SSD_OUT
    i3 = i2 + N_RET_OUT
    _ret_kernel(*refs[i0:i1], *refs[i2:i3], ltrue=float(nvalid))
    _ssd_kernel(*refs[0:i0], *refs[i1:i2], *refs[i3:], nvalid=nvalid, nchunks=nchunks)


def _scans(proj, dt_raw, tail0, h0, cos_t, sin_t, s0, p, nvalid):
    b, t, _ = proj.shape
    nchunks = t // CHUNK
    L = CHUNK
    qk_w = RET_HEADS * RET_DK
    v_w = RET_HEADS * RET_DV
    const = lambda shape: pl.BlockSpec(shape, lambda i, c: (0,) * len(shape))
    ssd_in = [
        pl.BlockSpec((1, L, CONV_DIM), lambda i, c: (i, c, C_XBC // CONV_DIM)),
        pl.BlockSpec((1, L, SSM_D), lambda i, c: (i, c, C_Z // SSM_D)),
        pl.BlockSpec((1, L, LANES), lambda i, c: (i, c, 0)),
        pl.BlockSpec((1, CONV_TAIL, CONV_DIM), lambda i, c: (i, 0, 0)),
        pl.BlockSpec((1, SSM_D, SSM_STATE), lambda i, c: (i, 0, 0)),
        const((CONV_W, CONV_DIM)), const((1, CONV_DIM)), const((1, LANES)), const((1, LANES)),
        const((1, SSM_D)), const((1, SSM_D)), const((LANES, SSM_D)),
    ]
    ret_in = [
        pl.BlockSpec((1, L, qk_w), lambda i, c: (i, c, C_RQ // qk_w)),
        pl.BlockSpec((1, L, qk_w), lambda i, c: (i, c, C_RK // qk_w)),
        pl.BlockSpec((1, L, v_w), lambda i, c: (i, c, C_RV // v_w)),
        pl.BlockSpec((1, L, v_w), lambda i, c: (i, c, C_RG // v_w)),
        pl.BlockSpec((L, RET_DK), lambda i, c: (c, 0)),
        pl.BlockSpec((L, RET_DK), lambda i, c: (c, 0)),
        pl.BlockSpec((1, RET_HEADS, RET_DK, RET_DV), lambda i, c: (i, 0, 0, 0)),
        const((1, v_w)),
    ]
    ssd_out = [
        pl.BlockSpec((1, L, SSM_D), lambda i, c: (i, c, 0)),
        pl.BlockSpec((1, SSM_D, SSM_STATE), lambda i, c: (i, 0, 0)),
        pl.BlockSpec((1, CONV_TAIL, CONV_DIM), lambda i, c: (i, 0, 0)),
    ]
    ret_out = [
        pl.BlockSpec((1, L, v_w), lambda i, c: (i, c, 0)),
        pl.BlockSpec((1, RET_HEADS, RET_DK, RET_DV), lambda i, c: (i, 0, 0, 0)),
    ]
    assert (len(ssd_in), len(ret_in), len(ssd_out), len(ret_out)) == (N_SSD_IN, N_RET_IN, N_SSD_OUT, N_RET_OUT)
    return pl.pallas_call(
        functools.partial(_scan_kernel, nvalid=nvalid, nchunks=nchunks),
        grid=(b, nchunks),
        in_specs=ssd_in + ret_in,
        out_specs=ssd_out + ret_out,
        out_shape=[
            jax.ShapeDtypeStruct((b, t, SSM_D), BF16),
            jax.ShapeDtypeStruct((b, SSM_D, SSM_STATE), F32),
            jax.ShapeDtypeStruct((b, CONV_TAIL, CONV_DIM), F32),
            jax.ShapeDtypeStruct((b, t, v_w), BF16),
            jax.ShapeDtypeStruct((b, RET_HEADS, RET_DK, RET_DV), F32),
        ],
        scratch_shapes=[
            pltpu.VMEM((CONV_TAIL + L, CONV_DIM), BF16),
            pltpu.VMEM((CONV_TAIL, CONV_DIM), BF16),
            pltpu.VMEM((L, CONV_DIM), F32),
            pltpu.VMEM((SSM_STATE, SSM_D), F32),
        ],
        compiler_params=_cparams("arbitrary", "arbitrary"),
        name="scans",
    )(proj, proj, dt_raw, tail0, h0, p["conv_w"], p["conv_b"], p["dt_bias"], p["a_log"], p["d_skip"],
      p["ssm_norm"], p["head_expand"], proj, proj, proj, proj, cos_t, sin_t, s0, p["ret_norm"])


def _bias_kernel(tab_ref, idx_ref, out_ref, *, scale):
    h = pl.program_id(0)
    idx = idx_ref[...]
    acc = jnp.zeros(idx.shape, F32)
    for b in range(N_BUCKETS):
        acc = acc + jnp.where(idx == b, tab_ref[b * DIFF_HEADS + h], 0.0)
    out_ref[0] = acc * scale


def _bias_tiles(rel_bias, idx, scale):
    r, c = idx.shape
    return pl.pallas_call(
        functools.partial(_bias_kernel, scale=scale),
        grid=(DIFF_HEADS,),
        in_specs=[pl.BlockSpec(memory_space=pltpu.SMEM), pl.BlockSpec((r, c), lambda h: (0, 0))],
        out_specs=pl.BlockSpec((1, r, c), lambda h: (h, 0, 0)),
        out_shape=jax.ShapeDtypeStruct((DIFF_HEADS, r, c), F32),
        compiler_params=_cparams("arbitrary"),
        name="t5_bias",
    )(rel_bias.reshape(-1), idx)


def _half_rmsnorm(x, gain, bd):
    x2 = x * x
    hi = x2.astype(BF16)
    lo = (x2 - hi.astype(F32)).astype(BF16)
    ss = _dot(hi, bd) + _dot(lo, bd)
    return x * lax.rsqrt(ss * (1.0 / DIFF_HD) + NORM_EPS) * gain


def _half_rmsnorm_blocks(blocks, gain, bd):
    rows = blocks[0].shape[0]
    y = _half_rmsnorm(jnp.concatenate(blocks, axis=0), gain, bd)
    return [y[i * rows:(i + 1) * rows] for i in range(len(blocks))]


def _lambda(lamv_ref, lam_init):
    s1 = jnp.sum(lamv_ref[0:1, :] * lamv_ref[1:2, :], axis=-1, keepdims=True)
    s2 = jnp.sum(lamv_ref[2:3, :] * lamv_ref[3:4, :], axis=-1, keepdims=True)
    return jnp.exp(s1) - jnp.exp(s2) + lam_init


FAR_UNITS = 4


def _dattn_kernel(q_ref, kf_ref, vf_ref, bias_ref, gq_ref, gk_ref, lamv_ref, sgc_ref, bd_ref, kprev_ref,
                  y_ref, kout_ref, kn_ref, vt_ref, qs_ref, m_ref, l_ref, acc_ref, *, t, lam_init):
    del kprev_ref
    qi = pl.program_id(1)
    TQ = CHUNK
    G = DIFF_KV_HEADS
    R = DIFF_HEADS // DIFF_KV_HEADS
    NC = 2 * R * TQ
    bd = bd_ref[...]

    @pl.when(qi == 0)
    def _():
        for i in range(t // TQ):
            r = slice(i * TQ, (i + 1) * TQ)
            kns = _half_rmsnorm_blocks([kf_ref[0, r, g * LANES:(g + 1) * LANES] for g in range(G)], gk_ref[...], bd)
            for g in range(G):
                kout_ref[0, g * LANES:(g + 1) * LANES, r] = kns[g].T
                kn_ref[g, r, :] = kns[g].astype(BF16)
                vt_ref[g, i] = vf_ref[0, pl.ds(i * TQ * G + g, TQ, stride=G), :].T.astype(BF16)

    lane = lax.broadcasted_iota(jnp.int32, (TQ, LANES), 1)
    left = lane < DIFF_HD
    qns = _half_rmsnorm_blocks([q_ref[0, :, h * LANES:(h + 1) * LANES].astype(F32) for h in range(DIFF_HEADS)],
                               gq_ref[...], bd)
    for g in range(G):
        for r in range(R):
            qn = qns[g * R + r] * (DIFF_HD ** -0.5 * LOG2E)
            qs_ref[g, (0 * R + r) * TQ:(0 * R + r + 1) * TQ, :] = jnp.where(left, qn, 0.0).astype(BF16)
            qs_ref[g, (1 * R + r) * TQ:(1 * R + r + 1) * TQ, :] = jnp.where(left, 0.0, qn).astype(BF16)
    m_ref[...] = jnp.full(m_ref.shape, NEG_BIG, F32)
    l_ref[...] = jnp.zeros(l_ref.shape, F32)
    acc_ref[...] = jnp.zeros(acc_ref.shape, F32)

    def step(k0, tiles):
        nunits = len(tiles)
        rows = pl.ds(pl.multiple_of(k0 * TQ, TQ), nunits * TQ)
        ss = [_nt_dot(kn_ref[g, rows, :], qs_ref[g]) for g in range(G)]
        ps, alphas = [], []
        for g in range(G):
            s = ss[g] + jnp.concatenate([bias_ref[g, tl] for tl in tiles], axis=0)
            if 0 in tiles:
                first = tiles.index(0) * TQ
                key = lax.broadcasted_iota(jnp.int32, (nunits * TQ, NC), 0) - first
                qry = lax.broadcasted_iota(jnp.int32, (nunits * TQ, NC), 1) & (TQ - 1)
                s = jnp.where(key <= qry, s, NEG_BIG)
            m_old = m_ref[g]
            m_new = jnp.maximum(m_old, jnp.max(s, axis=0, keepdims=True))
            alpha = jnp.exp2(m_old - m_new)
            p = jnp.exp2(s - m_new)
            l_ref[g] = alpha * l_ref[g] + jnp.sum(p, axis=0, keepdims=True)
            m_ref[g] = m_new
            ps.append(p.astype(BF16))
            alphas.append(alpha)
        for g in range(G):
            if nunits == 1:
                vt = vt_ref[g, k0]
            else:
                vt = jnp.concatenate([vt_ref[g, k0 + u] for u in range(nunits)], axis=1)
            acc_ref[g] = acc_ref[g] * alphas[g] + _dot(vt, ps[g])

    nfar = jnp.maximum(qi - 1, 0)
    nbig = nfar // FAR_UNITS
    rem = nfar - nbig * FAR_UNITS

    def far_big(i, carry):
        step(i * FAR_UNITS, (2,) * FAR_UNITS)
        return carry

    lax.fori_loop(0, nbig, far_big, 0)

    @pl.when(rem >= 2)
    def _():
        step(nbig * FAR_UNITS, (2, 2))

    @pl.when((rem & 1) == 1)
    def _():
        step(nbig * FAR_UNITS + (rem & 2), (2,))

    @pl.when(qi >= 1)
    def _():
        step(qi - 1, (1, 0))

    @pl.when(qi == 0)
    def _():
        step(0, (0,))

    lam = _lambda(lamv_ref, lam_init)
    for g in range(G):
        inv_l = 1.0 / l_ref[g]
        for r in range(R):
            c0 = slice((0 * R + r) * TQ, (0 * R + r + 1) * TQ)
            c1 = slice((1 * R + r) * TQ, (1 * R + r + 1) * TQ)
            o = acc_ref[g, :, c0] * inv_l[:, c0] - lam * (acc_ref[g, :, c1] * inv_l[:, c1])
            o = o * lax.rsqrt(jnp.mean(o * o, axis=0, keepdims=True) + NORM_EPS)
            o = o * sgc_ref[...] * (1.0 - lam_init)
            y_ref[0, :, (g * R + r) * LANES:(g * R + r + 1) * LANES] = o.T.astype(BF16)


def _diff_attention_prompt(proj, kf, vbuf, kbuf, bias, p, layer, lam_init):
    b, t, _ = proj.shape
    vf = vbuf.reshape(vbuf.shape[0], b, t * DIFF_KV_HEADS, DIFF_VD)
    TQ = CHUNK
    G = DIFF_KV_HEADS
    R = DIFF_HEADS // DIFF_KV_HEADS
    NC = 2 * R * TQ
    qw = DIFF_HEADS * 2 * DIFF_HD
    const = lambda shape: pl.BlockSpec(shape, lambda i, q: (0,) * len(shape))
    return pl.pallas_call(
        functools.partial(_dattn_kernel, t=t, lam_init=lam_init),
        grid=(b, t // TQ),
        in_specs=[
            pl.BlockSpec((1, TQ, qw), lambda i, q: (i, q, C_DQ // qw)),
            pl.BlockSpec((1, t, KV_COLS), lambda i, q: (i, 0, 0)),
            pl.BlockSpec((None, 1, t * G, DIFF_VD), lambda i, q: (layer, i, 0, 0)),
            const((G, 3, TQ, NC)),
            const((1, LANES)), const((1, LANES)), const((4, LANES)), const((DIFF_VD, 1)), const((LANES, LANES)),
            pl.BlockSpec(memory_space=pl.ANY),
        ],
        out_specs=[
            pl.BlockSpec((1, TQ, qw), lambda i, q: (i, q, 0)),
            pl.BlockSpec((None, 1, KV_COLS, t), lambda i, q: (layer, i, 0, 0)),
        ],
        out_shape=[
            jax.ShapeDtypeStruct((b, t, DIFF_HEADS * DIFF_VD), BF16),
            jax.ShapeDtypeStruct(kbuf.shape, F32),
        ],
        input_output_aliases={9: 1},
        scratch_shapes=[
            pltpu.VMEM((G, t, LANES), BF16), pltpu.VMEM((G, t // TQ, DIFF_VD, TQ), BF16),
            pltpu.VMEM((G, NC, LANES), BF16),
            pltpu.VMEM((G, 1, NC), F32), pltpu.VMEM((G, 1, NC), F32),
            pltpu.VMEM((G, DIFF_VD, NC), F32),
        ],
        compiler_params=_cparams("arbitrary", "arbitrary"),
        name="diff_attn_prompt",
    )(proj, kf, vf, bias, p["qk_norm_q"], p["qk_norm_k"], p["lamv"], p["diff_norm_col"], p["blockdiag"], kbuf)


PAGES_PER_STEP = 32


def _sattn_kernel(pt_ref, *refs, layer, nvalid, lam_init, npp):
    del pt_ref, layer
    k_refs = refs[0:npp]
    v_refs = refs[npp:2 * npp]
    (q_ref, kf_ref, vf_ref, bias_ref, gq_ref, gk_ref, lamv_ref, sg_ref, bd_ref,
     y_ref, kout_ref, qs_ref, knew_ref, vnew_ref, m_ref, l_ref, acc_ref) = refs[2 * npp:]
    s_id = pl.program_id(1)
    nsteps = pl.num_programs(1)
    G = DIFF_KV_HEADS
    R = DIFF_HEADS // DIFF_KV_HEADS
    TP = SUBLANES
    MR = 2 * R * TP
    bd = bd_ref[...]
    lane = lax.broadcasted_iota(jnp.int32, (TP, LANES), 1)
    left = lane < DIFF_HD

    @pl.when(s_id == 0)
    def _():
        knew_ref[...] = jnp.zeros(knew_ref.shape, BF16)
        vnew_ref[...] = jnp.zeros(vnew_ref.shape, BF16)
        for g in range(G):
            gc = slice(g * LANES, (g + 1) * LANES)
            kn = _half_rmsnorm(kf_ref[0, :, gc], gk_ref[...], bd)
            kout_ref[0, :, gc] = kn
            knew_ref[g, 0:TP, :] = kn.astype(BF16)
            vnew_ref[g, 0:TP, :] = vf_ref[0, pl.ds(g, TP, stride=G), :].astype(BF16)
            for r in range(R):
                hc = slice((g * R + r) * LANES, (g * R + r + 1) * LANES)
                qn = _half_rmsnorm(q_ref[0, :, hc].astype(F32), gq_ref[...], bd) * DIFF_HD ** -0.5
                qs_ref[g, (0 * R + r) * TP:(0 * R + r + 1) * TP, :] = jnp.where(left, qn, 0.0).astype(BF16)
                qs_ref[g, (1 * R + r) * TP:(1 * R + r + 1) * TP, :] = jnp.where(left, 0.0, qn).astype(BF16)
        m_ref[...] = jnp.full(m_ref.shape, NEG_BIG, F32)
        l_ref[...] = jnp.zeros(l_ref.shape, F32)
        acc_ref[...] = jnp.zeros(acc_ref.shape, F32)

    def bias_rows(g, seg):
        per_head = [bias_ref[g * R + r, :, seg * LANES:(seg + 1) * LANES] for r in range(R)]
        return jnp.concatenate(per_head + per_head, axis=0)

    def update(g, s, vs):
        m_old = m_ref[g]
        m_new = jnp.maximum(m_old, jnp.max(s, axis=-1, keepdims=True))
        alpha = jnp.exp(m_old - m_new)
        p = jnp.exp(s - m_new)
        l_ref[g] = alpha * l_ref[g] + jnp.sum(p, axis=-1, keepdims=True)
        m_ref[g] = m_new
        pb = p.astype(BF16)
        pv = _dot(pb[:, 0:LANES], vs[0])
        for i in range(1, len(vs)):
            pv = pv + _dot(pb[:, i * LANES:(i + 1) * LANES], vs[i])
        acc_ref[g] = acc_ref[g] * alpha + pv

    def pages(last):
        scores = []
        for g in range(G):
            gc = slice(g * LANES, (g + 1) * LANES)
            kcat = jnp.concatenate([k_refs[i][gc, :].astype(BF16) for i in range(npp)], axis=1)
            far = bias_rows(g, 0)
            near = bias_rows(g, 1) if last else far
            scores.append(_dot(qs_ref[g], kcat) + jnp.concatenate([far] * (npp - 1) + [near], axis=1))
        pvs, alphas = [], []
        for g in range(G):
            s = scores[g]
            m_old = m_ref[g]
            m_new = jnp.maximum(m_old, jnp.max(s, axis=-1, keepdims=True))
            alpha = jnp.exp(m_old - m_new)
            p = jnp.exp(s - m_new)
            l_ref[g] = alpha * l_ref[g] + jnp.sum(p, axis=-1, keepdims=True)
            m_ref[g] = m_new
            vcat = jnp.concatenate([v_refs[i][pl.ds(g, PAGE_SIZE, stride=G), :].astype(BF16) for i in range(npp)],
                                   axis=0)
            pvs.append(_dot(p.astype(BF16), vcat))
            alphas.append(alpha)
        for g in range(G):
            acc_ref[g] = acc_ref[g] * alphas[g] + pvs[g]

    @pl.when(s_id < nsteps - 1)
    def _():
        pages(False)

    @pl.when(s_id == nsteps - 1)
    def _():
        pages(True)
        rowt = lax.broadcasted_iota(jnp.int32, (MR, LANES), 0) % TP
        colj = lax.broadcasted_iota(jnp.int32, (MR, LANES), 1)
        ok = jnp.logical_and(colj <= rowt, colj < nvalid)
        lam = _lambda(lamv_ref, lam_init)
        for g in range(G):
            sc = _nt_dot(qs_ref[g], knew_ref[g]) + bias_rows(g, 2)
            update(g, jnp.where(ok, sc, NEG_BIG), [vnew_ref[g]])
            acc = acc_ref[g] / l_ref[g]
            for r in range(R):
                o = acc[(0 * R + r) * TP:(0 * R + r + 1) * TP, :] - lam * acc[(1 * R + r) * TP:(1 * R + r + 1) * TP, :]
                o = o * lax.rsqrt(jnp.mean(o * o, axis=-1, keepdims=True) + NORM_EPS)
                hc = slice((g * R + r) * LANES, (g * R + r + 1) * LANES)
                y_ref[0, :, hc] = (o * sg_ref[...] * (1.0 - lam_init)).astype(BF16)


def _diff_attention_sample(proj, kf, vbuf, cache_k, cache_v, page_table, bias, p, layer, nvalid, lam_init):
    b, tp, _ = proj.shape
    vf = vbuf.reshape(vbuf.shape[0], b, tp * DIFF_KV_HEADS, DIFF_VD)
    n_pages = page_table.shape[1]
    npp = PAGES_PER_STEP
    while n_pages % npp:
        npp //= 2
    nsteps = n_pages // npp
    G = DIFF_KV_HEADS
    R = DIFF_HEADS // DIFF_KV_HEADS
    MR = 2 * R * tp
    ck = jnp.transpose(cache_k, (0, 1, 3, 4, 5, 2)).reshape(cache_k.shape[0], cache_k.shape[1], KV_COLS, PAGE_SIZE)
    cv = cache_v.reshape(cache_v.shape[0], cache_v.shape[1], PAGE_SIZE * DIFF_KV_HEADS, DIFF_VD)

    def page_spec(i):
        return pl.BlockSpec((None, None, KV_COLS, PAGE_SIZE),
                            lambda bi, s, pt: (layer, pt[bi * n_pages + s * npp + i], 0, 0))

    const = lambda shape: pl.BlockSpec(shape, lambda bi, s, pt: (0,) * len(shape))
    grid_spec = pltpu.PrefetchScalarGridSpec(
        num_scalar_prefetch=1,
        grid=(b, nsteps),
        in_specs=[page_spec(i) for i in range(npp)] + [page_spec(i) for i in range(npp)] + [
            pl.BlockSpec((1, tp, DIFF_HEADS * 2 * DIFF_HD), lambda bi, s, pt: (bi, 0, C_DQ // (DIFF_HEADS * 2 * DIFF_HD))),
            pl.BlockSpec((1, tp, KV_COLS), lambda bi, s, pt: (bi, 0, 0)),
            pl.BlockSpec((None, 1, tp * G, DIFF_VD), lambda bi, s, pt: (layer, bi, 0, 0)),
            const((DIFF_HEADS, tp, 3 * LANES)),
            const((1, LANES)), const((1, LANES)), const((4, LANES)), const((1, LANES)), const((LANES, LANES)),
        ],
        out_specs=[
            pl.BlockSpec((1, tp, DIFF_HEADS * DIFF_VD), lambda bi, s, pt: (bi, 0, 0)),
            pl.BlockSpec((1, tp, KV_COLS), lambda bi, s, pt: (bi, 0, 0)),
        ],
        scratch_shapes=[
            pltpu.VMEM((G, MR, LANES), BF16),
            pltpu.VMEM((G, PAGE_SIZE, LANES), BF16), pltpu.VMEM((G, PAGE_SIZE, LANES), BF16),
            pltpu.VMEM((G, MR, 1), F32), pltpu.VMEM((G, MR, 1), F32), pltpu.VMEM((G, MR, LANES), F32),
        ],
    )
    return pl.pallas_call(
        functools.partial(_sattn_kernel, layer=layer, nvalid=nvalid, lam_init=lam_init, npp=npp),
        grid_spec=grid_spec,
        out_shape=[
            jax.ShapeDtypeStruct((b, tp, DIFF_HEADS * DIFF_VD), BF16),
            jax.ShapeDtypeStruct((b, tp, KV_COLS), F32),
        ],
        compiler_params=_cparams("arbitrary", "arbitrary"),
        name="diff_attn_sample",
    )(page_table.reshape(-1), *([ck] * npp), *([cv] * npp), proj, kf, vf, bias,
      p["qk_norm_q"], p["qk_norm_k"], p["lamv"], p["diff_norm"], p["blockdiag"])


def _merge_kernel(ys_ref, yr_ref, yd_ref, g0_ref, g1_ref, g2_ref, x_ref, ws_ref, wr_ref, wd_ref, wo_ref,
                  bg_ref, h_ref):
    merged = None
    for i, (y_ref, w_ref, g_ref) in enumerate(((ys_ref, ws_ref, g0_ref), (yr_ref, wr_ref, g1_ref),
                                               (yd_ref, wd_ref, g2_ref))):
        br = _dot(y_ref[...], w_ref[...])
        t = _sigmoid(g_ref[...].astype(F32) + bg_ref[i:i + 1, :]) * br
        merged = t if merged is None else merged + t
    h_ref[...] = x_ref[...] + _dot(merged.astype(BF16), wo_ref[...])


def _merge(y_ssm, y_ret, y_diff, proj2d, x2d, p, tm):
    n = x2d.shape[0]
    tok = lambda cb: pl.BlockSpec((tm, D_MODEL), lambda i: (i, cb))
    wspec = pl.BlockSpec((D_MODEL, D_MODEL), lambda i: (0, 0))
    g0 = C_GATE // D_MODEL
    return pl.pallas_call(
        _merge_kernel,
        grid=(n // tm,),
        in_specs=[tok(0), tok(0), tok(0), tok(g0), tok(g0 + 1), tok(g0 + 2), tok(0),
                  wspec, wspec, wspec, wspec, pl.BlockSpec((N_BRANCHES, D_MODEL), lambda i: (0, 0))],
        out_specs=tok(0),
        out_shape=jax.ShapeDtypeStruct((n, D_MODEL), F32),
        compiler_params=_cparams("arbitrary"),
        name="merge",
    )(y_ssm, y_ret, y_diff, proj2d, proj2d, proj2d, x2d, p["w_ssm_out"], p["w_ret_out"], p["w_diff_out"],
      p["w_o"], p["b_gate"])


def _ffn_kernel(h_ref, g_ref, wg_ref, wu_ref, wd_ref, y_ref):
    h = h_ref[...]
    ms = jnp.mean(h * h, axis=-1, keepdims=True)
    hn = (h * lax.rsqrt(ms + NORM_EPS) * g_ref[...]).astype(BF16)
    act = _silu(_dot(hn, wg_ref[...])) * _dot(hn, wu_ref[...])
    y_ref[...] = h + _dot(act.astype(BF16), wd_ref[...])


def _ffn(h2d, p, tm):
    n = h2d.shape[0]
    resident = lambda shape, cb: pl.BlockSpec(shape, lambda i: (0, cb), pipeline_mode=pl.Buffered(1))
    return pl.pallas_call(
        _ffn_kernel,
        grid=(n // tm,),
        in_specs=[
            pl.BlockSpec((tm, D_MODEL), lambda i: (i, 0)),
            pl.BlockSpec((1, D_MODEL), lambda i: (0, 0)),
            resident((D_MODEL, D_FF), 0),
            resident((D_MODEL, D_FF), 1),
            resident((D_FF, D_MODEL), 0),
        ],
        out_specs=pl.BlockSpec((tm, D_MODEL), lambda i: (i, 0)),
        out_shape=jax.ShapeDtypeStruct((n, D_MODEL), F32),
        compiler_params=_cparams("arbitrary"),
        name="ffn",
    )(h2d, p["norm_ffn"], p["w_gate_up"], p["w_gate_up"], p["w_down"])


def _t5_bucket(dist):
    n = jnp.maximum(dist, 0)
    max_exact = N_BUCKETS // 2
    large = max_exact + (jnp.log(jnp.maximum(n, 1).astype(F32) / max_exact)
                         / math.log(MAX_DISTANCE / max_exact) * (N_BUCKETS - max_exact)).astype(jnp.int32)
    large = jnp.minimum(large, N_BUCKETS - 1)
    return jnp.where(n < max_exact, n, large)


def _far_bucket_is_constant(min_dist):
    max_exact = N_BUCKETS // 2
    d = np.float32(min_dist)
    large = max_exact + int(np.float32(np.log(d / np.float32(max_exact))) / np.float32(math.log(MAX_DISTANCE / max_exact))
                            * (N_BUCKETS - max_exact))
    return min_dist >= max_exact and large >= N_BUCKETS - 1


def _rope_tables(pos):
    half = RET_DK // 2
    inv = 1.0 / (ROPE_BASE ** (jnp.arange(half, dtype=F32) / half))
    ang = pos.astype(F32)[:, None] * inv[None, :]
    cos, sin = jnp.cos(ang), jnp.sin(ang)
    return jnp.concatenate([cos, cos], axis=1), jnp.concatenate([-sin, sin], axis=1)


def _layer_params(l, w_in, named):
    p = {k: v[l] for k, v in named.items()}
    w = w_in[l]
    o = (0,) + IN_OFFSETS + (w.shape[1],)
    z, xbc, dt, rq, rk, rv, rg, dq, dk, dv, gates = [w[:, o[i]:o[i + 1]] for i in range(len(IN_SPLITS))]
    out = {}
    out["w_main"] = jnp.concatenate([xbc, z, rq, rk, rv, rg, dq, gates], axis=1).astype(BF16)
    out["w_kvdt"] = jnp.concatenate([dk, dv, jnp.pad(dt, ((0, 0), (0, LANES - SSM_HEADS)))], axis=1).astype(BF16)
    out["norm_mix"] = p["norm_mix"].reshape(1, D_MODEL)
    out["conv_w"] = p["conv_w"]
    out["conv_b"] = p["conv_b"].reshape(1, CONV_DIM)
    out["dt_bias"] = jnp.pad(p["dt_bias"], (0, LANES - SSM_HEADS)).reshape(1, LANES)
    out["a_log"] = jnp.pad(p["a_log"], (0, LANES - SSM_HEADS)).reshape(1, LANES)
    out["d_skip"] = jnp.repeat(p["d_skip"], SSM_HEAD_DIM).reshape(1, SSM_D)
    out["ssm_norm"] = p["ssm_norm"].reshape(1, SSM_D)
    head_of_channel = np.arange(SSM_D) // SSM_HEAD_DIM
    out["head_expand"] = jnp.asarray(np.arange(LANES)[:, None] == head_of_channel[None, :], dtype=BF16)
    out["ret_norm"] = p["ret_norm"].reshape(1, RET_HEADS * RET_DV)
    out["qk_norm_q"] = jnp.tile(p["qk_norm_q"], 2).reshape(1, LANES)
    out["qk_norm_k"] = jnp.tile(p["qk_norm_k"], 2).reshape(1, LANES)
    lamv = jnp.stack([p["lambda_q1"], p["lambda_k1"], p["lambda_q2"], p["lambda_k2"]])
    out["lamv"] = jnp.pad(lamv, ((0, 0), (0, LANES - DIFF_HD)))
    out["diff_norm"] = p["diff_norm"].reshape(1, DIFF_VD)
    out["diff_norm_col"] = p["diff_norm"].reshape(DIFF_VD, 1)
    half = np.arange(LANES) // DIFF_HD
    out["blockdiag"] = jnp.asarray(half[:, None] == half[None, :], dtype=BF16)
    for k in ("w_ssm_out", "w_ret_out", "w_diff_out", "w_o", "w_gate_up", "w_down"):
        out[k] = p[k].astype(BF16)
    out["b_gate"] = p["b_gate"]
    out["norm_ffn"] = p["norm_ffn"].reshape(1, D_MODEL)
    return out


def _token_tile(n, cap):
    tm = min(n, cap)
    while n % tm:
        tm //= 2
    return tm


def _layer_common(x, p, layer, vbuf, tail0, h0, s0, cos_t, sin_t, nvalid, attn_fn):
    b, tp, _ = x.shape
    n = b * tp
    x2d = x.reshape(n, D_MODEL)
    proj, kf, vbuf, dt_raw = _inproj(x2d, p["norm_mix"], p["w_main"], p["w_kvdt"], _token_tile(n, 1024), vbuf, layer)
    proj3 = proj.reshape(b, tp, C_MAIN)
    dt3 = dt_raw.reshape(b, tp, LANES)
    tpad = -tp % CHUNK
    scan_in = lambda a: jnp.pad(a, ((0, 0), (0, tpad), (0, 0))) if tpad else a
    y_ssm, h_new, conv_rows, y_ret, s_new = _scans(scan_in(proj3), scan_in(dt3), tail0, h0, cos_t, sin_t, s0, p,
                                                   nvalid)
    y_diff, k_new = attn_fn(proj3, kf.reshape(b, tp, KV_COLS), vbuf)
    tm = _token_tile(n, 512)
    h = _merge(y_ssm[:, :tp].reshape(n, -1), y_ret[:, :tp].reshape(n, -1), y_diff.reshape(n, -1), proj, x2d, p, tm)
    y = _ffn(h, p, tm)
    last = (nvalid - 1) % CONV_TAIL
    assert last >= CONV_W - 2
    return (y.reshape(b, tp, D_MODEL), k_new, vbuf,
            h_new.reshape(b, SSM_HEADS, SSM_HEAD_DIM, SSM_STATE), conv_rows[:, last - (CONV_W - 2):last + 1, :], s_new)


def kernel(x_prompt, x_sample, cache_k, cache_v, page_table, state_ssm, state_conv, state_ret, rel_bias, norm_mix, w_in, b_gate, conv_w, conv_b, dt_bias, a_log, d_skip, ssm_norm, w_ssm_out, ret_norm, w_ret_out, qk_norm_q, qk_norm_k, lambda_q1, lambda_k1, lambda_q2, lambda_k2, diff_norm, w_diff_out, w_o, norm_ffn, w_gate_up, w_down):
    named = dict(norm_mix=norm_mix, b_gate=b_gate, conv_w=conv_w, conv_b=conv_b, dt_bias=dt_bias, a_log=a_log,
                 d_skip=d_skip, ssm_norm=ssm_norm, w_ssm_out=w_ssm_out, ret_norm=ret_norm, w_ret_out=w_ret_out,
                 qk_norm_q=qk_norm_q, qk_norm_k=qk_norm_k, lambda_q1=lambda_q1, lambda_k1=lambda_k1,
                 lambda_q2=lambda_q2, lambda_k2=lambda_k2, diff_norm=diff_norm, w_diff_out=w_diff_out, w_o=w_o,
                 norm_ffn=norm_ffn, w_gate_up=w_gate_up, w_down=w_down)
    depth = w_in.shape[0]
    bp, seq, _ = x_prompt.shape
    bs, dec, _ = x_sample.shape
    n_pages = page_table.shape[1]
    past = n_pages * PAGE_SIZE
    assert seq % CHUNK == 0 and CONV_W - 1 <= dec <= SUBLANES
    assert _far_bucket_is_constant(CHUNK + 1)

    ii = jnp.arange(CHUNK)[:, None]
    jj = jnp.arange(CHUNK)[None, :]
    idx_p = jnp.concatenate([_t5_bucket(k * CHUNK + jj - ii) for k in range(3)], axis=0)
    bias_h = _bias_tiles(rel_bias, idx_p, LOG2E).reshape(DIFF_KV_HEADS, DIFF_HEADS // DIFF_KV_HEADS, 3, CHUNK, CHUNK)
    bias_p = jnp.concatenate([bias_h[:, r] for r in range(DIFF_HEADS // DIFF_KV_HEADS)] * 2, axis=-1)
    tt = jnp.arange(SUBLANES)[:, None]
    idx_s = jnp.concatenate([_t5_bucket(jnp.broadcast_to(past + tt, (SUBLANES, LANES))),
                             _t5_bucket(tt + PAGE_SIZE - jj), _t5_bucket(tt - jj)], axis=1)
    bias_s = _bias_tiles(rel_bias, idx_s, 1.0)

    cos_p, sin_p = _rope_tables(jnp.arange(seq))
    cos_s, sin_s = _rope_tables(past + jnp.arange(CHUNK))

    xs = jnp.pad(x_sample, ((0, 0), (0, SUBLANES - dec), (0, 0)))
    zeros_tail = jnp.zeros((bp, CONV_TAIL, CONV_DIM), F32)
    zeros_h = jnp.zeros((bp, SSM_D, SSM_STATE), F32)
    zeros_s = jnp.zeros((bp, RET_HEADS, RET_DK, RET_DV), F32)

    vbuf_p = jnp.zeros((depth, bp * seq * DIFF_KV_HEADS, DIFF_VD), F32)
    vbuf_s = jnp.zeros((depth, bs * SUBLANES * DIFF_KV_HEADS, DIFF_VD), F32)
    kbuf_p = jnp.zeros((depth, bp, KV_COLS, seq), F32)

    yp, ys = x_prompt, xs
    outs_p, outs_s = [], []
    for l in range(depth):
        lam_init = 0.8 - 0.6 * math.exp(-0.3 * l)
        p = _layer_params(l, w_in, named)

        def attn_p(proj3, kf, vbuf, l=l, p=p, lam_init=lam_init, kbuf=kbuf_p):
            return _diff_attention_prompt(proj3, kf, vbuf, kbuf, bias_p, p, l, lam_init)

        yp, kbuf_p, vbuf_p, h1, c1, r1 = _layer_common(yp, p, l, vbuf_p, zeros_tail, zeros_h, zeros_s, cos_p, sin_p,
                                                       CHUNK, attn_p)
        outs_p.append((h1, c1, r1))

        tail_s = jnp.pad(state_conv[l], ((0, 0), (CONV_TAIL - (CONV_W - 1), 0), (0, 0)))
        h0_s = state_ssm[l].reshape(bs, SSM_D, SSM_STATE)

        def attn_s(proj3, kf, vbuf, l=l, p=p, lam_init=lam_init):
            return _diff_attention_sample(proj3, kf, vbuf, cache_k, cache_v, page_table, bias_s, p, l, dec, lam_init)

        ys_new, k2, vbuf_s, h2, c2, r2 = _layer_common(ys, p, l, vbuf_s, tail_s, h0_s, state_ret[l], cos_s, sin_s, dec,
                                                       attn_s)
        ys = jnp.where(jnp.arange(SUBLANES)[None, :, None] < dec, ys_new, 0.0)
        outs_s.append((k2[:, :dec].reshape(bs, dec, DIFF_KV_HEADS, 2, DIFF_HD), h2, c2, r2))

    stack = lambda outs, i: jnp.stack([o[i] for o in outs])
    k_prompt = kbuf_p.reshape(depth, bp, DIFF_KV_HEADS, 2, DIFF_HD, seq).transpose(0, 1, 5, 2, 3, 4)
    v_prompt = vbuf_p.reshape(depth, bp, seq, DIFF_KV_HEADS, DIFF_VD)
    v_sample = vbuf_s.reshape(depth, bs, SUBLANES, DIFF_KV_HEADS, DIFF_VD)[:, :, :dec]
    return (yp, ys[:, :dec],
            k_prompt, v_prompt, stack(outs_p, 0), stack(outs_p, 1), stack(outs_p, 2),
            stack(outs_s, 0), v_sample, stack(outs_s, 1), stack(outs_s, 2), stack(outs_s, 3))
```

```python
import functools
import math

import numpy as np
import jax
import jax.numpy as jnp
from jax import lax
from jax.experimental import pallas as pl
from jax.experimental.pallas import tpu as pltpu

F32 = jnp.float32
BF16 = jnp.bfloat16

D_MODEL = 1024
SSM_HEADS = 16
SSM_HEAD_DIM = 64
SSM_D = SSM_HEADS * SSM_HEAD_DIM
SSM_STATE = 128
SSM_GROUPS = 4
CONV_W = 4
CONV_DIM = SSM_D + 2 * SSM_GROUPS * SSM_STATE
RET_HEADS = 4
RET_DK = 128
RET_DV = 256
ROPE_BASE = 10000.0
DIFF_HEADS = 8
DIFF_KV_HEADS = 4
DIFF_HD = 64
DIFF_VD = 2 * DIFF_HD
N_BUCKETS = 32
MAX_DISTANCE = 128
N_BRANCHES = 3
D_FF = 2816
NORM_EPS = 1e-6
PAGE_SIZE = 128

CHUNK = 128
LANES = 128
SUBLANES = 8
GROUPS_PER_PASS = 4
CONV_TAIL = 16
NEG_BIG = -1e30
LOG2E = 1.4426950408889634
VMEM_LIMIT = 52 * 1024 * 1024

IN_SPLITS = (SSM_D, CONV_DIM, SSM_HEADS,
             RET_HEADS * RET_DK, RET_HEADS * RET_DK, RET_HEADS * RET_DV, RET_HEADS * RET_DV,
             DIFF_HEADS * 2 * DIFF_HD, DIFF_KV_HEADS * 2 * DIFF_HD, DIFF_KV_HEADS * DIFF_VD,
             N_BRANCHES * D_MODEL)
IN_OFFSETS = tuple(int(v) for v in np.cumsum(IN_SPLITS)[:-1])

C_XBC, C_Z, C_RQ, C_RK, C_RV, C_RG, C_DQ, C_GATE, C_MAIN = 0, 2048, 3072, 3584, 4096, 5120, 6144, 7168, 10240
TN_PROJ = 2048
KV_COLS = DIFF_KV_HEADS * 2 * DIFF_HD


def _cparams(*sem):
    return pltpu.CompilerParams(dimension_semantics=sem, vmem_limit_bytes=VMEM_LIMIT)


def _nt_dot(a, b):
    return lax.dot_general(a, b, (((1,), (1,)), ((), ())), preferred_element_type=F32)


def _tn_dot(a, b):
    return lax.dot_general(a, b, (((0,), (0,)), ((), ())), preferred_element_type=F32)


def _dot(a, b):
    return jnp.dot(a, b, preferred_element_type=F32)


def _split3(x):
    hi = x.astype(BF16)
    r1 = x - hi.astype(F32)
    mid = r1.astype(BF16)
    lo = (r1 - mid.astype(F32)).astype(BF16)
    return hi, mid, lo


def _sigmoid(x):
    return 0.5 * jnp.tanh(0.5 * x) + 0.5


def _silu(x):
    h = 0.5 * x
    return h + h * jnp.tanh(h)


def _inproj_kernel(x_ref, g_ref, w_ref, wkvdt_ref, vprev_ref, main_ref, kf_ref, vf_ref, dt_ref, xn_ref):
    del vprev_ref
    tm = x_ref.shape[0]

    @pl.when(pl.program_id(1) == 0)
    def _():
        x = x_ref[...]
        ms = jnp.mean(x * x, axis=-1, keepdims=True)
        xn = (x * lax.rsqrt(ms + NORM_EPS) * g_ref[...]).astype(BF16)
        xn_ref[...] = xn
        kvdt = _dot(xn, wkvdt_ref[...])
        kf_ref[...] = kvdt[:, :KV_COLS]
        for g in range(DIFF_KV_HEADS):
            vf_ref[pl.ds(g, tm, stride=DIFF_KV_HEADS), :] = kvdt[:, KV_COLS + g * DIFF_VD:KV_COLS + (g + 1) * DIFF_VD]
        dt_ref[...] = kvdt[:, 2 * KV_COLS:]

    main_ref[...] = _dot(xn_ref[...], w_ref[...]).astype(BF16)


def _inproj(x2d, gain, w_main, w_kvdt, tm, vbuf, layer):
    n = x2d.shape[0]
    grid = (n // tm, C_MAIN // TN_PROJ)
    return pl.pallas_call(
        _inproj_kernel,
        grid=grid,
        in_specs=[
            pl.BlockSpec((tm, D_MODEL), lambda i, j: (i, 0)),
            pl.BlockSpec((1, D_MODEL), lambda i, j: (0, 0)),
            pl.BlockSpec((D_MODEL, TN_PROJ), lambda i, j: (0, j)),
            pl.BlockSpec((D_MODEL, 2 * KV_COLS + LANES), lambda i, j: (0, 0)),
            pl.BlockSpec(memory_space=pl.ANY),
        ],
        out_specs=[
            pl.BlockSpec((tm, TN_PROJ), lambda i, j: (i, j)),
            pl.BlockSpec((tm, KV_COLS), lambda i, j: (i, 0)),
            pl.BlockSpec((None, tm * DIFF_KV_HEADS, DIFF_VD), lambda i, j: (layer, i, 0)),
            pl.BlockSpec((tm, LANES), lambda i, j: (i, 0)),
        ],
        out_shape=[
            jax.ShapeDtypeStruct((n, C_MAIN), BF16),
            jax.ShapeDtypeStruct((n, KV_COLS), F32),
            jax.ShapeDtypeStruct(vbuf.shape, F32),
            jax.ShapeDtypeStruct((n, LANES), F32),
        ],
        scratch_shapes=[pltpu.VMEM((tm, D_MODEL), BF16)],
        input_output_aliases={4: 2},
        compiler_params=_cparams("arbitrary", "arbitrary"),
        name="inproj",
    )(x2d, gain, w_main, w_kvdt, vbuf)


def _ssd_kernel(xbc_ref, z_ref, dt_ref, tail0_ref, h0_ref, cw_ref, cb_ref, dtb_ref, alog_ref, dskip_ref,
                gn_ref, ex_ref, y_ref, hout_ref, convout_ref, xext_ref, tlo_ref, xc_ref, ht_ref, *, c, nvalid, nchunks):
    L = CHUNK
    GW = SSM_D // SSM_GROUPS
    SL = 512

    @pl.when(c == 0)
    def _():
        t0 = tail0_ref[0]
        t0b = t0.astype(BF16)
        xext_ref[0:CONV_TAIL, :] = t0b
        tlo_ref[...] = (t0 - t0b.astype(F32)).astype(BF16)
        for k in range(SSM_D // LANES):
            ht_ref[:, k * LANES:(k + 1) * LANES] = h0_ref[0, k * LANES:(k + 1) * LANES, :].T

    xext_ref[CONV_TAIL:CONV_TAIL + L, :] = xbc_ref[0]
    srow = lax.broadcasted_iota(jnp.int32, (L, CONV_TAIL + L), 0)
    scol = lax.broadcasted_iota(jnp.int32, (L, CONV_TAIL + L), 1)
    shifts = [jnp.where(scol == srow + CONV_TAIL - s, 1.0, 0.0).astype(BF16) for s in range(1, CONV_W)]
    shifted = [[_dot(shifts[s - 1], xext_ref[:, sl * SL:(sl + 1) * SL]) for s in range(1, CONV_W)]
               for sl in range(CONV_DIM // SL)]
    for sl in range(CONV_DIM // SL):
        cols = slice(sl * SL, (sl + 1) * SL)
        conv = cw_ref[CONV_W - 1:CONV_W, cols] * xbc_ref[0, :, cols].astype(F32) + cb_ref[:, cols]
        for s in range(1, CONV_W):
            conv = conv + cw_ref[CONV_W - 1 - s:CONV_W - s, cols] * shifted[sl][s - 1]
        xc_ref[:, cols] = conv

    @pl.when(c == 0)
    def _():
        for sl in range(CONV_DIM // SL):
            cols = slice(sl * SL, (sl + 1) * SL)
            corr = None
            for s in range(1, CONV_W):
                t = cw_ref[CONV_W - 1 - s:CONV_W - s, cols] * _dot(shifts[s - 1][0:CONV_TAIL, 0:CONV_TAIL],
                                                                    tlo_ref[:, cols])
                corr = t if corr is None else corr + t
            xc_ref[0:CONV_TAIL, cols] += corr

    @pl.when(c == nchunks - 1)
    def _():
        a = CONV_TAIL * ((nvalid - 1) // CONV_TAIL)
        convout_ref[0] = xext_ref[CONV_TAIL + a:CONV_TAIL + a + CONV_TAIL, :].astype(F32)

    xext_ref[0:CONV_TAIL, :] = xext_ref[L:L + CONV_TAIL, :]

    row = lax.broadcasted_iota(jnp.int32, (L, L), 0)
    col = lax.broadcasted_iota(jnp.int32, (L, L), 1)
    causal = row >= col
    left = col < SSM_HEAD_DIM

    x = dt_ref[0] + dtb_ref[...]
    dt = jnp.maximum(x, 0.0) + jnp.log1p(jnp.exp(-jnp.abs(x)))
    if nvalid < L:
        dt = jnp.where(row < nvalid, dt, 0.0)
    a = -jnp.exp(alog_ref[...])
    da = dt * a
    tri = jnp.where(causal, 1.0, 0.0).astype(BF16)
    cs = sum(_dot(tri, p) for p in _split3(da))
    cs_t = cs.T
    cs_parts = _split3(cs)
    dt_parts = _split3(dt)

    NG = SSM_GROUPS
    HPG = SSM_HEADS // SSM_GROUPS
    gcs = [slice(g * GW, (g + 1) * GW) for g in range(NG)]
    bcol = lambda g: slice(SSM_D + g * SSM_STATE, SSM_D + (g + 1) * SSM_STATE)
    ccol = lambda g: slice(SSM_D + NG * SSM_STATE + g * SSM_STATE, SSM_D + NG * SSM_STATE + (g + 1) * SSM_STATE)

    def scan_groups(gs):
        csx = {g: sum(_dot(p, ex_ref[:, gcs[g]]) for p in cs_parts) for g in gs}
        dtx = {g: sum(_dot(p, ex_ref[:, gcs[g]]) for p in dt_parts) for g in gs}
        bgs = {g: _silu(xc_ref[:, bcol(g)]).astype(BF16) for g in gs}
        cgs = {g: _silu(xc_ref[:, ccol(g)]).astype(BF16) for g in gs}
        cbs = {g: _nt_dot(cgs[g], bgs[g]) for g in gs}
        htgs = {g: ht_ref[:, gcs[g]] for g in gs}
        ysts = {g: _dot(cgs[g], htgs[g].astype(BF16)) for g in gs}
        xss = {g: _silu(xc_ref[:, gcs[g]]) for g in gs}
        xdts = {g: xss[g] * dtx[g] for g in gs}
        xdt_bs = {g: xdts[g].astype(BF16) for g in gs}
        lastxs = {g: csx[g][L - 1:L, :] for g in gs}
        xdtw_bs = {g: (xdts[g] * jnp.exp(lastxs[g] - csx[g])).astype(BF16) for g in gs}
        prods = {}
        for g in gs:
            for e in range(HPG):
                h = g * HPG + e
                seg = cs[:, h:h + 1] - cs_t[h:h + 1, :]
                decay = jnp.exp(jnp.where(causal, seg, NEG_BIG))
                mat = (cbs[g] * decay).astype(BF16)
                xp = xdt_bs[g][:, (e // 2) * LANES:(e // 2 + 1) * LANES]
                keep = left if e % 2 == 0 else jnp.logical_not(left)
                prods[h] = _dot(mat, jnp.where(keep, xp, jnp.zeros_like(xp)))
        upd = {g: _tn_dot(bgs[g], xdtw_bs[g]) for g in gs}
        for g in gs:
            ht_ref[:, gcs[g]] = htgs[g] * jnp.exp(lastxs[g]) + upd[g]
        for g in gs:
            gc = gcs[g]
            y_in = jnp.concatenate([prods[g * HPG + 2 * pr] + prods[g * HPG + 2 * pr + 1] for pr in range(HPG // 2)],
                                   axis=1)
            y = y_in + ysts[g] * jnp.exp(csx[g]) + dskip_ref[:, gc] * xss[g]
            y = y * _silu(z_ref[0, :, gc].astype(F32))
            ms = jnp.mean(y * y, axis=-1, keepdims=True)
            y_ref[0, :, gc] = (y * lax.rsqrt(ms + NORM_EPS) * gn_ref[:, gc]).astype(BF16)

    for first in range(0, NG, GROUPS_PER_PASS):
        scan_groups(range(first, first + GROUPS_PER_PASS))

    @pl.when(c == nchunks - 1)
    def _():
        for k in range(SSM_D // LANES):
            hout_ref[0, k * LANES:(k + 1) * LANES, :] = ht_ref[:, k * LANES:(k + 1) * LANES].T


def _ret_kernel(q_ref, k_ref, v_ref, rg_ref, cos_ref, sin_ref, s0_ref, gn_ref, y_ref, sout_ref, *, c, ltrue):
    L = CHUNK

    @pl.when(c == 0)
    def _():
        sout_ref[...] = s0_ref[...]

    row = lax.broadcasted_iota(jnp.int32, (L, L), 0)
    col = lax.broadcasted_iota(jnp.int32, (L, L), 1)
    rel = (row - col).astype(F32)
    idx = row[:, 0:1].astype(F32)
    cosf = cos_ref[...]
    sins = sin_ref[...]
    H = RET_HEADS
    lgs = [math.log(1.0 - 2.0 ** (-5.0 - h)) for h in range(H)]
    kcs = [slice(h * RET_DK, (h + 1) * RET_DK) for h in range(H)]
    vcs = [slice(h * RET_DV, (h + 1) * RET_DV) for h in range(H)]
    qrs, krs = [], []
    for h in range(H):
        qh = q_ref[0, :, kcs[h]].astype(F32)
        kh = k_ref[0, :, kcs[h]].astype(F32)
        qrs.append(qh * cosf + pltpu.roll(qh, RET_DK // 2, 1) * sins)
        krs.append((kh * cosf + pltpu.roll(kh, RET_DK // 2, 1) * sins) * RET_DK ** -0.5)
    qr_bs = [q_.astype(BF16) for q_ in qrs]
    vhs = [v_ref[0, :, vcs[h]] for h in range(H)]
    s_olds = [sout_ref[0, h] for h in range(H)]
    scores = [_nt_dot(qr_bs[h], krs[h].astype(BF16)) for h in range(H)]
    cross = [_dot(qr_bs[h], s_olds[h].astype(BF16)) for h in range(H)]
    atts = []
    for h in range(H):
        dmat = jnp.where(rel >= 0, jnp.exp(jnp.maximum(rel, 0.0) * lgs[h]), 0.0)
        atts.append((scores[h] * dmat).astype(BF16))
    inner = [_dot(atts[h], vhs[h]) for h in range(H)]
    for h in range(H):
        k_dec = jnp.exp((ltrue - 1.0 - idx) * lgs[h])
        sout_ref[0, h] = s_olds[h] * math.exp(ltrue * lgs[h]) + _tn_dot((krs[h] * k_dec).astype(BF16), vhs[h])
    for h in range(H):
        o = inner[h] + cross[h] * jnp.exp((idx + 1.0) * lgs[h])
        oc = o - jnp.mean(o, axis=-1, keepdims=True)
        on = oc * lax.rsqrt(jnp.mean(oc * oc, axis=-1, keepdims=True) + NORM_EPS)
        y_ref[0, :, vcs[h]] = (on * gn_ref[:, vcs[h]] * _silu(rg_ref[0, :, vcs[h]].astype(F32))).astype(BF16)


N_SSD_IN, N_RET_IN, N_SSD_OUT, N_RET_OUT = 12, 8, 3, 2


CHUNKS_PER_STEP = 4


def _scan_kernel(*refs, nvalid, nchunks, per_step):
    i0 = N_SSD_IN
    i1 = i0 + N_RET_IN
    i2 = i1 + N_SSD_OUT
    i3 = i2 + N_RET_OUT
    ssd_in, ret_in, ssd_out, ret_out = refs[0:i0], refs[i0:i1], refs[i1:i2], refs[i2:i3]

    def chunk(sc, carry):
        c = pl.program_id(1) * per_step + sc
        rows = pl.ds(pl.multiple_of(sc * CHUNK, CHUNK), CHUNK)
        seq = lambda r: r.at[:, rows, :]
        tab = lambda r: r.at[rows, :]
        _ret_kernel(*[seq(r) for r in ret_in[0:4]], tab(ret_in[4]), tab(ret_in[5]), *ret_in[6:],
                    seq(ret_out[0]), ret_out[1], c=c, ltrue=float(nvalid))
        _ssd_kernel(*[seq(r) for r in ssd_in[0:3]], *ssd_in[3:], seq(ssd_out[0]), *ssd_out[1:], *refs[i3:],
                    c=c, nvalid=nvalid, nchunks=nchunks)
        return carry

    lax.fori_loop(0, per_step, chunk, 0)


def _scans(proj, dt_raw, tail0, h0, cos_t, sin_t, s0, p, nvalid):
    b, t, _ = proj.shape
    nchunks = t // CHUNK
    per_step = math.gcd(nchunks, CHUNKS_PER_STEP)
    L = CHUNK * per_step
    qk_w = RET_HEADS * RET_DK
    v_w = RET_HEADS * RET_DV
    const = lambda shape: pl.BlockSpec(shape, lambda i, c: (0,) * len(shape))
    ssd_in = [
        pl.BlockSpec((1, L, CONV_DIM), lambda i, c: (i, c, C_XBC // CONV_DIM)),
        pl.BlockSpec((1, L, SSM_D), lambda i, c: (i, c, C_Z // SSM_D)),
        pl.BlockSpec((1, L, LANES), lambda i, c: (i, c, 0)),
        pl.BlockSpec((1, CONV_TAIL, CONV_DIM), lambda i, c: (i, 0, 0)),
        pl.BlockSpec((1, SSM_D, SSM_STATE), lambda i, c: (i, 0, 0)),
        const((CONV_W, CONV_DIM)), const((1, CONV_DIM)), const((1, LANES)), const((1, LANES)),
        const((1, SSM_D)), const((1, SSM_D)), const((LANES, SSM_D)),
    ]
    ret_in = [
        pl.BlockSpec((1, L, qk_w), lambda i, c: (i, c, C_RQ // qk_w)),
        pl.BlockSpec((1, L, qk_w), lambda i, c: (i, c, C_RK // qk_w)),
        pl.BlockSpec((1, L, v_w), lambda i, c: (i, c, C_RV // v_w)),
        pl.BlockSpec((1, L, v_w), lambda i, c: (i, c, C_RG // v_w)),
        pl.BlockSpec((L, RET_DK), lambda i, c: (c, 0)),
        pl.BlockSpec((L, RET_DK), lambda i, c: (c, 0)),
        pl.BlockSpec((1, RET_HEADS, RET_DK, RET_DV), lambda i, c: (i, 0, 0, 0)),
        const((1, v_w)),
    ]
    ssd_out = [
        pl.BlockSpec((1, L, SSM_D), lambda i, c: (i, c, 0)),
        pl.BlockSpec((1, SSM_D, SSM_STATE), lambda i, c: (i, 0, 0)),
        pl.BlockSpec((1, CONV_TAIL, CONV_DIM), lambda i, c: (i, 0, 0)),
    ]
    ret_out = [
        pl.BlockSpec((1, L, v_w), lambda i, c: (i, c, 0)),
        pl.BlockSpec((1, RET_HEADS, RET_DK, RET_DV), lambda i, c: (i, 0, 0, 0)),
    ]
    assert (len(ssd_in), len(ret_in), len(ssd_out), len(ret_out)) == (N_SSD_IN, N_RET_IN, N_SSD_OUT, N_RET_OUT)
    return pl.pallas_call(
        functools.partial(_scan_kernel, nvalid=nvalid, nchunks=nchunks, per_step=per_step),
        grid=(b, nchunks // per_step),
        in_specs=ssd_in + ret_in,
        out_specs=ssd_out + ret_out,
        out_shape=[
            jax.ShapeDtypeStruct((b, t, SSM_D), BF16),
            jax.ShapeDtypeStruct((b, SSM_D, SSM_STATE), F32),
            jax.ShapeDtypeStruct((b, CONV_TAIL, CONV_DIM), F32),
            jax.ShapeDtypeStruct((b, t, v_w), BF16),
            jax.ShapeDtypeStruct((b, RET_HEADS, RET_DK, RET_DV), F32),
        ],
        scratch_shapes=[
            pltpu.VMEM((CONV_TAIL + CHUNK, CONV_DIM), BF16),
            pltpu.VMEM((CONV_TAIL, CONV_DIM), BF16),
            pltpu.VMEM((CHUNK, CONV_DIM), F32),
            pltpu.VMEM((SSM_STATE, SSM_D), F32),
        ],
        compiler_params=_cparams("arbitrary", "arbitrary"),
        name="scans",
    )(proj, proj, dt_raw, tail0, h0, p["conv_w"], p["conv_b"], p["dt_bias"], p["a_log"], p["d_skip"],
      p["ssm_norm"], p["head_expand"], proj, proj, proj, proj, cos_t, sin_t, s0, p["ret_norm"])


def _bias_kernel(tab_ref, idx_ref, out_ref, *, scale):
    h = pl.program_id(0)
    idx = idx_ref[...]
    acc = jnp.zeros(idx.shape, F32)
    for b in range(N_BUCKETS):
        acc = acc + jnp.where(idx == b, tab_ref[b * DIFF_HEADS + h], 0.0)
    out_ref[0] = acc * scale


def _bias_tiles(rel_bias, idx, scale):
    r, c = idx.shape
    return pl.pallas_call(
        functools.partial(_bias_kernel, scale=scale),
        grid=(DIFF_HEADS,),
        in_specs=[pl.BlockSpec(memory_space=pltpu.SMEM), pl.BlockSpec((r, c), lambda h: (0, 0))],
        out_specs=pl.BlockSpec((1, r, c), lambda h: (h, 0, 0)),
        out_shape=jax.ShapeDtypeStruct((DIFF_HEADS, r, c), F32),
        compiler_params=_cparams("arbitrary"),
        name="t5_bias",
    )(rel_bias.reshape(-1), idx)


def _half_rmsnorm(x, gain, bd):
    x2 = x * x
    hi = x2.astype(BF16)
    lo = (x2 - hi.astype(F32)).astype(BF16)
    ss = _dot(hi, bd) + _dot(lo, bd)
    return x * lax.rsqrt(ss * (1.0 / DIFF_HD) + NORM_EPS) * gain


def _half_rmsnorm_blocks(blocks, gain, bd):
    rows = blocks[0].shape[0]
    y = _half_rmsnorm(jnp.concatenate(blocks, axis=0), gain, bd)
    return [y[i * rows:(i + 1) * rows] for i in range(len(blocks))]


def _lambda(lamv_ref, lam_init):
    s1 = jnp.sum(lamv_ref[0:1, :] * lamv_ref[1:2, :], axis=-1, keepdims=True)
    s2 = jnp.sum(lamv_ref[2:3, :] * lamv_ref[3:4, :], axis=-1, keepdims=True)
    return jnp.exp(s1) - jnp.exp(s2) + lam_init


FAR_UNITS = 4
Q_TILES_PER_STEP = 4


def _dattn_kernel(q_ref, kf_ref, vf_ref, bias_ref, gq_ref, gk_ref, lamv_ref, sgc_ref, bd_ref, kprev_ref,
                  y_ref, kout_ref, kn_ref, vt_ref, qs_ref, m_ref, l_ref, acc_ref, *, t, q_tiles, lam_init):
    del kprev_ref
    step_id = pl.program_id(1)
    TQ = CHUNK
    G = DIFF_KV_HEADS
    R = DIFF_HEADS // DIFF_KV_HEADS
    NC = 2 * R * TQ
    bd = bd_ref[...]

    @pl.when(step_id == 0)
    def _():
        for i in range(t // TQ):
            r = slice(i * TQ, (i + 1) * TQ)
            kns = _half_rmsnorm_blocks([kf_ref[0, r, g * LANES:(g + 1) * LANES] for g in range(G)], gk_ref[...], bd)
            for g in range(G):
                kout_ref[0, g * LANES:(g + 1) * LANES, r] = kns[g].T
                kn_ref[g, r, :] = kns[g].astype(BF16)
                vt_ref[g, i] = vf_ref[0, pl.ds(i * TQ * G + g, TQ, stride=G), :].T.astype(BF16)

    def tile(ti, carry):
        qi = step_id * q_tiles + ti
        qrows = pl.ds(pl.multiple_of(ti * TQ, TQ), TQ)
        lane = lax.broadcasted_iota(jnp.int32, (TQ, LANES), 1)
        left = lane < DIFF_HD
        qns = _half_rmsnorm_blocks([q_ref[0, qrows, h * LANES:(h + 1) * LANES].astype(F32)
                                    for h in range(DIFF_HEADS)], gq_ref[...], bd)
        for g in range(G):
            for r in range(R):
                qn = qns[g * R + r] * (DIFF_HD ** -0.5 * LOG2E)
                qs_ref[g, (0 * R + r) * TQ:(0 * R + r + 1) * TQ, :] = jnp.where(left, qn, 0.0).astype(BF16)
                qs_ref[g, (1 * R + r) * TQ:(1 * R + r + 1) * TQ, :] = jnp.where(left, 0.0, qn).astype(BF16)
        m_ref[...] = jnp.full(m_ref.shape, NEG_BIG, F32)
        l_ref[...] = jnp.zeros(l_ref.shape, F32)
        acc_ref[...] = jnp.zeros(acc_ref.shape, F32)

        def step(k0, tiles):
            nunits = len(tiles)
            rows = pl.ds(pl.multiple_of(k0 * TQ, TQ), nunits * TQ)
            ss = [_nt_dot(kn_ref[g, rows, :], qs_ref[g]) for g in range(G)]
            ps, alphas = [], []
            for g in range(G):
                s = ss[g] + jnp.concatenate([bias_ref[g, tl] for tl in tiles], axis=0)
                if 0 in tiles:
                    first = tiles.index(0) * TQ
                    key = lax.broadcasted_iota(jnp.int32, (nunits * TQ, NC), 0) - first
                    qry = lax.broadcasted_iota(jnp.int32, (nunits * TQ, NC), 1) & (TQ - 1)
                    s = jnp.where(key <= qry, s, NEG_BIG)
                m_old = m_ref[g]
                m_new = jnp.maximum(m_old, jnp.max(s, axis=0, keepdims=True))
                alpha = jnp.exp2(m_old - m_new)
                p = jnp.exp2(s - m_new)
                l_ref[g] = alpha * l_ref[g] + jnp.sum(p, axis=0, keepdims=True)
                m_ref[g] = m_new
                ps.append(p.astype(BF16))
                alphas.append(alpha)
            for g in range(G):
                if nunits == 1:
                    vt = vt_ref[g, k0]
                else:
                    vt = jnp.concatenate([vt_ref[g, k0 + u] for u in range(nunits)], axis=1)
                acc_ref[g] = acc_ref[g] * alphas[g] + _dot(vt, ps[g])

        nfar = jnp.maximum(qi - 1, 0)
        nbig = nfar // FAR_UNITS
        rem = nfar - nbig * FAR_UNITS

        def far_big(i, c):
            step(i * FAR_UNITS, (2,) * FAR_UNITS)
            return c

        lax.fori_loop(0, nbig, far_big, 0)

        @pl.when(rem >= 2)
        def _():
            step(nbig * FAR_UNITS, (2, 2))

        @pl.when((rem & 1) == 1)
        def _():
            step(nbig * FAR_UNITS + (rem & 2), (2,))

        @pl.when(qi >= 1)
        def _():
            step(qi - 1, (1, 0))

        @pl.when(qi == 0)
        def _():
            step(0, (0,))

        lam = _lambda(lamv_ref, lam_init)
        for g in range(G):
            inv_l = 1.0 / l_ref[g]
            for r in range(R):
                c0 = slice((0 * R + r) * TQ, (0 * R + r + 1) * TQ)
                c1 = slice((1 * R + r) * TQ, (1 * R + r + 1) * TQ)
                o = acc_ref[g, :, c0] * inv_l[:, c0] - lam * (acc_ref[g, :, c1] * inv_l[:, c1])
                o = o * lax.rsqrt(jnp.mean(o * o, axis=0, keepdims=True) + NORM_EPS)
                o = o * sgc_ref[...] * (1.0 - lam_init)
                y_ref[0, qrows, (g * R + r) * LANES:(g * R + r + 1) * LANES] = o.T.astype(BF16)
        return carry

    lax.fori_loop(0, q_tiles, tile, 0)


def _diff_attention_prompt(proj, kf, vbuf, kbuf, bias, p, layer, lam_init):
    b, t, _ = proj.shape
    vf = vbuf.reshape(vbuf.shape[0], b, t * DIFF_KV_HEADS, DIFF_VD)
    TQ = CHUNK
    G = DIFF_KV_HEADS
    R = DIFF_HEADS // DIFF_KV_HEADS
    NC = 2 * R * TQ
    qw = DIFF_HEADS * 2 * DIFF_HD
    const = lambda shape: pl.BlockSpec(shape, lambda i, q: (0,) * len(shape))
    q_tiles = math.gcd(t // TQ, Q_TILES_PER_STEP)
    return pl.pallas_call(
        functools.partial(_dattn_kernel, t=t, q_tiles=q_tiles, lam_init=lam_init),
        grid=(b, t // (TQ * q_tiles)),
        in_specs=[
            pl.BlockSpec((1, TQ * q_tiles, qw), lambda i, q: (i, q, C_DQ // qw)),
            pl.BlockSpec((1, t, KV_COLS), lambda i, q: (i, 0, 0)),
            pl.BlockSpec((None, 1, t * G, DIFF_VD), lambda i, q: (layer, i, 0, 0)),
            const((G, 3, TQ, NC)),
            const((1, LANES)), const((1, LANES)), const((4, LANES)), const((DIFF_VD, 1)), const((LANES, LANES)),
            pl.BlockSpec(memory_space=pl.ANY),
        ],
        out_specs=[
            pl.BlockSpec((1, TQ * q_tiles, qw), lambda i, q: (i, q, 0)),
            pl.BlockSpec((None, 1, KV_COLS, t), lambda i, q: (layer, i, 0, 0)),
        ],
        out_shape=[
            jax.ShapeDtypeStruct((b, t, DIFF_HEADS * DIFF_VD), BF16),
            jax.ShapeDtypeStruct(kbuf.shape, F32),
        ],
        input_output_aliases={9: 1},
        scratch_shapes=[
            pltpu.VMEM((G, t, LANES), BF16), pltpu.VMEM((G, t // TQ, DIFF_VD, TQ), BF16),
            pltpu.VMEM((G, NC, LANES), BF16),
            pltpu.VMEM((G, 1, NC), F32), pltpu.VMEM((G, 1, NC), F32),
            pltpu.VMEM((G, DIFF_VD, NC), F32),
        ],
        compiler_params=_cparams("arbitrary", "arbitrary"),
        name="diff_attn_prompt",
    )(proj, kf, vf, bias, p["qk_norm_q"], p["qk_norm_k"], p["lamv"], p["diff_norm_col"], p["blockdiag"], kbuf)


PAGES_PER_STEP = 32


def _sattn_kernel(pt_ref, *refs, layer, nvalid, lam_init, npp):
    del pt_ref, layer
    k_refs = refs[0:npp]
    v_refs = refs[npp:2 * npp]
    (q_ref, kf_ref, vf_ref, bias_ref, gq_ref, gk_ref, lamv_ref, sg_ref, bd_ref,
     y_ref, kout_ref, qs_ref, knew_ref, vnew_ref, m_ref, l_ref, acc_ref) = refs[2 * npp:]
    s_id = pl.program_id(1)
    nsteps = pl.num_programs(1)
    G = DIFF_KV_HEADS
    R = DIFF_HEADS // DIFF_KV_HEADS
    TP = SUBLANES
    MR = 2 * R * TP
    bd = bd_ref[...]
    lane = lax.broadcasted_iota(jnp.int32, (TP, LANES), 1)
    left = lane < DIFF_HD

    @pl.when(s_id == 0)
    def _():
        knew_ref[...] = jnp.zeros(knew_ref.shape, BF16)
        vnew_ref[...] = jnp.zeros(vnew_ref.shape, BF16)
        for g in range(G):
            gc = slice(g * LANES, (g + 1) * LANES)
            kn = _half_rmsnorm(kf_ref[0, :, gc], gk_ref[...], bd)
            kout_ref[0, :, gc] = kn
            knew_ref[g, 0:TP, :] = kn.astype(BF16)
            vnew_ref[g, 0:TP, :] = vf_ref[0, pl.ds(g, TP, stride=G), :].astype(BF16)
            for r in range(R):
                hc = slice((g * R + r) * LANES, (g * R + r + 1) * LANES)
                qn = _half_rmsnorm(q_ref[0, :, hc].astype(F32), gq_ref[...], bd) * DIFF_HD ** -0.5
                qs_ref[g, (0 * R + r) * TP:(0 * R + r + 1) * TP, :] = jnp.where(left, qn, 0.0).astype(BF16)
                qs_ref[g, (1 * R + r) * TP:(1 * R + r + 1) * TP, :] = jnp.where(left, 0.0, qn).astype(BF16)
        m_ref[...] = jnp.full(m_ref.shape, NEG_BIG, F32)
        l_ref[...] = jnp.zeros(l_ref.shape, F32)
        acc_ref[...] = jnp.zeros(acc_ref.shape, F32)

    def bias_rows(g, seg):
        per_head = [bias_ref[g * R + r, :, seg * LANES:(seg + 1) * LANES] for r in range(R)]
        return jnp.concatenate(per_head + per_head, axis=0)

    def update(g, s, vs):
        m_old = m_ref[g]
        m_new = jnp.maximum(m_old, jnp.max(s, axis=-1, keepdims=True))
        alpha = jnp.exp(m_old - m_new)
        p = jnp.exp(s - m_new)
        l_ref[g] = alpha * l_ref[g] + jnp.sum(p, axis=-1, keepdims=True)
        m_ref[g] = m_new
        pb = p.astype(BF16)
        pv = _dot(pb[:, 0:LANES], vs[0])
        for i in range(1, len(vs)):
            pv = pv + _dot(pb[:, i * LANES:(i + 1) * LANES], vs[i])
        acc_ref[g] = acc_ref[g] * alpha + pv

    def pages(last):
        scores = []
        for g in range(G):
            gc = slice(g * LANES, (g + 1) * LANES)
            kcat = jnp.concatenate([k_refs[i][gc, :].astype(BF16) for i in range(npp)], axis=1)
            far = bias_rows(g, 0)
            near = bias_rows(g, 1) if last else far
            scores.append(_dot(qs_ref[g], kcat) + jnp.concatenate([far] * (npp - 1) + [near], axis=1))
        pvs, alphas = [], []
        for g in range(G):
            s = scores[g]
            m_old = m_ref[g]
            m_new = jnp.maximum(m_old, jnp.max(s, axis=-1, keepdims=True))
            alpha = jnp.exp(m_old - m_new)
            p = jnp.exp(s - m_new)
            l_ref[g] = alpha * l_ref[g] + jnp.sum(p, axis=-1, keepdims=True)
            m_ref[g] = m_new
            vcat = jnp.concatenate([v_refs[i][pl.ds(g, PAGE_SIZE, stride=G), :].astype(BF16) for i in range(npp)],
                                   axis=0)
            pvs.append(_dot(p.astype(BF16), vcat))
            alphas.append(alpha)
        for g in range(G):
            acc_ref[g] = acc_ref[g] * alphas[g] + pvs[g]

    @pl.when(s_id < nsteps - 1)
    def _():
        pages(False)

    @pl.when(s_id == nsteps - 1)
    def _():
        pages(True)
        rowt = lax.broadcasted_iota(jnp.int32, (MR, LANES), 0) % TP
        colj = lax.broadcasted_iota(jnp.int32, (MR, LANES), 1)
        ok = jnp.logical_and(colj <= rowt, colj < nvalid)
        lam = _lambda(lamv_ref, lam_init)
        for g in range(G):
            sc = _nt_dot(qs_ref[g], knew_ref[g]) + bias_rows(g, 2)
            update(g, jnp.where(ok, sc, NEG_BIG), [vnew_ref[g]])
            acc = acc_ref[g] / l_ref[g]
            for r in range(R):
                o = acc[(0 * R + r) * TP:(0 * R + r + 1) * TP, :] - lam * acc[(1 * R + r) * TP:(1 * R + r + 1) * TP, :]
                o = o * lax.rsqrt(jnp.mean(o * o, axis=-1, keepdims=True) + NORM_EPS)
                hc = slice((g * R + r) * LANES, (g * R + r + 1) * LANES)
                y_ref[0, :, hc] = (o * sg_ref[...] * (1.0 - lam_init)).astype(BF16)


def _diff_attention_sample(proj, kf, vbuf, cache_k, cache_v, page_table, bias, p, layer, nvalid, lam_init):
    b, tp, _ = proj.shape
    vf = vbuf.reshape(vbuf.shape[0], b, tp * DIFF_KV_HEADS, DIFF_VD)
    n_pages = page_table.shape[1]
    npp = PAGES_PER_STEP
    while n_pages % npp:
        npp //= 2
    nsteps = n_pages // npp
    G = DIFF_KV_HEADS
    R = DIFF_HEADS // DIFF_KV_HEADS
    MR = 2 * R * tp
    ck = jnp.transpose(cache_k, (0, 1, 3, 4, 5, 2)).reshape(cache_k.shape[0], cache_k.shape[1], KV_COLS, PAGE_SIZE)
    cv = cache_v.reshape(cache_v.shape[0], cache_v.shape[1], PAGE_SIZE * DIFF_KV_HEADS, DIFF_VD)

    def page_spec(i):
        return pl.BlockSpec((None, None, KV_COLS, PAGE_SIZE),
                            lambda bi, s, pt: (layer, pt[bi * n_pages + s * npp + i], 0, 0))

    const = lambda shape: pl.BlockSpec(shape, lambda bi, s, pt: (0,) * len(shape))
    grid_spec = pltpu.PrefetchScalarGridSpec(
        num_scalar_prefetch=1,
        grid=(b, nsteps),
        in_specs=[page_spec(i) for i in range(npp)] + [page_spec(i) for i in range(npp)] + [
            pl.BlockSpec((1, tp, DIFF_HEADS * 2 * DIFF_HD), lambda bi, s, pt: (bi, 0, C_DQ // (DIFF_HEADS * 2 * DIFF_HD))),
            pl.BlockSpec((1, tp, KV_COLS), lambda bi, s, pt: (bi, 0, 0)),
            pl.BlockSpec((None, 1, tp * G, DIFF_VD), lambda bi, s, pt: (layer, bi, 0, 0)),
            const((DIFF_HEADS, tp, 3 * LANES)),
            const((1, LANES)), const((1, LANES)), const((4, LANES)), const((1, LANES)), const((LANES, LANES)),
        ],
        out_specs=[
            pl.BlockSpec((1, tp, DIFF_HEADS * DIFF_VD), lambda bi, s, pt: (bi, 0, 0)),
            pl.BlockSpec((1, tp, KV_COLS), lambda bi, s, pt: (bi, 0, 0)),
        ],
        scratch_shapes=[
            pltpu.VMEM((G, MR, LANES), BF16),
            pltpu.VMEM((G, PAGE_SIZE, LANES), BF16), pltpu.VMEM((G, PAGE_SIZE, LANES), BF16),
            pltpu.VMEM((G, MR, 1), F32), pltpu.VMEM((G, MR, 1), F32), pltpu.VMEM((G, MR, LANES), F32),
        ],
    )
    return pl.pallas_call(
        functools.partial(_sattn_kernel, layer=layer, nvalid=nvalid, lam_init=lam_init, npp=npp),
        grid_spec=grid_spec,
        out_shape=[
            jax.ShapeDtypeStruct((b, tp, DIFF_HEADS * DIFF_VD), BF16),
            jax.ShapeDtypeStruct((b, tp, KV_COLS), F32),
        ],
        compiler_params=_cparams("arbitrary", "arbitrary"),
        name="diff_attn_sample",
    )(page_table.reshape(-1), *([ck] * npp), *([cv] * npp), proj, kf, vf, bias,
      p["qk_norm_q"], p["qk_norm_k"], p["lamv"], p["diff_norm"], p["blockdiag"])


def _merge_kernel(ys_ref, yr_ref, yd_ref, g0_ref, g1_ref, g2_ref, x_ref, ws_ref, wr_ref, wd_ref, wo_ref,
                  bg_ref, h_ref):
    merged = None
    for i, (y_ref, w_ref, g_ref) in enumerate(((ys_ref, ws_ref, g0_ref), (yr_ref, wr_ref, g1_ref),
                                               (yd_ref, wd_ref, g2_ref))):
        br = _dot(y_ref[...], w_ref[...])
        t = _sigmoid(g_ref[...].astype(F32) + bg_ref[i:i + 1, :]) * br
        merged = t if merged is None else merged + t
    h_ref[...] = x_ref[...] + _dot(merged.astype(BF16), wo_ref[...])


def _merge(y_ssm, y_ret, y_diff, proj2d, x2d, p, tm):
    n = x2d.shape[0]
    tok = lambda cb: pl.BlockSpec((tm, D_MODEL), lambda i: (i, cb))
    wspec = pl.BlockSpec((D_MODEL, D_MODEL), lambda i: (0, 0))
    g0 = C_GATE // D_MODEL
    return pl.pallas_call(
        _merge_kernel,
        grid=(n // tm,),
        in_specs=[tok(0), tok(0), tok(0), tok(g0), tok(g0 + 1), tok(g0 + 2), tok(0),
                  wspec, wspec, wspec, wspec, pl.BlockSpec((N_BRANCHES, D_MODEL), lambda i: (0, 0))],
        out_specs=tok(0),
        out_shape=jax.ShapeDtypeStruct((n, D_MODEL), F32),
        compiler_params=_cparams("arbitrary"),
        name="merge",
    )(y_ssm, y_ret, y_diff, proj2d, proj2d, proj2d, x2d, p["w_ssm_out"], p["w_ret_out"], p["w_diff_out"],
      p["w_o"], p["b_gate"])


def _ffn_kernel(h_ref, g_ref, wg_ref, wu_ref, wd_ref, y_ref):
    h = h_ref[...]
    ms = jnp.mean(h * h, axis=-1, keepdims=True)
    hn = (h * lax.rsqrt(ms + NORM_EPS) * g_ref[...]).astype(BF16)
    act = _silu(_dot(hn, wg_ref[...])) * _dot(hn, wu_ref[...])
    y_ref[...] = h + _dot(act.astype(BF16), wd_ref[...])


def _ffn(h2d, p, tm):
    n = h2d.shape[0]
    resident = lambda shape, cb: pl.BlockSpec(shape, lambda i: (0, cb), pipeline_mode=pl.Buffered(1))
    return pl.pallas_call(
        _ffn_kernel,
        grid=(n // tm,),
        in_specs=[
            pl.BlockSpec((tm, D_MODEL), lambda i: (i, 0)),
            pl.BlockSpec((1, D_MODEL), lambda i: (0, 0)),
            resident((D_MODEL, D_FF), 0),
            resident((D_MODEL, D_FF), 1),
            resident((D_FF, D_MODEL), 0),
        ],
        out_specs=pl.BlockSpec((tm, D_MODEL), lambda i: (i, 0)),
        out_shape=jax.ShapeDtypeStruct((n, D_MODEL), F32),
        compiler_params=_cparams("arbitrary"),
        name="ffn",
    )(h2d, p["norm_ffn"], p["w_gate_up"], p["w_gate_up"], p["w_down"])


def _t5_bucket(dist):
    n = jnp.maximum(dist, 0)
    max_exact = N_BUCKETS // 2
    large = max_exact + (jnp.log(jnp.maximum(n, 1).astype(F32) / max_exact)
                         / math.log(MAX_DISTANCE / max_exact) * (N_BUCKETS - max_exact)).astype(jnp.int32)
    large = jnp.minimum(large, N_BUCKETS - 1)
    return jnp.where(n < max_exact, n, large)


def _far_bucket_is_constant(min_dist):
    max_exact = N_BUCKETS // 2
    d = np.float32(min_dist)
    large = max_exact + int(np.float32(np.log(d / np.float32(max_exact))) / np.float32(math.log(MAX_DISTANCE / max_exact))
                            * (N_BUCKETS - max_exact))
    return min_dist >= max_exact and large >= N_BUCKETS - 1


def _rope_tables(pos):
    half = RET_DK // 2
    inv = 1.0 / (ROPE_BASE ** (jnp.arange(half, dtype=F32) / half))
    ang = pos.astype(F32)[:, None] * inv[None, :]
    cos, sin = jnp.cos(ang), jnp.sin(ang)
    return jnp.concatenate([cos, cos], axis=1), jnp.concatenate([-sin, sin], axis=1)


def _layer_params(l, w_in, named):
    p = {k: v[l] for k, v in named.items()}
    w = w_in[l]
    o = (0,) + IN_OFFSETS + (w.shape[1],)
    z, xbc, dt, rq, rk, rv, rg, dq, dk, dv, gates = [w[:, o[i]:o[i + 1]] for i in range(len(IN_SPLITS))]
    out = {}
    out["w_main"] = jnp.concatenate([xbc, z, rq, rk, rv, rg, dq, gates], axis=1).astype(BF16)
    out["w_kvdt"] = jnp.concatenate([dk, dv, jnp.pad(dt, ((0, 0), (0, LANES - SSM_HEADS)))], axis=1).astype(BF16)
    out["norm_mix"] = p["norm_mix"].reshape(1, D_MODEL)
    out["conv_w"] = p["conv_w"]
    out["conv_b"] = p["conv_b"].reshape(1, CONV_DIM)
    out["dt_bias"] = jnp.pad(p["dt_bias"], (0, LANES - SSM_HEADS)).reshape(1, LANES)
    out["a_log"] = jnp.pad(p["a_log"], (0, LANES - SSM_HEADS)).reshape(1, LANES)
    out["d_skip"] = jnp.repeat(p["d_skip"], SSM_HEAD_DIM).reshape(1, SSM_D)
    out["ssm_norm"] = p["ssm_norm"].reshape(1, SSM_D)
    head_of_channel = np.arange(SSM_D) // SSM_HEAD_DIM
    out["head_expand"] = jnp.asarray(np.arange(LANES)[:, None] == head_of_channel[None, :], dtype=BF16)
    out["ret_norm"] = p["ret_norm"].reshape(1, RET_HEADS * RET_DV)
    out["qk_norm_q"] = jnp.tile(p["qk_norm_q"], 2).reshape(1, LANES)
    out["qk_norm_k"] = jnp.tile(p["qk_norm_k"], 2).reshape(1, LANES)
    lamv = jnp.stack([p["lambda_q1"], p["lambda_k1"], p["lambda_q2"], p["lambda_k2"]])
    out["lamv"] = jnp.pad(lamv, ((0, 0), (0, LANES - DIFF_HD)))
    out["diff_norm"] = p["diff_norm"].reshape(1, DIFF_VD)
    out["diff_norm_col"] = p["diff_norm"].reshape(DIFF_VD, 1)
    half = np.arange(LANES) // DIFF_HD
    out["blockdiag"] = jnp.asarray(half[:, None] == half[None, :], dtype=BF16)
    for k in ("w_ssm_out", "w_ret_out", "w_diff_out", "w_o", "w_gate_up", "w_down"):
        out[k] = p[k].astype(BF16)
    out["b_gate"] = p["b_gate"]
    out["norm_ffn"] = p["norm_ffn"].reshape(1, D_MODEL)
    return out


def _token_tile(n, cap):
    tm = min(n, cap)
    while n % tm:
        tm //= 2
    return tm


def _layer_common(x, p, layer, vbuf, tail0, h0, s0, cos_t, sin_t, nvalid, attn_fn):
    b, tp, _ = x.shape
    n = b * tp
    x2d = x.reshape(n, D_MODEL)
    proj, kf, vbuf, dt_raw = _inproj(x2d, p["norm_mix"], p["w_main"], p["w_kvdt"], _token_tile(n, 1024), vbuf, layer)
    proj3 = proj.reshape(b, tp, C_MAIN)
    dt3 = dt_raw.reshape(b, tp, LANES)
    tpad = -tp % CHUNK
    scan_in = lambda a: jnp.pad(a, ((0, 0), (0, tpad), (0, 0))) if tpad else a
    y_ssm, h_new, conv_rows, y_ret, s_new = _scans(scan_in(proj3), scan_in(dt3), tail0, h0, cos_t, sin_t, s0, p,
                                                   nvalid)
    y_diff, k_new = attn_fn(proj3, kf.reshape(b, tp, KV_COLS), vbuf)
    tm = _token_tile(n, 512)
    h = _merge(y_ssm[:, :tp].reshape(n, -1), y_ret[:, :tp].reshape(n, -1), y_diff.reshape(n, -1), proj, x2d, p, tm)
    y = _ffn(h, p, tm)
    last = (nvalid - 1) % CONV_TAIL
    assert last >= CONV_W - 2
    return (y.reshape(b, tp, D_MODEL), k_new, vbuf,
            h_new.reshape(b, SSM_HEADS, SSM_HEAD_DIM, SSM_STATE), conv_rows[:, last - (CONV_W - 2):last + 1, :], s_new)


def kernel(x_prompt, x_sample, cache_k, cache_v, page_table, state_ssm, state_conv, state_ret, rel_bias, norm_mix, w_in, b_gate, conv_w, conv_b, dt_bias, a_log, d_skip, ssm_norm, w_ssm_out, ret_norm, w_ret_out, qk_norm_q, qk_norm_k, lambda_q1, lambda_k1, lambda_q2, lambda_k2, diff_norm, w_diff_out, w_o, norm_ffn, w_gate_up, w_down):
    named = dict(norm_mix=norm_mix, b_gate=b_gate, conv_w=conv_w, conv_b=conv_b, dt_bias=dt_bias, a_log=a_log,
                 d_skip=d_skip, ssm_norm=ssm_norm, w_ssm_out=w_ssm_out, ret_norm=ret_norm, w_ret_out=w_ret_out,
                 qk_norm_q=qk_norm_q, qk_norm_k=qk_norm_k, lambda_q1=lambda_q1, lambda_k1=lambda_k1,
                 lambda_q2=lambda_q2, lambda_k2=lambda_k2, diff_norm=diff_norm, w_diff_out=w_diff_out, w_o=w_o,
                 norm_ffn=norm_ffn, w_gate_up=w_gate_up, w_down=w_down)
    depth = w_in.shape[0]
    bp, seq, _ = x_prompt.shape
    bs, dec, _ = x_sample.shape
    n_pages = page_table.shape[1]
    past = n_pages * PAGE_SIZE
    assert seq % CHUNK == 0 and CONV_W - 1 <= dec <= SUBLANES
    assert _far_bucket_is_constant(CHUNK + 1)

    ii = jnp.arange(CHUNK)[:, None]
    jj = jnp.arange(CHUNK)[None, :]
    idx_p = jnp.concatenate([_t5_bucket(k * CHUNK + jj - ii) for k in range(3)], axis=0)
    bias_h = _bias_tiles(rel_bias, idx_p, LOG2E).reshape(DIFF_KV_HEADS, DIFF_HEADS // DIFF_KV_HEADS, 3, CHUNK, CHUNK)
    bias_p = jnp.concatenate([bias_h[:, r] for r in range(DIFF_HEADS // DIFF_KV_HEADS)] * 2, axis=-1)
    tt = jnp.arange(SUBLANES)[:, None]
    idx_s = jnp.concatenate([_t5_bucket(jnp.broadcast_to(past + tt, (SUBLANES, LANES))),
                             _t5_bucket(tt + PAGE_SIZE - jj), _t5_bucket(tt - jj)], axis=1)
    bias_s = _bias_tiles(rel_bias, idx_s, 1.0)

    cos_p, sin_p = _rope_tables(jnp.arange(seq))
    cos_s, sin_s = _rope_tables(past + jnp.arange(CHUNK))

    xs = jnp.pad(x_sample, ((0, 0), (0, SUBLANES - dec), (0, 0)))
    zeros_tail = jnp.zeros((bp, CONV_TAIL, CONV_DIM), F32)
    zeros_h = jnp.zeros((bp, SSM_D, SSM_STATE), F32)
    zeros_s = jnp.zeros((bp, RET_HEADS, RET_DK, RET_DV), F32)

    vbuf_p = jnp.zeros((depth, bp * seq * DIFF_KV_HEADS, DIFF_VD), F32)
    vbuf_s = jnp.zeros((depth, bs * SUBLANES * DIFF_KV_HEADS, DIFF_VD), F32)
    kbuf_p = jnp.zeros((depth, bp, KV_COLS, seq), F32)

    yp, ys = x_prompt, xs
    outs_p, outs_s = [], []
    for l in range(depth):
        lam_init = 0.8 - 0.6 * math.exp(-0.3 * l)
        p = _layer_params(l, w_in, named)

        def attn_p(proj3, kf, vbuf, l=l, p=p, lam_init=lam_init, kbuf=kbuf_p):
            return _diff_attention_prompt(proj3, kf, vbuf, kbuf, bias_p, p, l, lam_init)

        yp, kbuf_p, vbuf_p, h1, c1, r1 = _layer_common(yp, p, l, vbuf_p, zeros_tail, zeros_h, zeros_s, cos_p, sin_p,
                                                       CHUNK, attn_p)
        outs_p.append((h1, c1, r1))

        tail_s = jnp.pad(state_conv[l], ((0, 0), (CONV_TAIL - (CONV_W - 1), 0), (0, 0)))
        h0_s = state_ssm[l].reshape(bs, SSM_D, SSM_STATE)

        def attn_s(proj3, kf, vbuf, l=l, p=p, lam_init=lam_init):
            return _diff_attention_sample(proj3, kf, vbuf, cache_k, cache_v, page_table, bias_s, p, l, dec, lam_init)

        ys_new, k2, vbuf_s, h2, c2, r2 = _layer_common(ys, p, l, vbuf_s, tail_s, h0_s, state_ret[l], cos_s, sin_s, dec,
                                                       attn_s)
        ys = jnp.where(jnp.arange(SUBLANES)[None, :, None] < dec, ys_new, 0.0)
        outs_s.append((k2[:, :dec].reshape(bs, dec, DIFF_KV_HEADS, 2, DIFF_HD), h2, c2, r2))

    stack = lambda outs, i: jnp.stack([o[i] for o in outs])
    k_prompt = kbuf_p.reshape(depth, bp, DIFF_KV_HEADS, 2, DIFF_HD, seq).transpose(0, 1, 5, 2, 3, 4)
    v_prompt = vbuf_p.reshape(depth, bp, seq, DIFF_KV_HEADS, DIFF_VD)
    v_sample = vbuf_s.reshape(depth, bs, SUBLANES, DIFF_KV_HEADS, DIFF_VD)[:, :, :dec]
    return (yp, ys[:, :dec],
            k_prompt, v_prompt, stack(outs_p, 0), stack(outs_p, 1), stack(outs_p, 2),
            stack(outs_s, 0), v_sample, stack(outs_s, 1), stack(outs_s, 2), stack(outs_s, 3))
```

```python
import functools
import math

import numpy as np
import jax
import jax.numpy as jnp
from jax import lax
from jax.experimental import pallas as pl
from jax.experimental.pallas import tpu as pltpu

F32 = jnp.float32
BF16 = jnp.bfloat16

D_MODEL = 1024
SSM_HEADS = 16
SSM_HEAD_DIM = 64
SSM_D = SSM_HEADS * SSM_HEAD_DIM
SSM_STATE = 128
SSM_GROUPS = 4
CONV_W = 4
CONV_DIM = SSM_D + 2 * SSM_GROUPS * SSM_STATE
RET_HEADS = 4
RET_DK = 128
RET_DV = 256
ROPE_BASE = 10000.0
DIFF_HEADS = 8
DIFF_KV_HEADS = 4
DIFF_HD = 64
DIFF_VD = 2 * DIFF_HD
N_BUCKETS = 32
MAX_DISTANCE = 128
N_BRANCHES = 3
D_FF = 2816
NORM_EPS = 1e-6
PAGE_SIZE = 128

CHUNK = 128
LANES = 128
SUBLANES = 8
GROUPS_PER_PASS = 4
CONV_TAIL = 16
NEG_BIG = -1e30
LOG2E = 1.4426950408889634
VMEM_LIMIT = 52 * 1024 * 1024

IN_SPLITS = (SSM_D, CONV_DIM, SSM_HEADS,
             RET_HEADS * RET_DK, RET_HEADS * RET_DK, RET_HEADS * RET_DV, RET_HEADS * RET_DV,
             DIFF_HEADS * 2 * DIFF_HD, DIFF_KV_HEADS * 2 * DIFF_HD, DIFF_KV_HEADS * DIFF_VD,
             N_BRANCHES * D_MODEL)
IN_OFFSETS = tuple(int(v) for v in np.cumsum(IN_SPLITS)[:-1])

C_XBC, C_Z, C_RQ, C_RK, C_RV, C_RG, C_DQ, C_GATE, C_MAIN = 0, 2048, 3072, 3584, 4096, 5120, 6144, 7168, 10240
TN_PROJ = 2560
KV_COLS = DIFF_KV_HEADS * 2 * DIFF_HD


def _cparams(*sem):
    return pltpu.CompilerParams(dimension_semantics=sem, vmem_limit_bytes=VMEM_LIMIT)


def _nt_dot(a, b):
    return lax.dot_general(a, b, (((1,), (1,)), ((), ())), preferred_element_type=F32)


def _tn_dot(a, b):
    return lax.dot_general(a, b, (((0,), (0,)), ((), ())), preferred_element_type=F32)


def _dot(a, b):
    return jnp.dot(a, b, preferred_element_type=F32)


def _split3(x):
    hi = x.astype(BF16)
    r1 = x - hi.astype(F32)
    mid = r1.astype(BF16)
    lo = (r1 - mid.astype(F32)).astype(BF16)
    return hi, mid, lo


def _sigmoid(x):
    return 0.5 * jnp.tanh(0.5 * x) + 0.5


def _silu(x):
    h = 0.5 * x
    return h + h * jnp.tanh(h)


def _inproj_kernel(x_ref, g_ref, w_ref, wkvdt_ref, vprev_ref, main_ref, kf_ref, vf_ref, dt_ref, xn_ref):
    del vprev_ref
    tm = x_ref.shape[0]

    @pl.when(pl.program_id(1) == 0)
    def _():
        x = x_ref[...]
        ms = jnp.mean(x * x, axis=-1, keepdims=True)
        xn = (x * lax.rsqrt(ms + NORM_EPS) * g_ref[...]).astype(BF16)
        xn_ref[...] = xn
        kvdt = _dot(xn, wkvdt_ref[...])
        kf_ref[...] = kvdt[:, :KV_COLS]
        for g in range(DIFF_KV_HEADS):
            vf_ref[pl.ds(g, tm, stride=DIFF_KV_HEADS), :] = kvdt[:, KV_COLS + g * DIFF_VD:KV_COLS + (g + 1) * DIFF_VD]
        dt_ref[...] = kvdt[:, 2 * KV_COLS:]

    main_ref[...] = _dot(xn_ref[...], w_ref[...]).astype(BF16)


def _inproj(x2d, gain, w_main, w_kvdt, tm, vbuf, layer):
    n = x2d.shape[0]
    grid = (n // tm, C_MAIN // TN_PROJ)
    return pl.pallas_call(
        _inproj_kernel,
        grid=grid,
        in_specs=[
            pl.BlockSpec((tm, D_MODEL), lambda i, j: (i, 0)),
            pl.BlockSpec((1, D_MODEL), lambda i, j: (0, 0)),
            pl.BlockSpec((D_MODEL, TN_PROJ), lambda i, j: (0, j)),
            pl.BlockSpec((D_MODEL, 2 * KV_COLS + LANES), lambda i, j: (0, 0)),
            pl.BlockSpec(memory_space=pl.ANY),
        ],
        out_specs=[
            pl.BlockSpec((tm, TN_PROJ), lambda i, j: (i, j)),
            pl.BlockSpec((tm, KV_COLS), lambda i, j: (i, 0)),
            pl.BlockSpec((None, tm * DIFF_KV_HEADS, DIFF_VD), lambda i, j: (layer, i, 0)),
            pl.BlockSpec((tm, LANES), lambda i, j: (i, 0)),
        ],
        out_shape=[
            jax.ShapeDtypeStruct((n, C_MAIN), BF16),
            jax.ShapeDtypeStruct((n, KV_COLS), F32),
            jax.ShapeDtypeStruct(vbuf.shape, F32),
            jax.ShapeDtypeStruct((n, LANES), F32),
        ],
        scratch_shapes=[pltpu.VMEM((tm, D_MODEL), BF16)],
        input_output_aliases={4: 2},
        compiler_params=_cparams("arbitrary", "arbitrary"),
        name="inproj",
    )(x2d, gain, w_main, w_kvdt, vbuf)


def _ssd_kernel(xbc_ref, z_ref, dt_ref, tail0_ref, h0_ref, cw_ref, cb_ref, dtb_ref, alog_ref, dskip_ref,
                gn_ref, ex_ref, y_ref, hout_ref, convout_ref, xext_ref, tlo_ref, xc_ref, ht_ref, *, c, nvalid, nchunks):
    L = CHUNK
    GW = SSM_D // SSM_GROUPS
    SL = 512

    @pl.when(c == 0)
    def _():
        t0 = tail0_ref[0]
        t0b = t0.astype(BF16)
        xext_ref[0:CONV_TAIL, :] = t0b
        tlo_ref[...] = (t0 - t0b.astype(F32)).astype(BF16)
        for k in range(SSM_D // LANES):
            ht_ref[:, k * LANES:(k + 1) * LANES] = h0_ref[0, k * LANES:(k + 1) * LANES, :].T

    xext_ref[CONV_TAIL:CONV_TAIL + L, :] = xbc_ref[0]
    srow = lax.broadcasted_iota(jnp.int32, (L, CONV_TAIL + L), 0)
    scol = lax.broadcasted_iota(jnp.int32, (L, CONV_TAIL + L), 1)
    shifts = [jnp.where(scol == srow + CONV_TAIL - s, 1.0, 0.0).astype(BF16) for s in range(1, CONV_W)]
    shifted = [[_dot(shifts[s - 1], xext_ref[:, sl * SL:(sl + 1) * SL]) for s in range(1, CONV_W)]
               for sl in range(CONV_DIM // SL)]
    for sl in range(CONV_DIM // SL):
        cols = slice(sl * SL, (sl + 1) * SL)
        conv = cw_ref[CONV_W - 1:CONV_W, cols] * xbc_ref[0, :, cols].astype(F32) + cb_ref[:, cols]
        for s in range(1, CONV_W):
            conv = conv + cw_ref[CONV_W - 1 - s:CONV_W - s, cols] * shifted[sl][s - 1]
        xc_ref[:, cols] = conv

    @pl.when(c == 0)
    def _():
        for sl in range(CONV_DIM // SL):
            cols = slice(sl * SL, (sl + 1) * SL)
            corr = None
            for s in range(1, CONV_W):
                t = cw_ref[CONV_W - 1 - s:CONV_W - s, cols] * _dot(shifts[s - 1][0:CONV_TAIL, 0:CONV_TAIL],
                                                                    tlo_ref[:, cols])
                corr = t if corr is None else corr + t
            xc_ref[0:CONV_TAIL, cols] += corr

    @pl.when(c == nchunks - 1)
    def _():
        a = CONV_TAIL * ((nvalid - 1) // CONV_TAIL)
        convout_ref[0] = xext_ref[CONV_TAIL + a:CONV_TAIL + a + CONV_TAIL, :].astype(F32)

    xext_ref[0:CONV_TAIL, :] = xext_ref[L:L + CONV_TAIL, :]

    row = lax.broadcasted_iota(jnp.int32, (L, L), 0)
    col = lax.broadcasted_iota(jnp.int32, (L, L), 1)
    causal = row >= col
    left = col < SSM_HEAD_DIM

    x = dt_ref[0] + dtb_ref[...]
    dt = jnp.maximum(x, 0.0) + jnp.log1p(jnp.exp(-jnp.abs(x)))
    if nvalid < L:
        dt = jnp.where(row < nvalid, dt, 0.0)
    a = -jnp.exp(alog_ref[...])
    da = dt * a
    tri = jnp.where(causal, 1.0, 0.0).astype(BF16)
    cs = sum(_dot(tri, p) for p in _split3(da))
    cs_t = cs.T
    cs_parts = _split3(cs)
    dt_parts = _split3(dt)

    NG = SSM_GROUPS
    HPG = SSM_HEADS // SSM_GROUPS
    gcs = [slice(g * GW, (g + 1) * GW) for g in range(NG)]
    bcol = lambda g: slice(SSM_D + g * SSM_STATE, SSM_D + (g + 1) * SSM_STATE)
    ccol = lambda g: slice(SSM_D + NG * SSM_STATE + g * SSM_STATE, SSM_D + NG * SSM_STATE + (g + 1) * SSM_STATE)

    def scan_groups(gs):
        csx = {g: sum(_dot(p, ex_ref[:, gcs[g]]) for p in cs_parts) for g in gs}
        dtx = {g: sum(_dot(p, ex_ref[:, gcs[g]]) for p in dt_parts) for g in gs}
        bgs = {g: _silu(xc_ref[:, bcol(g)]).astype(BF16) for g in gs}
        cgs = {g: _silu(xc_ref[:, ccol(g)]).astype(BF16) for g in gs}
        cbs = {g: _nt_dot(cgs[g], bgs[g]) for g in gs}
        htgs = {g: ht_ref[:, gcs[g]] for g in gs}
        ysts = {g: _dot(cgs[g], htgs[g].astype(BF16)) for g in gs}
        xss = {g: _silu(xc_ref[:, gcs[g]]) for g in gs}
        xdts = {g: xss[g] * dtx[g] for g in gs}
        xdt_bs = {g: xdts[g].astype(BF16) for g in gs}
        lastxs = {g: csx[g][L - 1:L, :] for g in gs}
        xdtw_bs = {g: (xdts[g] * jnp.exp(lastxs[g] - csx[g])).astype(BF16) for g in gs}
        prods = {}
        for g in gs:
            for e in range(HPG):
                h = g * HPG + e
                seg = cs[:, h:h + 1] - cs_t[h:h + 1, :]
                decay = jnp.exp(jnp.where(causal, seg, NEG_BIG))
                mat = (cbs[g] * decay).astype(BF16)
                xp = xdt_bs[g][:, (e // 2) * LANES:(e // 2 + 1) * LANES]
                keep = left if e % 2 == 0 else jnp.logical_not(left)
                prods[h] = _dot(mat, jnp.where(keep, xp, jnp.zeros_like(xp)))
        upd = {g: _tn_dot(bgs[g], xdtw_bs[g]) for g in gs}
        for g in gs:
            ht_ref[:, gcs[g]] = htgs[g] * jnp.exp(lastxs[g]) + upd[g]
        for g in gs:
            gc = gcs[g]
            y_in = jnp.concatenate([prods[g * HPG + 2 * pr] + prods[g * HPG + 2 * pr + 1] for pr in range(HPG // 2)],
                                   axis=1)
            y = y_in + ysts[g] * jnp.exp(csx[g]) + dskip_ref[:, gc] * xss[g]
            y = y * _silu(z_ref[0, :, gc].astype(F32))
            ms = jnp.mean(y * y, axis=-1, keepdims=True)
            y_ref[0, :, gc] = (y * lax.rsqrt(ms + NORM_EPS) * gn_ref[:, gc]).astype(BF16)

    for first in range(0, NG, GROUPS_PER_PASS):
        scan_groups(range(first, first + GROUPS_PER_PASS))

    @pl.when(c == nchunks - 1)
    def _():
        for k in range(SSM_D // LANES):
            hout_ref[0, k * LANES:(k + 1) * LANES, :] = ht_ref[:, k * LANES:(k + 1) * LANES].T


def _ret_kernel(q_ref, k_ref, v_ref, rg_ref, cos_ref, sin_ref, s0_ref, gn_ref, y_ref, sout_ref, *, c, ltrue):
    L = CHUNK

    @pl.when(c == 0)
    def _():
        sout_ref[...] = s0_ref[...]

    row = lax.broadcasted_iota(jnp.int32, (L, L), 0)
    col = lax.broadcasted_iota(jnp.int32, (L, L), 1)
    rel = (row - col).astype(F32)
    idx = row[:, 0:1].astype(F32)
    cosf = cos_ref[...]
    sins = sin_ref[...]
    H = RET_HEADS
    lgs = [math.log(1.0 - 2.0 ** (-5.0 - h)) for h in range(H)]
    kcs = [slice(h * RET_DK, (h + 1) * RET_DK) for h in range(H)]
    vcs = [slice(h * RET_DV, (h + 1) * RET_DV) for h in range(H)]
    qrs, krs = [], []
    for h in range(H):
        qh = q_ref[0, :, kcs[h]].astype(F32)
        kh = k_ref[0, :, kcs[h]].astype(F32)
        qrs.append(qh * cosf + pltpu.roll(qh, RET_DK // 2, 1) * sins)
        krs.append((kh * cosf + pltpu.roll(kh, RET_DK // 2, 1) * sins) * RET_DK ** -0.5)
    qr_bs = [q_.astype(BF16) for q_ in qrs]
    vhs = [v_ref[0, :, vcs[h]] for h in range(H)]
    s_olds = [sout_ref[0, h] for h in range(H)]
    scores = [_nt_dot(qr_bs[h], krs[h].astype(BF16)) for h in range(H)]
    cross = [_dot(qr_bs[h], s_olds[h].astype(BF16)) for h in range(H)]
    atts = []
    for h in range(H):
        dmat = jnp.where(rel >= 0, jnp.exp(jnp.maximum(rel, 0.0) * lgs[h]), 0.0)
        atts.append((scores[h] * dmat).astype(BF16))
    inner = [_dot(atts[h], vhs[h]) for h in range(H)]
    for h in range(H):
        k_dec = jnp.exp((ltrue - 1.0 - idx) * lgs[h])
        sout_ref[0, h] = s_olds[h] * math.exp(ltrue * lgs[h]) + _tn_dot((krs[h] * k_dec).astype(BF16), vhs[h])
    for h in range(H):
        o = inner[h] + cross[h] * jnp.exp((idx + 1.0) * lgs[h])
        oc = o - jnp.mean(o, axis=-1, keepdims=True)
        on = oc * lax.rsqrt(jnp.mean(oc * oc, axis=-1, keepdims=True) + NORM_EPS)
        y_ref[0, :, vcs[h]] = (on * gn_ref[:, vcs[h]] * _silu(rg_ref[0, :, vcs[h]].astype(F32))).astype(BF16)


N_SSD_IN, N_RET_IN, N_SSD_OUT, N_RET_OUT = 12, 8, 3, 2


CHUNKS_PER_STEP = 4


def _scan_kernel(*refs, nvalid, nchunks, per_step):
    i0 = N_SSD_IN
    i1 = i0 + N_RET_IN
    i2 = i1 + N_SSD_OUT
    i3 = i2 + N_RET_OUT
    ssd_in, ret_in, ssd_out, ret_out = refs[0:i0], refs[i0:i1], refs[i1:i2], refs[i2:i3]

    def chunk(sc, carry):
        c = pl.program_id(1) * per_step + sc
        rows = pl.ds(pl.multiple_of(sc * CHUNK, CHUNK), CHUNK)
        seq = lambda r: r.at[:, rows, :]
        tab = lambda r: r.at[rows, :]
        _ret_kernel(*[seq(r) for r in ret_in[0:4]], tab(ret_in[4]), tab(ret_in[5]), *ret_in[6:],
                    seq(ret_out[0]), ret_out[1], c=c, ltrue=float(nvalid))
        _ssd_kernel(*[seq(r) for r in ssd_in[0:3]], *ssd_in[3:], seq(ssd_out[0]), *ssd_out[1:], *refs[i3:],
                    c=c, nvalid=nvalid, nchunks=nchunks)
        return carry

    lax.fori_loop(0, per_step, chunk, 0)


def _scans(proj, dt_raw, tail0, h0, cos_t, sin_t, s0, p, nvalid):
    b, t, _ = proj.shape
    nchunks = t // CHUNK
    per_step = math.gcd(nchunks, CHUNKS_PER_STEP)
    L = CHUNK * per_step
    qk_w = RET_HEADS * RET_DK
    v_w = RET_HEADS * RET_DV
    const = lambda shape: pl.BlockSpec(shape, lambda i, c: (0,) * len(shape))
    ssd_in = [
        pl.BlockSpec((1, L, CONV_DIM), lambda i, c: (i, c, C_XBC // CONV_DIM)),
        pl.BlockSpec((1, L, SSM_D), lambda i, c: (i, c, C_Z // SSM_D)),
        pl.BlockSpec((1, L, LANES), lambda i, c: (i, c, 0)),
        pl.BlockSpec((1, CONV_TAIL, CONV_DIM), lambda i, c: (i, 0, 0)),
        pl.BlockSpec((1, SSM_D, SSM_STATE), lambda i, c: (i, 0, 0)),
        const((CONV_W, CONV_DIM)), const((1, CONV_DIM)), const((1, LANES)), const((1, LANES)),
        const((1, SSM_D)), const((1, SSM_D)), const((LANES, SSM_D)),
    ]
    ret_in = [
        pl.BlockSpec((1, L, qk_w), lambda i, c: (i, c, C_RQ // qk_w)),
        pl.BlockSpec((1, L, qk_w), lambda i, c: (i, c, C_RK // qk_w)),
        pl.BlockSpec((1, L, v_w), lambda i, c: (i, c, C_RV // v_w)),
        pl.BlockSpec((1, L, v_w), lambda i, c: (i, c, C_RG // v_w)),
        pl.BlockSpec((L, RET_DK), lambda i, c: (c, 0)),
        pl.BlockSpec((L, RET_DK), lambda i, c: (c, 0)),
        pl.BlockSpec((1, RET_HEADS, RET_DK, RET_DV), lambda i, c: (i, 0, 0, 0)),
        const((1, v_w)),
    ]
    ssd_out = [
        pl.BlockSpec((1, L, SSM_D), lambda i, c: (i, c, 0)),
        pl.BlockSpec((1, SSM_D, SSM_STATE), lambda i, c: (i, 0, 0)),
        pl.BlockSpec((1, CONV_TAIL, CONV_DIM), lambda i, c: (i, 0, 0)),
    ]
    ret_out = [
        pl.BlockSpec((1, L, v_w), lambda i, c: (i, c, 0)),
        pl.BlockSpec((1, RET_HEADS, RET_DK, RET_DV), lambda i, c: (i, 0, 0, 0)),
    ]
    assert (len(ssd_in), len(ret_in), len(ssd_out), len(ret_out)) == (N_SSD_IN, N_RET_IN, N_SSD_OUT, N_RET_OUT)
    return pl.pallas_call(
        functools.partial(_scan_kernel, nvalid=nvalid, nchunks=nchunks, per_step=per_step),
        grid=(b, nchunks // per_step),
        in_specs=ssd_in + ret_in,
        out_specs=ssd_out + ret_out,
        out_shape=[
            jax.ShapeDtypeStruct((b, t, SSM_D), BF16),
            jax.ShapeDtypeStruct((b, SSM_D, SSM_STATE), F32),
            jax.ShapeDtypeStruct((b, CONV_TAIL, CONV_DIM), F32),
            jax.ShapeDtypeStruct((b, t, v_w), BF16),
            jax.ShapeDtypeStruct((b, RET_HEADS, RET_DK, RET_DV), F32),
        ],
        scratch_shapes=[
            pltpu.VMEM((CONV_TAIL + CHUNK, CONV_DIM), BF16),
            pltpu.VMEM((CONV_TAIL, CONV_DIM), BF16),
            pltpu.VMEM((CHUNK, CONV_DIM), F32),
            pltpu.VMEM((SSM_STATE, SSM_D), F32),
        ],
        compiler_params=_cparams("arbitrary", "arbitrary"),
        name="scans",
    )(proj, proj, dt_raw, tail0, h0, p["conv_w"], p["conv_b"], p["dt_bias"], p["a_log"], p["d_skip"],
      p["ssm_norm"], p["head_expand"], proj, proj, proj, proj, cos_t, sin_t, s0, p["ret_norm"])


def _bias_kernel(tab_ref, idx_ref, out_ref, *, scale):
    h = pl.program_id(0)
    idx = idx_ref[...]
    acc = jnp.zeros(idx.shape, F32)
    for b in range(N_BUCKETS):
        acc = acc + jnp.where(idx == b, tab_ref[b * DIFF_HEADS + h], 0.0)
    out_ref[0] = acc * scale


def _bias_tiles(rel_bias, idx, scale):
    r, c = idx.shape
    return pl.pallas_call(
        functools.partial(_bias_kernel, scale=scale),
        grid=(DIFF_HEADS,),
        in_specs=[pl.BlockSpec(memory_space=pltpu.SMEM), pl.BlockSpec((r, c), lambda h: (0, 0))],
        out_specs=pl.BlockSpec((1, r, c), lambda h: (h, 0, 0)),
        out_shape=jax.ShapeDtypeStruct((DIFF_HEADS, r, c), F32),
        compiler_params=_cparams("arbitrary"),
        name="t5_bias",
    )(rel_bias.reshape(-1), idx)


def _half_rmsnorm(x, gain, bd):
    x2 = x * x
    hi = x2.astype(BF16)
    lo = (x2 - hi.astype(F32)).astype(BF16)
    ss = _dot(hi, bd) + _dot(lo, bd)
    return x * lax.rsqrt(ss * (1.0 / DIFF_HD) + NORM_EPS) * gain


def _half_rmsnorm_blocks(blocks, gain, bd):
    rows = blocks[0].shape[0]
    y = _half_rmsnorm(jnp.concatenate(blocks, axis=0), gain, bd)
    return [y[i * rows:(i + 1) * rows] for i in range(len(blocks))]


def _lambda(lamv_ref, lam_init):
    s1 = jnp.sum(lamv_ref[0:1, :] * lamv_ref[1:2, :], axis=-1, keepdims=True)
    s2 = jnp.sum(lamv_ref[2:3, :] * lamv_ref[3:4, :], axis=-1, keepdims=True)
    return jnp.exp(s1) - jnp.exp(s2) + lam_init


FAR_UNITS = 4
ONES_ROWS = 16
Q_TILES_PER_STEP = 4


def _dattn_kernel(q_ref, kf_ref, vf_ref, bias_ref, gq_ref, gk_ref, lamv_ref, sgc_ref, bd_ref, kprev_ref,
                  y_ref, kout_ref, kn_ref, vt_ref, qs_ref, m_ref, acc_ref, *, t, q_tiles, lam_init):
    del kprev_ref
    step_id = pl.program_id(1)
    TQ = CHUNK
    G = DIFF_KV_HEADS
    R = DIFF_HEADS // DIFF_KV_HEADS
    NC = 2 * R * TQ
    bd = bd_ref[...]

    @pl.when(step_id == 0)
    def _():
        for i in range(t // TQ):
            r = slice(i * TQ, (i + 1) * TQ)
            kns = _half_rmsnorm_blocks([kf_ref[0, r, g * LANES:(g + 1) * LANES] for g in range(G)], gk_ref[...], bd)
            for g in range(G):
                kout_ref[0, g * LANES:(g + 1) * LANES, r] = kns[g].T
                kn_ref[g, r, :] = kns[g].astype(BF16)
                vt_ref[g, i, 0:DIFF_VD, :] = vf_ref[0, pl.ds(i * TQ * G + g, TQ, stride=G), :].T.astype(BF16)
                vt_ref[g, i, DIFF_VD:, :] = jnp.ones((ONES_ROWS, TQ), BF16)

    def tile(ti, carry):
        qi = step_id * q_tiles + ti
        qrows = pl.ds(pl.multiple_of(ti * TQ, TQ), TQ)
        lane = lax.broadcasted_iota(jnp.int32, (TQ, LANES), 1)
        left = lane < DIFF_HD
        qns = _half_rmsnorm_blocks([q_ref[0, qrows, h * LANES:(h + 1) * LANES].astype(F32)
                                    for h in range(DIFF_HEADS)], gq_ref[...], bd)
        for g in range(G):
            for r in range(R):
                qn = qns[g * R + r] * (DIFF_HD ** -0.5 * LOG2E)
                qs_ref[g, (0 * R + r) * TQ:(0 * R + r + 1) * TQ, :] = jnp.where(left, qn, 0.0).astype(BF16)
                qs_ref[g, (1 * R + r) * TQ:(1 * R + r + 1) * TQ, :] = jnp.where(left, 0.0, qn).astype(BF16)
        m_ref[...] = jnp.full(m_ref.shape, NEG_BIG, F32)
        acc_ref[...] = jnp.zeros(acc_ref.shape, F32)

        def step(k0, tiles):
            nunits = len(tiles)
            rows = pl.ds(pl.multiple_of(k0 * TQ, TQ), nunits * TQ)
            ss = [_nt_dot(kn_ref[g, rows, :], qs_ref[g]) for g in range(G)]
            ps, alphas = [], []
            for g in range(G):
                s = ss[g] + jnp.concatenate([bias_ref[g, tl] for tl in tiles], axis=0)
                if 0 in tiles:
                    first = tiles.index(0) * TQ
                    key = lax.broadcasted_iota(jnp.int32, (nunits * TQ, NC), 0) - first
                    qry = lax.broadcasted_iota(jnp.int32, (nunits * TQ, NC), 1) & (TQ - 1)
                    s = jnp.where(key <= qry, s, NEG_BIG)
                m_old = m_ref[g]
                m_new = jnp.maximum(m_old, jnp.max(s, axis=0, keepdims=True))
                alpha = jnp.exp2(m_old - m_new)
                p = jnp.exp2(s - m_new)
                m_ref[g] = m_new
                ps.append(p.astype(BF16))
                alphas.append(alpha)
            for g in range(G):
                if nunits == 1:
                    vt = vt_ref[g, k0]
                else:
                    vt = jnp.concatenate([vt_ref[g, k0 + u] for u in range(nunits)], axis=1)
                acc_ref[g] = acc_ref[g] * alphas[g] + _dot(vt, ps[g])

        nfar = jnp.maximum(qi - 1, 0)
        nbig = nfar // FAR_UNITS
        rem = nfar - nbig * FAR_UNITS

        def far_big(i, c):
            step(i * FAR_UNITS, (2,) * FAR_UNITS)
            return c

        lax.fori_loop(0, nbig, far_big, 0)

        @pl.when(rem >= 2)
        def _():
            step(nbig * FAR_UNITS, (2, 2))

        @pl.when((rem & 1) == 1)
        def _():
            step(nbig * FAR_UNITS + (rem & 2), (2,))

        @pl.when(qi >= 1)
        def _():
            step(qi - 1, (1, 0))

        @pl.when(qi == 0)
        def _():
            step(0, (0,))

        lam = _lambda(lamv_ref, lam_init)
        for g in range(G):
            inv_l = 1.0 / acc_ref[g, DIFF_VD:DIFF_VD + 1, :]
            for r in range(R):
                c0 = slice((0 * R + r) * TQ, (0 * R + r + 1) * TQ)
                c1 = slice((1 * R + r) * TQ, (1 * R + r + 1) * TQ)
                o = acc_ref[g, 0:DIFF_VD, c0] * inv_l[:, c0] - lam * (acc_ref[g, 0:DIFF_VD, c1] * inv_l[:, c1])
                o = o * lax.rsqrt(jnp.mean(o * o, axis=0, keepdims=True) + NORM_EPS)
                o = o * sgc_ref[...] * (1.0 - lam_init)
                y_ref[0, qrows, (g * R + r) * LANES:(g * R + r + 1) * LANES] = o.T.astype(BF16)
        return carry

    lax.fori_loop(0, q_tiles, tile, 0)


def _diff_attention_prompt(proj, kf, vbuf, kbuf, bias, p, layer, lam_init):
    b, t, _ = proj.shape
    vf = vbuf.reshape(vbuf.shape[0], b, t * DIFF_KV_HEADS, DIFF_VD)
    TQ = CHUNK
    G = DIFF_KV_HEADS
    R = DIFF_HEADS // DIFF_KV_HEADS
    NC = 2 * R * TQ
    qw = DIFF_HEADS * 2 * DIFF_HD
    const = lambda shape: pl.BlockSpec(shape, lambda i, q: (0,) * len(shape))
    q_tiles = math.gcd(t // TQ, Q_TILES_PER_STEP)
    return pl.pallas_call(
        functools.partial(_dattn_kernel, t=t, q_tiles=q_tiles, lam_init=lam_init),
        grid=(b, t // (TQ * q_tiles)),
        in_specs=[
            pl.BlockSpec((1, TQ * q_tiles, qw), lambda i, q: (i, q, C_DQ // qw)),
            pl.BlockSpec((1, t, KV_COLS), lambda i, q: (i, 0, 0)),
            pl.BlockSpec((None, 1, t * G, DIFF_VD), lambda i, q: (layer, i, 0, 0)),
            const((G, 3, TQ, NC)),
            const((1, LANES)), const((1, LANES)), const((4, LANES)), const((DIFF_VD, 1)), const((LANES, LANES)),
            pl.BlockSpec(memory_space=pl.ANY),
        ],
        out_specs=[
            pl.BlockSpec((1, TQ * q_tiles, qw), lambda i, q: (i, q, 0)),
            pl.BlockSpec((None, 1, KV_COLS, t), lambda i, q: (layer, i, 0, 0)),
        ],
        out_shape=[
            jax.ShapeDtypeStruct((b, t, DIFF_HEADS * DIFF_VD), BF16),
            jax.ShapeDtypeStruct(kbuf.shape, F32),
        ],
        input_output_aliases={9: 1},
        scratch_shapes=[
            pltpu.VMEM((G, t, LANES), BF16), pltpu.VMEM((G, t // TQ, DIFF_VD + ONES_ROWS, TQ), BF16),
            pltpu.VMEM((G, NC, LANES), BF16),
            pltpu.VMEM((G, 1, NC), F32),
            pltpu.VMEM((G, DIFF_VD + ONES_ROWS, NC), F32),
        ],
        compiler_params=_cparams("arbitrary", "arbitrary"),
        name="diff_attn_prompt",
    )(proj, kf, vf, bias, p["qk_norm_q"], p["qk_norm_k"], p["lamv"], p["diff_norm_col"], p["blockdiag"], kbuf)


PAGES_PER_STEP = 32


def _sattn_kernel(pt_ref, *refs, layer, nvalid, lam_init, npp):
    del pt_ref, layer
    k_refs = refs[0:npp]
    v_refs = refs[npp:2 * npp]
    (q_ref, kf_ref, vf_ref, bias_ref, gq_ref, gk_ref, lamv_ref, sg_ref, bd_ref,
     y_ref, kout_ref, qs_ref, knew_ref, vnew_ref, m_ref, l_ref, acc_ref) = refs[2 * npp:]
    s_id = pl.program_id(1)
    nsteps = pl.num_programs(1)
    G = DIFF_KV_HEADS
    R = DIFF_HEADS // DIFF_KV_HEADS
    TP = SUBLANES
    MR = 2 * R * TP
    bd = bd_ref[...]
    lane = lax.broadcasted_iota(jnp.int32, (TP, LANES), 1)
    left = lane < DIFF_HD

    @pl.when(s_id == 0)
    def _():
        knew_ref[...] = jnp.zeros(knew_ref.shape, BF16)
        vnew_ref[...] = jnp.zeros(vnew_ref.shape, BF16)
        for g in range(G):
            gc = slice(g * LANES, (g + 1) * LANES)
            kn = _half_rmsnorm(kf_ref[0, :, gc], gk_ref[...], bd)
            kout_ref[0, :, gc] = kn
            knew_ref[g, 0:TP, :] = kn.astype(BF16)
            vnew_ref[g, 0:TP, :] = vf_ref[0, pl.ds(g, TP, stride=G), :].astype(BF16)
            for r in range(R):
                hc = slice((g * R + r) * LANES, (g * R + r + 1) * LANES)
                qn = _half_rmsnorm(q_ref[0, :, hc].astype(F32), gq_ref[...], bd) * DIFF_HD ** -0.5
                qs_ref[g, (0 * R + r) * TP:(0 * R + r + 1) * TP, :] = jnp.where(left, qn, 0.0).astype(BF16)
                qs_ref[g, (1 * R + r) * TP:(1 * R + r + 1) * TP, :] = jnp.where(left, 0.0, qn).astype(BF16)
        m_ref[...] = jnp.full(m_ref.shape, NEG_BIG, F32)
        l_ref[...] = jnp.zeros(l_ref.shape, F32)
        acc_ref[...] = jnp.zeros(acc_ref.shape, F32)

    def bias_rows(g, seg):
        per_head = [bias_ref[g * R + r, :, seg * LANES:(seg + 1) * LANES] for r in range(R)]
        return jnp.concatenate(per_head + per_head, axis=0)

    def update(g, s, vs):
        m_old = m_ref[g]
        m_new = jnp.maximum(m_old, jnp.max(s, axis=-1, keepdims=True))
        alpha = jnp.exp(m_old - m_new)
        p = jnp.exp(s - m_new)
        l_ref[g] = alpha * l_ref[g] + jnp.sum(p, axis=-1, keepdims=True)
        m_ref[g] = m_new
        pb = p.astype(BF16)
        pv = _dot(pb[:, 0:LANES], vs[0])
        for i in range(1, len(vs)):
            pv = pv + _dot(pb[:, i * LANES:(i + 1) * LANES], vs[i])
        acc_ref[g] = acc_ref[g] * alpha + pv

    def pages(last):
        scores = []
        for g in range(G):
            gc = slice(g * LANES, (g + 1) * LANES)
            kcat = jnp.concatenate([k_refs[i][gc, :].astype(BF16) for i in range(npp)], axis=1)
            far = bias_rows(g, 0)
            near = bias_rows(g, 1) if last else far
            scores.append(_dot(qs_ref[g], kcat) + jnp.concatenate([far] * (npp - 1) + [near], axis=1))
        pvs, alphas = [], []
        for g in range(G):
            s = scores[g]
            m_old = m_ref[g]
            m_new = jnp.maximum(m_old, jnp.max(s, axis=-1, keepdims=True))
            alpha = jnp.exp(m_old - m_new)
            p = jnp.exp(s - m_new)
            l_ref[g] = alpha * l_ref[g] + jnp.sum(p, axis=-1, keepdims=True)
            m_ref[g] = m_new
            vcat = jnp.concatenate([v_refs[i][pl.ds(g, PAGE_SIZE, stride=G), :].astype(BF16) for i in range(npp)],
                                   axis=0)
            pvs.append(_dot(p.astype(BF16), vcat))
            alphas.append(alpha)
        for g in range(G):
            acc_ref[g] = acc_ref[g] * alphas[g] + pvs[g]

    @pl.when(s_id < nsteps - 1)
    def _():
        pages(False)

    @pl.when(s_id == nsteps - 1)
    def _():
        pages(True)
        rowt = lax.broadcasted_iota(jnp.int32, (MR, LANES), 0) % TP
        colj = lax.broadcasted_iota(jnp.int32, (MR, LANES), 1)
        ok = jnp.logical_and(colj <= rowt, colj < nvalid)
        lam = _lambda(lamv_ref, lam_init)
        for g in range(G):
            sc = _nt_dot(qs_ref[g], knew_ref[g]) + bias_rows(g, 2)
            update(g, jnp.where(ok, sc, NEG_BIG), [vnew_ref[g]])
            acc = acc_ref[g] / l_ref[g]
            for r in range(R):
                o = acc[(0 * R + r) * TP:(0 * R + r + 1) * TP, :] - lam * acc[(1 * R + r) * TP:(1 * R + r + 1) * TP, :]
                o = o * lax.rsqrt(jnp.mean(o * o, axis=-1, keepdims=True) + NORM_EPS)
                hc = slice((g * R + r) * LANES, (g * R + r + 1) * LANES)
                y_ref[0, :, hc] = (o * sg_ref[...] * (1.0 - lam_init)).astype(BF16)


def _diff_attention_sample(proj, kf, vbuf, cache_k, cache_v, page_table, bias, p, layer, nvalid, lam_init):
    b, tp, _ = proj.shape
    vf = vbuf.reshape(vbuf.shape[0], b, tp * DIFF_KV_HEADS, DIFF_VD)
    n_pages = page_table.shape[1]
    npp = PAGES_PER_STEP
    while n_pages % npp:
        npp //= 2
    nsteps = n_pages // npp
    G = DIFF_KV_HEADS
    R = DIFF_HEADS // DIFF_KV_HEADS
    MR = 2 * R * tp
    ck = jnp.transpose(cache_k, (0, 1, 3, 4, 5, 2)).reshape(cache_k.shape[0], cache_k.shape[1], KV_COLS, PAGE_SIZE)
    cv = cache_v.reshape(cache_v.shape[0], cache_v.shape[1], PAGE_SIZE * DIFF_KV_HEADS, DIFF_VD)

    def page_spec(i):
        return pl.BlockSpec((None, None, KV_COLS, PAGE_SIZE),
                            lambda bi, s, pt: (layer, pt[bi * n_pages + s * npp + i], 0, 0))

    const = lambda shape: pl.BlockSpec(shape, lambda bi, s, pt: (0,) * len(shape))
    grid_spec = pltpu.PrefetchScalarGridSpec(
        num_scalar_prefetch=1,
        grid=(b, nsteps),
        in_specs=[page_spec(i) for i in range(npp)] + [page_spec(i) for i in range(npp)] + [
            pl.BlockSpec((1, tp, DIFF_HEADS * 2 * DIFF_HD), lambda bi, s, pt: (bi, 0, C_DQ // (DIFF_HEADS * 2 * DIFF_HD))),
            pl.BlockSpec((1, tp, KV_COLS), lambda bi, s, pt: (bi, 0, 0)),
            pl.BlockSpec((None, 1, tp * G, DIFF_VD), lambda bi, s, pt: (layer, bi, 0, 0)),
            const((DIFF_HEADS, tp, 3 * LANES)),
            const((1, LANES)), const((1, LANES)), const((4, LANES)), const((1, LANES)), const((LANES, LANES)),
        ],
        out_specs=[
            pl.BlockSpec((1, tp, DIFF_HEADS * DIFF_VD), lambda bi, s, pt: (bi, 0, 0)),
            pl.BlockSpec((1, tp, KV_COLS), lambda bi, s, pt: (bi, 0, 0)),
        ],
        scratch_shapes=[
            pltpu.VMEM((G, MR, LANES), BF16),
            pltpu.VMEM((G, PAGE_SIZE, LANES), BF16), pltpu.VMEM((G, PAGE_SIZE, LANES), BF16),
            pltpu.VMEM((G, MR, 1), F32), pltpu.VMEM((G, MR, 1), F32), pltpu.VMEM((G, MR, LANES), F32),
        ],
    )
    return pl.pallas_call(
        functools.partial(_sattn_kernel, layer=layer, nvalid=nvalid, lam_init=lam_init, npp=npp),
        grid_spec=grid_spec,
        out_shape=[
            jax.ShapeDtypeStruct((b, tp, DIFF_HEADS * DIFF_VD), BF16),
            jax.ShapeDtypeStruct((b, tp, KV_COLS), F32),
        ],
        compiler_params=_cparams("arbitrary", "arbitrary"),
        name="diff_attn_sample",
    )(page_table.reshape(-1), *([ck] * npp), *([cv] * npp), proj, kf, vf, bias,
      p["qk_norm_q"], p["qk_norm_k"], p["lamv"], p["diff_norm"], p["blockdiag"])


def _merge_kernel(ys_ref, yr_ref, yd_ref, g0_ref, g1_ref, g2_ref, x_ref, ws_ref, wr_ref, wd_ref, wo_ref,
                  bg_ref, h_ref):
    merged = None
    for i, (y_ref, w_ref, g_ref) in enumerate(((ys_ref, ws_ref, g0_ref), (yr_ref, wr_ref, g1_ref),
                                               (yd_ref, wd_ref, g2_ref))):
        br = _dot(y_ref[...], w_ref[...])
        t = _sigmoid(g_ref[...].astype(F32) + bg_ref[i:i + 1, :]) * br
        merged = t if merged is None else merged + t
    h_ref[...] = x_ref[...] + _dot(merged.astype(BF16), wo_ref[...])


def _merge(y_ssm, y_ret, y_diff, proj2d, x2d, p, tm):
    n = x2d.shape[0]
    tok = lambda cb: pl.BlockSpec((tm, D_MODEL), lambda i: (i, cb))
    wspec = pl.BlockSpec((D_MODEL, D_MODEL), lambda i: (0, 0))
    g0 = C_GATE // D_MODEL
    return pl.pallas_call(
        _merge_kernel,
        grid=(n // tm,),
        in_specs=[tok(0), tok(0), tok(0), tok(g0), tok(g0 + 1), tok(g0 + 2), tok(0),
                  wspec, wspec, wspec, wspec, pl.BlockSpec((N_BRANCHES, D_MODEL), lambda i: (0, 0))],
        out_specs=tok(0),
        out_shape=jax.ShapeDtypeStruct((n, D_MODEL), F32),
        compiler_params=_cparams("arbitrary"),
        name="merge",
    )(y_ssm, y_ret, y_diff, proj2d, proj2d, proj2d, x2d, p["w_ssm_out"], p["w_ret_out"], p["w_diff_out"],
      p["w_o"], p["b_gate"])


def _ffn_kernel(h_ref, g_ref, wg_ref, wu_ref, wd_ref, y_ref):
    h = h_ref[...]
    ms = jnp.mean(h * h, axis=-1, keepdims=True)
    hn = (h * lax.rsqrt(ms + NORM_EPS) * g_ref[...]).astype(BF16)
    act = _silu(_dot(hn, wg_ref[...])) * _dot(hn, wu_ref[...])
    y_ref[...] = h + _dot(act.astype(BF16), wd_ref[...])


def _ffn(h2d, p, tm):
    n = h2d.shape[0]
    resident = lambda shape, cb: pl.BlockSpec(shape, lambda i: (0, cb), pipeline_mode=pl.Buffered(1))
    return pl.pallas_call(
        _ffn_kernel,
        grid=(n // tm,),
        in_specs=[
            pl.BlockSpec((tm, D_MODEL), lambda i: (i, 0)),
            pl.BlockSpec((1, D_MODEL), lambda i: (0, 0)),
            resident((D_MODEL, D_FF), 0),
            resident((D_MODEL, D_FF), 1),
            resident((D_FF, D_MODEL), 0),
        ],
        out_specs=pl.BlockSpec((tm, D_MODEL), lambda i: (i, 0)),
        out_shape=jax.ShapeDtypeStruct((n, D_MODEL), F32),
        compiler_params=_cparams("arbitrary"),
        name="ffn",
    )(h2d, p["norm_ffn"], p["w_gate_up"], p["w_gate_up"], p["w_down"])


def _t5_bucket(dist):
    n = jnp.maximum(dist, 0)
    max_exact = N_BUCKETS // 2
    large = max_exact + (jnp.log(jnp.maximum(n, 1).astype(F32) / max_exact)
                         / math.log(MAX_DISTANCE / max_exact) * (N_BUCKETS - max_exact)).astype(jnp.int32)
    large = jnp.minimum(large, N_BUCKETS - 1)
    return jnp.where(n < max_exact, n, large)


def _far_bucket_is_constant(min_dist):
    max_exact = N_BUCKETS // 2
    d = np.float32(min_dist)
    large = max_exact + int(np.float32(np.log(d / np.float32(max_exact))) / np.float32(math.log(MAX_DISTANCE / max_exact))
                            * (N_BUCKETS - max_exact))
    return min_dist >= max_exact and large >= N_BUCKETS - 1


def _rope_tables(pos):
    half = RET_DK // 2
    inv = 1.0 / (ROPE_BASE ** (jnp.arange(half, dtype=F32) / half))
    ang = pos.astype(F32)[:, None] * inv[None, :]
    cos, sin = jnp.cos(ang), jnp.sin(ang)
    return jnp.concatenate([cos, cos], axis=1), jnp.concatenate([-sin, sin], axis=1)


def _layer_params(l, w_in, named):
    p = {k: v[l] for k, v in named.items()}
    w = w_in[l]
    o = (0,) + IN_OFFSETS + (w.shape[1],)
    z, xbc, dt, rq, rk, rv, rg, dq, dk, dv, gates = [w[:, o[i]:o[i + 1]] for i in range(len(IN_SPLITS))]
    out = {}
    out["w_main"] = jnp.concatenate([xbc, z, rq, rk, rv, rg, dq, gates], axis=1).astype(BF16)
    out["w_kvdt"] = jnp.concatenate([dk, dv, jnp.pad(dt, ((0, 0), (0, LANES - SSM_HEADS)))], axis=1).astype(BF16)
    out["norm_mix"] = p["norm_mix"].reshape(1, D_MODEL)
    out["conv_w"] = p["conv_w"]
    out["conv_b"] = p["conv_b"].reshape(1, CONV_DIM)
    out["dt_bias"] = jnp.pad(p["dt_bias"], (0, LANES - SSM_HEADS)).reshape(1, LANES)
    out["a_log"] = jnp.pad(p["a_log"], (0, LANES - SSM_HEADS)).reshape(1, LANES)
    out["d_skip"] = jnp.repeat(p["d_skip"], SSM_HEAD_DIM).reshape(1, SSM_D)
    out["ssm_norm"] = p["ssm_norm"].reshape(1, SSM_D)
    head_of_channel = np.arange(SSM_D) // SSM_HEAD_DIM
    out["head_expand"] = jnp.asarray(np.arange(LANES)[:, None] == head_of_channel[None, :], dtype=BF16)
    out["ret_norm"] = p["ret_norm"].reshape(1, RET_HEADS * RET_DV)
    out["qk_norm_q"] = jnp.tile(p["qk_norm_q"], 2).reshape(1, LANES)
    out["qk_norm_k"] = jnp.tile(p["qk_norm_k"], 2).reshape(1, LANES)
    lamv = jnp.stack([p["lambda_q1"], p["lambda_k1"], p["lambda_q2"], p["lambda_k2"]])
    out["lamv"] = jnp.pad(lamv, ((0, 0), (0, LANES - DIFF_HD)))
    out["diff_norm"] = p["diff_norm"].reshape(1, DIFF_VD)
    out["diff_norm_col"] = p["diff_norm"].reshape(DIFF_VD, 1)
    half = np.arange(LANES) // DIFF_HD
    out["blockdiag"] = jnp.asarray(half[:, None] == half[None, :], dtype=BF16)
    for k in ("w_ssm_out", "w_ret_out", "w_diff_out", "w_o", "w_gate_up", "w_down"):
        out[k] = p[k].astype(BF16)
    out["b_gate"] = p["b_gate"]
    out["norm_ffn"] = p["norm_ffn"].reshape(1, D_MODEL)
    return out


def _token_tile(n, cap):
    tm = min(n, cap)
    while n % tm:
        tm //= 2
    return tm


def _layer_common(x, p, layer, vbuf, tail0, h0, s0, cos_t, sin_t, nvalid, attn_fn):
    b, tp, _ = x.shape
    n = b * tp
    x2d = x.reshape(n, D_MODEL)
    proj, kf, vbuf, dt_raw = _inproj(x2d, p["norm_mix"], p["w_main"], p["w_kvdt"], _token_tile(n, 1024), vbuf, layer)
    proj3 = proj.reshape(b, tp, C_MAIN)
    dt3 = dt_raw.reshape(b, tp, LANES)
    tpad = -tp % CHUNK
    scan_in = lambda a: jnp.pad(a, ((0, 0), (0, tpad), (0, 0))) if tpad else a
    y_ssm, h_new, conv_rows, y_ret, s_new = _scans(scan_in(proj3), scan_in(dt3), tail0, h0, cos_t, sin_t, s0, p,
                                                   nvalid)
    y_diff, k_new = attn_fn(proj3, kf.reshape(b, tp, KV_COLS), vbuf)
    tm = _token_tile(n, 512)
    h = _merge(y_ssm[:, :tp].reshape(n, -1), y_ret[:, :tp].reshape(n, -1), y_diff.reshape(n, -1), proj, x2d, p, tm)
    y = _ffn(h, p, tm)
    last = (nvalid - 1) % CONV_TAIL
    assert last >= CONV_W - 2
    return (y.reshape(b, tp, D_MODEL), k_new, vbuf,
            h_new.reshape(b, SSM_HEADS, SSM_HEAD_DIM, SSM_STATE), conv_rows[:, last - (CONV_W - 2):last + 1, :], s_new)


def kernel(x_prompt, x_sample, cache_k, cache_v, page_table, state_ssm, state_conv, state_ret, rel_bias, norm_mix, w_in, b_gate, conv_w, conv_b, dt_bias, a_log, d_skip, ssm_norm, w_ssm_out, ret_norm, w_ret_out, qk_norm_q, qk_norm_k, lambda_q1, lambda_k1, lambda_q2, lambda_k2, diff_norm, w_diff_out, w_o, norm_ffn, w_gate_up, w_down):
    named = dict(norm_mix=norm_mix, b_gate=b_gate, conv_w=conv_w, conv_b=conv_b, dt_bias=dt_bias, a_log=a_log,
                 d_skip=d_skip, ssm_norm=ssm_norm, w_ssm_out=w_ssm_out, ret_norm=ret_norm, w_ret_out=w_ret_out,
                 qk_norm_q=qk_norm_q, qk_norm_k=qk_norm_k, lambda_q1=lambda_q1, lambda_k1=lambda_k1,
                 lambda_q2=lambda_q2, lambda_k2=lambda_k2, diff_norm=diff_norm, w_diff_out=w_diff_out, w_o=w_o,
                 norm_ffn=norm_ffn, w_gate_up=w_gate_up, w_down=w_down)
    depth = w_in.shape[0]
    bp, seq, _ = x_prompt.shape
    bs, dec, _ = x_sample.shape
    n_pages = page_table.shape[1]
    past = n_pages * PAGE_SIZE
    assert seq % CHUNK == 0 and CONV_W - 1 <= dec <= SUBLANES
    assert _far_bucket_is_constant(CHUNK + 1)

    ii = jnp.arange(CHUNK)[:, None]
    jj = jnp.arange(CHUNK)[None, :]
    idx_p = jnp.concatenate([_t5_bucket(k * CHUNK + jj - ii) for k in range(3)], axis=0)
    bias_h = _bias_tiles(rel_bias, idx_p, LOG2E).reshape(DIFF_KV_HEADS, DIFF_HEADS // DIFF_KV_HEADS, 3, CHUNK, CHUNK)
    bias_p = jnp.concatenate([bias_h[:, r] for r in range(DIFF_HEADS // DIFF_KV_HEADS)] * 2, axis=-1)
    tt = jnp.arange(SUBLANES)[:, None]
    idx_s = jnp.concatenate([_t5_bucket(jnp.broadcast_to(past + tt, (SUBLANES, LANES))),
                             _t5_bucket(tt + PAGE_SIZE - jj), _t5_bucket(tt - jj)], axis=1)
    bias_s = _bias_tiles(rel_bias, idx_s, 1.0)

    cos_p, sin_p = _rope_tables(jnp.arange(seq))
    cos_s, sin_s = _rope_tables(past + jnp.arange(CHUNK))

    xs = jnp.pad(x_sample, ((0, 0), (0, SUBLANES - dec), (0, 0)))
    zeros_tail = jnp.zeros((bp, CONV_TAIL, CONV_DIM), F32)
    zeros_h = jnp.zeros((bp, SSM_D, SSM_STATE), F32)
    zeros_s = jnp.zeros((bp, RET_HEADS, RET_DK, RET_DV), F32)

    vbuf_p = jnp.zeros((depth, bp * seq * DIFF_KV_HEADS, DIFF_VD), F32)
    vbuf_s = jnp.zeros((depth, bs * SUBLANES * DIFF_KV_HEADS, DIFF_VD), F32)
    kbuf_p = jnp.zeros((depth, bp, KV_COLS, seq), F32)

    yp, ys = x_prompt, xs
    outs_p, outs_s = [], []
    for l in range(depth):
        lam_init = 0.8 - 0.6 * math.exp(-0.3 * l)
        p = _layer_params(l, w_in, named)

        def attn_p(proj3, kf, vbuf, l=l, p=p, lam_init=lam_init, kbuf=kbuf_p):
            return _diff_attention_prompt(proj3, kf, vbuf, kbuf, bias_p, p, l, lam_init)

        yp, kbuf_p, vbuf_p, h1, c1, r1 = _layer_common(yp, p, l, vbuf_p, zeros_tail, zeros_h, zeros_s, cos_p, sin_p,
                                                       CHUNK, attn_p)
        outs_p.append((h1, c1, r1))

        tail_s = jnp.pad(state_conv[l], ((0, 0), (CONV_TAIL - (CONV_W - 1), 0), (0, 0)))
        h0_s = state_ssm[l].reshape(bs, SSM_D, SSM_STATE)

        def attn_s(proj3, kf, vbuf, l=l, p=p, lam_init=lam_init):
            return _diff_attention_sample(proj3, kf, vbuf, cache_k, cache_v, page_table, bias_s, p, l, dec, lam_init)

        ys_new, k2, vbuf_s, h2, c2, r2 = _layer_common(ys, p, l, vbuf_s, tail_s, h0_s, state_ret[l], cos_s, sin_s, dec,
                                                       attn_s)
        ys = jnp.where(jnp.arange(SUBLANES)[None, :, None] < dec, ys_new, 0.0)
        outs_s.append((k2[:, :dec].reshape(bs, dec, DIFF_KV_HEADS, 2, DIFF_HD), h2, c2, r2))

    stack = lambda outs, i: jnp.stack([o[i] for o in outs])
    k_prompt = kbuf_p.reshape(depth, bp, DIFF_KV_HEADS, 2, DIFF_HD, seq).transpose(0, 1, 5, 2, 3, 4)
    v_prompt = vbuf_p.reshape(depth, bp, seq, DIFF_KV_HEADS, DIFF_VD)
    v_sample = vbuf_s.reshape(depth, bs, SUBLANES, DIFF_KV_HEADS, DIFF_VD)[:, :, :dec]
    return (yp, ys[:, :dec],
            k_prompt, v_prompt, stack(outs_p, 0), stack(outs_p, 1), stack(outs_p, 2),
            stack(outs_s, 0), v_sample, stack(outs_s, 1), stack(outs_s, 2), stack(outs_s, 3))
```

```python
import functools
import math

import numpy as np
import jax
import jax.numpy as jnp
from jax import lax
from jax.experimental import pallas as pl
from jax.experimental.pallas import tpu as pltpu

F32 = jnp.float32
BF16 = jnp.bfloat16

D_MODEL = 1024
SSM_HEADS = 16
SSM_HEAD_DIM = 64
SSM_D = SSM_HEADS * SSM_HEAD_DIM
SSM_STATE = 128
SSM_GROUPS = 4
CONV_W = 4
CONV_DIM = SSM_D + 2 * SSM_GROUPS * SSM_STATE
RET_HEADS = 4
RET_DK = 128
RET_DV = 256
ROPE_BASE = 10000.0
DIFF_HEADS = 8
DIFF_KV_HEADS = 4
DIFF_HD = 64
DIFF_VD = 2 * DIFF_HD
N_BUCKETS = 32
MAX_DISTANCE = 128
N_BRANCHES = 3
D_FF = 2816
NORM_EPS = 1e-6
PAGE_SIZE = 128

CHUNK = 128
LANES = 128
SUBLANES = 8
GROUPS_PER_PASS = 4
CONV_TAIL = 16
NEG_BIG = -1e30
LOG2E = 1.4426950408889634
VMEM_LIMIT = 52 * 1024 * 1024

IN_SPLITS = (SSM_D, CONV_DIM, SSM_HEADS,
             RET_HEADS * RET_DK, RET_HEADS * RET_DK, RET_HEADS * RET_DV, RET_HEADS * RET_DV,
             DIFF_HEADS * 2 * DIFF_HD, DIFF_KV_HEADS * 2 * DIFF_HD, DIFF_KV_HEADS * DIFF_VD,
             N_BRANCHES * D_MODEL)
IN_OFFSETS = tuple(int(v) for v in np.cumsum(IN_SPLITS)[:-1])

C_XBC, C_Z, C_RQ, C_RK, C_RV, C_RG, C_DQ, C_GATE, C_MAIN = 0, 2048, 3072, 3584, 4096, 5120, 6144, 7168, 10240
TN_PROJ = 2048
KV_COLS = DIFF_KV_HEADS * 2 * DIFF_HD


def _cparams(*sem):
    return pltpu.CompilerParams(dimension_semantics=sem, vmem_limit_bytes=VMEM_LIMIT)


def _nt_dot(a, b):
    return lax.dot_general(a, b, (((1,), (1,)), ((), ())), preferred_element_type=F32)


def _tn_dot(a, b):
    return lax.dot_general(a, b, (((0,), (0,)), ((), ())), preferred_element_type=F32)


def _dot(a, b):
    return jnp.dot(a, b, preferred_element_type=F32)


def _split3(x):
    hi = x.astype(BF16)
    r1 = x - hi.astype(F32)
    mid = r1.astype(BF16)
    lo = (r1 - mid.astype(F32)).astype(BF16)
    return hi, mid, lo


def _sigmoid(x):
    return 0.5 * jnp.tanh(0.5 * x) + 0.5


def _silu(x):
    h = 0.5 * x
    return h + h * jnp.tanh(h)


def _inproj_kernel(x_ref, g_ref, w_ref, wkvdt_ref, vprev_ref, main_ref, kf_ref, vf_ref, dt_ref, xn_ref):
    del vprev_ref
    tm = x_ref.shape[0]

    @pl.when(pl.program_id(1) == 0)
    def _():
        x = x_ref[...]
        ms = jnp.mean(x * x, axis=-1, keepdims=True)
        xn = (x * lax.rsqrt(ms + NORM_EPS) * g_ref[...]).astype(BF16)
        xn_ref[...] = xn
        kvdt = _dot(xn, wkvdt_ref[...])
        kf_ref[...] = kvdt[:, :KV_COLS]
        for g in range(DIFF_KV_HEADS):
            vf_ref[pl.ds(g, tm, stride=DIFF_KV_HEADS), :] = kvdt[:, KV_COLS + g * DIFF_VD:KV_COLS + (g + 1) * DIFF_VD]
        dt_ref[...] = kvdt[:, 2 * KV_COLS:]

    main_ref[...] = _dot(xn_ref[...], w_ref[...]).astype(BF16)


def _inproj(x2d, gain, w_main, w_kvdt, tm, vbuf, layer):
    n = x2d.shape[0]
    grid = (n // tm, C_MAIN // TN_PROJ)
    return pl.pallas_call(
        _inproj_kernel,
        grid=grid,
        in_specs=[
            pl.BlockSpec((tm, D_MODEL), lambda i, j: (i, 0)),
            pl.BlockSpec((1, D_MODEL), lambda i, j: (0, 0)),
            pl.BlockSpec((D_MODEL, TN_PROJ), lambda i, j: (0, j)),
            pl.BlockSpec((D_MODEL, 2 * KV_COLS + LANES), lambda i, j: (0, 0)),
            pl.BlockSpec(memory_space=pl.ANY),
        ],
        out_specs=[
            pl.BlockSpec((tm, TN_PROJ), lambda i, j: (i, j)),
            pl.BlockSpec((tm, KV_COLS), lambda i, j: (i, 0)),
            pl.BlockSpec((None, tm * DIFF_KV_HEADS, DIFF_VD), lambda i, j: (layer, i, 0)),
            pl.BlockSpec((tm, LANES), lambda i, j: (i, 0)),
        ],
        out_shape=[
            jax.ShapeDtypeStruct((n, C_MAIN), BF16),
            jax.ShapeDtypeStruct((n, KV_COLS), F32),
            jax.ShapeDtypeStruct(vbuf.shape, F32),
            jax.ShapeDtypeStruct((n, LANES), F32),
        ],
        scratch_shapes=[pltpu.VMEM((tm, D_MODEL), BF16)],
        input_output_aliases={4: 2},
        compiler_params=_cparams("arbitrary", "arbitrary"),
        name="inproj",
    )(x2d, gain, w_main, w_kvdt, vbuf)


def _ssd_kernel(xbc_ref, z_ref, dt_ref, tail0_ref, h0_ref, cw_ref, cb_ref, dtb_ref, alog_ref, dskip_ref,
                gn_ref, ex_ref, y_ref, hout_ref, convout_ref, xext_ref, tlo_ref, xc_ref, ht_ref, *, c, nvalid, nchunks):
    L = CHUNK
    GW = SSM_D // SSM_GROUPS
    SL = 512

    @pl.when(c == 0)
    def _():
        t0 = tail0_ref[0]
        t0b = t0.astype(BF16)
        xext_ref[0:CONV_TAIL, :] = t0b
        tlo_ref[...] = (t0 - t0b.astype(F32)).astype(BF16)
        for k in range(SSM_D // LANES):
            ht_ref[:, k * LANES:(k + 1) * LANES] = h0_ref[0, k * LANES:(k + 1) * LANES, :].T

    xext_ref[CONV_TAIL:CONV_TAIL + L, :] = xbc_ref[0]
    srow = lax.broadcasted_iota(jnp.int32, (L, CONV_TAIL + L), 0)
    scol = lax.broadcasted_iota(jnp.int32, (L, CONV_TAIL + L), 1)
    shifts = [jnp.where(scol == srow + CONV_TAIL - s, 1.0, 0.0).astype(BF16) for s in range(1, CONV_W)]
    shifted = [[_dot(shifts[s - 1], xext_ref[:, sl * SL:(sl + 1) * SL]) for s in range(1, CONV_W)]
               for sl in range(CONV_DIM // SL)]
    for sl in range(CONV_DIM // SL):
        cols = slice(sl * SL, (sl + 1) * SL)
        conv = cw_ref[CONV_W - 1:CONV_W, cols] * xbc_ref[0, :, cols].astype(F32) + cb_ref[:, cols]
        for s in range(1, CONV_W):
            conv = conv + cw_ref[CONV_W - 1 - s:CONV_W - s, cols] * shifted[sl][s - 1]
        xc_ref[:, cols] = conv

    @pl.when(c == 0)
    def _():
        for sl in range(CONV_DIM // SL):
            cols = slice(sl * SL, (sl + 1) * SL)
            corr = None
            for s in range(1, CONV_W):
                t = cw_ref[CONV_W - 1 - s:CONV_W - s, cols] * _dot(shifts[s - 1][0:CONV_TAIL, 0:CONV_TAIL],
                                                                    tlo_ref[:, cols])
                corr = t if corr is None else corr + t
            xc_ref[0:CONV_TAIL, cols] += corr

    @pl.when(c == nchunks - 1)
    def _():
        a = CONV_TAIL * ((nvalid - 1) // CONV_TAIL)
        convout_ref[0] = xext_ref[CONV_TAIL + a:CONV_TAIL + a + CONV_TAIL, :].astype(F32)

    xext_ref[0:CONV_TAIL, :] = xext_ref[L:L + CONV_TAIL, :]

    row = lax.broadcasted_iota(jnp.int32, (L, L), 0)
    col = lax.broadcasted_iota(jnp.int32, (L, L), 1)
    causal = row >= col
    left = col < SSM_HEAD_DIM

    x = dt_ref[0] + dtb_ref[...]
    dt = jnp.maximum(x, 0.0) + jnp.log1p(jnp.exp(-jnp.abs(x)))
    if nvalid < L:
        dt = jnp.where(row < nvalid, dt, 0.0)
    a = -jnp.exp(alog_ref[...])
    da = dt * a
    tri = jnp.where(causal, 1.0, 0.0).astype(BF16)
    cs = sum(_dot(tri, p) for p in _split3(da))
    cs_t = cs.T
    cs_parts = _split3(cs)
    dt_parts = _split3(dt)

    NG = SSM_GROUPS
    HPG = SSM_HEADS // SSM_GROUPS
    gcs = [slice(g * GW, (g + 1) * GW) for g in range(NG)]
    bcol = lambda g: slice(SSM_D + g * SSM_STATE, SSM_D + (g + 1) * SSM_STATE)
    ccol = lambda g: slice(SSM_D + NG * SSM_STATE + g * SSM_STATE, SSM_D + NG * SSM_STATE + (g + 1) * SSM_STATE)

    def scan_groups(gs):
        csx = {g: sum(_dot(p, ex_ref[:, gcs[g]]) for p in cs_parts) for g in gs}
        dtx = {g: sum(_dot(p, ex_ref[:, gcs[g]]) for p in dt_parts) for g in gs}
        bgs = {g: _silu(xc_ref[:, bcol(g)]).astype(BF16) for g in gs}
        cgs = {g: _silu(xc_ref[:, ccol(g)]).astype(BF16) for g in gs}
        cbs = {g: _nt_dot(cgs[g], bgs[g]) for g in gs}
        htgs = {g: ht_ref[:, gcs[g]] for g in gs}
        ysts = {g: _dot(cgs[g], htgs[g].astype(BF16)) for g in gs}
        xss = {g: _silu(xc_ref[:, gcs[g]]) for g in gs}
        xdts = {g: xss[g] * dtx[g] for g in gs}
        xdt_bs = {g: xdts[g].astype(BF16) for g in gs}
        lastxs = {g: csx[g][L - 1:L, :] for g in gs}
        xdtw_bs = {g: (xdts[g] * jnp.exp(lastxs[g] - csx[g])).astype(BF16) for g in gs}
        prods = {}
        for g in gs:
            for e in range(HPG):
                h = g * HPG + e
                seg = cs[:, h:h + 1] - cs_t[h:h + 1, :]
                decay = jnp.exp(jnp.where(causal, seg, NEG_BIG))
                mat = (cbs[g] * decay).astype(BF16)
                xp = xdt_bs[g][:, (e // 2) * LANES:(e // 2 + 1) * LANES]
                keep = left if e % 2 == 0 else jnp.logical_not(left)
                prods[h] = _dot(mat, jnp.where(keep, xp, jnp.zeros_like(xp)))
        upd = {g: _tn_dot(bgs[g], xdtw_bs[g]) for g in gs}
        for g in gs:
            ht_ref[:, gcs[g]] = htgs[g] * jnp.exp(lastxs[g]) + upd[g]
        for g in gs:
            gc = gcs[g]
            y_in = jnp.concatenate([prods[g * HPG + 2 * pr] + prods[g * HPG + 2 * pr + 1] for pr in range(HPG // 2)],
                                   axis=1)
            y = y_in + ysts[g] * jnp.exp(csx[g]) + dskip_ref[:, gc] * xss[g]
            y = y * _silu(z_ref[0, :, gc].astype(F32))
            ms = jnp.mean(y * y, axis=-1, keepdims=True)
            y_ref[0, :, gc] = (y * lax.rsqrt(ms + NORM_EPS) * gn_ref[:, gc]).astype(BF16)

    for first in range(0, NG, GROUPS_PER_PASS):
        scan_groups(range(first, first + GROUPS_PER_PASS))

    @pl.when(c == nchunks - 1)
    def _():
        for k in range(SSM_D // LANES):
            hout_ref[0, k * LANES:(k + 1) * LANES, :] = ht_ref[:, k * LANES:(k + 1) * LANES].T


def _ret_kernel(q_ref, k_ref, v_ref, rg_ref, cos_ref, sin_ref, s0_ref, gn_ref, y_ref, sout_ref, *, c, ltrue):
    L = CHUNK

    @pl.when(c == 0)
    def _():
        sout_ref[...] = s0_ref[...]

    row = lax.broadcasted_iota(jnp.int32, (L, L), 0)
    col = lax.broadcasted_iota(jnp.int32, (L, L), 1)
    rel = (row - col).astype(F32)
    idx = row[:, 0:1].astype(F32)
    cosf = cos_ref[...]
    sins = sin_ref[...]
    H = RET_HEADS
    lgs = [math.log(1.0 - 2.0 ** (-5.0 - h)) for h in range(H)]
    kcs = [slice(h * RET_DK, (h + 1) * RET_DK) for h in range(H)]
    vcs = [slice(h * RET_DV, (h + 1) * RET_DV) for h in range(H)]
    qrs, krs = [], []
    for h in range(H):
        qh = q_ref[0, :, kcs[h]].astype(F32)
        kh = k_ref[0, :, kcs[h]].astype(F32)
        qrs.append(qh * cosf + pltpu.roll(qh, RET_DK // 2, 1) * sins)
        krs.append((kh * cosf + pltpu.roll(kh, RET_DK // 2, 1) * sins) * RET_DK ** -0.5)
    qr_bs = [q_.astype(BF16) for q_ in qrs]
    vhs = [v_ref[0, :, vcs[h]] for h in range(H)]
    s_olds = [sout_ref[0, h] for h in range(H)]
    scores = [_nt_dot(qr_bs[h], krs[h].astype(BF16)) for h in range(H)]
    cross = [_dot(qr_bs[h], s_olds[h].astype(BF16)) for h in range(H)]
    atts = []
    for h in range(H):
        dmat = jnp.where(rel >= 0, jnp.exp(jnp.maximum(rel, 0.0) * lgs[h]), 0.0)
        atts.append((scores[h] * dmat).astype(BF16))
    inner = [_dot(atts[h], vhs[h]) for h in range(H)]
    for h in range(H):
        k_dec = jnp.exp((ltrue - 1.0 - idx) * lgs[h])
        sout_ref[0, h] = s_olds[h] * math.exp(ltrue * lgs[h]) + _tn_dot((krs[h] * k_dec).astype(BF16), vhs[h])
    for h in range(H):
        o = inner[h] + cross[h] * jnp.exp((idx + 1.0) * lgs[h])
        oc = o - jnp.mean(o, axis=-1, keepdims=True)
        on = oc * lax.rsqrt(jnp.mean(oc * oc, axis=-1, keepdims=True) + NORM_EPS)
        y_ref[0, :, vcs[h]] = (on * gn_ref[:, vcs[h]] * _silu(rg_ref[0, :, vcs[h]].astype(F32))).astype(BF16)


N_SSD_IN, N_RET_IN, N_SSD_OUT, N_RET_OUT = 12, 8, 3, 2


CHUNKS_PER_STEP = 4


def _scan_kernel(*refs, nvalid, nchunks, per_step):
    i0 = N_SSD_IN
    i1 = i0 + N_RET_IN
    i2 = i1 + N_SSD_OUT
    i3 = i2 + N_RET_OUT
    ssd_in, ret_in, ssd_out, ret_out = refs[0:i0], refs[i0:i1], refs[i1:i2], refs[i2:i3]

    def chunk(sc, carry):
        c = pl.program_id(1) * per_step + sc
        rows = pl.ds(pl.multiple_of(sc * CHUNK, CHUNK), CHUNK)
        seq = lambda r: r.at[:, rows, :]
        tab = lambda r: r.at[rows, :]
        _ret_kernel(*[seq(r) for r in ret_in[0:4]], tab(ret_in[4]), tab(ret_in[5]), *ret_in[6:],
                    seq(ret_out[0]), ret_out[1], c=c, ltrue=float(nvalid))
        _ssd_kernel(*[seq(r) for r in ssd_in[0:3]], *ssd_in[3:], seq(ssd_out[0]), *ssd_out[1:], *refs[i3:],
                    c=c, nvalid=nvalid, nchunks=nchunks)
        return carry

    lax.fori_loop(0, per_step, chunk, 0)


def _scans(proj, dt_raw, tail0, h0, cos_t, sin_t, s0, p, nvalid):
    b, t, _ = proj.shape
    nchunks = t // CHUNK
    per_step = math.gcd(nchunks, CHUNKS_PER_STEP)
    L = CHUNK * per_step
    qk_w = RET_HEADS * RET_DK
    v_w = RET_HEADS * RET_DV
    const = lambda shape: pl.BlockSpec(shape, lambda i, c: (0,) * len(shape))
    ssd_in = [
        pl.BlockSpec((1, L, CONV_DIM), lambda i, c: (i, c, C_XBC // CONV_DIM)),
        pl.BlockSpec((1, L, SSM_D), lambda i, c: (i, c, C_Z // SSM_D)),
        pl.BlockSpec((1, L, LANES), lambda i, c: (i, c, 0)),
        pl.BlockSpec((1, CONV_TAIL, CONV_DIM), lambda i, c: (i, 0, 0)),
        pl.BlockSpec((1, SSM_D, SSM_STATE), lambda i, c: (i, 0, 0)),
        const((CONV_W, CONV_DIM)), const((1, CONV_DIM)), const((1, LANES)), const((1, LANES)),
        const((1, SSM_D)), const((1, SSM_D)), const((LANES, SSM_D)),
    ]
    ret_in = [
        pl.BlockSpec((1, L, qk_w), lambda i, c: (i, c, C_RQ // qk_w)),
        pl.BlockSpec((1, L, qk_w), lambda i, c: (i, c, C_RK // qk_w)),
        pl.BlockSpec((1, L, v_w), lambda i, c: (i, c, C_RV // v_w)),
        pl.BlockSpec((1, L, v_w), lambda i, c: (i, c, C_RG // v_w)),
        pl.BlockSpec((L, RET_DK), lambda i, c: (c, 0)),
        pl.BlockSpec((L, RET_DK), lambda i, c: (c, 0)),
        pl.BlockSpec((1, RET_HEADS, RET_DK, RET_DV), lambda i, c: (i, 0, 0, 0)),
        const((1, v_w)),
    ]
    ssd_out = [
        pl.BlockSpec((1, L, SSM_D), lambda i, c: (i, c, 0)),
        pl.BlockSpec((1, SSM_D, SSM_STATE), lambda i, c: (i, 0, 0)),
        pl.BlockSpec((1, CONV_TAIL, CONV_DIM), lambda i, c: (i, 0, 0)),
    ]
    ret_out = [
        pl.BlockSpec((1, L, v_w), lambda i, c: (i, c, 0)),
        pl.BlockSpec((1, RET_HEADS, RET_DK, RET_DV), lambda i, c: (i, 0, 0, 0)),
    ]
    assert (len(ssd_in), len(ret_in), len(ssd_out), len(ret_out)) == (N_SSD_IN, N_RET_IN, N_SSD_OUT, N_RET_OUT)
    return pl.pallas_call(
        functools.partial(_scan_kernel, nvalid=nvalid, nchunks=nchunks, per_step=per_step),
        grid=(b, nchunks // per_step),
        in_specs=ssd_in + ret_in,
        out_specs=ssd_out + ret_out,
        out_shape=[
            jax.ShapeDtypeStruct((b, t, SSM_D), BF16),
            jax.ShapeDtypeStruct((b, SSM_D, SSM_STATE), F32),
            jax.ShapeDtypeStruct((b, CONV_TAIL, CONV_DIM), F32),
            jax.ShapeDtypeStruct((b, t, v_w), BF16),
            jax.ShapeDtypeStruct((b, RET_HEADS, RET_DK, RET_DV), F32),
        ],
        scratch_shapes=[
            pltpu.VMEM((CONV_TAIL + CHUNK, CONV_DIM), BF16),
            pltpu.VMEM((CONV_TAIL, CONV_DIM), BF16),
            pltpu.VMEM((CHUNK, CONV_DIM), F32),
            pltpu.VMEM((SSM_STATE, SSM_D), F32),
        ],
        compiler_params=_cparams("arbitrary", "arbitrary"),
        name="scans",
    )(proj, proj, dt_raw, tail0, h0, p["conv_w"], p["conv_b"], p["dt_bias"], p["a_log"], p["d_skip"],
      p["ssm_norm"], p["head_expand"], proj, proj, proj, proj, cos_t, sin_t, s0, p["ret_norm"])


def _bias_kernel(tab_ref, idx_ref, out_ref, *, scale):
    h = pl.program_id(0)
    idx = idx_ref[...]
    acc = jnp.zeros(idx.shape, F32)
    for b in range(N_BUCKETS):
        acc = acc + jnp.where(idx == b, tab_ref[b * DIFF_HEADS + h], 0.0)
    out_ref[0] = acc * scale


def _bias_tiles(rel_bias, idx, scale):
    r, c = idx.shape
    return pl.pallas_call(
        functools.partial(_bias_kernel, scale=scale),
        grid=(DIFF_HEADS,),
        in_specs=[pl.BlockSpec(memory_space=pltpu.SMEM), pl.BlockSpec((r, c), lambda h: (0, 0))],
        out_specs=pl.BlockSpec((1, r, c), lambda h: (h, 0, 0)),
        out_shape=jax.ShapeDtypeStruct((DIFF_HEADS, r, c), F32),
        compiler_params=_cparams("arbitrary"),
        name="t5_bias",
    )(rel_bias.reshape(-1), idx)


def _half_rmsnorm(x, gain, bd):
    x2 = x * x
    hi = x2.astype(BF16)
    lo = (x2 - hi.astype(F32)).astype(BF16)
    ss = _dot(hi, bd) + _dot(lo, bd)
    return x * lax.rsqrt(ss * (1.0 / DIFF_HD) + NORM_EPS) * gain


def _half_rmsnorm_blocks(blocks, gain, bd):
    rows = blocks[0].shape[0]
    y = _half_rmsnorm(jnp.concatenate(blocks, axis=0), gain, bd)
    return [y[i * rows:(i + 1) * rows] for i in range(len(blocks))]


def _lambda(lamv_ref, lam_init):
    s1 = jnp.sum(lamv_ref[0:1, :] * lamv_ref[1:2, :], axis=-1, keepdims=True)
    s2 = jnp.sum(lamv_ref[2:3, :] * lamv_ref[3:4, :], axis=-1, keepdims=True)
    return jnp.exp(s1) - jnp.exp(s2) + lam_init


FAR_UNITS = 4
Q_TILES_PER_STEP = 4


def _dattn_kernel(q_ref, kf_ref, vf_ref, bias_ref, gq_ref, gk_ref, lamv_ref, sg_ref, bd_ref, kprev_ref,
                  y_ref, kout_ref, knt_ref, vx_ref, qs_ref, m_ref, acc_ref, *, t, q_tiles, lam_init):
    del kprev_ref
    step_id = pl.program_id(1)
    TQ = CHUNK
    G = DIFF_KV_HEADS
    R = DIFF_HEADS // DIFF_KV_HEADS
    NC = 2 * R * TQ
    bd = bd_ref[...]

    @pl.when(step_id == 0)
    def _():
        for i in range(t // TQ):
            r = slice(i * TQ, (i + 1) * TQ)
            kns = _half_rmsnorm_blocks([kf_ref[0, r, g * LANES:(g + 1) * LANES] for g in range(G)], gk_ref[...], bd)
            for g in range(G):
                knt = kns[g].T
                kout_ref[0, g * LANES:(g + 1) * LANES, r] = knt
                knt_ref[g, i] = knt.astype(BF16)
                vx_ref[g, r, 0:DIFF_VD] = vf_ref[0, pl.ds(i * TQ * G + g, TQ, stride=G), :].astype(BF16)
                vx_ref[g, r, DIFF_VD:] = jnp.ones((TQ, LANES), BF16)

    def tile(ti, carry):
        qi = step_id * q_tiles + ti
        qrows = pl.ds(pl.multiple_of(ti * TQ, TQ), TQ)
        lane = lax.broadcasted_iota(jnp.int32, (TQ, LANES), 1)
        left = lane < DIFF_HD
        qns = _half_rmsnorm_blocks([q_ref[0, qrows, h * LANES:(h + 1) * LANES].astype(F32)
                                    for h in range(DIFF_HEADS)], gq_ref[...], bd)
        for g in range(G):
            for r in range(R):
                qn = qns[g * R + r] * (DIFF_HD ** -0.5 * LOG2E)
                qs_ref[g, (0 * R + r) * TQ:(0 * R + r + 1) * TQ, :] = jnp.where(left, qn, 0.0).astype(BF16)
                qs_ref[g, (1 * R + r) * TQ:(1 * R + r + 1) * TQ, :] = jnp.where(left, 0.0, qn).astype(BF16)
        m_ref[...] = jnp.full(m_ref.shape, NEG_BIG, F32)
        acc_ref[...] = jnp.zeros(acc_ref.shape, F32)

        def step(k0, tiles):
            nunits = len(tiles)
            rows = pl.ds(pl.multiple_of(k0 * TQ, TQ), nunits * TQ)
            kts = [knt_ref[g, k0] if nunits == 1 else
                   jnp.concatenate([knt_ref[g, k0 + u] for u in range(nunits)], axis=1) for g in range(G)]
            ss = [_dot(qs_ref[g], kts[g]) for g in range(G)]
            ps, alphas = [], []
            for g in range(G):
                s = ss[g] + jnp.concatenate([bias_ref[g, tl] for tl in tiles], axis=1)
                if 0 in tiles:
                    first = tiles.index(0) * TQ
                    key = lax.broadcasted_iota(jnp.int32, (NC, nunits * TQ), 1) - first
                    qry = lax.broadcasted_iota(jnp.int32, (NC, nunits * TQ), 0) & (TQ - 1)
                    s = jnp.where(key <= qry, s, NEG_BIG)
                m_old = m_ref[g]
                m_new = jnp.maximum(m_old, jnp.max(s, axis=1, keepdims=True))
                alpha = jnp.exp2(m_old - m_new)
                p = jnp.concatenate([jnp.exp2(s[:, u * TQ:(u + 1) * TQ] - m_new) for u in range(nunits)], axis=1)
                m_ref[g] = m_new
                ps.append(p.astype(BF16))
                alphas.append(alpha)
            for g in range(G):
                scale = jnp.concatenate([alphas[g], alphas[g]], axis=1)
                acc_ref[g] = acc_ref[g] * scale + _dot(ps[g], vx_ref[g, rows, :])

        nfar = jnp.maximum(qi - 1, 0)
        nbig = nfar // FAR_UNITS
        rem = nfar - nbig * FAR_UNITS

        def far_big(i, c):
            step(i * FAR_UNITS, (2,) * FAR_UNITS)
            return c

        lax.fori_loop(0, nbig, far_big, 0)

        @pl.when(rem >= 2)
        def _():
            step(nbig * FAR_UNITS, (2, 2))

        @pl.when((rem & 1) == 1)
        def _():
            step(nbig * FAR_UNITS + (rem & 2), (2,))

        @pl.when(qi >= 1)
        def _():
            step(qi - 1, (1, 0))

        @pl.when(qi == 0)
        def _():
            step(0, (0,))

        lam = _lambda(lamv_ref, lam_init)
        for g in range(G):
            for r in range(R):
                r0 = slice((0 * R + r) * TQ, (0 * R + r + 1) * TQ)
                r1 = slice((1 * R + r) * TQ, (1 * R + r + 1) * TQ)
                o = (acc_ref[g, r0, 0:DIFF_VD] / acc_ref[g, r0, DIFF_VD:]
                     - lam * (acc_ref[g, r1, 0:DIFF_VD] / acc_ref[g, r1, DIFF_VD:]))
                o = o * lax.rsqrt(jnp.mean(o * o, axis=1, keepdims=True) + NORM_EPS)
                o = o * sg_ref[...] * (1.0 - lam_init)
                y_ref[0, qrows, (g * R + r) * LANES:(g * R + r + 1) * LANES] = o.astype(BF16)
        return carry

    lax.fori_loop(0, q_tiles, tile, 0)


def _diff_attention_prompt(proj, kf, vbuf, kbuf, bias, p, layer, lam_init):
    b, t, _ = proj.shape
    vf = vbuf.reshape(vbuf.shape[0], b, t * DIFF_KV_HEADS, DIFF_VD)
    TQ = CHUNK
    G = DIFF_KV_HEADS
    R = DIFF_HEADS // DIFF_KV_HEADS
    NC = 2 * R * TQ
    qw = DIFF_HEADS * 2 * DIFF_HD
    const = lambda shape: pl.BlockSpec(shape, lambda i, q: (0,) * len(shape))
    q_tiles = math.gcd(t // TQ, Q_TILES_PER_STEP)
    return pl.pallas_call(
        functools.partial(_dattn_kernel, t=t, q_tiles=q_tiles, lam_init=lam_init),
        grid=(b, t // (TQ * q_tiles)),
        in_specs=[
            pl.BlockSpec((1, TQ * q_tiles, qw), lambda i, q: (i, q, C_DQ // qw)),
            pl.BlockSpec((1, t, KV_COLS), lambda i, q: (i, 0, 0)),
            pl.BlockSpec((None, 1, t * G, DIFF_VD), lambda i, q: (layer, i, 0, 0)),
            const((G, 3, NC, TQ)),
            const((1, LANES)), const((1, LANES)), const((4, LANES)), const((1, DIFF_VD)), const((LANES, LANES)),
            pl.BlockSpec(memory_space=pl.ANY),
        ],
        out_specs=[
            pl.BlockSpec((1, TQ * q_tiles, qw), lambda i, q: (i, q, 0)),
            pl.BlockSpec((None, 1, KV_COLS, t), lambda i, q: (layer, i, 0, 0)),
        ],
        out_shape=[
            jax.ShapeDtypeStruct((b, t, DIFF_HEADS * DIFF_VD), BF16),
            jax.ShapeDtypeStruct(kbuf.shape, F32),
        ],
        input_output_aliases={9: 1},
        scratch_shapes=[
            pltpu.VMEM((G, t // TQ, LANES, TQ), BF16), pltpu.VMEM((G, t, DIFF_VD + LANES), BF16),
            pltpu.VMEM((G, NC, LANES), BF16),
            pltpu.VMEM((G, NC, LANES), F32),
            pltpu.VMEM((G, NC, DIFF_VD + LANES), F32),
        ],
        compiler_params=_cparams("arbitrary", "arbitrary"),
        name="diff_attn_prompt",
    )(proj, kf, vf, bias, p["qk_norm_q"], p["qk_norm_k"], p["lamv"], p["diff_norm"], p["blockdiag"], kbuf)


PAGES_PER_STEP = 32


def _sattn_kernel(pt_ref, *refs, layer, nvalid, lam_init, npp):
    del pt_ref, layer
    k_refs = refs[0:npp]
    v_refs = refs[npp:2 * npp]
    (q_ref, kf_ref, vf_ref, bias_ref, gq_ref, gk_ref, lamv_ref, sg_ref, bd_ref,
     y_ref, kout_ref, qs_ref, knew_ref, vnew_ref, m_ref, l_ref, acc_ref) = refs[2 * npp:]
    s_id = pl.program_id(1)
    nsteps = pl.num_programs(1)
    G = DIFF_KV_HEADS
    R = DIFF_HEADS // DIFF_KV_HEADS
    TP = SUBLANES
    MR = 2 * R * TP
    bd = bd_ref[...]
    lane = lax.broadcasted_iota(jnp.int32, (TP, LANES), 1)
    left = lane < DIFF_HD

    @pl.when(s_id == 0)
    def _():
        knew_ref[...] = jnp.zeros(knew_ref.shape, BF16)
        vnew_ref[...] = jnp.zeros(vnew_ref.shape, BF16)
        for g in range(G):
            gc = slice(g * LANES, (g + 1) * LANES)
            kn = _half_rmsnorm(kf_ref[0, :, gc], gk_ref[...], bd)
            kout_ref[0, :, gc] = kn
            knew_ref[g, 0:TP, :] = kn.astype(BF16)
            vnew_ref[g, 0:TP, :] = vf_ref[0, pl.ds(g, TP, stride=G), :].astype(BF16)
            for r in range(R):
                hc = slice((g * R + r) * LANES, (g * R + r + 1) * LANES)
                qn = _half_rmsnorm(q_ref[0, :, hc].astype(F32), gq_ref[...], bd) * DIFF_HD ** -0.5
                qs_ref[g, (0 * R + r) * TP:(0 * R + r + 1) * TP, :] = jnp.where(left, qn, 0.0).astype(BF16)
                qs_ref[g, (1 * R + r) * TP:(1 * R + r + 1) * TP, :] = jnp.where(left, 0.0, qn).astype(BF16)
        m_ref[...] = jnp.full(m_ref.shape, NEG_BIG, F32)
        l_ref[...] = jnp.zeros(l_ref.shape, F32)
        acc_ref[...] = jnp.zeros(acc_ref.shape, F32)

    def bias_rows(g, seg):
        per_head = [bias_ref[g * R + r, :, seg * LANES:(seg + 1) * LANES] for r in range(R)]
        return jnp.concatenate(per_head + per_head, axis=0)

    def update(g, s, vs):
        m_old = m_ref[g]
        m_new = jnp.maximum(m_old, jnp.max(s, axis=-1, keepdims=True))
        alpha = jnp.exp(m_old - m_new)
        p = jnp.exp(s - m_new)
        l_ref[g] = alpha * l_ref[g] + jnp.sum(p, axis=-1, keepdims=True)
        m_ref[g] = m_new
        pb = p.astype(BF16)
        pv = _dot(pb[:, 0:LANES], vs[0])
        for i in range(1, len(vs)):
            pv = pv + _dot(pb[:, i * LANES:(i + 1) * LANES], vs[i])
        acc_ref[g] = acc_ref[g] * alpha + pv

    def pages(last):
        scores = []
        for g in range(G):
            gc = slice(g * LANES, (g + 1) * LANES)
            kcat = jnp.concatenate([k_refs[i][gc, :].astype(BF16) for i in range(npp)], axis=1)
            far = bias_rows(g, 0)
            near = bias_rows(g, 1) if last else far
            scores.append(_dot(qs_ref[g], kcat) + jnp.concatenate([far] * (npp - 1) + [near], axis=1))
        pvs, alphas = [], []
        for g in range(G):
            s = scores[g]
            m_old = m_ref[g]
            m_new = jnp.maximum(m_old, jnp.max(s, axis=-1, keepdims=True))
            alpha = jnp.exp(m_old - m_new)
            p = jnp.exp(s - m_new)
            l_ref[g] = alpha * l_ref[g] + jnp.sum(p, axis=-1, keepdims=True)
            m_ref[g] = m_new
            vcat = jnp.concatenate([v_refs[i][pl.ds(g, PAGE_SIZE, stride=G), :].astype(BF16) for i in range(npp)],
                                   axis=0)
            pvs.append(_dot(p.astype(BF16), vcat))
            alphas.append(alpha)
        for g in range(G):
            acc_ref[g] = acc_ref[g] * alphas[g] + pvs[g]

    @pl.when(s_id < nsteps - 1)
    def _():
        pages(False)

    @pl.when(s_id == nsteps - 1)
    def _():
        pages(True)
        rowt = lax.broadcasted_iota(jnp.int32, (MR, LANES), 0) % TP
        colj = lax.broadcasted_iota(jnp.int32, (MR, LANES), 1)
        ok = jnp.logical_and(colj <= rowt, colj < nvalid)
        lam = _lambda(lamv_ref, lam_init)
        for g in range(G):
            sc = _nt_dot(qs_ref[g], knew_ref[g]) + bias_rows(g, 2)
            update(g, jnp.where(ok, sc, NEG_BIG), [vnew_ref[g]])
            acc = acc_ref[g] / l_ref[g]
            for r in range(R):
                o = acc[(0 * R + r) * TP:(0 * R + r + 1) * TP, :] - lam * acc[(1 * R + r) * TP:(1 * R + r + 1) * TP, :]
                o = o * lax.rsqrt(jnp.mean(o * o, axis=-1, keepdims=True) + NORM_EPS)
                hc = slice((g * R + r) * LANES, (g * R + r + 1) * LANES)
                y_ref[0, :, hc] = (o * sg_ref[...] * (1.0 - lam_init)).astype(BF16)


def _diff_attention_sample(proj, kf, vbuf, cache_k, cache_v, page_table, bias, p, layer, nvalid, lam_init):
    b, tp, _ = proj.shape
    vf = vbuf.reshape(vbuf.shape[0], b, tp * DIFF_KV_HEADS, DIFF_VD)
    n_pages = page_table.shape[1]
    npp = PAGES_PER_STEP
    while n_pages % npp:
        npp //= 2
    nsteps = n_pages // npp
    G = DIFF_KV_HEADS
    R = DIFF_HEADS // DIFF_KV_HEADS
    MR = 2 * R * tp
    ck = jnp.transpose(cache_k, (0, 1, 3, 4, 5, 2)).reshape(cache_k.shape[0], cache_k.shape[1], KV_COLS, PAGE_SIZE)
    cv = cache_v.reshape(cache_v.shape[0], cache_v.shape[1], PAGE_SIZE * DIFF_KV_HEADS, DIFF_VD)

    def page_spec(i):
        return pl.BlockSpec((None, None, KV_COLS, PAGE_SIZE),
                            lambda bi, s, pt: (layer, pt[bi * n_pages + s * npp + i], 0, 0))

    const = lambda shape: pl.BlockSpec(shape, lambda bi, s, pt: (0,) * len(shape))
    grid_spec = pltpu.PrefetchScalarGridSpec(
        num_scalar_prefetch=1,
        grid=(b, nsteps),
        in_specs=[page_spec(i) for i in range(npp)] + [page_spec(i) for i in range(npp)] + [
            pl.BlockSpec((1, tp, DIFF_HEADS * 2 * DIFF_HD), lambda bi, s, pt: (bi, 0, C_DQ // (DIFF_HEADS * 2 * DIFF_HD))),
            pl.BlockSpec((1, tp, KV_COLS), lambda bi, s, pt: (bi, 0, 0)),
            pl.BlockSpec((None, 1, tp * G, DIFF_VD), lambda bi, s, pt: (layer, bi, 0, 0)),
            const((DIFF_HEADS, tp, 3 * LANES)),
            const((1, LANES)), const((1, LANES)), const((4, LANES)), const((1, LANES)), const((LANES, LANES)),
        ],
        out_specs=[
            pl.BlockSpec((1, tp, DIFF_HEADS * DIFF_VD), lambda bi, s, pt: (bi, 0, 0)),
            pl.BlockSpec((1, tp, KV_COLS), lambda bi, s, pt: (bi, 0, 0)),
        ],
        scratch_shapes=[
            pltpu.VMEM((G, MR, LANES), BF16),
            pltpu.VMEM((G, PAGE_SIZE, LANES), BF16), pltpu.VMEM((G, PAGE_SIZE, LANES), BF16),
            pltpu.VMEM((G, MR, 1), F32), pltpu.VMEM((G, MR, 1), F32), pltpu.VMEM((G, MR, LANES), F32),
        ],
    )
    return pl.pallas_call(
        functools.partial(_sattn_kernel, layer=layer, nvalid=nvalid, lam_init=lam_init, npp=npp),
        grid_spec=grid_spec,
        out_shape=[
            jax.ShapeDtypeStruct((b, tp, DIFF_HEADS * DIFF_VD), BF16),
            jax.ShapeDtypeStruct((b, tp, KV_COLS), F32),
        ],
        compiler_params=_cparams("arbitrary", "arbitrary"),
        name="diff_attn_sample",
    )(page_table.reshape(-1), *([ck] * npp), *([cv] * npp), proj, kf, vf, bias,
      p["qk_norm_q"], p["qk_norm_k"], p["lamv"], p["diff_norm"], p["blockdiag"])


def _merge_kernel(ys_ref, yr_ref, yd_ref, g0_ref, g1_ref, g2_ref, x_ref, ws_ref, wr_ref, wd_ref, wo_ref,
                  bg_ref, h_ref):
    merged = None
    for i, (y_ref, w_ref, g_ref) in enumerate(((ys_ref, ws_ref, g0_ref), (yr_ref, wr_ref, g1_ref),
                                               (yd_ref, wd_ref, g2_ref))):
        br = _dot(y_ref[...], w_ref[...])
        t = _sigmoid(g_ref[...].astype(F32) + bg_ref[i:i + 1, :]) * br
        merged = t if merged is None else merged + t
    h_ref[...] = x_ref[...] + _dot(merged.astype(BF16), wo_ref[...])


def _merge(y_ssm, y_ret, y_diff, proj2d, x2d, p, tm):
    n = x2d.shape[0]
    tok = lambda cb: pl.BlockSpec((tm, D_MODEL), lambda i: (i, cb))
    wspec = pl.BlockSpec((D_MODEL, D_MODEL), lambda i: (0, 0))
    g0 = C_GATE // D_MODEL
    return pl.pallas_call(
        _merge_kernel,
        grid=(n // tm,),
        in_specs=[tok(0), tok(0), tok(0), tok(g0), tok(g0 + 1), tok(g0 + 2), tok(0),
                  wspec, wspec, wspec, wspec, pl.BlockSpec((N_BRANCHES, D_MODEL), lambda i: (0, 0))],
        out_specs=tok(0),
        out_shape=jax.ShapeDtypeStruct((n, D_MODEL), F32),
        compiler_params=_cparams("arbitrary"),
        name="merge",
    )(y_ssm, y_ret, y_diff, proj2d, proj2d, proj2d, x2d, p["w_ssm_out"], p["w_ret_out"], p["w_diff_out"],
      p["w_o"], p["b_gate"])


def _ffn_kernel(h_ref, g_ref, wg_ref, wu_ref, wd_ref, y_ref):
    h = h_ref[...]
    ms = jnp.mean(h * h, axis=-1, keepdims=True)
    hn = (h * lax.rsqrt(ms + NORM_EPS) * g_ref[...]).astype(BF16)
    act = _silu(_dot(hn, wg_ref[...])) * _dot(hn, wu_ref[...])
    y_ref[...] = h + _dot(act.astype(BF16), wd_ref[...])


def _ffn(h2d, p, tm):
    n = h2d.shape[0]
    resident = lambda shape, cb: pl.BlockSpec(shape, lambda i: (0, cb), pipeline_mode=pl.Buffered(1))
    return pl.pallas_call(
        _ffn_kernel,
        grid=(n // tm,),
        in_specs=[
            pl.BlockSpec((tm, D_MODEL), lambda i: (i, 0)),
            pl.BlockSpec((1, D_MODEL), lambda i: (0, 0)),
            resident((D_MODEL, D_FF), 0),
            resident((D_MODEL, D_FF), 1),
            resident((D_FF, D_MODEL), 0),
        ],
        out_specs=pl.BlockSpec((tm, D_MODEL), lambda i: (i, 0)),
        out_shape=jax.ShapeDtypeStruct((n, D_MODEL), F32),
        compiler_params=_cparams("arbitrary"),
        name="ffn",
    )(h2d, p["norm_ffn"], p["w_gate_up"], p["w_gate_up"], p["w_down"])


def _t5_bucket(dist):
    n = jnp.maximum(dist, 0)
    max_exact = N_BUCKETS // 2
    large = max_exact + (jnp.log(jnp.maximum(n, 1).astype(F32) / max_exact)
                         / math.log(MAX_DISTANCE / max_exact) * (N_BUCKETS - max_exact)).astype(jnp.int32)
    large = jnp.minimum(large, N_BUCKETS - 1)
    return jnp.where(n < max_exact, n, large)


def _far_bucket_is_constant(min_dist):
    max_exact = N_BUCKETS // 2
    d = np.float32(min_dist)
    large = max_exact + int(np.float32(np.log(d / np.float32(max_exact))) / np.float32(math.log(MAX_DISTANCE / max_exact))
                            * (N_BUCKETS - max_exact))
    return min_dist >= max_exact and large >= N_BUCKETS - 1


def _rope_tables(pos):
    half = RET_DK // 2
    inv = 1.0 / (ROPE_BASE ** (jnp.arange(half, dtype=F32) / half))
    ang = pos.astype(F32)[:, None] * inv[None, :]
    cos, sin = jnp.cos(ang), jnp.sin(ang)
    return jnp.concatenate([cos, cos], axis=1), jnp.concatenate([-sin, sin], axis=1)


def _layer_params(l, w_in, named):
    p = {k: v[l] for k, v in named.items()}
    w = w_in[l]
    o = (0,) + IN_OFFSETS + (w.shape[1],)
    z, xbc, dt, rq, rk, rv, rg, dq, dk, dv, gates = [w[:, o[i]:o[i + 1]] for i in range(len(IN_SPLITS))]
    out = {}
    out["w_main"] = jnp.concatenate([xbc, z, rq, rk, rv, rg, dq, gates], axis=1).astype(BF16)
    out["w_kvdt"] = jnp.concatenate([dk, dv, jnp.pad(dt, ((0, 0), (0, LANES - SSM_HEADS)))], axis=1).astype(BF16)
    out["norm_mix"] = p["norm_mix"].reshape(1, D_MODEL)
    out["conv_w"] = p["conv_w"]
    out["conv_b"] = p["conv_b"].reshape(1, CONV_DIM)
    out["dt_bias"] = jnp.pad(p["dt_bias"], (0, LANES - SSM_HEADS)).reshape(1, LANES)
    out["a_log"] = jnp.pad(p["a_log"], (0, LANES - SSM_HEADS)).reshape(1, LANES)
    out["d_skip"] = jnp.repeat(p["d_skip"], SSM_HEAD_DIM).reshape(1, SSM_D)
    out["ssm_norm"] = p["ssm_norm"].reshape(1, SSM_D)
    head_of_channel = np.arange(SSM_D) // SSM_HEAD_DIM
    out["head_expand"] = jnp.asarray(np.arange(LANES)[:, None] == head_of_channel[None, :], dtype=BF16)
    out["ret_norm"] = p["ret_norm"].reshape(1, RET_HEADS * RET_DV)
    out["qk_norm_q"] = jnp.tile(p["qk_norm_q"], 2).reshape(1, LANES)
    out["qk_norm_k"] = jnp.tile(p["qk_norm_k"], 2).reshape(1, LANES)
    lamv = jnp.stack([p["lambda_q1"], p["lambda_k1"], p["lambda_q2"], p["lambda_k2"]])
    out["lamv"] = jnp.pad(lamv, ((0, 0), (0, LANES - DIFF_HD)))
    out["diff_norm"] = p["diff_norm"].reshape(1, DIFF_VD)
    half = np.arange(LANES) // DIFF_HD
    out["blockdiag"] = jnp.asarray(half[:, None] == half[None, :], dtype=BF16)
    for k in ("w_ssm_out", "w_ret_out", "w_diff_out", "w_o", "w_gate_up", "w_down"):
        out[k] = p[k].astype(BF16)
    out["b_gate"] = p["b_gate"]
    out["norm_ffn"] = p["norm_ffn"].reshape(1, D_MODEL)
    return out


def _token_tile(n, cap):
    tm = min(n, cap)
    while n % tm:
        tm //= 2
    return tm


def _layer_common(x, p, layer, vbuf, tail0, h0, s0, cos_t, sin_t, nvalid, attn_fn):
    b, tp, _ = x.shape
    n = b * tp
    x2d = x.reshape(n, D_MODEL)
    proj, kf, vbuf, dt_raw = _inproj(x2d, p["norm_mix"], p["w_main"], p["w_kvdt"], _token_tile(n, 1024), vbuf, layer)
    proj3 = proj.reshape(b, tp, C_MAIN)
    dt3 = dt_raw.reshape(b, tp, LANES)
    tpad = -tp % CHUNK
    scan_in = lambda a: jnp.pad(a, ((0, 0), (0, tpad), (0, 0))) if tpad else a
    scan_proj = scan_in(proj3[:, :, :C_DQ]) if tpad else proj3
    y_ssm, h_new, conv_rows, y_ret, s_new = _scans(scan_proj, scan_in(dt3), tail0, h0, cos_t, sin_t, s0, p,
                                                   nvalid)
    y_diff, k_new = attn_fn(proj3, kf.reshape(b, tp, KV_COLS), vbuf)
    tm = _token_tile(n, 512)
    h = _merge(y_ssm[:, :tp].reshape(n, -1), y_ret[:, :tp].reshape(n, -1), y_diff.reshape(n, -1), proj, x2d, p, tm)
    y = _ffn(h, p, tm)
    last = (nvalid - 1) % CONV_TAIL
    assert last >= CONV_W - 2
    return (y.reshape(b, tp, D_MODEL), k_new, vbuf,
            h_new.reshape(b, SSM_HEADS, SSM_HEAD_DIM, SSM_STATE), conv_rows[:, last - (CONV_W - 2):last + 1, :], s_new)


def kernel(x_prompt, x_sample, cache_k, cache_v, page_table, state_ssm, state_conv, state_ret, rel_bias, norm_mix, w_in, b_gate, conv_w, conv_b, dt_bias, a_log, d_skip, ssm_norm, w_ssm_out, ret_norm, w_ret_out, qk_norm_q, qk_norm_k, lambda_q1, lambda_k1, lambda_q2, lambda_k2, diff_norm, w_diff_out, w_o, norm_ffn, w_gate_up, w_down):
    named = dict(norm_mix=norm_mix, b_gate=b_gate, conv_w=conv_w, conv_b=conv_b, dt_bias=dt_bias, a_log=a_log,
                 d_skip=d_skip, ssm_norm=ssm_norm, w_ssm_out=w_ssm_out, ret_norm=ret_norm, w_ret_out=w_ret_out,
                 qk_norm_q=qk_norm_q, qk_norm_k=qk_norm_k, lambda_q1=lambda_q1, lambda_k1=lambda_k1,
                 lambda_q2=lambda_q2, lambda_k2=lambda_k2, diff_norm=diff_norm, w_diff_out=w_diff_out, w_o=w_o,
                 norm_ffn=norm_ffn, w_gate_up=w_gate_up, w_down=w_down)
    depth = w_in.shape[0]
    bp, seq, _ = x_prompt.shape
    bs, dec, _ = x_sample.shape
    n_pages = page_table.shape[1]
    past = n_pages * PAGE_SIZE
    assert seq % CHUNK == 0 and CONV_W - 1 <= dec <= SUBLANES
    assert _far_bucket_is_constant(CHUNK + 1)

    ii = jnp.arange(CHUNK)[:, None]
    jj = jnp.arange(CHUNK)[None, :]
    idx_p = jnp.concatenate([_t5_bucket(k * CHUNK + ii - jj) for k in range(3)], axis=0)
    bias_h = _bias_tiles(rel_bias, idx_p, LOG2E).reshape(DIFF_KV_HEADS, DIFF_HEADS // DIFF_KV_HEADS, 3, CHUNK, CHUNK)
    bias_p = jnp.concatenate([bias_h[:, r] for r in range(DIFF_HEADS // DIFF_KV_HEADS)] * 2, axis=-2)
    tt = jnp.arange(SUBLANES)[:, None]
    idx_s = jnp.concatenate([_t5_bucket(jnp.broadcast_to(past + tt, (SUBLANES, LANES))),
                             _t5_bucket(tt + PAGE_SIZE - jj), _t5_bucket(tt - jj)], axis=1)
    bias_s = _bias_tiles(rel_bias, idx_s, 1.0)

    cos_p, sin_p = _rope_tables(jnp.arange(seq))
    cos_s, sin_s = _rope_tables(past + jnp.arange(CHUNK))

    xs = jnp.pad(x_sample, ((0, 0), (0, SUBLANES - dec), (0, 0)))
    zeros_tail = jnp.zeros((bp, CONV_TAIL, CONV_DIM), F32)
    zeros_h = jnp.zeros((bp, SSM_D, SSM_STATE), F32)
    zeros_s = jnp.zeros((bp, RET_HEADS, RET_DK, RET_DV), F32)

    vbuf_p = jnp.zeros((depth, bp * seq * DIFF_KV_HEADS, DIFF_VD), F32)
    vbuf_s = jnp.zeros((depth, bs * SUBLANES * DIFF_KV_HEADS, DIFF_VD), F32)
    kbuf_p = jnp.zeros((depth, bp, KV_COLS, seq), F32)

    yp, ys = x_prompt, xs
    outs_p, outs_s = [], []
    for l in range(depth):
        lam_init = 0.8 - 0.6 * math.exp(-0.3 * l)
        p = _layer_params(l, w_in, named)

        def attn_p(proj3, kf, vbuf, l=l, p=p, lam_init=lam_init, kbuf=kbuf_p):
            return _diff_attention_prompt(proj3, kf, vbuf, kbuf, bias_p, p, l, lam_init)

        yp, kbuf_p, vbuf_p, h1, c1, r1 = _layer_common(yp, p, l, vbuf_p, zeros_tail, zeros_h, zeros_s, cos_p, sin_p,
                                                       CHUNK, attn_p)
        outs_p.append((h1, c1, r1))

        tail_s = jnp.pad(state_conv[l], ((0, 0), (CONV_TAIL - (CONV_W - 1), 0), (0, 0)))
        h0_s = state_ssm[l].reshape(bs, SSM_D, SSM_STATE)

        def attn_s(proj3, kf, vbuf, l=l, p=p, lam_init=lam_init):
            return _diff_attention_sample(proj3, kf, vbuf, cache_k, cache_v, page_table, bias_s, p, l, dec, lam_init)

        ys_new, k2, vbuf_s, h2, c2, r2 = _layer_common(ys, p, l, vbuf_s, tail_s, h0_s, state_ret[l], cos_s, sin_s, dec,
                                                       attn_s)
        ys = jnp.where(jnp.arange(SUBLANES)[None, :, None] < dec, ys_new, 0.0)
        outs_s.append((k2[:, :dec].reshape(bs, dec, DIFF_KV_HEADS, 2, DIFF_HD), h2, c2, r2))

    stack = lambda outs, i: jnp.stack([o[i] for o in outs])
    k_prompt = kbuf_p.reshape(depth, bp, DIFF_KV_HEADS, 2, DIFF_HD, seq).transpose(0, 1, 5, 2, 3, 4)
    v_prompt = vbuf_p.reshape(depth, bp, seq, DIFF_KV_HEADS, DIFF_VD)
    v_sample = vbuf_s.reshape(depth, bs, SUBLANES, DIFF_KV_HEADS, DIFF_VD)[:, :, :dec]
    return (yp, ys[:, :dec],
            k_prompt, v_prompt, stack(outs_p, 0), stack(outs_p, 1), stack(outs_p, 2),
            stack(outs_s, 0), v_sample, stack(outs_s, 1), stack(outs_s, 2), stack(outs_s, 3))
```

```python
import functools
import math

import numpy as np
import jax
import jax.numpy as jnp
from jax import lax
from jax.experimental import pallas as pl
from jax.experimental.pallas import tpu as pltpu

F32 = jnp.float32
BF16 = jnp.bfloat16

D_MODEL = 1024
SSM_HEADS = 16
SSM_HEAD_DIM = 64
SSM_D = SSM_HEADS * SSM_HEAD_DIM
SSM_STATE = 128
SSM_GROUPS = 4
CONV_W = 4
CONV_DIM = SSM_D + 2 * SSM_GROUPS * SSM_STATE
RET_HEADS = 4
RET_DK = 128
RET_DV = 256
ROPE_BASE = 10000.0
DIFF_HEADS = 8
DIFF_KV_HEADS = 4
DIFF_HD = 64
DIFF_VD = 2 * DIFF_HD
N_BUCKETS = 32
MAX_DISTANCE = 128
N_BRANCHES = 3
D_FF = 2816
NORM_EPS = 1e-6
PAGE_SIZE = 128

CHUNK = 128
LANES = 128
SUBLANES = 8
GROUPS_PER_PASS = 4
CONV_TAIL = 16
NEG_BIG = -1e30
LOG2E = 1.4426950408889634
VMEM_LIMIT = 52 * 1024 * 1024

IN_SPLITS = (SSM_D, CONV_DIM, SSM_HEADS,
             RET_HEADS * RET_DK, RET_HEADS * RET_DK, RET_HEADS * RET_DV, RET_HEADS * RET_DV,
             DIFF_HEADS * 2 * DIFF_HD, DIFF_KV_HEADS * 2 * DIFF_HD, DIFF_KV_HEADS * DIFF_VD,
             N_BRANCHES * D_MODEL)
IN_OFFSETS = tuple(int(v) for v in np.cumsum(IN_SPLITS)[:-1])

C_XBC, C_Z, C_RQ, C_RK, C_RV, C_RG, C_DQ, C_GATE, C_MAIN = 0, 2048, 3072, 3584, 4096, 5120, 6144, 7168, 10240
TN_PROJ = 2048
KV_COLS = DIFF_KV_HEADS * 2 * DIFF_HD


def _cparams(*sem):
    return pltpu.CompilerParams(dimension_semantics=sem, vmem_limit_bytes=VMEM_LIMIT)


def _nt_dot(a, b):
    return lax.dot_general(a, b, (((1,), (1,)), ((), ())), preferred_element_type=F32)


def _tn_dot(a, b):
    return lax.dot_general(a, b, (((0,), (0,)), ((), ())), preferred_element_type=F32)


def _dot(a, b):
    return jnp.dot(a, b, preferred_element_type=F32)


def _split3(x):
    hi = x.astype(BF16)
    r1 = x - hi.astype(F32)
    mid = r1.astype(BF16)
    lo = (r1 - mid.astype(F32)).astype(BF16)
    return hi, mid, lo


def _sigmoid(x):
    return 0.5 * jnp.tanh(0.5 * x) + 0.5


def _silu(x):
    h = 0.5 * x
    return h + h * jnp.tanh(h)


def _inproj_kernel(x_ref, g_ref, w_ref, wkvdt_ref, vprev_ref, main_ref, kf_ref, vf_ref, dt_ref, xn_ref):
    del vprev_ref
    tm = x_ref.shape[0]

    @pl.when(pl.program_id(1) == 0)
    def _():
        x = x_ref[...]
        ms = jnp.mean(x * x, axis=-1, keepdims=True)
        xn = (x * lax.rsqrt(ms + NORM_EPS) * g_ref[...]).astype(BF16)
        xn_ref[...] = xn
        kvdt = _dot(xn, wkvdt_ref[...])
        kf_ref[...] = kvdt[:, :KV_COLS]
        for g in range(DIFF_KV_HEADS):
            vf_ref[pl.ds(g, tm, stride=DIFF_KV_HEADS), :] = kvdt[:, KV_COLS + g * DIFF_VD:KV_COLS + (g + 1) * DIFF_VD]
        dt_ref[...] = kvdt[:, 2 * KV_COLS:]

    main_ref[...] = _dot(xn_ref[...], w_ref[...]).astype(BF16)


def _inproj(x2d, gain, w_main, w_kvdt, tm, vbuf, layer):
    n = x2d.shape[0]
    grid = (n // tm, C_MAIN // TN_PROJ)
    return pl.pallas_call(
        _inproj_kernel,
        grid=grid,
        in_specs=[
            pl.BlockSpec((tm, D_MODEL), lambda i, j: (i, 0)),
            pl.BlockSpec((1, D_MODEL), lambda i, j: (0, 0)),
            pl.BlockSpec((D_MODEL, TN_PROJ), lambda i, j: (0, j)),
            pl.BlockSpec((D_MODEL, 2 * KV_COLS + LANES), lambda i, j: (0, 0)),
            pl.BlockSpec(memory_space=pl.ANY),
        ],
        out_specs=[
            pl.BlockSpec((tm, TN_PROJ), lambda i, j: (i, j)),
            pl.BlockSpec((tm, KV_COLS), lambda i, j: (i, 0)),
            pl.BlockSpec((None, tm * DIFF_KV_HEADS, DIFF_VD), lambda i, j: (layer, i, 0)),
            pl.BlockSpec((tm, LANES), lambda i, j: (i, 0)),
        ],
        out_shape=[
            jax.ShapeDtypeStruct((n, C_MAIN), BF16),
            jax.ShapeDtypeStruct((n, KV_COLS), F32),
            jax.ShapeDtypeStruct(vbuf.shape, F32),
            jax.ShapeDtypeStruct((n, LANES), F32),
        ],
        scratch_shapes=[pltpu.VMEM((tm, D_MODEL), BF16)],
        input_output_aliases={4: 2},
        compiler_params=_cparams("arbitrary", "arbitrary"),
        name="inproj",
    )(x2d, gain, w_main, w_kvdt, vbuf)


def _ssd_kernel(xbc_ref, z_ref, dt_ref, tail0_ref, h0_ref, cw_ref, cb_ref, dtb_ref, alog_ref, dskip_ref,
                gn_ref, ex_ref, y_ref, hout_ref, convout_ref, xext_ref, tlo_ref, xc_ref, ht_ref, *, c, nvalid, nchunks):
    L = CHUNK
    GW = SSM_D // SSM_GROUPS
    SL = 512

    @pl.when(c == 0)
    def _():
        t0 = tail0_ref[0]
        t0b = t0.astype(BF16)
        xext_ref[0:CONV_TAIL, :] = t0b
        tlo_ref[...] = (t0 - t0b.astype(F32)).astype(BF16)
        for k in range(SSM_D // LANES):
            ht_ref[:, k * LANES:(k + 1) * LANES] = h0_ref[0, k * LANES:(k + 1) * LANES, :].T

    xext_ref[CONV_TAIL:CONV_TAIL + L, :] = xbc_ref[0]
    srow = lax.broadcasted_iota(jnp.int32, (L, CONV_TAIL + L), 0)
    scol = lax.broadcasted_iota(jnp.int32, (L, CONV_TAIL + L), 1)
    shifts = [jnp.where(scol == srow + CONV_TAIL - s, 1.0, 0.0).astype(BF16) for s in range(1, CONV_W)]
    shifted = [[_dot(shifts[s - 1], xext_ref[:, sl * SL:(sl + 1) * SL]) for s in range(1, CONV_W)]
               for sl in range(CONV_DIM // SL)]
    for sl in range(CONV_DIM // SL):
        cols = slice(sl * SL, (sl + 1) * SL)
        conv = cw_ref[CONV_W - 1:CONV_W, cols] * xbc_ref[0, :, cols].astype(F32) + cb_ref[:, cols]
        for s in range(1, CONV_W):
            conv = conv + cw_ref[CONV_W - 1 - s:CONV_W - s, cols] * shifted[sl][s - 1]
        xc_ref[:, cols] = conv

    @pl.when(c == 0)
    def _():
        for sl in range(CONV_DIM // SL):
            cols = slice(sl * SL, (sl + 1) * SL)
            corr = None
            for s in range(1, CONV_W):
                t = cw_ref[CONV_W - 1 - s:CONV_W - s, cols] * _dot(shifts[s - 1][0:CONV_TAIL, 0:CONV_TAIL],
                                                                    tlo_ref[:, cols])
                corr = t if corr is None else corr + t
            xc_ref[0:CONV_TAIL, cols] += corr

    @pl.when(c == nchunks - 1)
    def _():
        a = CONV_TAIL * ((nvalid - 1) // CONV_TAIL)
        convout_ref[0] = xext_ref[CONV_TAIL + a:CONV_TAIL + a + CONV_TAIL, :].astype(F32)

    xext_ref[0:CONV_TAIL, :] = xext_ref[L:L + CONV_TAIL, :]

    row = lax.broadcasted_iota(jnp.int32, (L, L), 0)
    col = lax.broadcasted_iota(jnp.int32, (L, L), 1)
    causal = row >= col
    left = col < SSM_HEAD_DIM

    x = dt_ref[0] + dtb_ref[...]
    dt = jnp.maximum(x, 0.0) + jnp.log1p(jnp.exp(-jnp.abs(x)))
    if nvalid < L:
        dt = jnp.where(row < nvalid, dt, 0.0)
    a = -jnp.exp(alog_ref[...])
    da = dt * a
    tri = jnp.where(causal, 1.0, 0.0).astype(BF16)
    cs = sum(_dot(tri, p) for p in _split3(da))
    cs_t = cs.T
    cs_parts = _split3(cs)
    dt_parts = _split3(dt)

    NG = SSM_GROUPS
    HPG = SSM_HEADS // SSM_GROUPS
    gcs = [slice(g * GW, (g + 1) * GW) for g in range(NG)]
    bcol = lambda g: slice(SSM_D + g * SSM_STATE, SSM_D + (g + 1) * SSM_STATE)
    ccol = lambda g: slice(SSM_D + NG * SSM_STATE + g * SSM_STATE, SSM_D + NG * SSM_STATE + (g + 1) * SSM_STATE)

    def scan_groups(gs):
        csx = {g: sum(_dot(p, ex_ref[:, gcs[g]]) for p in cs_parts) for g in gs}
        dtx = {g: sum(_dot(p, ex_ref[:, gcs[g]]) for p in dt_parts) for g in gs}
        bgs = {g: _silu(xc_ref[:, bcol(g)]).astype(BF16) for g in gs}
        cgs = {g: _silu(xc_ref[:, ccol(g)]).astype(BF16) for g in gs}
        cbs = {g: _nt_dot(cgs[g], bgs[g]) for g in gs}
        htgs = {g: ht_ref[:, gcs[g]] for g in gs}
        ysts = {g: _dot(cgs[g], htgs[g].astype(BF16)) for g in gs}
        xss = {g: _silu(xc_ref[:, gcs[g]]) for g in gs}
        xdts = {g: xss[g] * dtx[g] for g in gs}
        xdt_bs = {g: xdts[g].astype(BF16) for g in gs}
        lastxs = {g: csx[g][L - 1:L, :] for g in gs}
        xdtw_bs = {g: (xdts[g] * jnp.exp(lastxs[g] - csx[g])).astype(BF16) for g in gs}
        prods = {}
        for g in gs:
            for e in range(HPG):
                h = g * HPG + e
                seg = cs[:, h:h + 1] - cs_t[h:h + 1, :]
                decay = jnp.exp(jnp.where(causal, seg, NEG_BIG))
                mat = (cbs[g] * decay).astype(BF16)
                xp = xdt_bs[g][:, (e // 2) * LANES:(e // 2 + 1) * LANES]
                keep = left if e % 2 == 0 else jnp.logical_not(left)
                prods[h] = _dot(mat, jnp.where(keep, xp, jnp.zeros_like(xp)))
        upd = {g: _tn_dot(bgs[g], xdtw_bs[g]) for g in gs}
        for g in gs:
            ht_ref[:, gcs[g]] = htgs[g] * jnp.exp(lastxs[g]) + upd[g]
        for g in gs:
            gc = gcs[g]
            y_in = jnp.concatenate([prods[g * HPG + 2 * pr] + prods[g * HPG + 2 * pr + 1] for pr in range(HPG // 2)],
                                   axis=1)
            y = y_in + ysts[g] * jnp.exp(csx[g]) + dskip_ref[:, gc] * xss[g]
            y = y * _silu(z_ref[0, :, gc].astype(F32))
            ms = jnp.mean(y * y, axis=-1, keepdims=True)
            y_ref[0, :, gc] = (y * lax.rsqrt(ms + NORM_EPS) * gn_ref[:, gc]).astype(BF16)

    for first in range(0, NG, GROUPS_PER_PASS):
        scan_groups(range(first, first + GROUPS_PER_PASS))

    @pl.when(c == nchunks - 1)
    def _():
        for k in range(SSM_D // LANES):
            hout_ref[0, k * LANES:(k + 1) * LANES, :] = ht_ref[:, k * LANES:(k + 1) * LANES].T


def _ret_kernel(q_ref, k_ref, v_ref, rg_ref, cos_ref, sin_ref, s0_ref, gn_ref, y_ref, sout_ref, *, c, ltrue):
    L = CHUNK

    @pl.when(c == 0)
    def _():
        sout_ref[...] = s0_ref[...]

    row = lax.broadcasted_iota(jnp.int32, (L, L), 0)
    col = lax.broadcasted_iota(jnp.int32, (L, L), 1)
    rel = (row - col).astype(F32)
    idx = row[:, 0:1].astype(F32)
    cosf = cos_ref[...]
    sins = sin_ref[...]
    H = RET_HEADS
    lgs = [math.log(1.0 - 2.0 ** (-5.0 - h)) for h in range(H)]
    kcs = [slice(h * RET_DK, (h + 1) * RET_DK) for h in range(H)]
    vcs = [slice(h * RET_DV, (h + 1) * RET_DV) for h in range(H)]
    qrs, krs = [], []
    for h in range(H):
        qh = q_ref[0, :, kcs[h]].astype(F32)
        kh = k_ref[0, :, kcs[h]].astype(F32)
        qrs.append(qh * cosf + pltpu.roll(qh, RET_DK // 2, 1) * sins)
        krs.append((kh * cosf + pltpu.roll(kh, RET_DK // 2, 1) * sins) * RET_DK ** -0.5)
    qr_bs = [q_.astype(BF16) for q_ in qrs]
    vhs = [v_ref[0, :, vcs[h]] for h in range(H)]
    s_olds = [sout_ref[0, h] for h in range(H)]
    scores = [_nt_dot(qr_bs[h], krs[h].astype(BF16)) for h in range(H)]
    cross = [_dot(qr_bs[h], s_olds[h].astype(BF16)) for h in range(H)]
    atts = []
    for h in range(H):
        dmat = jnp.where(rel >= 0, jnp.exp(jnp.maximum(rel, 0.0) * lgs[h]), 0.0)
        atts.append((scores[h] * dmat).astype(BF16))
    inner = [_dot(atts[h], vhs[h]) for h in range(H)]
    for h in range(H):
        k_dec = jnp.exp((ltrue - 1.0 - idx) * lgs[h])
        sout_ref[0, h] = s_olds[h] * math.exp(ltrue * lgs[h]) + _tn_dot((krs[h] * k_dec).astype(BF16), vhs[h])
    for h in range(H):
        o = inner[h] + cross[h] * jnp.exp((idx + 1.0) * lgs[h])
        oc = o - jnp.mean(o, axis=-1, keepdims=True)
        on = oc * lax.rsqrt(jnp.mean(oc * oc, axis=-1, keepdims=True) + NORM_EPS)
        y_ref[0, :, vcs[h]] = (on * gn_ref[:, vcs[h]] * _silu(rg_ref[0, :, vcs[h]].astype(F32))).astype(BF16)


N_SSD_IN, N_RET_IN, N_SSD_OUT, N_RET_OUT = 12, 8, 3, 2


CHUNKS_PER_STEP = 4


def _scan_kernel(*refs, nvalid, nchunks, per_step):
    i0 = N_SSD_IN
    i1 = i0 + N_RET_IN
    i2 = i1 + N_SSD_OUT
    i3 = i2 + N_RET_OUT
    ssd_in, ret_in, ssd_out, ret_out = refs[0:i0], refs[i0:i1], refs[i1:i2], refs[i2:i3]

    def chunk(sc, carry):
        c = pl.program_id(1) * per_step + sc
        rows = pl.ds(pl.multiple_of(sc * CHUNK, CHUNK), CHUNK)
        seq = lambda r: r.at[:, rows, :]
        tab = lambda r: r.at[rows, :]
        _ret_kernel(*[seq(r) for r in ret_in[0:4]], tab(ret_in[4]), tab(ret_in[5]), *ret_in[6:],
                    seq(ret_out[0]), ret_out[1], c=c, ltrue=float(nvalid))
        _ssd_kernel(*[seq(r) for r in ssd_in[0:3]], *ssd_in[3:], seq(ssd_out[0]), *ssd_out[1:], *refs[i3:],
                    c=c, nvalid=nvalid, nchunks=nchunks)
        return carry

    lax.fori_loop(0, per_step, chunk, 0)


def _scans(proj, dt_raw, tail0, h0, cos_t, sin_t, s0, p, nvalid):
    b, t, _ = proj.shape
    nchunks = t // CHUNK
    per_step = math.gcd(nchunks, CHUNKS_PER_STEP)
    L = CHUNK * per_step
    qk_w = RET_HEADS * RET_DK
    v_w = RET_HEADS * RET_DV
    const = lambda shape: pl.BlockSpec(shape, lambda i, c: (0,) * len(shape))
    ssd_in = [
        pl.BlockSpec((1, L, CONV_DIM), lambda i, c: (i, c, C_XBC // CONV_DIM)),
        pl.BlockSpec((1, L, SSM_D), lambda i, c: (i, c, C_Z // SSM_D)),
        pl.BlockSpec((1, L, LANES), lambda i, c: (i, c, 0)),
        pl.BlockSpec((1, CONV_TAIL, CONV_DIM), lambda i, c: (i, 0, 0)),
        pl.BlockSpec((1, SSM_D, SSM_STATE), lambda i, c: (i, 0, 0)),
        const((CONV_W, CONV_DIM)), const((1, CONV_DIM)), const((1, LANES)), const((1, LANES)),
        const((1, SSM_D)), const((1, SSM_D)), const((LANES, SSM_D)),
    ]
    ret_in = [
        pl.BlockSpec((1, L, qk_w), lambda i, c: (i, c, C_RQ // qk_w)),
        pl.BlockSpec((1, L, qk_w), lambda i, c: (i, c, C_RK // qk_w)),
        pl.BlockSpec((1, L, v_w), lambda i, c: (i, c, C_RV // v_w)),
        pl.BlockSpec((1, L, v_w), lambda i, c: (i, c, C_RG // v_w)),
        pl.BlockSpec((L, RET_DK), lambda i, c: (c, 0)),
        pl.BlockSpec((L, RET_DK), lambda i, c: (c, 0)),
        pl.BlockSpec((1, RET_HEADS, RET_DK, RET_DV), lambda i, c: (i, 0, 0, 0)),
        const((1, v_w)),
    ]
    ssd_out = [
        pl.BlockSpec((1, L, SSM_D), lambda i, c: (i, c, 0)),
        pl.BlockSpec((1, SSM_D, SSM_STATE), lambda i, c: (i, 0, 0)),
        pl.BlockSpec((1, CONV_TAIL, CONV_DIM), lambda i, c: (i, 0, 0)),
    ]
    ret_out = [
        pl.BlockSpec((1, L, v_w), lambda i, c: (i, c, 0)),
        pl.BlockSpec((1, RET_HEADS, RET_DK, RET_DV), lambda i, c: (i, 0, 0, 0)),
    ]
    assert (len(ssd_in), len(ret_in), len(ssd_out), len(ret_out)) == (N_SSD_IN, N_RET_IN, N_SSD_OUT, N_RET_OUT)
    return pl.pallas_call(
        functools.partial(_scan_kernel, nvalid=nvalid, nchunks=nchunks, per_step=per_step),
        grid=(b, nchunks // per_step),
        in_specs=ssd_in + ret_in,
        out_specs=ssd_out + ret_out,
        out_shape=[
            jax.ShapeDtypeStruct((b, t, SSM_D), BF16),
            jax.ShapeDtypeStruct((b, SSM_D, SSM_STATE), F32),
            jax.ShapeDtypeStruct((b, CONV_TAIL, CONV_DIM), F32),
            jax.ShapeDtypeStruct((b, t, v_w), BF16),
            jax.ShapeDtypeStruct((b, RET_HEADS, RET_DK, RET_DV), F32),
        ],
        scratch_shapes=[
            pltpu.VMEM((CONV_TAIL + CHUNK, CONV_DIM), BF16),
            pltpu.VMEM((CONV_TAIL, CONV_DIM), BF16),
            pltpu.VMEM((CHUNK, CONV_DIM), F32),
            pltpu.VMEM((SSM_STATE, SSM_D), F32),
        ],
        compiler_params=_cparams("arbitrary", "arbitrary"),
        name="scans",
    )(proj, proj, dt_raw, tail0, h0, p["conv_w"], p["conv_b"], p["dt_bias"], p["a_log"], p["d_skip"],
      p["ssm_norm"], p["head_expand"], proj, proj, proj, proj, cos_t, sin_t, s0, p["ret_norm"])


def _bias_kernel(tab_ref, idx_ref, out_ref, *, scale):
    h = pl.program_id(0)
    idx = idx_ref[...]
    acc = jnp.zeros(idx.shape, F32)
    for b in range(N_BUCKETS):
        acc = acc + jnp.where(idx == b, tab_ref[b * DIFF_HEADS + h], 0.0)
    out_ref[0] = acc * scale


def _bias_tiles(rel_bias, idx, scale):
    r, c = idx.shape
    return pl.pallas_call(
        functools.partial(_bias_kernel, scale=scale),
        grid=(DIFF_HEADS,),
        in_specs=[pl.BlockSpec(memory_space=pltpu.SMEM), pl.BlockSpec((r, c), lambda h: (0, 0))],
        out_specs=pl.BlockSpec((1, r, c), lambda h: (h, 0, 0)),
        out_shape=jax.ShapeDtypeStruct((DIFF_HEADS, r, c), F32),
        compiler_params=_cparams("arbitrary"),
        name="t5_bias",
    )(rel_bias.reshape(-1), idx)


def _half_rmsnorm(x, gain, bd):
    x2 = x * x
    hi = x2.astype(BF16)
    lo = (x2 - hi.astype(F32)).astype(BF16)
    ss = _dot(hi, bd) + _dot(lo, bd)
    return x * lax.rsqrt(ss * (1.0 / DIFF_HD) + NORM_EPS) * gain


def _half_rmsnorm_blocks(blocks, gain, bd):
    rows = blocks[0].shape[0]
    y = _half_rmsnorm(jnp.concatenate(blocks, axis=0), gain, bd)
    return [y[i * rows:(i + 1) * rows] for i in range(len(blocks))]


def _lambda(lamv_ref, lam_init):
    s1 = jnp.sum(lamv_ref[0:1, :] * lamv_ref[1:2, :], axis=-1, keepdims=True)
    s2 = jnp.sum(lamv_ref[2:3, :] * lamv_ref[3:4, :], axis=-1, keepdims=True)
    return jnp.exp(s1) - jnp.exp(s2) + lam_init


FAR_UNITS = 4
Q_TILES_PER_STEP = 4


def _dattn_kernel(q_ref, kf_ref, vf_ref, bias_ref, gq_ref, gk_ref, lamv_ref, sg_ref, bd_ref, kprev_ref,
                  y_ref, kout_ref, knt_ref, vx_ref, qs_ref, m_ref, acc_ref, *, t, q_tiles, lam_init):
    del kprev_ref
    step_id = pl.program_id(1)
    TQ = CHUNK
    G = DIFF_KV_HEADS
    R = DIFF_HEADS // DIFF_KV_HEADS
    NC = 2 * R * TQ
    bd = bd_ref[...]

    @pl.when(step_id == 0)
    def _():
        for i in range(t // TQ):
            r = slice(i * TQ, (i + 1) * TQ)
            kns = _half_rmsnorm_blocks([kf_ref[0, r, g * LANES:(g + 1) * LANES] for g in range(G)], gk_ref[...], bd)
            for g in range(G):
                knt = kns[g].T
                kout_ref[0, g * LANES:(g + 1) * LANES, r] = knt
                knt_ref[g, i] = knt.astype(BF16)
                vx_ref[g, r, 0:DIFF_VD] = vf_ref[0, pl.ds(i * TQ * G + g, TQ, stride=G), :].astype(BF16)
                vx_ref[g, r, DIFF_VD:] = jnp.ones((TQ, LANES), BF16)

    def tile(ti, carry):
        qi = step_id * q_tiles + ti
        qrows = pl.ds(pl.multiple_of(ti * TQ, TQ), TQ)
        lane = lax.broadcasted_iota(jnp.int32, (TQ, LANES), 1)
        left = lane < DIFF_HD
        qns = _half_rmsnorm_blocks([q_ref[0, qrows, h * LANES:(h + 1) * LANES].astype(F32)
                                    for h in range(DIFF_HEADS)], gq_ref[...], bd)
        for g in range(G):
            for r in range(R):
                qn = qns[g * R + r] * (DIFF_HD ** -0.5 * LOG2E)
                qs_ref[g, (0 * R + r) * TQ:(0 * R + r + 1) * TQ, :] = jnp.where(left, qn, 0.0).astype(BF16)
                qs_ref[g, (1 * R + r) * TQ:(1 * R + r + 1) * TQ, :] = jnp.where(left, 0.0, qn).astype(BF16)
        m_ref[...] = jnp.full(m_ref.shape, NEG_BIG, F32)
        acc_ref[...] = jnp.zeros(acc_ref.shape, F32)

        def step(k0, tiles):
            nunits = len(tiles)
            rows = pl.ds(pl.multiple_of(k0 * TQ, TQ), nunits * TQ)
            kts = [knt_ref[g, k0] if nunits == 1 else
                   jnp.concatenate([knt_ref[g, k0 + u] for u in range(nunits)], axis=1) for g in range(G)]
            ss = [_dot(qs_ref[g], kts[g]) for g in range(G)]
            ps, alphas = [], []
            for g in range(G):
                s = ss[g] + jnp.concatenate([bias_ref[g, tl] for tl in tiles], axis=1)
                if 0 in tiles:
                    first = tiles.index(0) * TQ
                    key = lax.broadcasted_iota(jnp.int32, (NC, nunits * TQ), 1) - first
                    qry = lax.broadcasted_iota(jnp.int32, (NC, nunits * TQ), 0) & (TQ - 1)
                    s = jnp.where(key <= qry, s, NEG_BIG)
                m_old = m_ref[g]
                m_new = jnp.maximum(m_old, jnp.max(s, axis=1, keepdims=True))
                alpha = jnp.exp2(m_old - m_new)
                p = jnp.concatenate([jnp.exp2(s[:, u * TQ:(u + 1) * TQ] - m_new) for u in range(nunits)], axis=1)
                m_ref[g] = m_new
                ps.append(p.astype(BF16))
                alphas.append(alpha)
            for g in range(G):
                scale = jnp.concatenate([alphas[g], alphas[g]], axis=1)
                acc_ref[g] = acc_ref[g] * scale + _dot(ps[g], vx_ref[g, rows, :])

        nfar = jnp.maximum(qi - (FAR_UNITS - 1), 0)
        nbig = nfar // FAR_UNITS
        rem = nfar - nbig * FAR_UNITS

        def far_big(i, c):
            step(i * FAR_UNITS, (2,) * FAR_UNITS)
            return c

        lax.fori_loop(0, nbig, far_big, 0)

        @pl.when(rem >= 2)
        def _():
            step(nbig * FAR_UNITS, (2, 2))

        @pl.when((rem & 1) == 1)
        def _():
            step(nbig * FAR_UNITS + (rem & 2), (2,))

        @pl.when(qi >= FAR_UNITS - 1)
        def _():
            step(qi - (FAR_UNITS - 1), (2,) * (FAR_UNITS - 2) + (1, 0))

        for early in range(FAR_UNITS - 1):
            @pl.when(qi == early)
            def _(early=early):
                step(0, ((2,) * early + (1, 0))[-(early + 1):])

        lam = _lambda(lamv_ref, lam_init)
        for g in range(G):
            for r in range(R):
                r0 = slice((0 * R + r) * TQ, (0 * R + r + 1) * TQ)
                r1 = slice((1 * R + r) * TQ, (1 * R + r + 1) * TQ)
                o = (acc_ref[g, r0, 0:DIFF_VD] / acc_ref[g, r0, DIFF_VD:]
                     - lam * (acc_ref[g, r1, 0:DIFF_VD] / acc_ref[g, r1, DIFF_VD:]))
                o = o * lax.rsqrt(jnp.mean(o * o, axis=1, keepdims=True) + NORM_EPS)
                o = o * sg_ref[...] * (1.0 - lam_init)
                y_ref[0, qrows, (g * R + r) * LANES:(g * R + r + 1) * LANES] = o.astype(BF16)
        return carry

    lax.fori_loop(0, q_tiles, tile, 0)


def _diff_attention_prompt(proj, kf, vbuf, kbuf, bias, p, layer, lam_init):
    b, t, _ = proj.shape
    vf = vbuf.reshape(vbuf.shape[0], b, t * DIFF_KV_HEADS, DIFF_VD)
    TQ = CHUNK
    G = DIFF_KV_HEADS
    R = DIFF_HEADS // DIFF_KV_HEADS
    NC = 2 * R * TQ
    qw = DIFF_HEADS * 2 * DIFF_HD
    const = lambda shape: pl.BlockSpec(shape, lambda i, q: (0,) * len(shape))
    q_tiles = math.gcd(t // TQ, Q_TILES_PER_STEP)
    return pl.pallas_call(
        functools.partial(_dattn_kernel, t=t, q_tiles=q_tiles, lam_init=lam_init),
        grid=(b, t // (TQ * q_tiles)),
        in_specs=[
            pl.BlockSpec((1, TQ * q_tiles, qw), lambda i, q: (i, q, C_DQ // qw)),
            pl.BlockSpec((1, t, KV_COLS), lambda i, q: (i, 0, 0)),
            pl.BlockSpec((None, 1, t * G, DIFF_VD), lambda i, q: (layer, i, 0, 0)),
            const((G, 3, NC, TQ)),
            const((1, LANES)), const((1, LANES)), const((4, LANES)), const((1, DIFF_VD)), const((LANES, LANES)),
            pl.BlockSpec(memory_space=pl.ANY),
        ],
        out_specs=[
            pl.BlockSpec((1, TQ * q_tiles, qw), lambda i, q: (i, q, 0)),
            pl.BlockSpec((None, 1, KV_COLS, t), lambda i, q: (layer, i, 0, 0)),
        ],
        out_shape=[
            jax.ShapeDtypeStruct((b, t, DIFF_HEADS * DIFF_VD), BF16),
            jax.ShapeDtypeStruct(kbuf.shape, F32),
        ],
        input_output_aliases={9: 1},
        scratch_shapes=[
            pltpu.VMEM((G, t // TQ, LANES, TQ), BF16), pltpu.VMEM((G, t, DIFF_VD + LANES), BF16),
            pltpu.VMEM((G, NC, LANES), BF16),
            pltpu.VMEM((G, NC, LANES), F32),
            pltpu.VMEM((G, NC, DIFF_VD + LANES), F32),
        ],
        compiler_params=_cparams("arbitrary", "arbitrary"),
        name="diff_attn_prompt",
    )(proj, kf, vf, bias, p["qk_norm_q"], p["qk_norm_k"], p["lamv"], p["diff_norm"], p["blockdiag"], kbuf)


PAGES_PER_STEP = 32


def _sattn_kernel(pt_ref, *refs, layer, nvalid, lam_init, npp):
    del pt_ref, layer
    k_refs = refs[0:npp]
    v_refs = refs[npp:2 * npp]
    (q_ref, kf_ref, vf_ref, bias_ref, gq_ref, gk_ref, lamv_ref, sg_ref, bd_ref,
     y_ref, kout_ref, qs_ref, knew_ref, vnew_ref, m_ref, l_ref, acc_ref) = refs[2 * npp:]
    s_id = pl.program_id(1)
    nsteps = pl.num_programs(1)
    G = DIFF_KV_HEADS
    R = DIFF_HEADS // DIFF_KV_HEADS
    TP = SUBLANES
    MR = 2 * R * TP
    bd = bd_ref[...]
    lane = lax.broadcasted_iota(jnp.int32, (TP, LANES), 1)
    left = lane < DIFF_HD

    @pl.when(s_id == 0)
    def _():
        knew_ref[...] = jnp.zeros(knew_ref.shape, BF16)
        vnew_ref[...] = jnp.zeros(vnew_ref.shape, BF16)
        for g in range(G):
            gc = slice(g * LANES, (g + 1) * LANES)
            kn = _half_rmsnorm(kf_ref[0, :, gc], gk_ref[...], bd)
            kout_ref[0, :, gc] = kn
            knew_ref[g, 0:TP, :] = kn.astype(BF16)
            vnew_ref[g, 0:TP, :] = vf_ref[0, pl.ds(g, TP, stride=G), :].astype(BF16)
            for r in range(R):
                hc = slice((g * R + r) * LANES, (g * R + r + 1) * LANES)
                qn = _half_rmsnorm(q_ref[0, :, hc].astype(F32), gq_ref[...], bd) * DIFF_HD ** -0.5
                qs_ref[g, (0 * R + r) * TP:(0 * R + r + 1) * TP, :] = jnp.where(left, qn, 0.0).astype(BF16)
                qs_ref[g, (1 * R + r) * TP:(1 * R + r + 1) * TP, :] = jnp.where(left, 0.0, qn).astype(BF16)
        m_ref[...] = jnp.full(m_ref.shape, NEG_BIG, F32)
        l_ref[...] = jnp.zeros(l_ref.shape, F32)
        acc_ref[...] = jnp.zeros(acc_ref.shape, F32)

    def bias_rows(g, seg):
        per_head = [bias_ref[g * R + r, :, seg * LANES:(seg + 1) * LANES] for r in range(R)]
        return jnp.concatenate(per_head + per_head, axis=0)

    def update(g, s, vs):
        m_old = m_ref[g]
        m_new = jnp.maximum(m_old, jnp.max(s, axis=-1, keepdims=True))
        alpha = jnp.exp(m_old - m_new)
        p = jnp.exp(s - m_new)
        l_ref[g] = alpha * l_ref[g] + jnp.sum(p, axis=-1, keepdims=True)
        m_ref[g] = m_new
        pb = p.astype(BF16)
        pv = _dot(pb[:, 0:LANES], vs[0])
        for i in range(1, len(vs)):
            pv = pv + _dot(pb[:, i * LANES:(i + 1) * LANES], vs[i])
        acc_ref[g] = acc_ref[g] * alpha + pv

    def pages(last):
        scores = []
        for g in range(G):
            gc = slice(g * LANES, (g + 1) * LANES)
            kcat = jnp.concatenate([k_refs[i][gc, :].astype(BF16) for i in range(npp)], axis=1)
            far = bias_rows(g, 0)
            near = bias_rows(g, 1) if last else far
            scores.append(_dot(qs_ref[g], kcat) + jnp.concatenate([far] * (npp - 1) + [near], axis=1))
        pvs, alphas = [], []
        for g in range(G):
            s = scores[g]
            m_old = m_ref[g]
            m_new = jnp.maximum(m_old, jnp.max(s, axis=-1, keepdims=True))
            alpha = jnp.exp(m_old - m_new)
            p = jnp.exp(s - m_new)
            l_ref[g] = alpha * l_ref[g] + jnp.sum(p, axis=-1, keepdims=True)
            m_ref[g] = m_new
            vcat = jnp.concatenate([v_refs[i][pl.ds(g, PAGE_SIZE, stride=G), :].astype(BF16) for i in range(npp)],
                                   axis=0)
            pvs.append(_dot(p.astype(BF16), vcat))
            alphas.append(alpha)
        for g in range(G):
            acc_ref[g] = acc_ref[g] * alphas[g] + pvs[g]

    @pl.when(s_id < nsteps - 1)
    def _():
        pages(False)

    @pl.when(s_id == nsteps - 1)
    def _():
        pages(True)
        rowt = lax.broadcasted_iota(jnp.int32, (MR, LANES), 0) % TP
        colj = lax.broadcasted_iota(jnp.int32, (MR, LANES), 1)
        ok = jnp.logical_and(colj <= rowt, colj < nvalid)
        lam = _lambda(lamv_ref, lam_init)
        for g in range(G):
            sc = _nt_dot(qs_ref[g], knew_ref[g]) + bias_rows(g, 2)
            update(g, jnp.where(ok, sc, NEG_BIG), [vnew_ref[g]])
            acc = acc_ref[g] / l_ref[g]
            for r in range(R):
                o = acc[(0 * R + r) * TP:(0 * R + r + 1) * TP, :] - lam * acc[(1 * R + r) * TP:(1 * R + r + 1) * TP, :]
                o = o * lax.rsqrt(jnp.mean(o * o, axis=-1, keepdims=True) + NORM_EPS)
                hc = slice((g * R + r) * LANES, (g * R + r + 1) * LANES)
                y_ref[0, :, hc] = (o * sg_ref[...] * (1.0 - lam_init)).astype(BF16)


def _diff_attention_sample(proj, kf, vbuf, cache_k, cache_v, page_table, bias, p, layer, nvalid, lam_init):
    b, tp, _ = proj.shape
    vf = vbuf.reshape(vbuf.shape[0], b, tp * DIFF_KV_HEADS, DIFF_VD)
    n_pages = page_table.shape[1]
    npp = PAGES_PER_STEP
    while n_pages % npp:
        npp //= 2
    nsteps = n_pages // npp
    G = DIFF_KV_HEADS
    R = DIFF_HEADS // DIFF_KV_HEADS
    MR = 2 * R * tp
    ck = jnp.transpose(cache_k, (0, 1, 3, 4, 5, 2)).reshape(cache_k.shape[0], cache_k.shape[1], KV_COLS, PAGE_SIZE)
    cv = cache_v.reshape(cache_v.shape[0], cache_v.shape[1], PAGE_SIZE * DIFF_KV_HEADS, DIFF_VD)

    def page_spec(i):
        return pl.BlockSpec((None, None, KV_COLS, PAGE_SIZE),
                            lambda bi, s, pt: (layer, pt[bi * n_pages + s * npp + i], 0, 0))

    const = lambda shape: pl.BlockSpec(shape, lambda bi, s, pt: (0,) * len(shape))
    grid_spec = pltpu.PrefetchScalarGridSpec(
        num_scalar_prefetch=1,
        grid=(b, nsteps),
        in_specs=[page_spec(i) for i in range(npp)] + [page_spec(i) for i in range(npp)] + [
            pl.BlockSpec((1, tp, DIFF_HEADS * 2 * DIFF_HD), lambda bi, s, pt: (bi, 0, C_DQ // (DIFF_HEADS * 2 * DIFF_HD))),
            pl.BlockSpec((1, tp, KV_COLS), lambda bi, s, pt: (bi, 0, 0)),
            pl.BlockSpec((None, 1, tp * G, DIFF_VD), lambda bi, s, pt: (layer, bi, 0, 0)),
            const((DIFF_HEADS, tp, 3 * LANES)),
            const((1, LANES)), const((1, LANES)), const((4, LANES)), const((1, LANES)), const((LANES, LANES)),
        ],
        out_specs=[
            pl.BlockSpec((1, tp, DIFF_HEADS * DIFF_VD), lambda bi, s, pt: (bi, 0, 0)),
            pl.BlockSpec((1, tp, KV_COLS), lambda bi, s, pt: (bi, 0, 0)),
        ],
        scratch_shapes=[
            pltpu.VMEM((G, MR, LANES), BF16),
            pltpu.VMEM((G, PAGE_SIZE, LANES), BF16), pltpu.VMEM((G, PAGE_SIZE, LANES), BF16),
            pltpu.VMEM((G, MR, 1), F32), pltpu.VMEM((G, MR, 1), F32), pltpu.VMEM((G, MR, LANES), F32),
        ],
    )
    return pl.pallas_call(
        functools.partial(_sattn_kernel, layer=layer, nvalid=nvalid, lam_init=lam_init, npp=npp),
        grid_spec=grid_spec,
        out_shape=[
            jax.ShapeDtypeStruct((b, tp, DIFF_HEADS * DIFF_VD), BF16),
            jax.ShapeDtypeStruct((b, tp, KV_COLS), F32),
        ],
        compiler_params=_cparams("arbitrary", "arbitrary"),
        name="diff_attn_sample",
    )(page_table.reshape(-1), *([ck] * npp), *([cv] * npp), proj, kf, vf, bias,
      p["qk_norm_q"], p["qk_norm_k"], p["lamv"], p["diff_norm"], p["blockdiag"])


def _merge_kernel(ys_ref, yr_ref, yd_ref, g0_ref, g1_ref, g2_ref, x_ref, ws_ref, wr_ref, wd_ref, wo_ref,
                  bg_ref, h_ref):
    merged = None
    for i, (y_ref, w_ref, g_ref) in enumerate(((ys_ref, ws_ref, g0_ref), (yr_ref, wr_ref, g1_ref),
                                               (yd_ref, wd_ref, g2_ref))):
        br = _dot(y_ref[...], w_ref[...])
        t = _sigmoid(g_ref[...].astype(F32) + bg_ref[i:i + 1, :]) * br
        merged = t if merged is None else merged + t
    h_ref[...] = x_ref[...] + _dot(merged.astype(BF16), wo_ref[...])


def _merge(y_ssm, y_ret, y_diff, proj2d, x2d, p, tm):
    n = x2d.shape[0]
    tok = lambda cb: pl.BlockSpec((tm, D_MODEL), lambda i: (i, cb))
    wspec = pl.BlockSpec((D_MODEL, D_MODEL), lambda i: (0, 0))
    g0 = C_GATE // D_MODEL
    return pl.pallas_call(
        _merge_kernel,
        grid=(n // tm,),
        in_specs=[tok(0), tok(0), tok(0), tok(g0), tok(g0 + 1), tok(g0 + 2), tok(0),
                  wspec, wspec, wspec, wspec, pl.BlockSpec((N_BRANCHES, D_MODEL), lambda i: (0, 0))],
        out_specs=tok(0),
        out_shape=jax.ShapeDtypeStruct((n, D_MODEL), F32),
        compiler_params=_cparams("arbitrary"),
        name="merge",
    )(y_ssm, y_ret, y_diff, proj2d, proj2d, proj2d, x2d, p["w_ssm_out"], p["w_ret_out"], p["w_diff_out"],
      p["w_o"], p["b_gate"])


def _ffn_kernel(h_ref, g_ref, wg_ref, wu_ref, wd_ref, y_ref):
    h = h_ref[...]
    ms = jnp.mean(h * h, axis=-1, keepdims=True)
    hn = (h * lax.rsqrt(ms + NORM_EPS) * g_ref[...]).astype(BF16)
    act = _silu(_dot(hn, wg_ref[...])) * _dot(hn, wu_ref[...])
    y_ref[...] = h + _dot(act.astype(BF16), wd_ref[...])


def _ffn(h2d, p, tm):
    n = h2d.shape[0]
    resident = lambda shape, cb: pl.BlockSpec(shape, lambda i: (0, cb), pipeline_mode=pl.Buffered(1))
    return pl.pallas_call(
        _ffn_kernel,
        grid=(n // tm,),
        in_specs=[
            pl.BlockSpec((tm, D_MODEL), lambda i: (i, 0)),
            pl.BlockSpec((1, D_MODEL), lambda i: (0, 0)),
            resident((D_MODEL, D_FF), 0),
            resident((D_MODEL, D_FF), 1),
            resident((D_FF, D_MODEL), 0),
        ],
        out_specs=pl.BlockSpec((tm, D_MODEL), lambda i: (i, 0)),
        out_shape=jax.ShapeDtypeStruct((n, D_MODEL), F32),
        compiler_params=_cparams("arbitrary"),
        name="ffn",
    )(h2d, p["norm_ffn"], p["w_gate_up"], p["w_gate_up"], p["w_down"])


def _t5_bucket(dist):
    n = jnp.maximum(dist, 0)
    max_exact = N_BUCKETS // 2
    large = max_exact + (jnp.log(jnp.maximum(n, 1).astype(F32) / max_exact)
                         / math.log(MAX_DISTANCE / max_exact) * (N_BUCKETS - max_exact)).astype(jnp.int32)
    large = jnp.minimum(large, N_BUCKETS - 1)
    return jnp.where(n < max_exact, n, large)


def _far_bucket_is_constant(min_dist):
    max_exact = N_BUCKETS // 2
    d = np.float32(min_dist)
    large = max_exact + int(np.float32(np.log(d / np.float32(max_exact))) / np.float32(math.log(MAX_DISTANCE / max_exact))
                            * (N_BUCKETS - max_exact))
    return min_dist >= max_exact and large >= N_BUCKETS - 1


def _rope_tables(pos):
    half = RET_DK // 2
    inv = 1.0 / (ROPE_BASE ** (jnp.arange(half, dtype=F32) / half))
    ang = pos.astype(F32)[:, None] * inv[None, :]
    cos, sin = jnp.cos(ang), jnp.sin(ang)
    return jnp.concatenate([cos, cos], axis=1), jnp.concatenate([-sin, sin], axis=1)


def _layer_params(l, w_in, named):
    p = {k: v[l] for k, v in named.items()}
    w = w_in[l]
    o = (0,) + IN_OFFSETS + (w.shape[1],)
    z, xbc, dt, rq, rk, rv, rg, dq, dk, dv, gates = [w[:, o[i]:o[i + 1]] for i in range(len(IN_SPLITS))]
    out = {}
    out["w_main"] = jnp.concatenate([xbc, z, rq, rk, rv, rg, dq, gates], axis=1).astype(BF16)
    out["w_kvdt"] = jnp.concatenate([dk, dv, jnp.pad(dt, ((0, 0), (0, LANES - SSM_HEADS)))], axis=1).astype(BF16)
    out["norm_mix"] = p["norm_mix"].reshape(1, D_MODEL)
    out["conv_w"] = p["conv_w"]
    out["conv_b"] = p["conv_b"].reshape(1, CONV_DIM)
    out["dt_bias"] = jnp.pad(p["dt_bias"], (0, LANES - SSM_HEADS)).reshape(1, LANES)
    out["a_log"] = jnp.pad(p["a_log"], (0, LANES - SSM_HEADS)).reshape(1, LANES)
    out["d_skip"] = jnp.repeat(p["d_skip"], SSM_HEAD_DIM).reshape(1, SSM_D)
    out["ssm_norm"] = p["ssm_norm"].reshape(1, SSM_D)
    head_of_channel = np.arange(SSM_D) // SSM_HEAD_DIM
    out["head_expand"] = jnp.asarray(np.arange(LANES)[:, None] == head_of_channel[None, :], dtype=BF16)
    out["ret_norm"] = p["ret_norm"].reshape(1, RET_HEADS * RET_DV)
    out["qk_norm_q"] = jnp.tile(p["qk_norm_q"], 2).reshape(1, LANES)
    out["qk_norm_k"] = jnp.tile(p["qk_norm_k"], 2).reshape(1, LANES)
    lamv = jnp.stack([p["lambda_q1"], p["lambda_k1"], p["lambda_q2"], p["lambda_k2"]])
    out["lamv"] = jnp.pad(lamv, ((0, 0), (0, LANES - DIFF_HD)))
    out["diff_norm"] = p["diff_norm"].reshape(1, DIFF_VD)
    half = np.arange(LANES) // DIFF_HD
    out["blockdiag"] = jnp.asarray(half[:, None] == half[None, :], dtype=BF16)
    for k in ("w_ssm_out", "w_ret_out", "w_diff_out", "w_o", "w_gate_up", "w_down"):
        out[k] = p[k].astype(BF16)
    out["b_gate"] = p["b_gate"]
    out["norm_ffn"] = p["norm_ffn"].reshape(1, D_MODEL)
    return out


def _token_tile(n, cap):
    tm = min(n, cap)
    while n % tm:
        tm //= 2
    return tm


def _layer_common(x, p, layer, vbuf, tail0, h0, s0, cos_t, sin_t, nvalid, attn_fn):
    b, tp, _ = x.shape
    n = b * tp
    x2d = x.reshape(n, D_MODEL)
    proj, kf, vbuf, dt_raw = _inproj(x2d, p["norm_mix"], p["w_main"], p["w_kvdt"], _token_tile(n, 1024), vbuf, layer)
    proj3 = proj.reshape(b, tp, C_MAIN)
    dt3 = dt_raw.reshape(b, tp, LANES)
    tpad = -tp % CHUNK
    scan_in = lambda a: jnp.pad(a, ((0, 0), (0, tpad), (0, 0))) if tpad else a
    scan_proj = scan_in(proj3[:, :, :C_DQ]) if tpad else proj3
    y_ssm, h_new, conv_rows, y_ret, s_new = _scans(scan_proj, scan_in(dt3), tail0, h0, cos_t, sin_t, s0, p,
                                                   nvalid)
    y_diff, k_new = attn_fn(proj3, kf.reshape(b, tp, KV_COLS), vbuf)
    tm = _token_tile(n, 512)
    h = _merge(y_ssm[:, :tp].reshape(n, -1), y_ret[:, :tp].reshape(n, -1), y_diff.reshape(n, -1), proj, x2d, p, tm)
    y = _ffn(h, p, tm)
    last = (nvalid - 1) % CONV_TAIL
    assert last >= CONV_W - 2
    return (y.reshape(b, tp, D_MODEL), k_new, vbuf,
            h_new.reshape(b, SSM_HEADS, SSM_HEAD_DIM, SSM_STATE), conv_rows[:, last - (CONV_W - 2):last + 1, :], s_new)


def kernel(x_prompt, x_sample, cache_k, cache_v, page_table, state_ssm, state_conv, state_ret, rel_bias, norm_mix, w_in, b_gate, conv_w, conv_b, dt_bias, a_log, d_skip, ssm_norm, w_ssm_out, ret_norm, w_ret_out, qk_norm_q, qk_norm_k, lambda_q1, lambda_k1, lambda_q2, lambda_k2, diff_norm, w_diff_out, w_o, norm_ffn, w_gate_up, w_down):
    named = dict(norm_mix=norm_mix, b_gate=b_gate, conv_w=conv_w, conv_b=conv_b, dt_bias=dt_bias, a_log=a_log,
                 d_skip=d_skip, ssm_norm=ssm_norm, w_ssm_out=w_ssm_out, ret_norm=ret_norm, w_ret_out=w_ret_out,
                 qk_norm_q=qk_norm_q, qk_norm_k=qk_norm_k, lambda_q1=lambda_q1, lambda_k1=lambda_k1,
                 lambda_q2=lambda_q2, lambda_k2=lambda_k2, diff_norm=diff_norm, w_diff_out=w_diff_out, w_o=w_o,
                 norm_ffn=norm_ffn, w_gate_up=w_gate_up, w_down=w_down)
    depth = w_in.shape[0]
    bp, seq, _ = x_prompt.shape
    bs, dec, _ = x_sample.shape
    n_pages = page_table.shape[1]
    past = n_pages * PAGE_SIZE
    assert seq % CHUNK == 0 and CONV_W - 1 <= dec <= SUBLANES
    assert _far_bucket_is_constant(CHUNK + 1)

    ii = jnp.arange(CHUNK)[:, None]
    jj = jnp.arange(CHUNK)[None, :]
    idx_p = jnp.concatenate([_t5_bucket(k * CHUNK + ii - jj) for k in range(3)], axis=0)
    bias_h = _bias_tiles(rel_bias, idx_p, LOG2E).reshape(DIFF_KV_HEADS, DIFF_HEADS // DIFF_KV_HEADS, 3, CHUNK, CHUNK)
    bias_p = jnp.concatenate([bias_h[:, r] for r in range(DIFF_HEADS // DIFF_KV_HEADS)] * 2, axis=-2)
    tt = jnp.arange(SUBLANES)[:, None]
    idx_s = jnp.concatenate([_t5_bucket(jnp.broadcast_to(past + tt, (SUBLANES, LANES))),
                             _t5_bucket(tt + PAGE_SIZE - jj), _t5_bucket(tt - jj)], axis=1)
    bias_s = _bias_tiles(rel_bias, idx_s, 1.0)

    cos_p, sin_p = _rope_tables(jnp.arange(seq))
    cos_s, sin_s = _rope_tables(past + jnp.arange(CHUNK))

    xs = jnp.pad(x_sample, ((0, 0), (0, SUBLANES - dec), (0, 0)))
    zeros_tail = jnp.zeros((bp, CONV_TAIL, CONV_DIM), F32)
    zeros_h = jnp.zeros((bp, SSM_D, SSM_STATE), F32)
    zeros_s = jnp.zeros((bp, RET_HEADS, RET_DK, RET_DV), F32)

    vbuf_p = jnp.zeros((depth, bp * seq * DIFF_KV_HEADS, DIFF_VD), F32)
    vbuf_s = jnp.zeros((depth, bs * SUBLANES * DIFF_KV_HEADS, DIFF_VD), F32)
    kbuf_p = jnp.zeros((depth, bp, KV_COLS, seq), F32)

    yp, ys = x_prompt, xs
    outs_p, outs_s = [], []
    for l in range(depth):
        lam_init = 0.8 - 0.6 * math.exp(-0.3 * l)
        p = _layer_params(l, w_in, named)

        def attn_p(proj3, kf, vbuf, l=l, p=p, lam_init=lam_init, kbuf=kbuf_p):
            return _diff_attention_prompt(proj3, kf, vbuf, kbuf, bias_p, p, l, lam_init)

        yp, kbuf_p, vbuf_p, h1, c1, r1 = _layer_common(yp, p, l, vbuf_p, zeros_tail, zeros_h, zeros_s, cos_p, sin_p,
                                                       CHUNK, attn_p)
        outs_p.append((h1, c1, r1))

        tail_s = jnp.pad(state_conv[l], ((0, 0), (CONV_TAIL - (CONV_W - 1), 0), (0, 0)))
        h0_s = state_ssm[l].reshape(bs, SSM_D, SSM_STATE)

        def attn_s(proj3, kf, vbuf, l=l, p=p, lam_init=lam_init):
            return _diff_attention_sample(proj3, kf, vbuf, cache_k, cache_v, page_table, bias_s, p, l, dec, lam_init)

        ys_new, k2, vbuf_s, h2, c2, r2 = _layer_common(ys, p, l, vbuf_s, tail_s, h0_s, state_ret[l], cos_s, sin_s, dec,
                                                       attn_s)
        ys = jnp.where(jnp.arange(SUBLANES)[None, :, None] < dec, ys_new, 0.0)
        outs_s.append((k2[:, :dec].reshape(bs, dec, DIFF_KV_HEADS, 2, DIFF_HD), h2, c2, r2))

    stack = lambda outs, i: jnp.stack([o[i] for o in outs])
    k_prompt = kbuf_p.reshape(depth, bp, DIFF_KV_HEADS, 2, DIFF_HD, seq).transpose(0, 1, 5, 2, 3, 4)
    v_prompt = vbuf_p.reshape(depth, bp, seq, DIFF_KV_HEADS, DIFF_VD)
    v_sample = vbuf_s.reshape(depth, bs, SUBLANES, DIFF_KV_HEADS, DIFF_VD)[:, :, :dec]
    return (yp, ys[:, :dec],
            k_prompt, v_prompt, stack(outs_p, 0), stack(outs_p, 1), stack(outs_p, 2),
            stack(outs_s, 0), v_sample, stack(outs_s, 1), stack(outs_s, 2), stack(outs_s, 3))
```

```python
import functools
import math

import numpy as np
import jax
import jax.numpy as jnp
from jax import lax
from jax.experimental import pallas as pl
from jax.experimental.pallas import tpu as pltpu

F32 = jnp.float32
BF16 = jnp.bfloat16

D_MODEL = 1024
SSM_HEADS = 16
SSM_HEAD_DIM = 64
SSM_D = SSM_HEADS * SSM_HEAD_DIM
SSM_STATE = 128
SSM_GROUPS = 4
CONV_W = 4
CONV_DIM = SSM_D + 2 * SSM_GROUPS * SSM_STATE
RET_HEADS = 4
RET_DK = 128
RET_DV = 256
ROPE_BASE = 10000.0
DIFF_HEADS = 8
DIFF_KV_HEADS = 4
DIFF_HD = 64
DIFF_VD = 2 * DIFF_HD
N_BUCKETS = 32
MAX_DISTANCE = 128
N_BRANCHES = 3
D_FF = 2816
NORM_EPS = 1e-6
PAGE_SIZE = 128

CHUNK = 128
LANES = 128
SUBLANES = 8
GROUPS_PER_PASS = 4
CONV_TAIL = 16
NEG_BIG = -1e30
LOG2E = 1.4426950408889634
VMEM_LIMIT = 52 * 1024 * 1024

IN_SPLITS = (SSM_D, CONV_DIM, SSM_HEADS,
             RET_HEADS * RET_DK, RET_HEADS * RET_DK, RET_HEADS * RET_DV, RET_HEADS * RET_DV,
             DIFF_HEADS * 2 * DIFF_HD, DIFF_KV_HEADS * 2 * DIFF_HD, DIFF_KV_HEADS * DIFF_VD,
             N_BRANCHES * D_MODEL)
IN_OFFSETS = tuple(int(v) for v in np.cumsum(IN_SPLITS)[:-1])

C_XBC, C_Z, C_RQ, C_RK, C_RV, C_RG, C_DQ, C_GATE, C_MAIN = 0, 2048, 3072, 3584, 4096, 5120, 6144, 7168, 10240
TN_PROJ = 2048
KV_COLS = DIFF_KV_HEADS * 2 * DIFF_HD


def _cparams(*sem):
    return pltpu.CompilerParams(dimension_semantics=sem, vmem_limit_bytes=VMEM_LIMIT)


def _nt_dot(a, b):
    return lax.dot_general(a, b, (((1,), (1,)), ((), ())), preferred_element_type=F32)


def _tn_dot(a, b):
    return lax.dot_general(a, b, (((0,), (0,)), ((), ())), preferred_element_type=F32)


def _dot(a, b):
    return jnp.dot(a, b, preferred_element_type=F32)


def _split3(x):
    hi = x.astype(BF16)
    r1 = x - hi.astype(F32)
    mid = r1.astype(BF16)
    lo = (r1 - mid.astype(F32)).astype(BF16)
    return hi, mid, lo


def _sigmoid(x):
    return 0.5 * jnp.tanh(0.5 * x) + 0.5


def _silu(x):
    h = 0.5 * x
    return h + h * jnp.tanh(h)


def _inproj_kernel(x_ref, g_ref, w_ref, wkvdt_ref, vprev_ref, main_ref, kf_ref, vf_ref, dt_ref, xn_ref):
    del vprev_ref
    tm = x_ref.shape[0]

    @pl.when(pl.program_id(1) == 0)
    def _():
        x = x_ref[...]
        ms = jnp.mean(x * x, axis=-1, keepdims=True)
        xn = (x * lax.rsqrt(ms + NORM_EPS) * g_ref[...]).astype(BF16)
        xn_ref[...] = xn
        kvdt = _dot(xn, wkvdt_ref[...])
        kf_ref[...] = kvdt[:, :KV_COLS]
        for g in range(DIFF_KV_HEADS):
            vf_ref[pl.ds(g, tm, stride=DIFF_KV_HEADS), :] = kvdt[:, KV_COLS + g * DIFF_VD:KV_COLS + (g + 1) * DIFF_VD]
        dt_ref[...] = kvdt[:, 2 * KV_COLS:]

    main_ref[...] = _dot(xn_ref[...], w_ref[...]).astype(BF16)


def _inproj(x2d, gain, w_main, w_kvdt, tm, vbuf, layer):
    n = x2d.shape[0]
    grid = (n // tm, C_MAIN // TN_PROJ)
    return pl.pallas_call(
        _inproj_kernel,
        grid=grid,
        in_specs=[
            pl.BlockSpec((tm, D_MODEL), lambda i, j: (i, 0)),
            pl.BlockSpec((1, D_MODEL), lambda i, j: (0, 0)),
            pl.BlockSpec((D_MODEL, TN_PROJ), lambda i, j: (0, j)),
            pl.BlockSpec((D_MODEL, 2 * KV_COLS + LANES), lambda i, j: (0, 0)),
            pl.BlockSpec(memory_space=pl.ANY),
        ],
        out_specs=[
            pl.BlockSpec((tm, TN_PROJ), lambda i, j: (i, j)),
            pl.BlockSpec((tm, KV_COLS), lambda i, j: (i, 0)),
            pl.BlockSpec((None, tm * DIFF_KV_HEADS, DIFF_VD), lambda i, j: (layer, i, 0)),
            pl.BlockSpec((tm, LANES), lambda i, j: (i, 0)),
        ],
        out_shape=[
            jax.ShapeDtypeStruct((n, C_MAIN), BF16),
            jax.ShapeDtypeStruct((n, KV_COLS), F32),
            jax.ShapeDtypeStruct(vbuf.shape, F32),
            jax.ShapeDtypeStruct((n, LANES), F32),
        ],
        scratch_shapes=[pltpu.VMEM((tm, D_MODEL), BF16)],
        input_output_aliases={4: 2},
        compiler_params=_cparams("arbitrary", "arbitrary"),
        name="inproj",
    )(x2d, gain, w_main, w_kvdt, vbuf)


def _ssd_kernel(xbc_ref, z_ref, dt_ref, tail0_ref, h0_ref, cw_ref, cb_ref, dtb_ref, alog_ref, dskip_ref,
                gn_ref, ex_ref, y_ref, hout_ref, convout_ref, xext_ref, tlo_ref, xc_ref, ht_ref, *, c, nvalid, nchunks):
    L = CHUNK
    GW = SSM_D // SSM_GROUPS
    SL = 512

    @pl.when(c == 0)
    def _():
        t0 = tail0_ref[0]
        t0b = t0.astype(BF16)
        xext_ref[0:CONV_TAIL, :] = t0b
        tlo_ref[...] = (t0 - t0b.astype(F32)).astype(BF16)
        for k in range(SSM_D // LANES):
            ht_ref[:, k * LANES:(k + 1) * LANES] = h0_ref[0, k * LANES:(k + 1) * LANES, :].T

    xext_ref[CONV_TAIL:CONV_TAIL + L, :] = xbc_ref[0]
    srow = lax.broadcasted_iota(jnp.int32, (L, CONV_TAIL + L), 0)
    scol = lax.broadcasted_iota(jnp.int32, (L, CONV_TAIL + L), 1)
    shifts = [jnp.where(scol == srow + CONV_TAIL - s, 1.0, 0.0).astype(BF16) for s in range(1, CONV_W)]
    shifted = [[_dot(shifts[s - 1], xext_ref[:, sl * SL:(sl + 1) * SL]) for s in range(1, CONV_W)]
               for sl in range(CONV_DIM // SL)]
    for sl in range(CONV_DIM // SL):
        cols = slice(sl * SL, (sl + 1) * SL)
        conv = cw_ref[CONV_W - 1:CONV_W, cols] * xbc_ref[0, :, cols].astype(F32) + cb_ref[:, cols]
        for s in range(1, CONV_W):
            conv = conv + cw_ref[CONV_W - 1 - s:CONV_W - s, cols] * shifted[sl][s - 1]
        xc_ref[:, cols] = conv

    @pl.when(c == 0)
    def _():
        for sl in range(CONV_DIM // SL):
            cols = slice(sl * SL, (sl + 1) * SL)
            corr = None
            for s in range(1, CONV_W):
                t = cw_ref[CONV_W - 1 - s:CONV_W - s, cols] * _dot(shifts[s - 1][0:CONV_TAIL, 0:CONV_TAIL],
                                                                    tlo_ref[:, cols])
                corr = t if corr is None else corr + t
            xc_ref[0:CONV_TAIL, cols] += corr

    @pl.when(c == nchunks - 1)
    def _():
        a = CONV_TAIL * ((nvalid - 1) // CONV_TAIL)
        convout_ref[0] = xext_ref[CONV_TAIL + a:CONV_TAIL + a + CONV_TAIL, :].astype(F32)

    xext_ref[0:CONV_TAIL, :] = xext_ref[L:L + CONV_TAIL, :]

    row = lax.broadcasted_iota(jnp.int32, (L, L), 0)
    col = lax.broadcasted_iota(jnp.int32, (L, L), 1)
    causal = row >= col
    left = col < SSM_HEAD_DIM

    x = dt_ref[0] + dtb_ref[...]
    dt = jnp.maximum(x, 0.0) + jnp.log1p(jnp.exp(-jnp.abs(x)))
    if nvalid < L:
        dt = jnp.where(row < nvalid, dt, 0.0)
    a = -jnp.exp(alog_ref[...])
    da = dt * a
    tri = jnp.where(causal, 1.0, 0.0).astype(BF16)
    cs = sum(_dot(tri, p) for p in _split3(da))
    cs_t = cs.T
    cs_parts = _split3(cs)
    dt_parts = _split3(dt)

    NG = SSM_GROUPS
    HPG = SSM_HEADS // SSM_GROUPS
    gcs = [slice(g * GW, (g + 1) * GW) for g in range(NG)]
    bcol = lambda g: slice(SSM_D + g * SSM_STATE, SSM_D + (g + 1) * SSM_STATE)
    ccol = lambda g: slice(SSM_D + NG * SSM_STATE + g * SSM_STATE, SSM_D + NG * SSM_STATE + (g + 1) * SSM_STATE)

    def scan_groups(gs):
        csx = {g: sum(_dot(p, ex_ref[:, gcs[g]]) for p in cs_parts) for g in gs}
        dtx = {g: sum(_dot(p, ex_ref[:, gcs[g]]) for p in dt_parts) for g in gs}
        bgs = {g: _silu(xc_ref[:, bcol(g)]).astype(BF16) for g in gs}
        cgs = {g: _silu(xc_ref[:, ccol(g)]).astype(BF16) for g in gs}
        cbs = {g: _nt_dot(cgs[g], bgs[g]) for g in gs}
        htgs = {g: ht_ref[:, gcs[g]] for g in gs}
        ysts = {g: _dot(cgs[g], htgs[g].astype(BF16)) for g in gs}
        xss = {g: _silu(xc_ref[:, gcs[g]]) for g in gs}
        xdts = {g: xss[g] * dtx[g] for g in gs}
        xdt_bs = {g: xdts[g].astype(BF16) for g in gs}
        lastxs = {g: csx[g][L - 1:L, :] for g in gs}
        xdtw_bs = {g: (xdts[g] * jnp.exp(lastxs[g] - csx[g])).astype(BF16) for g in gs}
        prods = {}
        for g in gs:
            for e in range(HPG):
                h = g * HPG + e
                seg = cs[:, h:h + 1] - cs_t[h:h + 1, :]
                decay = jnp.exp(jnp.where(causal, seg, NEG_BIG))
                mat = (cbs[g] * decay).astype(BF16)
                xp = xdt_bs[g][:, (e // 2) * LANES:(e // 2 + 1) * LANES]
                keep = left if e % 2 == 0 else jnp.logical_not(left)
                prods[h] = _dot(mat, jnp.where(keep, xp, jnp.zeros_like(xp)))
        upd = {g: _tn_dot(bgs[g], xdtw_bs[g]) for g in gs}
        for g in gs:
            ht_ref[:, gcs[g]] = htgs[g] * jnp.exp(lastxs[g]) + upd[g]
        for g in gs:
            gc = gcs[g]
            y_in = jnp.concatenate([prods[g * HPG + 2 * pr] + prods[g * HPG + 2 * pr + 1] for pr in range(HPG // 2)],
                                   axis=1)
            y = y_in + ysts[g] * jnp.exp(csx[g]) + dskip_ref[:, gc] * xss[g]
            y = y * _silu(z_ref[0, :, gc].astype(F32))
            ms = jnp.mean(y * y, axis=-1, keepdims=True)
            y_ref[0, :, gc] = (y * lax.rsqrt(ms + NORM_EPS) * gn_ref[:, gc]).astype(BF16)

    for first in range(0, NG, GROUPS_PER_PASS):
        scan_groups(range(first, first + GROUPS_PER_PASS))

    @pl.when(c == nchunks - 1)
    def _():
        for k in range(SSM_D // LANES):
            hout_ref[0, k * LANES:(k + 1) * LANES, :] = ht_ref[:, k * LANES:(k + 1) * LANES].T


def _ret_kernel(q_ref, k_ref, v_ref, rg_ref, cos_ref, sin_ref, s0_ref, gn_ref, y_ref, sout_ref, *, c, ltrue):
    L = CHUNK

    @pl.when(c == 0)
    def _():
        sout_ref[...] = s0_ref[...]

    row = lax.broadcasted_iota(jnp.int32, (L, L), 0)
    col = lax.broadcasted_iota(jnp.int32, (L, L), 1)
    rel = (row - col).astype(F32)
    idx = row[:, 0:1].astype(F32)
    cosf = cos_ref[...]
    sins = sin_ref[...]
    H = RET_HEADS
    lgs = [math.log(1.0 - 2.0 ** (-5.0 - h)) for h in range(H)]
    kcs = [slice(h * RET_DK, (h + 1) * RET_DK) for h in range(H)]
    vcs = [slice(h * RET_DV, (h + 1) * RET_DV) for h in range(H)]
    qrs, krs = [], []
    for h in range(H):
        qh = q_ref[0, :, kcs[h]].astype(F32)
        kh = k_ref[0, :, kcs[h]].astype(F32)
        qrs.append(qh * cosf + pltpu.roll(qh, RET_DK // 2, 1) * sins)
        krs.append((kh * cosf + pltpu.roll(kh, RET_DK // 2, 1) * sins) * RET_DK ** -0.5)
    qr_bs = [q_.astype(BF16) for q_ in qrs]
    vhs = [v_ref[0, :, vcs[h]] for h in range(H)]
    s_olds = [sout_ref[0, h] for h in range(H)]
    scores = [_nt_dot(qr_bs[h], krs[h].astype(BF16)) for h in range(H)]
    cross = [_dot(qr_bs[h], s_olds[h].astype(BF16)) for h in range(H)]
    atts = []
    for h in range(H):
        dmat = jnp.where(rel >= 0, jnp.exp(jnp.maximum(rel, 0.0) * lgs[h]), 0.0)
        atts.append((scores[h] * dmat).astype(BF16))
    inner = [_dot(atts[h], vhs[h]) for h in range(H)]
    for h in range(H):
        k_dec = jnp.exp((ltrue - 1.0 - idx) * lgs[h])
        sout_ref[0, h] = s_olds[h] * math.exp(ltrue * lgs[h]) + _tn_dot((krs[h] * k_dec).astype(BF16), vhs[h])
    for h in range(H):
        o = inner[h] + cross[h] * jnp.exp((idx + 1.0) * lgs[h])
        oc = o - jnp.mean(o, axis=-1, keepdims=True)
        on = oc * lax.rsqrt(jnp.mean(oc * oc, axis=-1, keepdims=True) + NORM_EPS)
        y_ref[0, :, vcs[h]] = (on * gn_ref[:, vcs[h]] * _silu(rg_ref[0, :, vcs[h]].astype(F32))).astype(BF16)


N_SSD_IN, N_RET_IN, N_SSD_OUT, N_RET_OUT = 12, 8, 3, 2


CHUNKS_PER_STEP = 4


def _scan_kernel(*refs, nvalid, nchunks, per_step):
    i0 = N_SSD_IN
    i1 = i0 + N_RET_IN
    i2 = i1 + N_SSD_OUT
    i3 = i2 + N_RET_OUT
    ssd_in, ret_in, ssd_out, ret_out = refs[0:i0], refs[i0:i1], refs[i1:i2], refs[i2:i3]

    def chunk(sc, carry):
        c = pl.program_id(1) * per_step + sc
        rows = pl.ds(pl.multiple_of(sc * CHUNK, CHUNK), CHUNK)
        seq = lambda r: r.at[:, rows, :]
        tab = lambda r: r.at[rows, :]
        _ret_kernel(*[seq(r) for r in ret_in[0:4]], tab(ret_in[4]), tab(ret_in[5]), *ret_in[6:],
                    seq(ret_out[0]), ret_out[1], c=c, ltrue=float(nvalid))
        _ssd_kernel(*[seq(r) for r in ssd_in[0:3]], *ssd_in[3:], seq(ssd_out[0]), *ssd_out[1:], *refs[i3:],
                    c=c, nvalid=nvalid, nchunks=nchunks)
        return carry

    lax.fori_loop(0, per_step, chunk, 0)


def _scans(proj, dt_raw, tail0, h0, cos_t, sin_t, s0, p, nvalid):
    b, t, _ = proj.shape
    nchunks = t // CHUNK
    per_step = math.gcd(nchunks, CHUNKS_PER_STEP)
    L = CHUNK * per_step
    qk_w = RET_HEADS * RET_DK
    v_w = RET_HEADS * RET_DV
    const = lambda shape: pl.BlockSpec(shape, lambda i, c: (0,) * len(shape))
    ssd_in = [
        pl.BlockSpec((1, L, CONV_DIM), lambda i, c: (i, c, C_XBC // CONV_DIM)),
        pl.BlockSpec((1, L, SSM_D), lambda i, c: (i, c, C_Z // SSM_D)),
        pl.BlockSpec((1, L, LANES), lambda i, c: (i, c, 0)),
        pl.BlockSpec((1, CONV_TAIL, CONV_DIM), lambda i, c: (i, 0, 0)),
        pl.BlockSpec((1, SSM_D, SSM_STATE), lambda i, c: (i, 0, 0)),
        const((CONV_W, CONV_DIM)), const((1, CONV_DIM)), const((1, LANES)), const((1, LANES)),
        const((1, SSM_D)), const((1, SSM_D)), const((LANES, SSM_D)),
    ]
    ret_in = [
        pl.BlockSpec((1, L, qk_w), lambda i, c: (i, c, C_RQ // qk_w)),
        pl.BlockSpec((1, L, qk_w), lambda i, c: (i, c, C_RK // qk_w)),
        pl.BlockSpec((1, L, v_w), lambda i, c: (i, c, C_RV // v_w)),
        pl.BlockSpec((1, L, v_w), lambda i, c: (i, c, C_RG // v_w)),
        pl.BlockSpec((L, RET_DK), lambda i, c: (c, 0)),
        pl.BlockSpec((L, RET_DK), lambda i, c: (c, 0)),
        pl.BlockSpec((1, RET_HEADS, RET_DK, RET_DV), lambda i, c: (i, 0, 0, 0)),
        const((1, v_w)),
    ]
    ssd_out = [
        pl.BlockSpec((1, L, SSM_D), lambda i, c: (i, c, 0)),
        pl.BlockSpec((1, SSM_D, SSM_STATE), lambda i, c: (i, 0, 0)),
        pl.BlockSpec((1, CONV_TAIL, CONV_DIM), lambda i, c: (i, 0, 0)),
    ]
    ret_out = [
        pl.BlockSpec((1, L, v_w), lambda i, c: (i, c, 0)),
        pl.BlockSpec((1, RET_HEADS, RET_DK, RET_DV), lambda i, c: (i, 0, 0, 0)),
    ]
    assert (len(ssd_in), len(ret_in), len(ssd_out), len(ret_out)) == (N_SSD_IN, N_RET_IN, N_SSD_OUT, N_RET_OUT)
    return pl.pallas_call(
        functools.partial(_scan_kernel, nvalid=nvalid, nchunks=nchunks, per_step=per_step),
        grid=(b, nchunks // per_step),
        in_specs=ssd_in + ret_in,
        out_specs=ssd_out + ret_out,
        out_shape=[
            jax.ShapeDtypeStruct((b, t, SSM_D), BF16),
            jax.ShapeDtypeStruct((b, SSM_D, SSM_STATE), F32),
            jax.ShapeDtypeStruct((b, CONV_TAIL, CONV_DIM), F32),
            jax.ShapeDtypeStruct((b, t, v_w), BF16),
            jax.ShapeDtypeStruct((b, RET_HEADS, RET_DK, RET_DV), F32),
        ],
        scratch_shapes=[
            pltpu.VMEM((CONV_TAIL + CHUNK, CONV_DIM), BF16),
            pltpu.VMEM((CONV_TAIL, CONV_DIM), BF16),
            pltpu.VMEM((CHUNK, CONV_DIM), F32),
            pltpu.VMEM((SSM_STATE, SSM_D), F32),
        ],
        compiler_params=_cparams("arbitrary", "arbitrary"),
        name="scans",
    )(proj, proj, dt_raw, tail0, h0, p["conv_w"], p["conv_b"], p["dt_bias"], p["a_log"], p["d_skip"],
      p["ssm_norm"], p["head_expand"], proj, proj, proj, proj, cos_t, sin_t, s0, p["ret_norm"])


def _bias_kernel(tab_ref, idx_ref, out_ref, *, scale):
    h = pl.program_id(0)
    idx = idx_ref[...]
    acc = jnp.zeros(idx.shape, F32)
    for b in range(N_BUCKETS):
        acc = acc + jnp.where(idx == b, tab_ref[b * DIFF_HEADS + h], 0.0)
    out_ref[0] = acc * scale


def _bias_tiles(rel_bias, idx, scale):
    r, c = idx.shape
    return pl.pallas_call(
        functools.partial(_bias_kernel, scale=scale),
        grid=(DIFF_HEADS,),
        in_specs=[pl.BlockSpec(memory_space=pltpu.SMEM), pl.BlockSpec((r, c), lambda h: (0, 0))],
        out_specs=pl.BlockSpec((1, r, c), lambda h: (h, 0, 0)),
        out_shape=jax.ShapeDtypeStruct((DIFF_HEADS, r, c), F32),
        compiler_params=_cparams("arbitrary"),
        name="t5_bias",
    )(rel_bias.reshape(-1), idx)


def _half_rmsnorm(x, gain, bd):
    x2 = x * x
    hi = x2.astype(BF16)
    lo = (x2 - hi.astype(F32)).astype(BF16)
    ss = _dot(hi, bd) + _dot(lo, bd)
    return x * lax.rsqrt(ss * (1.0 / DIFF_HD) + NORM_EPS) * gain


def _half_rmsnorm_blocks(blocks, gain, bd):
    rows = blocks[0].shape[0]
    y = _half_rmsnorm(jnp.concatenate(blocks, axis=0), gain, bd)
    return [y[i * rows:(i + 1) * rows] for i in range(len(blocks))]


def _lambda(lamv_ref, lam_init):
    s1 = jnp.sum(lamv_ref[0:1, :] * lamv_ref[1:2, :], axis=-1, keepdims=True)
    s2 = jnp.sum(lamv_ref[2:3, :] * lamv_ref[3:4, :], axis=-1, keepdims=True)
    return jnp.exp(s1) - jnp.exp(s2) + lam_init


FAR_UNITS = 4
Q_TILES_PER_STEP = 4


def _dattn_kernel(q_ref, kf_ref, vf_ref, bias_ref, gq_ref, gk_ref, lamv_ref, sg_ref, bd_ref, kprev_ref,
                  y_ref, kout_ref, knt_ref, vx_ref, qs_ref, m_ref, acc_ref, *, t, q_tiles, lam_init):
    del kprev_ref
    step_id = pl.program_id(1)
    TQ = CHUNK
    G = DIFF_KV_HEADS
    R = DIFF_HEADS // DIFF_KV_HEADS
    NC = 2 * R * TQ
    bd = bd_ref[...]

    @pl.when(step_id == 0)
    def _():
        for i in range(t // TQ):
            r = slice(i * TQ, (i + 1) * TQ)
            kns = _half_rmsnorm_blocks([kf_ref[0, r, g * LANES:(g + 1) * LANES] for g in range(G)], gk_ref[...], bd)
            for g in range(G):
                knt = kns[g].T
                kout_ref[0, g * LANES:(g + 1) * LANES, r] = knt
                knt_ref[g, i] = knt.astype(BF16)
                vx_ref[g, r, 0:DIFF_VD] = vf_ref[0, pl.ds(i * TQ * G + g, TQ, stride=G), :].astype(BF16)
                vx_ref[g, r, DIFF_VD:] = jnp.ones((TQ, LANES), BF16)

    def tile(ti, carry):
        qi = step_id * q_tiles + ti
        qrows = pl.ds(pl.multiple_of(ti * TQ, TQ), TQ)
        lane = lax.broadcasted_iota(jnp.int32, (TQ, LANES), 1)
        left = lane < DIFF_HD
        qns = _half_rmsnorm_blocks([q_ref[0, qrows, h * LANES:(h + 1) * LANES].astype(F32)
                                    for h in range(DIFF_HEADS)], gq_ref[...], bd)
        for g in range(G):
            for r in range(R):
                qn = qns[g * R + r] * (DIFF_HD ** -0.5 * LOG2E)
                qs_ref[g, (0 * R + r) * TQ:(0 * R + r + 1) * TQ, :] = jnp.where(left, qn, 0.0).astype(BF16)
                qs_ref[g, (1 * R + r) * TQ:(1 * R + r + 1) * TQ, :] = jnp.where(left, 0.0, qn).astype(BF16)
        def step(k0, tiles, first=False):
            nunits = len(tiles)
            rows = pl.ds(pl.multiple_of(k0 * TQ, TQ), nunits * TQ)
            kts = [knt_ref[g, k0] if nunits == 1 else
                   jnp.concatenate([knt_ref[g, k0 + u] for u in range(nunits)], axis=1) for g in range(G)]
            ss = [_dot(qs_ref[g], kts[g]) for g in range(G)]
            ps, alphas = [], []
            for g in range(G):
                s = ss[g] + jnp.concatenate([bias_ref[g, tl] for tl in tiles], axis=1)
                if 0 in tiles:
                    diag0 = tiles.index(0) * TQ
                    key = lax.broadcasted_iota(jnp.int32, (NC, nunits * TQ), 1) - diag0
                    qry = lax.broadcasted_iota(jnp.int32, (NC, nunits * TQ), 0) & (TQ - 1)
                    s = jnp.where(key <= qry, s, NEG_BIG)
                row_max = jnp.max(s, axis=1, keepdims=True)
                if first:
                    m_new = jnp.broadcast_to(row_max, (NC, LANES))
                else:
                    m_old = m_ref[g]
                    m_new = jnp.maximum(m_old, row_max)
                    alphas.append(jnp.exp2(m_old - m_new))
                p = jnp.concatenate([jnp.exp2(s[:, u * TQ:(u + 1) * TQ] - m_new) for u in range(nunits)], axis=1)
                m_ref[g] = m_new
                ps.append(p.astype(BF16))
            for g in range(G):
                pv = _dot(ps[g], vx_ref[g, rows, :])
                if first:
                    acc_ref[g] = pv
                else:
                    acc_ref[g] = acc_ref[g] * jnp.concatenate([alphas[g], alphas[g]], axis=1) + pv

        @pl.when(qi >= FAR_UNITS - 1)
        def _():
            step(qi - (FAR_UNITS - 1), (2,) * (FAR_UNITS - 2) + (1, 0), first=True)

        for early in range(FAR_UNITS - 1):
            @pl.when(qi == early)
            def _(early=early):
                step(0, ((2,) * early + (1, 0))[-(early + 1):], first=True)

        nfar = jnp.maximum(qi - (FAR_UNITS - 1), 0)
        nbig = nfar // FAR_UNITS
        rem = nfar - nbig * FAR_UNITS

        def far_big(i, c):
            step(i * FAR_UNITS, (2,) * FAR_UNITS)
            return c

        lax.fori_loop(0, nbig, far_big, 0)

        @pl.when(rem >= 2)
        def _():
            step(nbig * FAR_UNITS, (2, 2))

        @pl.when((rem & 1) == 1)
        def _():
            step(nbig * FAR_UNITS + (rem & 2), (2,))

        lam = _lambda(lamv_ref, lam_init)
        for g in range(G):
            for r in range(R):
                r0 = slice((0 * R + r) * TQ, (0 * R + r + 1) * TQ)
                r1 = slice((1 * R + r) * TQ, (1 * R + r + 1) * TQ)
                o = (acc_ref[g, r0, 0:DIFF_VD] / acc_ref[g, r0, DIFF_VD:]
                     - lam * (acc_ref[g, r1, 0:DIFF_VD] / acc_ref[g, r1, DIFF_VD:]))
                o = o * lax.rsqrt(jnp.mean(o * o, axis=1, keepdims=True) + NORM_EPS)
                o = o * sg_ref[...] * (1.0 - lam_init)
                y_ref[0, qrows, (g * R + r) * LANES:(g * R + r + 1) * LANES] = o.astype(BF16)
        return carry

    lax.fori_loop(0, q_tiles, tile, 0)


def _diff_attention_prompt(proj, kf, vbuf, kbuf, bias, p, layer, lam_init):
    b, t, _ = proj.shape
    vf = vbuf.reshape(vbuf.shape[0], b, t * DIFF_KV_HEADS, DIFF_VD)
    TQ = CHUNK
    G = DIFF_KV_HEADS
    R = DIFF_HEADS // DIFF_KV_HEADS
    NC = 2 * R * TQ
    qw = DIFF_HEADS * 2 * DIFF_HD
    const = lambda shape: pl.BlockSpec(shape, lambda i, q: (0,) * len(shape))
    q_tiles = math.gcd(t // TQ, Q_TILES_PER_STEP)
    return pl.pallas_call(
        functools.partial(_dattn_kernel, t=t, q_tiles=q_tiles, lam_init=lam_init),
        grid=(b, t // (TQ * q_tiles)),
        in_specs=[
            pl.BlockSpec((1, TQ * q_tiles, qw), lambda i, q: (i, q, C_DQ // qw)),
            pl.BlockSpec((1, t, KV_COLS), lambda i, q: (i, 0, 0)),
            pl.BlockSpec((None, 1, t * G, DIFF_VD), lambda i, q: (layer, i, 0, 0)),
            const((G, 3, NC, TQ)),
            const((1, LANES)), const((1, LANES)), const((4, LANES)), const((1, DIFF_VD)), const((LANES, LANES)),
            pl.BlockSpec(memory_space=pl.ANY),
        ],
        out_specs=[
            pl.BlockSpec((1, TQ * q_tiles, qw), lambda i, q: (i, q, 0)),
            pl.BlockSpec((None, 1, KV_COLS, t), lambda i, q: (layer, i, 0, 0)),
        ],
        out_shape=[
            jax.ShapeDtypeStruct((b, t, DIFF_HEADS * DIFF_VD), BF16),
            jax.ShapeDtypeStruct(kbuf.shape, F32),
        ],
        input_output_aliases={9: 1},
        scratch_shapes=[
            pltpu.VMEM((G, t // TQ, LANES, TQ), BF16), pltpu.VMEM((G, t, DIFF_VD + LANES), BF16),
            pltpu.VMEM((G, NC, LANES), BF16),
            pltpu.VMEM((G, NC, LANES), F32),
            pltpu.VMEM((G, NC, DIFF_VD + LANES), F32),
        ],
        compiler_params=_cparams("arbitrary", "arbitrary"),
        name="diff_attn_prompt",
    )(proj, kf, vf, bias, p["qk_norm_q"], p["qk_norm_k"], p["lamv"], p["diff_norm"], p["blockdiag"], kbuf)


PAGES_PER_STEP = 32


def _sattn_kernel(pt_ref, *refs, layer, nvalid, lam_init, npp):
    del pt_ref, layer
    k_refs = refs[0:npp]
    v_refs = refs[npp:2 * npp]
    (q_ref, kf_ref, vf_ref, bias_ref, gq_ref, gk_ref, lamv_ref, sg_ref, bd_ref,
     y_ref, kout_ref, qs_ref, knew_ref, vnew_ref, m_ref, l_ref, acc_ref) = refs[2 * npp:]
    s_id = pl.program_id(1)
    nsteps = pl.num_programs(1)
    G = DIFF_KV_HEADS
    R = DIFF_HEADS // DIFF_KV_HEADS
    TP = SUBLANES
    MR = 2 * R * TP
    bd = bd_ref[...]
    lane = lax.broadcasted_iota(jnp.int32, (TP, LANES), 1)
    left = lane < DIFF_HD

    @pl.when(s_id == 0)
    def _():
        knew_ref[...] = jnp.zeros(knew_ref.shape, BF16)
        vnew_ref[...] = jnp.zeros(vnew_ref.shape, BF16)
        for g in range(G):
            gc = slice(g * LANES, (g + 1) * LANES)
            kn = _half_rmsnorm(kf_ref[0, :, gc], gk_ref[...], bd)
            kout_ref[0, :, gc] = kn
            knew_ref[g, 0:TP, :] = kn.astype(BF16)
            vnew_ref[g, 0:TP, :] = vf_ref[0, pl.ds(g, TP, stride=G), :].astype(BF16)
            for r in range(R):
                hc = slice((g * R + r) * LANES, (g * R + r + 1) * LANES)
                qn = _half_rmsnorm(q_ref[0, :, hc].astype(F32), gq_ref[...], bd) * DIFF_HD ** -0.5
                qs_ref[g, (0 * R + r) * TP:(0 * R + r + 1) * TP, :] = jnp.where(left, qn, 0.0).astype(BF16)
                qs_ref[g, (1 * R + r) * TP:(1 * R + r + 1) * TP, :] = jnp.where(left, 0.0, qn).astype(BF16)
        m_ref[...] = jnp.full(m_ref.shape, NEG_BIG, F32)
        l_ref[...] = jnp.zeros(l_ref.shape, F32)
        acc_ref[...] = jnp.zeros(acc_ref.shape, F32)

    def bias_rows(g, seg):
        per_head = [bias_ref[g * R + r, :, seg * LANES:(seg + 1) * LANES] for r in range(R)]
        return jnp.concatenate(per_head + per_head, axis=0)

    def update(g, s, vs):
        m_old = m_ref[g]
        m_new = jnp.maximum(m_old, jnp.max(s, axis=-1, keepdims=True))
        alpha = jnp.exp(m_old - m_new)
        p = jnp.exp(s - m_new)
        l_ref[g] = alpha * l_ref[g] + jnp.sum(p, axis=-1, keepdims=True)
        m_ref[g] = m_new
        pb = p.astype(BF16)
        pv = _dot(pb[:, 0:LANES], vs[0])
        for i in range(1, len(vs)):
            pv = pv + _dot(pb[:, i * LANES:(i + 1) * LANES], vs[i])
        acc_ref[g] = acc_ref[g] * alpha + pv

    def pages(last):
        scores = []
        for g in range(G):
            gc = slice(g * LANES, (g + 1) * LANES)
            kcat = jnp.concatenate([k_refs[i][gc, :].astype(BF16) for i in range(npp)], axis=1)
            far = bias_rows(g, 0)
            near = bias_rows(g, 1) if last else far
            scores.append(_dot(qs_ref[g], kcat) + jnp.concatenate([far] * (npp - 1) + [near], axis=1))
        pvs, alphas = [], []
        for g in range(G):
            s = scores[g]
            m_old = m_ref[g]
            m_new = jnp.maximum(m_old, jnp.max(s, axis=-1, keepdims=True))
            alpha = jnp.exp(m_old - m_new)
            p = jnp.exp(s - m_new)
            l_ref[g] = alpha * l_ref[g] + jnp.sum(p, axis=-1, keepdims=True)
            m_ref[g] = m_new
            vcat = jnp.concatenate([v_refs[i][pl.ds(g, PAGE_SIZE, stride=G), :].astype(BF16) for i in range(npp)],
                                   axis=0)
            pvs.append(_dot(p.astype(BF16), vcat))
            alphas.append(alpha)
        for g in range(G):
            acc_ref[g] = acc_ref[g] * alphas[g] + pvs[g]

    @pl.when(s_id < nsteps - 1)
    def _():
        pages(False)

    @pl.when(s_id == nsteps - 1)
    def _():
        pages(True)
        rowt = lax.broadcasted_iota(jnp.int32, (MR, LANES), 0) % TP
        colj = lax.broadcasted_iota(jnp.int32, (MR, LANES), 1)
        ok = jnp.logical_and(colj <= rowt, colj < nvalid)
        lam = _lambda(lamv_ref, lam_init)
        for g in range(G):
            sc = _nt_dot(qs_ref[g], knew_ref[g]) + bias_rows(g, 2)
            update(g, jnp.where(ok, sc, NEG_BIG), [vnew_ref[g]])
            acc = acc_ref[g] / l_ref[g]
            for r in range(R):
                o = acc[(0 * R + r) * TP:(0 * R + r + 1) * TP, :] - lam * acc[(1 * R + r) * TP:(1 * R + r + 1) * TP, :]
                o = o * lax.rsqrt(jnp.mean(o * o, axis=-1, keepdims=True) + NORM_EPS)
                hc = slice((g * R + r) * LANES, (g * R + r + 1) * LANES)
                y_ref[0, :, hc] = (o * sg_ref[...] * (1.0 - lam_init)).astype(BF16)


def _diff_attention_sample(proj, kf, vbuf, cache_k, cache_v, page_table, bias, p, layer, nvalid, lam_init):
    b, tp, _ = proj.shape
    vf = vbuf.reshape(vbuf.shape[0], b, tp * DIFF_KV_HEADS, DIFF_VD)
    n_pages = page_table.shape[1]
    npp = PAGES_PER_STEP
    while n_pages % npp:
        npp //= 2
    nsteps = n_pages // npp
    G = DIFF_KV_HEADS
    R = DIFF_HEADS // DIFF_KV_HEADS
    MR = 2 * R * tp
    ck = jnp.transpose(cache_k, (0, 1, 3, 4, 5, 2)).reshape(cache_k.shape[0], cache_k.shape[1], KV_COLS, PAGE_SIZE)
    cv = cache_v.reshape(cache_v.shape[0], cache_v.shape[1], PAGE_SIZE * DIFF_KV_HEADS, DIFF_VD)

    def page_spec(i):
        return pl.BlockSpec((None, None, KV_COLS, PAGE_SIZE),
                            lambda bi, s, pt: (layer, pt[bi * n_pages + s * npp + i], 0, 0))

    const = lambda shape: pl.BlockSpec(shape, lambda bi, s, pt: (0,) * len(shape))
    grid_spec = pltpu.PrefetchScalarGridSpec(
        num_scalar_prefetch=1,
        grid=(b, nsteps),
        in_specs=[page_spec(i) for i in range(npp)] + [page_spec(i) for i in range(npp)] + [
            pl.BlockSpec((1, tp, DIFF_HEADS * 2 * DIFF_HD), lambda bi, s, pt: (bi, 0, C_DQ // (DIFF_HEADS * 2 * DIFF_HD))),
            pl.BlockSpec((1, tp, KV_COLS), lambda bi, s, pt: (bi, 0, 0)),
            pl.BlockSpec((None, 1, tp * G, DIFF_VD), lambda bi, s, pt: (layer, bi, 0, 0)),
            const((DIFF_HEADS, tp, 3 * LANES)),
            const((1, LANES)), const((1, LANES)), const((4, LANES)), const((1, LANES)), const((LANES, LANES)),
        ],
        out_specs=[
            pl.BlockSpec((1, tp, DIFF_HEADS * DIFF_VD), lambda bi, s, pt: (bi, 0, 0)),
            pl.BlockSpec((1, tp, KV_COLS), lambda bi, s, pt: (bi, 0, 0)),
        ],
        scratch_shapes=[
            pltpu.VMEM((G, MR, LANES), BF16),
            pltpu.VMEM((G, PAGE_SIZE, LANES), BF16), pltpu.VMEM((G, PAGE_SIZE, LANES), BF16),
            pltpu.VMEM((G, MR, 1), F32), pltpu.VMEM((G, MR, 1), F32), pltpu.VMEM((G, MR, LANES), F32),
        ],
    )
    return pl.pallas_call(
        functools.partial(_sattn_kernel, layer=layer, nvalid=nvalid, lam_init=lam_init, npp=npp),
        grid_spec=grid_spec,
        out_shape=[
            jax.ShapeDtypeStruct((b, tp, DIFF_HEADS * DIFF_VD), BF16),
            jax.ShapeDtypeStruct((b, tp, KV_COLS), F32),
        ],
        compiler_params=_cparams("arbitrary", "arbitrary"),
        name="diff_attn_sample",
    )(page_table.reshape(-1), *([ck] * npp), *([cv] * npp), proj, kf, vf, bias,
      p["qk_norm_q"], p["qk_norm_k"], p["lamv"], p["diff_norm"], p["blockdiag"])


def _merge_kernel(ys_ref, yr_ref, yd_ref, g0_ref, g1_ref, g2_ref, x_ref, ws_ref, wr_ref, wd_ref, wo_ref,
                  bg_ref, h_ref):
    merged = None
    for i, (y_ref, w_ref, g_ref) in enumerate(((ys_ref, ws_ref, g0_ref), (yr_ref, wr_ref, g1_ref),
                                               (yd_ref, wd_ref, g2_ref))):
        br = _dot(y_ref[...], w_ref[...])
        t = _sigmoid(g_ref[...].astype(F32) + bg_ref[i:i + 1, :]) * br
        merged = t if merged is None else merged + t
    h_ref[...] = x_ref[...] + _dot(merged.astype(BF16), wo_ref[...])


def _merge(y_ssm, y_ret, y_diff, proj2d, x2d, p, tm):
    n = x2d.shape[0]
    tok = lambda cb: pl.BlockSpec((tm, D_MODEL), lambda i: (i, cb))
    wspec = pl.BlockSpec((D_MODEL, D_MODEL), lambda i: (0, 0))
    g0 = C_GATE // D_MODEL
    return pl.pallas_call(
        _merge_kernel,
        grid=(n // tm,),
        in_specs=[tok(0), tok(0), tok(0), tok(g0), tok(g0 + 1), tok(g0 + 2), tok(0),
                  wspec, wspec, wspec, wspec, pl.BlockSpec((N_BRANCHES, D_MODEL), lambda i: (0, 0))],
        out_specs=tok(0),
        out_shape=jax.ShapeDtypeStruct((n, D_MODEL), F32),
        compiler_params=_cparams("arbitrary"),
        name="merge",
    )(y_ssm, y_ret, y_diff, proj2d, proj2d, proj2d, x2d, p["w_ssm_out"], p["w_ret_out"], p["w_diff_out"],
      p["w_o"], p["b_gate"])


def _ffn_kernel(h_ref, g_ref, wg_ref, wu_ref, wd_ref, y_ref):
    h = h_ref[...]
    ms = jnp.mean(h * h, axis=-1, keepdims=True)
    hn = (h * lax.rsqrt(ms + NORM_EPS) * g_ref[...]).astype(BF16)
    act = _silu(_dot(hn, wg_ref[...])) * _dot(hn, wu_ref[...])
    y_ref[...] = h + _dot(act.astype(BF16), wd_ref[...])


def _ffn(h2d, p, tm):
    n = h2d.shape[0]
    resident = lambda shape, cb: pl.BlockSpec(shape, lambda i: (0, cb), pipeline_mode=pl.Buffered(1))
    return pl.pallas_call(
        _ffn_kernel,
        grid=(n // tm,),
        in_specs=[
            pl.BlockSpec((tm, D_MODEL), lambda i: (i, 0)),
            pl.BlockSpec((1, D_MODEL), lambda i: (0, 0)),
            resident((D_MODEL, D_FF), 0),
            resident((D_MODEL, D_FF), 1),
            resident((D_FF, D_MODEL), 0),
        ],
        out_specs=pl.BlockSpec((tm, D_MODEL), lambda i: (i, 0)),
        out_shape=jax.ShapeDtypeStruct((n, D_MODEL), F32),
        compiler_params=_cparams("arbitrary"),
        name="ffn",
    )(h2d, p["norm_ffn"], p["w_gate_up"], p["w_gate_up"], p["w_down"])


def _t5_bucket(dist):
    n = jnp.maximum(dist, 0)
    max_exact = N_BUCKETS // 2
    large = max_exact + (jnp.log(jnp.maximum(n, 1).astype(F32) / max_exact)
                         / math.log(MAX_DISTANCE / max_exact) * (N_BUCKETS - max_exact)).astype(jnp.int32)
    large = jnp.minimum(large, N_BUCKETS - 1)
    return jnp.where(n < max_exact, n, large)


def _far_bucket_is_constant(min_dist):
    max_exact = N_BUCKETS // 2
    d = np.float32(min_dist)
    large = max_exact + int(np.float32(np.log(d / np.float32(max_exact))) / np.float32(math.log(MAX_DISTANCE / max_exact))
                            * (N_BUCKETS - max_exact))
    return min_dist >= max_exact and large >= N_BUCKETS - 1


def _rope_tables(pos):
    half = RET_DK // 2
    inv = 1.0 / (ROPE_BASE ** (jnp.arange(half, dtype=F32) / half))
    ang = pos.astype(F32)[:, None] * inv[None, :]
    cos, sin = jnp.cos(ang), jnp.sin(ang)
    return jnp.concatenate([cos, cos], axis=1), jnp.concatenate([-sin, sin], axis=1)


def _layer_params(l, w_in, named):
    p = {k: v[l] for k, v in named.items()}
    w = w_in[l]
    o = (0,) + IN_OFFSETS + (w.shape[1],)
    z, xbc, dt, rq, rk, rv, rg, dq, dk, dv, gates = [w[:, o[i]:o[i + 1]] for i in range(len(IN_SPLITS))]
    out = {}
    out["w_main"] = jnp.concatenate([xbc, z, rq, rk, rv, rg, dq, gates], axis=1).astype(BF16)
    out["w_kvdt"] = jnp.concatenate([dk, dv, jnp.pad(dt, ((0, 0), (0, LANES - SSM_HEADS)))], axis=1).astype(BF16)
    out["norm_mix"] = p["norm_mix"].reshape(1, D_MODEL)
    out["conv_w"] = p["conv_w"]
    out["conv_b"] = p["conv_b"].reshape(1, CONV_DIM)
    out["dt_bias"] = jnp.pad(p["dt_bias"], (0, LANES - SSM_HEADS)).reshape(1, LANES)
    out["a_log"] = jnp.pad(p["a_log"], (0, LANES - SSM_HEADS)).reshape(1, LANES)
    out["d_skip"] = jnp.repeat(p["d_skip"], SSM_HEAD_DIM).reshape(1, SSM_D)
    out["ssm_norm"] = p["ssm_norm"].reshape(1, SSM_D)
    head_of_channel = np.arange(SSM_D) // SSM_HEAD_DIM
    out["head_expand"] = jnp.asarray(np.arange(LANES)[:, None] == head_of_channel[None, :], dtype=BF16)
    out["ret_norm"] = p["ret_norm"].reshape(1, RET_HEADS * RET_DV)
    out["qk_norm_q"] = jnp.tile(p["qk_norm_q"], 2).reshape(1, LANES)
    out["qk_norm_k"] = jnp.tile(p["qk_norm_k"], 2).reshape(1, LANES)
    lamv = jnp.stack([p["lambda_q1"], p["lambda_k1"], p["lambda_q2"], p["lambda_k2"]])
    out["lamv"] = jnp.pad(lamv, ((0, 0), (0, LANES - DIFF_HD)))
    out["diff_norm"] = p["diff_norm"].reshape(1, DIFF_VD)
    half = np.arange(LANES) // DIFF_HD
    out["blockdiag"] = jnp.asarray(half[:, None] == half[None, :], dtype=BF16)
    for k in ("w_ssm_out", "w_ret_out", "w_diff_out", "w_o", "w_gate_up", "w_down"):
        out[k] = p[k].astype(BF16)
    out["b_gate"] = p["b_gate"]
    out["norm_ffn"] = p["norm_ffn"].reshape(1, D_MODEL)
    return out


def _token_tile(n, cap):
    tm = min(n, cap)
    while n % tm:
        tm //= 2
    return tm


def _layer_common(x, p, layer, vbuf, tail0, h0, s0, cos_t, sin_t, nvalid, attn_fn):
    b, tp, _ = x.shape
    n = b * tp
    x2d = x.reshape(n, D_MODEL)
    proj, kf, vbuf, dt_raw = _inproj(x2d, p["norm_mix"], p["w_main"], p["w_kvdt"], _token_tile(n, 1024), vbuf, layer)
    proj3 = proj.reshape(b, tp, C_MAIN)
    dt3 = dt_raw.reshape(b, tp, LANES)
    tpad = -tp % CHUNK
    scan_in = lambda a: jnp.pad(a, ((0, 0), (0, tpad), (0, 0))) if tpad else a
    scan_proj = scan_in(proj3[:, :, :C_DQ]) if tpad else proj3
    y_ssm, h_new, conv_rows, y_ret, s_new = _scans(scan_proj, scan_in(dt3), tail0, h0, cos_t, sin_t, s0, p,
                                                   nvalid)
    y_diff, k_new = attn_fn(proj3, kf.reshape(b, tp, KV_COLS), vbuf)
    tm = _token_tile(n, 512)
    h = _merge(y_ssm[:, :tp].reshape(n, -1), y_ret[:, :tp].reshape(n, -1), y_diff.reshape(n, -1), proj, x2d, p, tm)
    y = _ffn(h, p, tm)
    last = (nvalid - 1) % CONV_TAIL
    assert last >= CONV_W - 2
    return (y.reshape(b, tp, D_MODEL), k_new, vbuf,
            h_new.reshape(b, SSM_HEADS, SSM_HEAD_DIM, SSM_STATE), conv_rows[:, last - (CONV_W - 2):last + 1, :], s_new)


def kernel(x_prompt, x_sample, cache_k, cache_v, page_table, state_ssm, state_conv, state_ret, rel_bias, norm_mix, w_in, b_gate, conv_w, conv_b, dt_bias, a_log, d_skip, ssm_norm, w_ssm_out, ret_norm, w_ret_out, qk_norm_q, qk_norm_k, lambda_q1, lambda_k1, lambda_q2, lambda_k2, diff_norm, w_diff_out, w_o, norm_ffn, w_gate_up, w_down):
    named = dict(norm_mix=norm_mix, b_gate=b_gate, conv_w=conv_w, conv_b=conv_b, dt_bias=dt_bias, a_log=a_log,
                 d_skip=d_skip, ssm_norm=ssm_norm, w_ssm_out=w_ssm_out, ret_norm=ret_norm, w_ret_out=w_ret_out,
                 qk_norm_q=qk_norm_q, qk_norm_k=qk_norm_k, lambda_q1=lambda_q1, lambda_k1=lambda_k1,
                 lambda_q2=lambda_q2, lambda_k2=lambda_k2, diff_norm=diff_norm, w_diff_out=w_diff_out, w_o=w_o,
                 norm_ffn=norm_ffn, w_gate_up=w_gate_up, w_down=w_down)
    depth = w_in.shape[0]
    bp, seq, _ = x_prompt.shape
    bs, dec, _ = x_sample.shape
    n_pages = page_table.shape[1]
    past = n_pages * PAGE_SIZE
    assert seq % CHUNK == 0 and CONV_W - 1 <= dec <= SUBLANES
    assert _far_bucket_is_constant(CHUNK + 1)

    ii = jnp.arange(CHUNK)[:, None]
    jj = jnp.arange(CHUNK)[None, :]
    idx_p = jnp.concatenate([_t5_bucket(k * CHUNK + ii - jj) for k in range(3)], axis=0)
    bias_h = _bias_tiles(rel_bias, idx_p, LOG2E).reshape(DIFF_KV_HEADS, DIFF_HEADS // DIFF_KV_HEADS, 3, CHUNK, CHUNK)
    bias_p = jnp.concatenate([bias_h[:, r] for r in range(DIFF_HEADS // DIFF_KV_HEADS)] * 2, axis=-2)
    tt = jnp.arange(SUBLANES)[:, None]
    idx_s = jnp.concatenate([_t5_bucket(jnp.broadcast_to(past + tt, (SUBLANES, LANES))),
                             _t5_bucket(tt + PAGE_SIZE - jj), _t5_bucket(tt - jj)], axis=1)
    bias_s = _bias_tiles(rel_bias, idx_s, 1.0)

    cos_p, sin_p = _rope_tables(jnp.arange(seq))
    cos_s, sin_s = _rope_tables(past + jnp.arange(CHUNK))

    xs = jnp.pad(x_sample, ((0, 0), (0, SUBLANES - dec), (0, 0)))
    zeros_tail = jnp.zeros((bp, CONV_TAIL, CONV_DIM), F32)
    zeros_h = jnp.zeros((bp, SSM_D, SSM_STATE), F32)
    zeros_s = jnp.zeros((bp, RET_HEADS, RET_DK, RET_DV), F32)

    vbuf_p = jnp.zeros((depth, bp * seq * DIFF_KV_HEADS, DIFF_VD), F32)
    vbuf_s = jnp.zeros((depth, bs * SUBLANES * DIFF_KV_HEADS, DIFF_VD), F32)
    kbuf_p = jnp.zeros((depth, bp, KV_COLS, seq), F32)

    yp, ys = x_prompt, xs
    outs_p, outs_s = [], []
    for l in range(depth):
        lam_init = 0.8 - 0.6 * math.exp(-0.3 * l)
        p = _layer_params(l, w_in, named)

        def attn_p(proj3, kf, vbuf, l=l, p=p, lam_init=lam_init, kbuf=kbuf_p):
            return _diff_attention_prompt(proj3, kf, vbuf, kbuf, bias_p, p, l, lam_init)

        yp, kbuf_p, vbuf_p, h1, c1, r1 = _layer_common(yp, p, l, vbuf_p, zeros_tail, zeros_h, zeros_s, cos_p, sin_p,
                                                       CHUNK, attn_p)
        outs_p.append((h1, c1, r1))

        tail_s = jnp.pad(state_conv[l], ((0, 0), (CONV_TAIL - (CONV_W - 1), 0), (0, 0)))
        h0_s = state_ssm[l].reshape(bs, SSM_D, SSM_STATE)

        def attn_s(proj3, kf, vbuf, l=l, p=p, lam_init=lam_init):
            return _diff_attention_sample(proj3, kf, vbuf, cache_k, cache_v, page_table, bias_s, p, l, dec, lam_init)

        ys_new, k2, vbuf_s, h2, c2, r2 = _layer_common(ys, p, l, vbuf_s, tail_s, h0_s, state_ret[l], cos_s, sin_s, dec,
                                                       attn_s)
        ys = jnp.where(jnp.arange(SUBLANES)[None, :, None] < dec, ys_new, 0.0)
        outs_s.append((k2[:, :dec].reshape(bs, dec, DIFF_KV_HEADS, 2, DIFF_HD), h2, c2, r2))

    stack = lambda outs, i: jnp.stack([o[i] for o in outs])
    k_prompt = kbuf_p.reshape(depth, bp, DIFF_KV_HEADS, 2, DIFF_HD, seq).transpose(0, 1, 5, 2, 3, 4)
    v_prompt = vbuf_p.reshape(depth, bp, seq, DIFF_KV_HEADS, DIFF_VD)
    v_sample = vbuf_s.reshape(depth, bs, SUBLANES, DIFF_KV_HEADS, DIFF_VD)[:, :, :dec]
    return (yp, ys[:, :dec],
            k_prompt, v_prompt, stack(outs_p, 0), stack(outs_p, 1), stack(outs_p, 2),
            stack(outs_s, 0), v_sample, stack(outs_s, 1), stack(outs_s, 2), stack(outs_s, 3))
```

```python
import functools
import math

import numpy as np
import jax
import jax.numpy as jnp
from jax import lax
from jax.experimental import pallas as pl
from jax.experimental.pallas import tpu as pltpu

F32 = jnp.float32
BF16 = jnp.bfloat16

D_MODEL = 1024
SSM_HEADS = 16
SSM_HEAD_DIM = 64
SSM_D = SSM_HEADS * SSM_HEAD_DIM
SSM_STATE = 128
SSM_GROUPS = 4
CONV_W = 4
CONV_DIM = SSM_D + 2 * SSM_GROUPS * SSM_STATE
RET_HEADS = 4
RET_DK = 128
RET_DV = 256
ROPE_BASE = 10000.0
DIFF_HEADS = 8
DIFF_KV_HEADS = 4
DIFF_HD = 64
DIFF_VD = 2 * DIFF_HD
N_BUCKETS = 32
MAX_DISTANCE = 128
N_BRANCHES = 3
D_FF = 2816
NORM_EPS = 1e-6
PAGE_SIZE = 128

CHUNK = 128
LANES = 128
SUBLANES = 8
GROUPS_PER_PASS = 4
CONV_TAIL = 16
NEG_BIG = -1e30
LOG2E = 1.4426950408889634
VMEM_LIMIT = 52 * 1024 * 1024

IN_SPLITS = (SSM_D, CONV_DIM, SSM_HEADS,
             RET_HEADS * RET_DK, RET_HEADS * RET_DK, RET_HEADS * RET_DV, RET_HEADS * RET_DV,
             DIFF_HEADS * 2 * DIFF_HD, DIFF_KV_HEADS * 2 * DIFF_HD, DIFF_KV_HEADS * DIFF_VD,
             N_BRANCHES * D_MODEL)
IN_OFFSETS = tuple(int(v) for v in np.cumsum(IN_SPLITS)[:-1])

C_XBC, C_Z, C_RQ, C_RK, C_RV, C_RG, C_DQ, C_GATE, C_MAIN = 0, 2048, 3072, 3584, 4096, 5120, 6144, 7168, 10240
TN_PROJ = 2048
KV_COLS = DIFF_KV_HEADS * 2 * DIFF_HD


def _cparams(*sem):
    return pltpu.CompilerParams(dimension_semantics=sem, vmem_limit_bytes=VMEM_LIMIT)


def _nt_dot(a, b):
    return lax.dot_general(a, b, (((1,), (1,)), ((), ())), preferred_element_type=F32)


def _tn_dot(a, b):
    return lax.dot_general(a, b, (((0,), (0,)), ((), ())), preferred_element_type=F32)


def _dot(a, b):
    return jnp.dot(a, b, preferred_element_type=F32)


def _split3(x):
    hi = x.astype(BF16)
    r1 = x - hi.astype(F32)
    mid = r1.astype(BF16)
    lo = (r1 - mid.astype(F32)).astype(BF16)
    return hi, mid, lo


def _sigmoid(x):
    return 0.5 * jnp.tanh(0.5 * x) + 0.5


def _silu(x):
    h = 0.5 * x
    return h + h * jnp.tanh(h)


def _inproj_kernel(x_ref, g_ref, w_ref, wkvdt_ref, vprev_ref, main_ref, kf_ref, vf_ref, dt_ref, xn_ref):
    del vprev_ref
    tm = x_ref.shape[0]

    @pl.when(pl.program_id(1) == 0)
    def _():
        x = x_ref[...]
        ms = jnp.mean(x * x, axis=-1, keepdims=True)
        xn = (x * lax.rsqrt(ms + NORM_EPS) * g_ref[...]).astype(BF16)
        xn_ref[...] = xn
        kvdt = _dot(xn, wkvdt_ref[...])
        kf_ref[...] = kvdt[:, :KV_COLS]
        for g in range(DIFF_KV_HEADS):
            vf_ref[pl.ds(g, tm, stride=DIFF_KV_HEADS), :] = kvdt[:, KV_COLS + g * DIFF_VD:KV_COLS + (g + 1) * DIFF_VD]
        dt_ref[...] = kvdt[:, 2 * KV_COLS:]

    main_ref[...] = _dot(xn_ref[...], w_ref[...]).astype(BF16)


def _inproj(x2d, gain, w_main, w_kvdt, tm, vbuf, layer):
    n = x2d.shape[0]
    grid = (n // tm, C_MAIN // TN_PROJ)
    return pl.pallas_call(
        _inproj_kernel,
        grid=grid,
        in_specs=[
            pl.BlockSpec((tm, D_MODEL), lambda i, j: (i, 0)),
            pl.BlockSpec((1, D_MODEL), lambda i, j: (0, 0)),
            pl.BlockSpec((D_MODEL, TN_PROJ), lambda i, j: (0, j)),
            pl.BlockSpec((D_MODEL, 2 * KV_COLS + LANES), lambda i, j: (0, 0)),
            pl.BlockSpec(memory_space=pl.ANY),
        ],
        out_specs=[
            pl.BlockSpec((tm, TN_PROJ), lambda i, j: (i, j)),
            pl.BlockSpec((tm, KV_COLS), lambda i, j: (i, 0)),
            pl.BlockSpec((None, tm * DIFF_KV_HEADS, DIFF_VD), lambda i, j: (layer, i, 0)),
            pl.BlockSpec((tm, LANES), lambda i, j: (i, 0)),
        ],
        out_shape=[
            jax.ShapeDtypeStruct((n, C_MAIN), BF16),
            jax.ShapeDtypeStruct((n, KV_COLS), F32),
            jax.ShapeDtypeStruct(vbuf.shape, F32),
            jax.ShapeDtypeStruct((n, LANES), F32),
        ],
        scratch_shapes=[pltpu.VMEM((tm, D_MODEL), BF16)],
        input_output_aliases={4: 2},
        compiler_params=_cparams("arbitrary", "arbitrary"),
        name="inproj",
    )(x2d, gain, w_main, w_kvdt, vbuf)


def _ssd_kernel(xbc_ref, z_ref, dt_ref, tail0_ref, h0_ref, cw_ref, cb_ref, dtb_ref, alog_ref, dskip_ref,
                gn_ref, ex_ref, y_ref, hout_ref, convout_ref, xext_ref, tlo_ref, xc_ref, ht_ref, *, c, nvalid, nchunks):
    L = CHUNK
    GW = SSM_D // SSM_GROUPS
    SL = 512

    @pl.when(c == 0)
    def _():
        t0 = tail0_ref[0]
        t0b = t0.astype(BF16)
        xext_ref[0:CONV_TAIL, :] = t0b
        tlo_ref[...] = (t0 - t0b.astype(F32)).astype(BF16)
        for k in range(SSM_D // LANES):
            ht_ref[:, k * LANES:(k + 1) * LANES] = h0_ref[0, k * LANES:(k + 1) * LANES, :].T

    xext_ref[CONV_TAIL:CONV_TAIL + L, :] = xbc_ref[0]
    srow = lax.broadcasted_iota(jnp.int32, (L, CONV_TAIL + L), 0)
    scol = lax.broadcasted_iota(jnp.int32, (L, CONV_TAIL + L), 1)
    shifts = [jnp.where(scol == srow + CONV_TAIL - s, 1.0, 0.0).astype(BF16) for s in range(1, CONV_W)]
    shifted = [[_dot(shifts[s - 1], xext_ref[:, sl * SL:(sl + 1) * SL]) for s in range(1, CONV_W)]
               for sl in range(CONV_DIM // SL)]
    for sl in range(CONV_DIM // SL):
        cols = slice(sl * SL, (sl + 1) * SL)
        conv = cw_ref[CONV_W - 1:CONV_W, cols] * xbc_ref[0, :, cols].astype(F32) + cb_ref[:, cols]
        for s in range(1, CONV_W):
            conv = conv + cw_ref[CONV_W - 1 - s:CONV_W - s, cols] * shifted[sl][s - 1]
        xc_ref[:, cols] = conv

    @pl.when(c == 0)
    def _():
        for sl in range(CONV_DIM // SL):
            cols = slice(sl * SL, (sl + 1) * SL)
            corr = None
            for s in range(1, CONV_W):
                t = cw_ref[CONV_W - 1 - s:CONV_W - s, cols] * _dot(shifts[s - 1][0:CONV_TAIL, 0:CONV_TAIL],
                                                                    tlo_ref[:, cols])
                corr = t if corr is None else corr + t
            xc_ref[0:CONV_TAIL, cols] += corr

    @pl.when(c == nchunks - 1)
    def _():
        a = CONV_TAIL * ((nvalid - 1) // CONV_TAIL)
        convout_ref[0] = xext_ref[CONV_TAIL + a:CONV_TAIL + a + CONV_TAIL, :].astype(F32)

    xext_ref[0:CONV_TAIL, :] = xext_ref[L:L + CONV_TAIL, :]

    row = lax.broadcasted_iota(jnp.int32, (L, L), 0)
    col = lax.broadcasted_iota(jnp.int32, (L, L), 1)
    causal = row >= col
    left = col < SSM_HEAD_DIM

    x = dt_ref[0] + dtb_ref[...]
    dt = jnp.maximum(x, 0.0) + jnp.log1p(jnp.exp(-jnp.abs(x)))
    if nvalid < L:
        dt = jnp.where(row < nvalid, dt, 0.0)
    a = -jnp.exp(alog_ref[...])
    da = dt * a
    tri = jnp.where(causal, 1.0, 0.0).astype(BF16)
    cs = sum(_dot(tri, p) for p in _split3(da))
    cs_t = cs.T
    cs_parts = _split3(cs)
    dt_parts = _split3(dt)

    NG = SSM_GROUPS
    HPG = SSM_HEADS // SSM_GROUPS
    gcs = [slice(g * GW, (g + 1) * GW) for g in range(NG)]
    bcol = lambda g: slice(SSM_D + g * SSM_STATE, SSM_D + (g + 1) * SSM_STATE)
    ccol = lambda g: slice(SSM_D + NG * SSM_STATE + g * SSM_STATE, SSM_D + NG * SSM_STATE + (g + 1) * SSM_STATE)

    def scan_groups(gs):
        csx = {g: sum(_dot(p, ex_ref[:, gcs[g]]) for p in cs_parts) for g in gs}
        dtx = {g: sum(_dot(p, ex_ref[:, gcs[g]]) for p in dt_parts) for g in gs}
        bgs = {g: _silu(xc_ref[:, bcol(g)]).astype(BF16) for g in gs}
        cgs = {g: _silu(xc_ref[:, ccol(g)]).astype(BF16) for g in gs}
        cbs = {g: _nt_dot(cgs[g], bgs[g]) for g in gs}
        htgs = {g: ht_ref[:, gcs[g]] for g in gs}
        ysts = {g: _dot(cgs[g], htgs[g].astype(BF16)) for g in gs}
        xss = {g: _silu(xc_ref[:, gcs[g]]) for g in gs}
        xdts = {g: xss[g] * dtx[g] for g in gs}
        xdt_bs = {g: xdts[g].astype(BF16) for g in gs}
        lastxs = {g: csx[g][L - 1:L, :] for g in gs}
        xdtw_bs = {g: (xdts[g] * jnp.exp(lastxs[g] - csx[g])).astype(BF16) for g in gs}
        prods = {}
        for g in gs:
            for e in range(HPG):
                h = g * HPG + e
                seg = cs[:, h:h + 1] - cs_t[h:h + 1, :]
                decay = jnp.exp(jnp.where(causal, seg, NEG_BIG))
                mat = (cbs[g] * decay).astype(BF16)
                xp = xdt_bs[g][:, (e // 2) * LANES:(e // 2 + 1) * LANES]
                keep = left if e % 2 == 0 else jnp.logical_not(left)
                prods[h] = _dot(mat, jnp.where(keep, xp, jnp.zeros_like(xp)))
        upd = {g: _tn_dot(bgs[g], xdtw_bs[g]) for g in gs}
        for g in gs:
            ht_ref[:, gcs[g]] = htgs[g] * jnp.exp(lastxs[g]) + upd[g]
        for g in gs:
            gc = gcs[g]
            y_in = jnp.concatenate([prods[g * HPG + 2 * pr] + prods[g * HPG + 2 * pr + 1] for pr in range(HPG // 2)],
                                   axis=1)
            y = y_in + ysts[g] * jnp.exp(csx[g]) + dskip_ref[:, gc] * xss[g]
            y = y * _silu(z_ref[0, :, gc].astype(F32))
            ms = jnp.mean(y * y, axis=-1, keepdims=True)
            y_ref[0, :, gc] = (y * lax.rsqrt(ms + NORM_EPS) * gn_ref[:, gc]).astype(BF16)

    for first in range(0, NG, GROUPS_PER_PASS):
        scan_groups(range(first, first + GROUPS_PER_PASS))

    @pl.when(c == nchunks - 1)
    def _():
        for k in range(SSM_D // LANES):
            hout_ref[0, k * LANES:(k + 1) * LANES, :] = ht_ref[:, k * LANES:(k + 1) * LANES].T


def _ret_kernel(q_ref, k_ref, v_ref, rg_ref, cos_ref, sin_ref, s0_ref, gn_ref, y_ref, sout_ref, *, c, ltrue):
    L = CHUNK

    @pl.when(c == 0)
    def _():
        sout_ref[...] = s0_ref[...]

    row = lax.broadcasted_iota(jnp.int32, (L, L), 0)
    col = lax.broadcasted_iota(jnp.int32, (L, L), 1)
    rel = (row - col).astype(F32)
    idx = row[:, 0:1].astype(F32)
    cosf = cos_ref[...]
    sins = sin_ref[...]
    H = RET_HEADS
    lgs = [math.log(1.0 - 2.0 ** (-5.0 - h)) for h in range(H)]
    kcs = [slice(h * RET_DK, (h + 1) * RET_DK) for h in range(H)]
    vcs = [slice(h * RET_DV, (h + 1) * RET_DV) for h in range(H)]
    qrs, krs = [], []
    for h in range(H):
        qh = q_ref[0, :, kcs[h]].astype(F32)
        kh = k_ref[0, :, kcs[h]].astype(F32)
        qrs.append(qh * cosf + pltpu.roll(qh, RET_DK // 2, 1) * sins)
        krs.append((kh * cosf + pltpu.roll(kh, RET_DK // 2, 1) * sins) * RET_DK ** -0.5)
    qr_bs = [q_.astype(BF16) for q_ in qrs]
    vhs = [v_ref[0, :, vcs[h]] for h in range(H)]
    s_olds = [sout_ref[0, h] for h in range(H)]
    scores = [_nt_dot(qr_bs[h], krs[h].astype(BF16)) for h in range(H)]
    cross = [_dot(qr_bs[h], s_olds[h].astype(BF16)) for h in range(H)]
    atts = []
    for h in range(H):
        dmat = jnp.where(rel >= 0, jnp.exp(jnp.maximum(rel, 0.0) * lgs[h]), 0.0)
        atts.append((scores[h] * dmat).astype(BF16))
    inner = [_dot(atts[h], vhs[h]) for h in range(H)]
    for h in range(H):
        k_dec = jnp.exp((ltrue - 1.0 - idx) * lgs[h])
        sout_ref[0, h] = s_olds[h] * math.exp(ltrue * lgs[h]) + _tn_dot((krs[h] * k_dec).astype(BF16), vhs[h])
    for h in range(H):
        o = inner[h] + cross[h] * jnp.exp((idx + 1.0) * lgs[h])
        oc = o - jnp.mean(o, axis=-1, keepdims=True)
        on = oc * lax.rsqrt(jnp.mean(oc * oc, axis=-1, keepdims=True) + NORM_EPS)
        y_ref[0, :, vcs[h]] = (on * gn_ref[:, vcs[h]] * _silu(rg_ref[0, :, vcs[h]].astype(F32))).astype(BF16)


N_SSD_IN, N_RET_IN, N_SSD_OUT, N_RET_OUT = 12, 8, 3, 2


CHUNKS_PER_STEP = 4


def _scan_kernel(*refs, nvalid, nchunks, per_step):
    i0 = N_SSD_IN
    i1 = i0 + N_RET_IN
    i2 = i1 + N_SSD_OUT
    i3 = i2 + N_RET_OUT
    ssd_in, ret_in, ssd_out, ret_out = refs[0:i0], refs[i0:i1], refs[i1:i2], refs[i2:i3]

    def chunk(sc, carry):
        c = pl.program_id(1) * per_step + sc
        rows = pl.ds(pl.multiple_of(sc * CHUNK, CHUNK), CHUNK)
        seq = lambda r: r.at[:, rows, :]
        tab = lambda r: r.at[rows, :]
        _ret_kernel(*[seq(r) for r in ret_in[0:4]], tab(ret_in[4]), tab(ret_in[5]), *ret_in[6:],
                    seq(ret_out[0]), ret_out[1], c=c, ltrue=float(nvalid))
        _ssd_kernel(*[seq(r) for r in ssd_in[0:3]], *ssd_in[3:], seq(ssd_out[0]), *ssd_out[1:], *refs[i3:],
                    c=c, nvalid=nvalid, nchunks=nchunks)
        return carry

    lax.fori_loop(0, per_step, chunk, 0)


def _scans(proj, dt_raw, tail0, h0, cos_t, sin_t, s0, p, nvalid):
    b, t, _ = proj.shape
    nchunks = t // CHUNK
    per_step = math.gcd(nchunks, CHUNKS_PER_STEP)
    L = CHUNK * per_step
    qk_w = RET_HEADS * RET_DK
    v_w = RET_HEADS * RET_DV
    const = lambda shape: pl.BlockSpec(shape, lambda i, c: (0,) * len(shape))
    ssd_in = [
        pl.BlockSpec((1, L, CONV_DIM), lambda i, c: (i, c, C_XBC // CONV_DIM)),
        pl.BlockSpec((1, L, SSM_D), lambda i, c: (i, c, C_Z // SSM_D)),
        pl.BlockSpec((1, L, LANES), lambda i, c: (i, c, 0)),
        pl.BlockSpec((1, CONV_TAIL, CONV_DIM), lambda i, c: (i, 0, 0)),
        pl.BlockSpec((1, SSM_D, SSM_STATE), lambda i, c: (i, 0, 0)),
        const((CONV_W, CONV_DIM)), const((1, CONV_DIM)), const((1, LANES)), const((1, LANES)),
        const((1, SSM_D)), const((1, SSM_D)), const((LANES, SSM_D)),
    ]
    ret_in = [
        pl.BlockSpec((1, L, qk_w), lambda i, c: (i, c, C_RQ // qk_w)),
        pl.BlockSpec((1, L, qk_w), lambda i, c: (i, c, C_RK // qk_w)),
        pl.BlockSpec((1, L, v_w), lambda i, c: (i, c, C_RV // v_w)),
        pl.BlockSpec((1, L, v_w), lambda i, c: (i, c, C_RG // v_w)),
        pl.BlockSpec((L, RET_DK), lambda i, c: (c, 0)),
        pl.BlockSpec((L, RET_DK), lambda i, c: (c, 0)),
        pl.BlockSpec((1, RET_HEADS, RET_DK, RET_DV), lambda i, c: (i, 0, 0, 0)),
        const((1, v_w)),
    ]
    ssd_out = [
        pl.BlockSpec((1, L, SSM_D), lambda i, c: (i, c, 0)),
        pl.BlockSpec((1, SSM_D, SSM_STATE), lambda i, c: (i, 0, 0)),
        pl.BlockSpec((1, CONV_TAIL, CONV_DIM), lambda i, c: (i, 0, 0)),
    ]
    ret_out = [
        pl.BlockSpec((1, L, v_w), lambda i, c: (i, c, 0)),
        pl.BlockSpec((1, RET_HEADS, RET_DK, RET_DV), lambda i, c: (i, 0, 0, 0)),
    ]
    assert (len(ssd_in), len(ret_in), len(ssd_out), len(ret_out)) == (N_SSD_IN, N_RET_IN, N_SSD_OUT, N_RET_OUT)
    return pl.pallas_call(
        functools.partial(_scan_kernel, nvalid=nvalid, nchunks=nchunks, per_step=per_step),
        grid=(b, nchunks // per_step),
        in_specs=ssd_in + ret_in,
        out_specs=ssd_out + ret_out,
        out_shape=[
            jax.ShapeDtypeStruct((b, t, SSM_D), BF16),
            jax.ShapeDtypeStruct((b, SSM_D, SSM_STATE), F32),
            jax.ShapeDtypeStruct((b, CONV_TAIL, CONV_DIM), F32),
            jax.ShapeDtypeStruct((b, t, v_w), BF16),
            jax.ShapeDtypeStruct((b, RET_HEADS, RET_DK, RET_DV), F32),
        ],
        scratch_shapes=[
            pltpu.VMEM((CONV_TAIL + CHUNK, CONV_DIM), BF16),
            pltpu.VMEM((CONV_TAIL, CONV_DIM), BF16),
            pltpu.VMEM((CHUNK, CONV_DIM), F32),
            pltpu.VMEM((SSM_STATE, SSM_D), F32),
        ],
        compiler_params=_cparams("arbitrary", "arbitrary"),
        name="scans",
    )(proj, proj, dt_raw, tail0, h0, p["conv_w"], p["conv_b"], p["dt_bias"], p["a_log"], p["d_skip"],
      p["ssm_norm"], p["head_expand"], proj, proj, proj, proj, cos_t, sin_t, s0, p["ret_norm"])


def _bias_kernel(tab_ref, idx_ref, out_ref, *, scale):
    h = pl.program_id(0)
    idx = idx_ref[...]
    acc = jnp.zeros(idx.shape, F32)
    for b in range(N_BUCKETS):
        acc = acc + jnp.where(idx == b, tab_ref[b * DIFF_HEADS + h], 0.0)
    out_ref[0] = acc * scale


def _bias_tiles(rel_bias, idx, scale):
    r, c = idx.shape
    return pl.pallas_call(
        functools.partial(_bias_kernel, scale=scale),
        grid=(DIFF_HEADS,),
        in_specs=[pl.BlockSpec(memory_space=pltpu.SMEM), pl.BlockSpec((r, c), lambda h: (0, 0))],
        out_specs=pl.BlockSpec((1, r, c), lambda h: (h, 0, 0)),
        out_shape=jax.ShapeDtypeStruct((DIFF_HEADS, r, c), F32),
        compiler_params=_cparams("arbitrary"),
        name="t5_bias",
    )(rel_bias.reshape(-1), idx)


def _half_rmsnorm(x, gain, bd):
    x2 = x * x
    hi = x2.astype(BF16)
    lo = (x2 - hi.astype(F32)).astype(BF16)
    ss = _dot(hi, bd) + _dot(lo, bd)
    return x * lax.rsqrt(ss * (1.0 / DIFF_HD) + NORM_EPS) * gain


def _half_rmsnorm_blocks(blocks, gain, bd):
    rows = blocks[0].shape[0]
    y = _half_rmsnorm(jnp.concatenate(blocks, axis=0), gain, bd)
    return [y[i * rows:(i + 1) * rows] for i in range(len(blocks))]


def _lambda(lamv_ref, lam_init):
    s1 = jnp.sum(lamv_ref[0:1, :] * lamv_ref[1:2, :], axis=-1, keepdims=True)
    s2 = jnp.sum(lamv_ref[2:3, :] * lamv_ref[3:4, :], axis=-1, keepdims=True)
    return jnp.exp(s1) - jnp.exp(s2) + lam_init


FAR_UNITS = 4
Q_TILES_PER_STEP = 4


def _dattn_kernel(q_ref, kf_ref, vf_ref, bias_ref, gq_ref, gk_ref, lamv_ref, sg_ref, bd_ref, kprev_ref,
                  y_ref, kout_ref, knt_ref, vx_ref, qs_ref, m_ref, acc_ref, *, t, q_tiles, lam_init):
    del kprev_ref
    step_id = pl.program_id(1)
    TQ = CHUNK
    G = DIFF_KV_HEADS
    R = DIFF_HEADS // DIFF_KV_HEADS
    NC = 2 * R * TQ
    bd = bd_ref[...]

    @pl.when(step_id == 0)
    def _():
        for i in range(t // TQ):
            r = slice(i * TQ, (i + 1) * TQ)
            kns = _half_rmsnorm_blocks([kf_ref[0, r, g * LANES:(g + 1) * LANES] for g in range(G)], gk_ref[...], bd)
            for g in range(G):
                knt = kns[g].T
                kout_ref[0, g * LANES:(g + 1) * LANES, r] = knt
                knt_ref[g, i] = knt.astype(BF16)
                vx_ref[g, r, 0:DIFF_VD] = vf_ref[0, pl.ds(i * TQ * G + g, TQ, stride=G), :].astype(BF16)
                vx_ref[g, r, DIFF_VD:] = jnp.ones((TQ, LANES), BF16)

    def tile(ti, carry):
        qi = step_id * q_tiles + ti
        qrows = pl.ds(pl.multiple_of(ti * TQ, TQ), TQ)
        lane = lax.broadcasted_iota(jnp.int32, (TQ, LANES), 1)
        left = lane < DIFF_HD
        qns = _half_rmsnorm_blocks([q_ref[0, qrows, h * LANES:(h + 1) * LANES].astype(F32)
                                    for h in range(DIFF_HEADS)], gq_ref[...], bd)
        for g in range(G):
            for r in range(R):
                qn = qns[g * R + r] * (DIFF_HD ** -0.5 * LOG2E)
                qs_ref[g, (0 * R + r) * TQ:(0 * R + r + 1) * TQ, :] = jnp.where(left, qn, 0.0).astype(BF16)
                qs_ref[g, (1 * R + r) * TQ:(1 * R + r + 1) * TQ, :] = jnp.where(left, 0.0, qn).astype(BF16)
        def step(k0, tiles, first=False):
            nunits = len(tiles)
            rows = pl.ds(pl.multiple_of(k0 * TQ, TQ), nunits * TQ)
            kts = [knt_ref[g, k0] if nunits == 1 else
                   jnp.concatenate([knt_ref[g, k0 + u] for u in range(nunits)], axis=1) for g in range(G)]
            ss = [_dot(qs_ref[g], kts[g]) for g in range(G)]
            ps, alphas = [], []
            for g in range(G):
                s = ss[g] + jnp.concatenate([bias_ref[g, tl] for tl in tiles], axis=1)
                if 0 in tiles:
                    diag0 = tiles.index(0) * TQ
                    key = lax.broadcasted_iota(jnp.int32, (NC, nunits * TQ), 1) - diag0
                    qry = lax.broadcasted_iota(jnp.int32, (NC, nunits * TQ), 0) & (TQ - 1)
                    s = jnp.where(key <= qry, s, NEG_BIG)
                row_max = jnp.max(s, axis=1, keepdims=True)
                if first:
                    m_new = jnp.broadcast_to(row_max, (NC, LANES))
                else:
                    m_old = m_ref[g]
                    m_new = jnp.maximum(m_old, row_max)
                    alphas.append(jnp.exp2(m_old - m_new))
                p = jnp.concatenate([jnp.exp2(s[:, u * TQ:(u + 1) * TQ] - m_new) for u in range(nunits)], axis=1)
                m_ref[g] = m_new
                ps.append(p.astype(BF16))
            for g in range(G):
                pv = _dot(ps[g], vx_ref[g, rows, :])
                if first:
                    acc_ref[g] = pv
                else:
                    acc_ref[g] = acc_ref[g] * jnp.concatenate([alphas[g], alphas[g]], axis=1) + pv

        nfar = jnp.maximum(qi - (FAR_UNITS - 1), 0)
        nbig = nfar // FAR_UNITS
        rem = nfar - nbig * FAR_UNITS

        for extra in range(FAR_UNITS):
            @pl.when(jnp.logical_and(qi >= FAR_UNITS - 1, rem == extra))
            def _(extra=extra):
                step(qi - (FAR_UNITS - 1) - extra, (2,) * (FAR_UNITS - 2 + extra) + (1, 0), first=True)

        for early in range(FAR_UNITS - 1):
            @pl.when(qi == early)
            def _(early=early):
                step(0, ((2,) * early + (1, 0))[-(early + 1):], first=True)

        def far_big(i, c):
            step(i * FAR_UNITS, (2,) * FAR_UNITS)
            return c

        lax.fori_loop(0, nbig, far_big, 0)

        lam = _lambda(lamv_ref, lam_init)
        for g in range(G):
            for r in range(R):
                r0 = slice((0 * R + r) * TQ, (0 * R + r + 1) * TQ)
                r1 = slice((1 * R + r) * TQ, (1 * R + r + 1) * TQ)
                o = (acc_ref[g, r0, 0:DIFF_VD] / acc_ref[g, r0, DIFF_VD:]
                     - lam * (acc_ref[g, r1, 0:DIFF_VD] / acc_ref[g, r1, DIFF_VD:]))
                o = o * lax.rsqrt(jnp.mean(o * o, axis=1, keepdims=True) + NORM_EPS)
                o = o * sg_ref[...] * (1.0 - lam_init)
                y_ref[0, qrows, (g * R + r) * LANES:(g * R + r + 1) * LANES] = o.astype(BF16)
        return carry

    lax.fori_loop(0, q_tiles, tile, 0)


def _diff_attention_prompt(proj, kf, vbuf, kbuf, bias, p, layer, lam_init):
    b, t, _ = proj.shape
    vf = vbuf.reshape(vbuf.shape[0], b, t * DIFF_KV_HEADS, DIFF_VD)
    TQ = CHUNK
    G = DIFF_KV_HEADS
    R = DIFF_HEADS // DIFF_KV_HEADS
    NC = 2 * R * TQ
    qw = DIFF_HEADS * 2 * DIFF_HD
    const = lambda shape: pl.BlockSpec(shape, lambda i, q: (0,) * len(shape))
    q_tiles = math.gcd(t // TQ, Q_TILES_PER_STEP)
    return pl.pallas_call(
        functools.partial(_dattn_kernel, t=t, q_tiles=q_tiles, lam_init=lam_init),
        grid=(b, t // (TQ * q_tiles)),
        in_specs=[
            pl.BlockSpec((1, TQ * q_tiles, qw), lambda i, q: (i, q, C_DQ // qw)),
            pl.BlockSpec((1, t, KV_COLS), lambda i, q: (i, 0, 0)),
            pl.BlockSpec((None, 1, t * G, DIFF_VD), lambda i, q: (layer, i, 0, 0)),
            const((G, 3, NC, TQ)),
            const((1, LANES)), const((1, LANES)), const((4, LANES)), const((1, DIFF_VD)), const((LANES, LANES)),
            pl.BlockSpec(memory_space=pl.ANY),
        ],
        out_specs=[
            pl.BlockSpec((1, TQ * q_tiles, qw), lambda i, q: (i, q, 0)),
            pl.BlockSpec((None, 1, KV_COLS, t), lambda i, q: (layer, i, 0, 0)),
        ],
        out_shape=[
            jax.ShapeDtypeStruct((b, t, DIFF_HEADS * DIFF_VD), BF16),
            jax.ShapeDtypeStruct(kbuf.shape, F32),
        ],
        input_output_aliases={9: 1},
        scratch_shapes=[
            pltpu.VMEM((G, t // TQ, LANES, TQ), BF16), pltpu.VMEM((G, t, DIFF_VD + LANES), BF16),
            pltpu.VMEM((G, NC, LANES), BF16),
            pltpu.VMEM((G, NC, LANES), F32),
            pltpu.VMEM((G, NC, DIFF_VD + LANES), F32),
        ],
        compiler_params=_cparams("arbitrary", "arbitrary"),
        name="diff_attn_prompt",
    )(proj, kf, vf, bias, p["qk_norm_q"], p["qk_norm_k"], p["lamv"], p["diff_norm"], p["blockdiag"], kbuf)


PAGES_PER_STEP = 32


def _sattn_kernel(pt_ref, *refs, layer, nvalid, lam_init, npp):
    del pt_ref, layer
    k_refs = refs[0:npp]
    v_refs = refs[npp:2 * npp]
    (q_ref, kf_ref, vf_ref, bias_ref, gq_ref, gk_ref, lamv_ref, sg_ref, bd_ref,
     y_ref, kout_ref, qs_ref, knew_ref, vnew_ref, m_ref, l_ref, acc_ref) = refs[2 * npp:]
    s_id = pl.program_id(1)
    nsteps = pl.num_programs(1)
    G = DIFF_KV_HEADS
    R = DIFF_HEADS // DIFF_KV_HEADS
    TP = SUBLANES
    MR = 2 * R * TP
    bd = bd_ref[...]
    lane = lax.broadcasted_iota(jnp.int32, (TP, LANES), 1)
    left = lane < DIFF_HD

    @pl.when(s_id == 0)
    def _():
        knew_ref[...] = jnp.zeros(knew_ref.shape, BF16)
        vnew_ref[...] = jnp.zeros(vnew_ref.shape, BF16)
        for g in range(G):
            gc = slice(g * LANES, (g + 1) * LANES)
            kn = _half_rmsnorm(kf_ref[0, :, gc], gk_ref[...], bd)
            kout_ref[0, :, gc] = kn
            knew_ref[g, 0:TP, :] = kn.astype(BF16)
            vnew_ref[g, 0:TP, :] = vf_ref[0, pl.ds(g, TP, stride=G), :].astype(BF16)
            for r in range(R):
                hc = slice((g * R + r) * LANES, (g * R + r + 1) * LANES)
                qn = _half_rmsnorm(q_ref[0, :, hc].astype(F32), gq_ref[...], bd) * DIFF_HD ** -0.5
                qs_ref[g, (0 * R + r) * TP:(0 * R + r + 1) * TP, :] = jnp.where(left, qn, 0.0).astype(BF16)
                qs_ref[g, (1 * R + r) * TP:(1 * R + r + 1) * TP, :] = jnp.where(left, 0.0, qn).astype(BF16)
        m_ref[...] = jnp.full(m_ref.shape, NEG_BIG, F32)
        l_ref[...] = jnp.zeros(l_ref.shape, F32)
        acc_ref[...] = jnp.zeros(acc_ref.shape, F32)

    def bias_rows(g, seg):
        per_head = [bias_ref[g * R + r, :, seg * LANES:(seg + 1) * LANES] for r in range(R)]
        return jnp.concatenate(per_head + per_head, axis=0)

    def update(g, s, vs):
        m_old = m_ref[g]
        m_new = jnp.maximum(m_old, jnp.max(s, axis=-1, keepdims=True))
        alpha = jnp.exp(m_old - m_new)
        p = jnp.exp(s - m_new)
        l_ref[g] = alpha * l_ref[g] + jnp.sum(p, axis=-1, keepdims=True)
        m_ref[g] = m_new
        pb = p.astype(BF16)
        pv = _dot(pb[:, 0:LANES], vs[0])
        for i in range(1, len(vs)):
            pv = pv + _dot(pb[:, i * LANES:(i + 1) * LANES], vs[i])
        acc_ref[g] = acc_ref[g] * alpha + pv

    def pages(last):
        scores = []
        for g in range(G):
            gc = slice(g * LANES, (g + 1) * LANES)
            kcat = jnp.concatenate([k_refs[i][gc, :].astype(BF16) for i in range(npp)], axis=1)
            far = bias_rows(g, 0)
            near = bias_rows(g, 1) if last else far
            scores.append(_dot(qs_ref[g], kcat) + jnp.concatenate([far] * (npp - 1) + [near], axis=1))
        pvs, alphas = [], []
        for g in range(G):
            s = scores[g]
            m_old = m_ref[g]
            m_new = jnp.maximum(m_old, jnp.max(s, axis=-1, keepdims=True))
            alpha = jnp.exp(m_old - m_new)
            p = jnp.exp(s - m_new)
            l_ref[g] = alpha * l_ref[g] + jnp.sum(p, axis=-1, keepdims=True)
            m_ref[g] = m_new
            vcat = jnp.concatenate([v_refs[i][pl.ds(g, PAGE_SIZE, stride=G), :].astype(BF16) for i in range(npp)],
                                   axis=0)
            pvs.append(_dot(p.astype(BF16), vcat))
            alphas.append(alpha)
        for g in range(G):
            acc_ref[g] = acc_ref[g] * alphas[g] + pvs[g]

    @pl.when(s_id < nsteps - 1)
    def _():
        pages(False)

    @pl.when(s_id == nsteps - 1)
    def _():
        pages(True)
        rowt = lax.broadcasted_iota(jnp.int32, (MR, LANES), 0) % TP
        colj = lax.broadcasted_iota(jnp.int32, (MR, LANES), 1)
        ok = jnp.logical_and(colj <= rowt, colj < nvalid)
        lam = _lambda(lamv_ref, lam_init)
        for g in range(G):
            sc = _nt_dot(qs_ref[g], knew_ref[g]) + bias_rows(g, 2)
            update(g, jnp.where(ok, sc, NEG_BIG), [vnew_ref[g]])
            acc = acc_ref[g] / l_ref[g]
            for r in range(R):
                o = acc[(0 * R + r) * TP:(0 * R + r + 1) * TP, :] - lam * acc[(1 * R + r) * TP:(1 * R + r + 1) * TP, :]
                o = o * lax.rsqrt(jnp.mean(o * o, axis=-1, keepdims=True) + NORM_EPS)
                hc = slice((g * R + r) * LANES, (g * R + r + 1) * LANES)
                y_ref[0, :, hc] = (o * sg_ref[...] * (1.0 - lam_init)).astype(BF16)


def _diff_attention_sample(proj, kf, vbuf, cache_k, cache_v, page_table, bias, p, layer, nvalid, lam_init):
    b, tp, _ = proj.shape
    vf = vbuf.reshape(vbuf.shape[0], b, tp * DIFF_KV_HEADS, DIFF_VD)
    n_pages = page_table.shape[1]
    npp = PAGES_PER_STEP
    while n_pages % npp:
        npp //= 2
    nsteps = n_pages // npp
    G = DIFF_KV_HEADS
    R = DIFF_HEADS // DIFF_KV_HEADS
    MR = 2 * R * tp
    ck = jnp.transpose(cache_k, (0, 1, 3, 4, 5, 2)).reshape(cache_k.shape[0], cache_k.shape[1], KV_COLS, PAGE_SIZE)
    cv = cache_v.reshape(cache_v.shape[0], cache_v.shape[1], PAGE_SIZE * DIFF_KV_HEADS, DIFF_VD)

    def page_spec(i):
        return pl.BlockSpec((None, None, KV_COLS, PAGE_SIZE),
                            lambda bi, s, pt: (layer, pt[bi * n_pages + s * npp + i], 0, 0))

    const = lambda shape: pl.BlockSpec(shape, lambda bi, s, pt: (0,) * len(shape))
    grid_spec = pltpu.PrefetchScalarGridSpec(
        num_scalar_prefetch=1,
        grid=(b, nsteps),
        in_specs=[page_spec(i) for i in range(npp)] + [page_spec(i) for i in range(npp)] + [
            pl.BlockSpec((1, tp, DIFF_HEADS * 2 * DIFF_HD), lambda bi, s, pt: (bi, 0, C_DQ // (DIFF_HEADS * 2 * DIFF_HD))),
            pl.BlockSpec((1, tp, KV_COLS), lambda bi, s, pt: (bi, 0, 0)),
            pl.BlockSpec((None, 1, tp * G, DIFF_VD), lambda bi, s, pt: (layer, bi, 0, 0)),
            const((DIFF_HEADS, tp, 3 * LANES)),
            const((1, LANES)), const((1, LANES)), const((4, LANES)), const((1, LANES)), const((LANES, LANES)),
        ],
        out_specs=[
            pl.BlockSpec((1, tp, DIFF_HEADS * DIFF_VD), lambda bi, s, pt: (bi, 0, 0)),
            pl.BlockSpec((1, tp, KV_COLS), lambda bi, s, pt: (bi, 0, 0)),
        ],
        scratch_shapes=[
            pltpu.VMEM((G, MR, LANES), BF16),
            pltpu.VMEM((G, PAGE_SIZE, LANES), BF16), pltpu.VMEM((G, PAGE_SIZE, LANES), BF16),
            pltpu.VMEM((G, MR, 1), F32), pltpu.VMEM((G, MR, 1), F32), pltpu.VMEM((G, MR, LANES), F32),
        ],
    )
    return pl.pallas_call(
        functools.partial(_sattn_kernel, layer=layer, nvalid=nvalid, lam_init=lam_init, npp=npp),
        grid_spec=grid_spec,
        out_shape=[
            jax.ShapeDtypeStruct((b, tp, DIFF_HEADS * DIFF_VD), BF16),
            jax.ShapeDtypeStruct((b, tp, KV_COLS), F32),
        ],
        compiler_params=_cparams("arbitrary", "arbitrary"),
        name="diff_attn_sample",
    )(page_table.reshape(-1), *([ck] * npp), *([cv] * npp), proj, kf, vf, bias,
      p["qk_norm_q"], p["qk_norm_k"], p["lamv"], p["diff_norm"], p["blockdiag"])


def _merge_kernel(ys_ref, yr_ref, yd_ref, g0_ref, g1_ref, g2_ref, x_ref, ws_ref, wr_ref, wd_ref, wo_ref,
                  bg_ref, h_ref):
    merged = None
    for i, (y_ref, w_ref, g_ref) in enumerate(((ys_ref, ws_ref, g0_ref), (yr_ref, wr_ref, g1_ref),
                                               (yd_ref, wd_ref, g2_ref))):
        br = _dot(y_ref[...], w_ref[...])
        t = _sigmoid(g_ref[...].astype(F32) + bg_ref[i:i + 1, :]) * br
        merged = t if merged is None else merged + t
    h_ref[...] = x_ref[...] + _dot(merged.astype(BF16), wo_ref[...])


def _merge(y_ssm, y_ret, y_diff, proj2d, x2d, p, tm):
    n = x2d.shape[0]
    tok = lambda cb: pl.BlockSpec((tm, D_MODEL), lambda i: (i, cb))
    wspec = pl.BlockSpec((D_MODEL, D_MODEL), lambda i: (0, 0))
    g0 = C_GATE // D_MODEL
    return pl.pallas_call(
        _merge_kernel,
        grid=(n // tm,),
        in_specs=[tok(0), tok(0), tok(0), tok(g0), tok(g0 + 1), tok(g0 + 2), tok(0),
                  wspec, wspec, wspec, wspec, pl.BlockSpec((N_BRANCHES, D_MODEL), lambda i: (0, 0))],
        out_specs=tok(0),
        out_shape=jax.ShapeDtypeStruct((n, D_MODEL), F32),
        compiler_params=_cparams("arbitrary"),
        name="merge",
    )(y_ssm, y_ret, y_diff, proj2d, proj2d, proj2d, x2d, p["w_ssm_out"], p["w_ret_out"], p["w_diff_out"],
      p["w_o"], p["b_gate"])


def _ffn_kernel(h_ref, g_ref, wg_ref, wu_ref, wd_ref, y_ref):
    h = h_ref[...]
    ms = jnp.mean(h * h, axis=-1, keepdims=True)
    hn = (h * lax.rsqrt(ms + NORM_EPS) * g_ref[...]).astype(BF16)
    act = _silu(_dot(hn, wg_ref[...])) * _dot(hn, wu_ref[...])
    y_ref[...] = h + _dot(act.astype(BF16), wd_ref[...])


def _ffn(h2d, p, tm):
    n = h2d.shape[0]
    resident = lambda shape, cb: pl.BlockSpec(shape, lambda i: (0, cb), pipeline_mode=pl.Buffered(1))
    return pl.pallas_call(
        _ffn_kernel,
        grid=(n // tm,),
        in_specs=[
            pl.BlockSpec((tm, D_MODEL), lambda i: (i, 0)),
            pl.BlockSpec((1, D_MODEL), lambda i: (0, 0)),
            resident((D_MODEL, D_FF), 0),
            resident((D_MODEL, D_FF), 1),
            resident((D_FF, D_MODEL), 0),
        ],
        out_specs=pl.BlockSpec((tm, D_MODEL), lambda i: (i, 0)),
        out_shape=jax.ShapeDtypeStruct((n, D_MODEL), F32),
        compiler_params=_cparams("arbitrary"),
        name="ffn",
    )(h2d, p["norm_ffn"], p["w_gate_up"], p["w_gate_up"], p["w_down"])


def _t5_bucket(dist):
    n = jnp.maximum(dist, 0)
    max_exact = N_BUCKETS // 2
    large = max_exact + (jnp.log(jnp.maximum(n, 1).astype(F32) / max_exact)
                         / math.log(MAX_DISTANCE / max_exact) * (N_BUCKETS - max_exact)).astype(jnp.int32)
    large = jnp.minimum(large, N_BUCKETS - 1)
    return jnp.where(n < max_exact, n, large)


def _far_bucket_is_constant(min_dist):
    max_exact = N_BUCKETS // 2
    d = np.float32(min_dist)
    large = max_exact + int(np.float32(np.log(d / np.float32(max_exact))) / np.float32(math.log(MAX_DISTANCE / max_exact))
                            * (N_BUCKETS - max_exact))
    return min_dist >= max_exact and large >= N_BUCKETS - 1


def _rope_tables(pos):
    half = RET_DK // 2
    inv = 1.0 / (ROPE_BASE ** (jnp.arange(half, dtype=F32) / half))
    ang = pos.astype(F32)[:, None] * inv[None, :]
    cos, sin = jnp.cos(ang), jnp.sin(ang)
    return jnp.concatenate([cos, cos], axis=1), jnp.concatenate([-sin, sin], axis=1)


def _layer_params(l, w_in, named):
    p = {k: v[l] for k, v in named.items()}
    w = w_in[l]
    o = (0,) + IN_OFFSETS + (w.shape[1],)
    z, xbc, dt, rq, rk, rv, rg, dq, dk, dv, gates = [w[:, o[i]:o[i + 1]] for i in range(len(IN_SPLITS))]
    out = {}
    out["w_main"] = jnp.concatenate([xbc, z, rq, rk, rv, rg, dq, gates], axis=1).astype(BF16)
    out["w_kvdt"] = jnp.concatenate([dk, dv, jnp.pad(dt, ((0, 0), (0, LANES - SSM_HEADS)))], axis=1).astype(BF16)
    out["norm_mix"] = p["norm_mix"].reshape(1, D_MODEL)
    out["conv_w"] = p["conv_w"]
    out["conv_b"] = p["conv_b"].reshape(1, CONV_DIM)
    out["dt_bias"] = jnp.pad(p["dt_bias"], (0, LANES - SSM_HEADS)).reshape(1, LANES)
    out["a_log"] = jnp.pad(p["a_log"], (0, LANES - SSM_HEADS)).reshape(1, LANES)
    out["d_skip"] = jnp.repeat(p["d_skip"], SSM_HEAD_DIM).reshape(1, SSM_D)
    out["ssm_norm"] = p["ssm_norm"].reshape(1, SSM_D)
    head_of_channel = np.arange(SSM_D) // SSM_HEAD_DIM
    out["head_expand"] = jnp.asarray(np.arange(LANES)[:, None] == head_of_channel[None, :], dtype=BF16)
    out["ret_norm"] = p["ret_norm"].reshape(1, RET_HEADS * RET_DV)
    out["qk_norm_q"] = jnp.tile(p["qk_norm_q"], 2).reshape(1, LANES)
    out["qk_norm_k"] = jnp.tile(p["qk_norm_k"], 2).reshape(1, LANES)
    lamv = jnp.stack([p["lambda_q1"], p["lambda_k1"], p["lambda_q2"], p["lambda_k2"]])
    out["lamv"] = jnp.pad(lamv, ((0, 0), (0, LANES - DIFF_HD)))
    out["diff_norm"] = p["diff_norm"].reshape(1, DIFF_VD)
    half = np.arange(LANES) // DIFF_HD
    out["blockdiag"] = jnp.asarray(half[:, None] == half[None, :], dtype=BF16)
    for k in ("w_ssm_out", "w_ret_out", "w_diff_out", "w_o", "w_gate_up", "w_down"):
        out[k] = p[k].astype(BF16)
    out["b_gate"] = p["b_gate"]
    out["norm_ffn"] = p["norm_ffn"].reshape(1, D_MODEL)
    return out


def _token_tile(n, cap):
    tm = min(n, cap)
    while n % tm:
        tm //= 2
    return tm


def _layer_common(x, p, layer, vbuf, tail0, h0, s0, cos_t, sin_t, nvalid, attn_fn):
    b, tp, _ = x.shape
    n = b * tp
    x2d = x.reshape(n, D_MODEL)
    proj, kf, vbuf, dt_raw = _inproj(x2d, p["norm_mix"], p["w_main"], p["w_kvdt"], _token_tile(n, 1024), vbuf, layer)
    proj3 = proj.reshape(b, tp, C_MAIN)
    dt3 = dt_raw.reshape(b, tp, LANES)
    tpad = -tp % CHUNK
    scan_in = lambda a: jnp.pad(a, ((0, 0), (0, tpad), (0, 0))) if tpad else a
    scan_proj = scan_in(proj3[:, :, :C_DQ]) if tpad else proj3
    y_ssm, h_new, conv_rows, y_ret, s_new = _scans(scan_proj, scan_in(dt3), tail0, h0, cos_t, sin_t, s0, p,
                                                   nvalid)
    y_diff, k_new = attn_fn(proj3, kf.reshape(b, tp, KV_COLS), vbuf)
    tm = _token_tile(n, 512)
    h = _merge(y_ssm[:, :tp].reshape(n, -1), y_ret[:, :tp].reshape(n, -1), y_diff.reshape(n, -1), proj, x2d, p, tm)
    y = _ffn(h, p, tm)
    last = (nvalid - 1) % CONV_TAIL
    assert last >= CONV_W - 2
    return (y.reshape(b, tp, D_MODEL), k_new, vbuf,
            h_new.reshape(b, SSM_HEADS, SSM_HEAD_DIM, SSM_STATE), conv_rows[:, last - (CONV_W - 2):last + 1, :], s_new)


def kernel(x_prompt, x_sample, cache_k, cache_v, page_table, state_ssm, state_conv, state_ret, rel_bias, norm_mix, w_in, b_gate, conv_w, conv_b, dt_bias, a_log, d_skip, ssm_norm, w_ssm_out, ret_norm, w_ret_out, qk_norm_q, qk_norm_k, lambda_q1, lambda_k1, lambda_q2, lambda_k2, diff_norm, w_diff_out, w_o, norm_ffn, w_gate_up, w_down):
    named = dict(norm_mix=norm_mix, b_gate=b_gate, conv_w=conv_w, conv_b=conv_b, dt_bias=dt_bias, a_log=a_log,
                 d_skip=d_skip, ssm_norm=ssm_norm, w_ssm_out=w_ssm_out, ret_norm=ret_norm, w_ret_out=w_ret_out,
                 qk_norm_q=qk_norm_q, qk_norm_k=qk_norm_k, lambda_q1=lambda_q1, lambda_k1=lambda_k1,
                 lambda_q2=lambda_q2, lambda_k2=lambda_k2, diff_norm=diff_norm, w_diff_out=w_diff_out, w_o=w_o,
                 norm_ffn=norm_ffn, w_gate_up=w_gate_up, w_down=w_down)
    depth = w_in.shape[0]
    bp, seq, _ = x_prompt.shape
    bs, dec, _ = x_sample.shape
    n_pages = page_table.shape[1]
    past = n_pages * PAGE_SIZE
    assert seq % CHUNK == 0 and CONV_W - 1 <= dec <= SUBLANES
    assert _far_bucket_is_constant(CHUNK + 1)

    ii = jnp.arange(CHUNK)[:, None]
    jj = jnp.arange(CHUNK)[None, :]
    idx_p = jnp.concatenate([_t5_bucket(k * CHUNK + ii - jj) for k in range(3)], axis=0)
    bias_h = _bias_tiles(rel_bias, idx_p, LOG2E).reshape(DIFF_KV_HEADS, DIFF_HEADS // DIFF_KV_HEADS, 3, CHUNK, CHUNK)
    bias_p = jnp.concatenate([bias_h[:, r] for r in range(DIFF_HEADS // DIFF_KV_HEADS)] * 2, axis=-2)
    tt = jnp.arange(SUBLANES)[:, None]
    idx_s = jnp.concatenate([_t5_bucket(jnp.broadcast_to(past + tt, (SUBLANES, LANES))),
                             _t5_bucket(tt + PAGE_SIZE - jj), _t5_bucket(tt - jj)], axis=1)
    bias_s = _bias_tiles(rel_bias, idx_s, 1.0)

    cos_p, sin_p = _rope_tables(jnp.arange(seq))
    cos_s, sin_s = _rope_tables(past + jnp.arange(CHUNK))

    xs = jnp.pad(x_sample, ((0, 0), (0, SUBLANES - dec), (0, 0)))
    zeros_tail = jnp.zeros((bp, CONV_TAIL, CONV_DIM), F32)
    zeros_h = jnp.zeros((bp, SSM_D, SSM_STATE), F32)
    zeros_s = jnp.zeros((bp, RET_HEADS, RET_DK, RET_DV), F32)

    vbuf_p = jnp.zeros((depth, bp * seq * DIFF_KV_HEADS, DIFF_VD), F32)
    vbuf_s = jnp.zeros((depth, bs * SUBLANES * DIFF_KV_HEADS, DIFF_VD), F32)
    kbuf_p = jnp.zeros((depth, bp, KV_COLS, seq), F32)

    yp, ys = x_prompt, xs
    outs_p, outs_s = [], []
    for l in range(depth):
        lam_init = 0.8 - 0.6 * math.exp(-0.3 * l)
        p = _layer_params(l, w_in, named)

        def attn_p(proj3, kf, vbuf, l=l, p=p, lam_init=lam_init, kbuf=kbuf_p):
            return _diff_attention_prompt(proj3, kf, vbuf, kbuf, bias_p, p, l, lam_init)

        yp, kbuf_p, vbuf_p, h1, c1, r1 = _layer_common(yp, p, l, vbuf_p, zeros_tail, zeros_h, zeros_s, cos_p, sin_p,
                                                       CHUNK, attn_p)
        outs_p.append((h1, c1, r1))

        tail_s = jnp.pad(state_conv[l], ((0, 0), (CONV_TAIL - (CONV_W - 1), 0), (0, 0)))
        h0_s = state_ssm[l].reshape(bs, SSM_D, SSM_STATE)

        def attn_s(proj3, kf, vbuf, l=l, p=p, lam_init=lam_init):
            return _diff_attention_sample(proj3, kf, vbuf, cache_k, cache_v, page_table, bias_s, p, l, dec, lam_init)

        ys_new, k2, vbuf_s, h2, c2, r2 = _layer_common(ys, p, l, vbuf_s, tail_s, h0_s, state_ret[l], cos_s, sin_s, dec,
                                                       attn_s)
        ys = jnp.where(jnp.arange(SUBLANES)[None, :, None] < dec, ys_new, 0.0)
        outs_s.append((k2[:, :dec].reshape(bs, dec, DIFF_KV_HEADS, 2, DIFF_HD), h2, c2, r2))

    stack = lambda outs, i: jnp.stack([o[i] for o in outs])
    k_prompt = kbuf_p.reshape(depth, bp, DIFF_KV_HEADS, 2, DIFF_HD, seq).transpose(0, 1, 5, 2, 3, 4)
    v_prompt = vbuf_p.reshape(depth, bp, seq, DIFF_KV_HEADS, DIFF_VD)
    v_sample = vbuf_s.reshape(depth, bs, SUBLANES, DIFF_KV_HEADS, DIFF_VD)[:, :, :dec]
    return (yp, ys[:, :dec],
            k_prompt, v_prompt, stack(outs_p, 0), stack(outs_p, 1), stack(outs_p, 2),
            stack(outs_s, 0), v_sample, stack(outs_s, 1), stack(outs_s, 2), stack(outs_s, 3))
```
